```python
import functools
import jax, jax.numpy as jnp
from jax import lax
import numpy as np

D_MODEL = 2048
BATCH = 8
SEQ = 8192
DEPTH = 4

GRID_W = 64
CTX_LEN = 256
EPS = 1e-6
N_MOD = 6

RET_DK = 128
RET_DV = 128
RET_WIDTH = D_MODEL // 2
RET_HEADS = RET_WIDTH // RET_DV
RET_CHUNK = 128
K_SCALE = RET_DK ** -0.5
ROPE_THETA = 10000.0

POOL_WIDTH = D_MODEL // 4
POOL_WINDOWS = (2, 4, 8, 16)
POOL_GROUPS = len(POOL_WINDOWS)
POOL_GROUP_DIM = POOL_WIDTH // POOL_GROUPS

SG_WIDTH = D_MODEL // 4
SG_GROUPS = 4
SG_GROUP_DIM = SG_WIDTH // SG_GROUPS
SG_CHUNK = 128

N_BRANCH = 3
D_FF = 4 * D_MODEL
N_IN = 4 * RET_WIDTH + POOL_WIDTH + 2 * SG_WIDTH + N_BRANCH * D_MODEL
SPLIT_IDX = (RET_WIDTH, 2 * RET_WIDTH, 3 * RET_WIDTH, 4 * RET_WIDTH,
             4 * RET_WIDTH + POOL_WIDTH, 4 * RET_WIDTH + POOL_WIDTH + SG_WIDTH,
             4 * RET_WIDTH + POOL_WIDTH + 2 * SG_WIDTH)

kernel_name = "hybrid_retention_pool_sgmlp_diffusion_trunk"

F32 = jnp.float32


def rms_norm(x, g):
    xf = x.astype(F32)
    y = xf * lax.rsqrt(jnp.mean(xf * xf, axis=-1, keepdims=True) + EPS)
    return (y * g.astype(F32)).astype(x.dtype)


def modulate(h, shift, scale):
    return h * (1 + scale) + shift


def to_heads(t):
    b, tl, _ = t.shape
    return t.astype(F32).reshape(b, tl, RET_HEADS, -1).transpose(0, 2, 1, 3)


def axial_rope(t, rows, cols):
    half = RET_DK // 2
    nf = half // 2
    inv = ROPE_THETA ** (-jnp.arange(nf, dtype=F32) / nf)

    def rot(tp, pos):
        ang = pos[:, None] * inv[None]
        cos, sin = jnp.cos(ang), jnp.sin(ang)
        t1, t2 = tp[..., :nf], tp[..., nf:]
        return jnp.concatenate([t1 * cos - t2 * sin, t1 * sin + t2 * cos], axis=-1)

    return jnp.concatenate([rot(t[..., :half], rows), rot(t[..., half:], cols)], axis=-1)


def retention_chunkwise(q, k, v, log_gamma, state0):
    b, h, t, dk = q.shape
    n = t // RET_CHUNK
    qc = q.reshape(b, h, n, RET_CHUNK, dk)
    kc = k.reshape(b, h, n, RET_CHUNK, dk)
    vc = v.reshape(b, h, n, RET_CHUNK, -1)
    idx = jnp.arange(RET_CHUNK, dtype=F32)
    dist = idx[:, None] - idx[None, :]
    decay = jnp.where(dist >= 0, jnp.exp(log_gamma[:, None, None] * jnp.maximum(dist, 0.0)), 0.0)
    scores = jnp.einsum('bhncd,bhnsd->bhncs', qc, kc) * decay[None, :, None]
    o_intra = jnp.einsum('bhncs,bhnse->bhnce', scores, vc)
    q_decay = jnp.exp(log_gamma[:, None] * (idx + 1.0)[None])[None, :, None, :, None]
    k_decay = jnp.exp(log_gamma[:, None] * (RET_CHUNK - 1.0 - idx)[None])[None, :, None, :, None]
    kv = jnp.einsum('bhnsd,bhnse->bhnde', kc * k_decay, vc)
    chunk_decay = jnp.exp(log_gamma * RET_CHUNK)[None, :, None, None]

    def step(state, kv_n):
        return state * chunk_decay + kv_n, state

    final, states = lax.scan(step, state0, jnp.moveaxis(kv, 2, 0))
    states = jnp.moveaxis(states, 0, 2)
    o_cross = jnp.einsum('bhncd,bhnde->bhnce', qc * q_decay, states)
    return (o_intra + o_cross).reshape(b, h, t, -1), final


def retention_final_state(k, v, log_gamma):
    t = k.shape[2]
    w = jnp.exp(log_gamma[:, None] * (t - 1.0 - jnp.arange(t, dtype=F32))[None])
    return jnp.einsum('bhtd,bhte,ht->bhde', k, v, w)


def bidir_retention(q, k, v, lg_f, lg_b, sf0, sb0):
    of, sf = retention_chunkwise(q, k, v, lg_f, sf0)
    ob, sb = retention_chunkwise(jnp.flip(q, 2), jnp.flip(k, 2), jnp.flip(v, 2), lg_b, sb0)
    return of + jnp.flip(ob, 2), sf, sb


def retention_out(o, gate, norm_g):
    o = o * lax.rsqrt(jnp.mean(o * o, axis=-1, keepdims=True) + EPS)
    b, h, t, dv = o.shape
    o = o.transpose(0, 2, 1, 3).reshape(b, t, h * dv) * norm_g.astype(F32)
    return (o * jax.nn.silu(gate.astype(F32))).astype(gate.dtype)


def multiscale_pool(p, seg_len, pool_w, pool_scale):
    b, t, _ = p.shape
    ns = t // seg_len
    pf = p.astype(F32).reshape(b, ns, seg_len, POOL_GROUPS, POOL_GROUP_DIM)
    cs = jnp.pad(jnp.cumsum(pf, axis=2), ((0, 0), (0, 0), (1, 0), (0, 0), (0, 0)))
    pos = jnp.arange(seg_len)
    outs = []
    for gi, w in enumerate(POOL_WINDOWS):
        lo = jnp.maximum(pos - w // 2, 0)
        hi = jnp.minimum(pos + w // 2 - 1, seg_len - 1)
        cnt = (hi - lo + 1).astype(F32)
        cg = cs[:, :, :, gi, :]
        mean = (cg[:, :, hi + 1] - cg[:, :, lo]) / cnt[:, None]
        outs.append(mean - pf[:, :, :, gi, :])
    y = jnp.stack(outs, axis=3)
    y = jnp.einsum('bnsgc,gcd->bnsgd', y, pool_w)
    return (y.reshape(b, t, POOL_WIDTH) * pool_scale).astype(p.dtype)


def chunk_spatial_gating(u, v, v_norm_g, w_s, b_s):
    b, t, _ = u.shape
    u = jax.nn.gelu(u)
    v = rms_norm(jax.nn.gelu(v), v_norm_g)
    vc = v.reshape(b, t // SG_CHUNK, SG_CHUNK, SG_GROUPS, SG_GROUP_DIM)
    mixed = jnp.einsum('gij,bnjgc->bnigc', w_s, vc) + b_s.T[None, None, :, :, None]
    return u * mixed.reshape(b, t, SG_WIDTH).astype(u.dtype)


def token_mixer(z, rope_pos, seg_len, sf0, sb0, lg_f, lg_b, ret_norm_g, pool_w, pool_scale,
                sg_norm_g, sg_w, sg_b, w_br, w_bp, w_bs, w_out):
    zq, zk, zv, zg, zp, zu, zsv, zgate = jnp.split(z, SPLIT_IDX, axis=-1)
    q = to_heads(zq)
    k = to_heads(zk) * K_SCALE
    v = to_heads(zv)
    if rope_pos is not None:
        q = axial_rope(q, *rope_pos)
        k = axial_rope(k, *rope_pos)
    o, sf, sb = bidir_retention(q, k, v, lg_f, lg_b, sf0, sb0)
    ret = retention_out(o, zg, ret_norm_g)
    pool = multiscale_pool(zp, seg_len, pool_w, pool_scale)
    sg = chunk_spatial_gating(zu, zsv, sg_norm_g, sg_w, sg_b)
    g_r, g_p, g_s = jnp.split(jax.nn.sigmoid(zgate), N_BRANCH, axis=-1)
    y = g_r * (ret @ w_br) + g_p * (pool @ w_bp) + g_s * (sg @ w_bs)
    return y @ w_out, sf, sb


def sq_relu_mlp(h, w1, w2):
    a = jax.nn.relu(h @ w1)
    return (a * a) @ w2


def _fwd_setup_inputs(seed: int = 0) -> dict:
    key = jax.random.key(seed)
    ks = jax.random.split(key, 24)

    def nrm(k, shape, scale):
        return jax.random.normal(k, shape, F32) * scale

    gamma0 = 1.0 - 2.0 ** (-5.0 - np.arange(RET_HEADS))
    logit0 = jnp.asarray(np.log(gamma0 / (1.0 - gamma0)).astype(np.float32))
    return {
        "x": nrm(ks[0], (BATCH, SEQ, D_MODEL), 1.0),
        "c": nrm(ks[1], (BATCH, D_MODEL), 1.0),
        "ctx": nrm(ks[2], (BATCH, CTX_LEN, D_MODEL), 1.0),
        "c_ctx": nrm(ks[3], (D_MODEL,), 1.0),
        "w_ada": nrm(ks[4], (DEPTH, D_MODEL, N_MOD * D_MODEL), D_MODEL ** -0.5),
        "b_ada": nrm(ks[5], (DEPTH, N_MOD * D_MODEL), 0.02),
        "norm1_g": 1.0 + nrm(ks[6], (DEPTH, D_MODEL), 0.02),
        "w_in": nrm(ks[7], (DEPTH, D_MODEL, N_IN), D_MODEL ** -0.5),
        "ret_decay_logit": logit0[None, None, :] + nrm(ks[8], (DEPTH, 2, RET_HEADS), 0.1),
        "ret_norm_g": 1.0 + nrm(ks[9], (DEPTH, RET_WIDTH), 0.02),
        "pool_w": nrm(ks[10], (DEPTH, POOL_GROUPS, POOL_GROUP_DIM, POOL_GROUP_DIM), POOL_GROUP_DIM ** -0.5),
        "pool_scale": 1.0 + nrm(ks[11], (DEPTH, POOL_WIDTH), 0.1),
        "sg_norm_g": 1.0 + nrm(ks[12], (DEPTH, SG_WIDTH), 0.02),
        "sg_w": nrm(ks[13], (DEPTH, SG_GROUPS, SG_CHUNK, SG_CHUNK), SG_CHUNK ** -0.5),
        "sg_b": 1.0 + nrm(ks[14], (DEPTH, SG_GROUPS, SG_CHUNK), 0.02),
        "w_br": nrm(ks[15], (DEPTH, RET_WIDTH, D_MODEL), RET_WIDTH ** -0.5),
        "w_bp": nrm(ks[16], (DEPTH, POOL_WIDTH, D_MODEL), POOL_WIDTH ** -0.5),
        "w_bs": nrm(ks[17], (DEPTH, SG_WIDTH, D_MODEL), SG_WIDTH ** -0.5),
        "w_out": nrm(ks[18], (DEPTH, D_MODEL, D_MODEL), D_MODEL ** -0.5),
        "norm2_g": 1.0 + nrm(ks[19], (DEPTH, D_MODEL), 0.02),
        "w1": nrm(ks[20], (DEPTH, D_MODEL, D_FF), D_MODEL ** -0.5),
        "w2": nrm(ks[21], (DEPTH, D_FF, D_MODEL), D_FF ** -0.5),
        "final_norm_g": 1.0 + nrm(ks[22], (D_MODEL,), 0.02),
    }


def _fwd_reference(x, c, ctx, c_ctx, w_ada, b_ada, norm1_g, w_in, ret_decay_logit, ret_norm_g, pool_w,
              pool_scale, sg_norm_g, sg_w, sg_b, w_br, w_bp, w_bs, w_out, norm2_g, w1, w2, final_norm_g):
    t = x.shape[1]
    rows_n = t // GRID_W
    tok = jnp.arange(rows_n * GRID_W)
    rope_pos = ((tok // GRID_W).astype(F32), (tok % GRID_W).astype(F32))
    silu_c = jax.nn.silu(c)
    silu_cc = jax.nn.silu(c_ctx)[None]
    xc = ctx
    for l in range(DEPTH):
        last = l == DEPTH - 1
        mod_x = jnp.split((silu_c @ w_ada[l] + b_ada[l])[:, None, :], N_MOD, axis=-1)
        mod_c = jnp.split((silu_cc @ w_ada[l] + b_ada[l])[:, None, :], N_MOD, axis=-1)
        lg = jax.nn.log_sigmoid(ret_decay_logit[l].astype(F32))
        mixer = functools.partial(
            token_mixer, lg_f=lg[0], lg_b=lg[1], ret_norm_g=ret_norm_g[l], pool_w=pool_w[l],
            pool_scale=pool_scale[l], sg_norm_g=sg_norm_g[l], sg_w=sg_w[l], sg_b=sg_b[l],
            w_br=w_br[l], w_bp=w_bp[l], w_bs=w_bs[l], w_out=w_out[l])

        hc = modulate(rms_norm(xc, norm1_g[l]), mod_c[0], mod_c[1])
        if last:
            zk, zv = jnp.split(hc @ w_in[l][:, RET_WIDTH:3 * RET_WIDTH], 2, axis=-1)
            kh = to_heads(zk) * K_SCALE
            vh = to_heads(zv)
            sf = retention_final_state(kh, vh, lg[0])
            sb = retention_final_state(jnp.flip(kh, 2), jnp.flip(vh, 2), lg[1])
        else:
            zero = jnp.zeros((xc.shape[0], RET_HEADS, RET_DK, RET_DV), F32)
            out_c, sf, sb = mixer(hc @ w_in[l], None, CTX_LEN, zero, zero)
            xc = xc + mod_c[2] * out_c
            hc2 = modulate(rms_norm(xc, norm2_g[l]), mod_c[3], mod_c[4])
            xc = xc + mod_c[5] * sq_relu_mlp(hc2, w1[l], w2[l])

        hx = modulate(rms_norm(x, norm1_g[l]), mod_x[0], mod_x[1])
        out_x, _, _ = mixer(hx @ w_in[l], rope_pos, GRID_W, sf, sb)
        x = x + mod_x[2] * out_x
        hx2 = modulate(rms_norm(x, norm2_g[l]), mod_x[3], mod_x[4])
        x = x + mod_x[5] * sq_relu_mlp(hx2, w1[l], w2[l])
    return rms_norm(x, final_norm_g)


import jax as _jax
import jax.numpy as _jnp

TWIN_FORMAT = 'train_step'
FWD_PARAMS = ['x', 'c', 'ctx', 'c_ctx', 'w_ada', 'b_ada', 'norm1_g', 'w_in', 'ret_decay_logit', 'ret_norm_g', 'pool_w', 'pool_scale', 'sg_norm_g', 'sg_w', 'sg_b', 'w_br', 'w_bp', 'w_bs', 'w_out', 'norm2_g', 'w1', 'w2', 'final_norm_g']
TWIN_WEIGHTS = ['c_ctx', 'w_ada', 'b_ada', 'norm1_g', 'w_in', 'ret_decay_logit', 'ret_norm_g', 'pool_w', 'pool_scale', 'sg_norm_g', 'sg_w', 'sg_b', 'w_br', 'w_bp', 'w_bs', 'w_out', 'norm2_g', 'w1', 'w2', 'final_norm_g']
TWIN_DIFF_INPUT = 'x'
TWIN_INPUTS = ['x', 'c', 'ctx', 'c_ctx', 'w_ada', 'b_ada', 'norm1_g', 'w_in', 'ret_decay_logit', 'ret_norm_g', 'pool_w', 'pool_scale', 'sg_norm_g', 'sg_w', 'sg_b', 'w_br', 'w_bp', 'w_bs', 'w_out', 'norm2_g', 'w1', 'w2', 'final_norm_g', 'loss_target', 'm_c_ctx', 'm_w_ada', 'm_b_ada', 'm_norm1_g', 'm_w_in', 'm_ret_decay_logit', 'm_ret_norm_g', 'm_pool_w', 'm_pool_scale', 'm_sg_norm_g', 'm_sg_w', 'm_sg_b', 'm_w_br', 'm_w_bp', 'm_w_bs', 'm_w_out', 'm_norm2_g', 'm_w1', 'm_w2', 'm_final_norm_g', 'v_c_ctx', 'v_w_ada', 'v_b_ada', 'v_norm1_g', 'v_w_in', 'v_ret_decay_logit', 'v_ret_norm_g', 'v_pool_w', 'v_pool_scale', 'v_sg_norm_g', 'v_sg_w', 'v_sg_b', 'v_w_br', 'v_w_bp', 'v_w_bs', 'v_w_out', 'v_norm2_g', 'v_w1', 'v_w2', 'v_final_norm_g']
TWIN_OUTPUTS = ['loss', 'grad_x', 'grad_c_ctx', 'grad_w_ada', 'grad_b_ada', 'grad_norm1_g', 'grad_w_in', 'grad_ret_decay_logit', 'grad_ret_norm_g', 'grad_pool_w', 'grad_pool_scale', 'grad_sg_norm_g', 'grad_sg_w', 'grad_sg_b', 'grad_w_br', 'grad_w_bp', 'grad_w_bs', 'grad_w_out', 'grad_norm2_g', 'grad_w1', 'grad_w2', 'grad_final_norm_g', 'delta_c_ctx', 'delta_w_ada', 'delta_b_ada', 'delta_norm1_g', 'delta_w_in', 'delta_ret_decay_logit', 'delta_ret_norm_g', 'delta_pool_w', 'delta_pool_scale', 'delta_sg_norm_g', 'delta_sg_w', 'delta_sg_b', 'delta_w_br', 'delta_w_bp', 'delta_w_bs', 'delta_w_out', 'delta_norm2_g', 'delta_w1', 'delta_w2', 'delta_final_norm_g', 'new_m_c_ctx', 'new_m_w_ada', 'new_m_b_ada', 'new_m_norm1_g', 'new_m_w_in', 'new_m_ret_decay_logit', 'new_m_ret_norm_g', 'new_m_pool_w', 'new_m_pool_scale', 'new_m_sg_norm_g', 'new_m_sg_w', 'new_m_sg_b', 'new_m_w_br', 'new_m_w_bp', 'new_m_w_bs', 'new_m_w_out', 'new_m_norm2_g', 'new_m_w1', 'new_m_w2', 'new_m_final_norm_g', 'new_v_c_ctx', 'new_v_w_ada', 'new_v_b_ada', 'new_v_norm1_g', 'new_v_w_in', 'new_v_ret_decay_logit', 'new_v_ret_norm_g', 'new_v_pool_w', 'new_v_pool_scale', 'new_v_sg_norm_g', 'new_v_sg_w', 'new_v_sg_b', 'new_v_w_br', 'new_v_w_bp', 'new_v_w_bs', 'new_v_w_out', 'new_v_norm2_g', 'new_v_w1', 'new_v_w2', 'new_v_final_norm_g']
TWIN_LEAF_KINDS = {'loss': 'loss', 'grad_x': 'grad_x', 'grad_c_ctx': 'grad_w', 'grad_w_ada': 'grad_w', 'grad_b_ada': 'grad_w', 'grad_norm1_g': 'grad_w', 'grad_w_in': 'grad_w', 'grad_ret_decay_logit': 'grad_w', 'grad_ret_norm_g': 'grad_w', 'grad_pool_w': 'grad_w', 'grad_pool_scale': 'grad_w', 'grad_sg_norm_g': 'grad_w', 'grad_sg_w': 'grad_w', 'grad_sg_b': 'grad_w', 'grad_w_br': 'grad_w', 'grad_w_bp': 'grad_w', 'grad_w_bs': 'grad_w', 'grad_w_out': 'grad_w', 'grad_norm2_g': 'grad_w', 'grad_w1': 'grad_w', 'grad_w2': 'grad_w', 'grad_final_norm_g': 'grad_w', 'delta_c_ctx': 'delta_w', 'delta_w_ada': 'delta_w', 'delta_b_ada': 'delta_w', 'delta_norm1_g': 'delta_w', 'delta_w_in': 'delta_w', 'delta_ret_decay_logit': 'delta_w', 'delta_ret_norm_g': 'delta_w', 'delta_pool_w': 'delta_w', 'delta_pool_scale': 'delta_w', 'delta_sg_norm_g': 'delta_w', 'delta_sg_w': 'delta_w', 'delta_sg_b': 'delta_w', 'delta_w_br': 'delta_w', 'delta_w_bp': 'delta_w', 'delta_w_bs': 'delta_w', 'delta_w_out': 'delta_w', 'delta_norm2_g': 'delta_w', 'delta_w1': 'delta_w', 'delta_w2': 'delta_w', 'delta_final_norm_g': 'delta_w', 'new_m_c_ctx': 'new_m', 'new_m_w_ada': 'new_m', 'new_m_b_ada': 'new_m', 'new_m_norm1_g': 'new_m', 'new_m_w_in': 'new_m', 'new_m_ret_decay_logit': 'new_m', 'new_m_ret_norm_g': 'new_m', 'new_m_pool_w': 'new_m', 'new_m_pool_scale': 'new_m', 'new_m_sg_norm_g': 'new_m', 'new_m_sg_w': 'new_m', 'new_m_sg_b': 'new_m', 'new_m_w_br': 'new_m', 'new_m_w_bp': 'new_m', 'new_m_w_bs': 'new_m', 'new_m_w_out': 'new_m', 'new_m_norm2_g': 'new_m', 'new_m_w1': 'new_m', 'new_m_w2': 'new_m', 'new_m_final_norm_g': 'new_m', 'new_v_c_ctx': 'new_v', 'new_v_w_ada': 'new_v', 'new_v_b_ada': 'new_v', 'new_v_norm1_g': 'new_v', 'new_v_w_in': 'new_v', 'new_v_ret_decay_logit': 'new_v', 'new_v_ret_norm_g': 'new_v', 'new_v_pool_w': 'new_v', 'new_v_pool_scale': 'new_v', 'new_v_sg_norm_g': 'new_v', 'new_v_sg_w': 'new_v', 'new_v_sg_b': 'new_v', 'new_v_w_br': 'new_v', 'new_v_w_bp': 'new_v', 'new_v_w_bs': 'new_v', 'new_v_w_out': 'new_v', 'new_v_norm2_g': 'new_v', 'new_v_w1': 'new_v', 'new_v_w2': 'new_v', 'new_v_final_norm_g': 'new_v'}


def _forward(args):
    return _fwd_reference(*[args[k] for k in FWD_PARAMS])


def _output_shape():
    def fwd():
        inp = _fwd_setup_inputs(0)
        return _fwd_reference(*[inp[k] for k in FWD_PARAMS])
    out = _jax.eval_shape(fwd)
    return out.shape, out.dtype

N_MICROBATCH = 1
ADAM_LR = 0.001
ADAM_B1 = 0.9
ADAM_B2 = 0.999
ADAM_EPS = 1e-08
ADAM_WD = 0.01
ADAM_STEP = 10
PER_EXAMPLE_BATCH_AXIS = {'x': 0, 'c': 0, 'ctx': 0, 'loss_target': 0}
SHARED_INPUTS = []
_WEIGHT_DTYPES = {'c_ctx': _jnp.float32, 'w_ada': _jnp.float32, 'b_ada': _jnp.float32, 'norm1_g': _jnp.float32, 'w_in': _jnp.float32, 'ret_decay_logit': _jnp.float32, 'ret_norm_g': _jnp.float32, 'pool_w': _jnp.float32, 'pool_scale': _jnp.float32, 'sg_norm_g': _jnp.float32, 'sg_w': _jnp.float32, 'sg_b': _jnp.float32, 'w_br': _jnp.float32, 'w_bp': _jnp.float32, 'w_bs': _jnp.float32, 'w_out': _jnp.float32, 'norm2_g': _jnp.float32, 'w1': _jnp.float32, 'w2': _jnp.float32, 'final_norm_g': _jnp.float32}
MOMENT_SCALE = {'c_ctx': 5.720865e-02, 'w_ada': 5.984482e-02, 'b_ada': 1.039525e-01, 'norm1_g': 7.279631e-02, 'w_in': 3.761979e-02, 'ret_decay_logit': 2.151610e-01, 'ret_norm_g': 3.532992e-02, 'pool_w': 5.422620e-02, 'pool_scale': 5.326133e-02, 'sg_norm_g': 4.852924e-02, 'sg_w': 5.004498e-02, 'sg_b': 5.105866e-02, 'w_br': 2.313846e-02, 'w_bp': 2.731644e-02, 'w_bs': 3.692805e-02, 'w_out': 5.145688e-02, 'norm2_g': 8.257612e-02, 'w1': 4.861228e-02, 'w2': 9.346916e-02, 'final_norm_g': 3.309151e+01}


def _to_microbatches(a, axis):
    t = _jnp.moveaxis(a, axis, 0)
    t = t.reshape((N_MICROBATCH, t.shape[0] // N_MICROBATCH) + t.shape[1:])
    return _jnp.moveaxis(t, 1, axis + 1)


def setup_inputs(seed: int = 0) -> dict:
    inp = _fwd_setup_inputs(seed)
    key = _jax.random.fold_in(_jax.random.key(seed), 7919)
    shape, _ = _output_shape()
    out = dict(inp)
    out["loss_target"] = _jax.random.normal(_jax.random.fold_in(key, 0), shape, _jnp.float32)
    for i, name in enumerate(TWIN_WEIGHTS):
        w = inp[name].astype(_jnp.float32)
        if MOMENT_SCALE is None:
            s = _jnp.sqrt(_jnp.mean(_jnp.square(w)) + 1e-30)
        else:
            s = MOMENT_SCALE[name]
        km, kv = _jax.random.split(_jax.random.fold_in(key, i + 1))
        out[name] = w
        out["m_" + name] = s * _jax.random.normal(km, w.shape, _jnp.float32)
        out["v_" + name] = (s * s) * _jax.random.uniform(kv, w.shape, _jnp.float32, 0.5, 1.5)
    if N_MICROBATCH > 1:
        for name, axis in PER_EXAMPLE_BATCH_AXIS.items():
            out[name] = _to_microbatches(out[name], axis)
    return {'x': out['x'], 'c': out['c'], 'ctx': out['ctx'], 'c_ctx': out['c_ctx'], 'w_ada': out['w_ada'], 'b_ada': out['b_ada'], 'norm1_g': out['norm1_g'], 'w_in': out['w_in'], 'ret_decay_logit': out['ret_decay_logit'], 'ret_norm_g': out['ret_norm_g'], 'pool_w': out['pool_w'], 'pool_scale': out['pool_scale'], 'sg_norm_g': out['sg_norm_g'], 'sg_w': out['sg_w'], 'sg_b': out['sg_b'], 'w_br': out['w_br'], 'w_bp': out['w_bp'], 'w_bs': out['w_bs'], 'w_out': out['w_out'], 'norm2_g': out['norm2_g'], 'w1': out['w1'], 'w2': out['w2'], 'final_norm_g': out['final_norm_g'], 'loss_target': out['loss_target'], 'm_c_ctx': out['m_c_ctx'], 'm_w_ada': out['m_w_ada'], 'm_b_ada': out['m_b_ada'], 'm_norm1_g': out['m_norm1_g'], 'm_w_in': out['m_w_in'], 'm_ret_decay_logit': out['m_ret_decay_logit'], 'm_ret_norm_g': out['m_ret_norm_g'], 'm_pool_w': out['m_pool_w'], 'm_pool_scale': out['m_pool_scale'], 'm_sg_norm_g': out['m_sg_norm_g'], 'm_sg_w': out['m_sg_w'], 'm_sg_b': out['m_sg_b'], 'm_w_br': out['m_w_br'], 'm_w_bp': out['m_w_bp'], 'm_w_bs': out['m_w_bs'], 'm_w_out': out['m_w_out'], 'm_norm2_g': out['m_norm2_g'], 'm_w1': out['m_w1'], 'm_w2': out['m_w2'], 'm_final_norm_g': out['m_final_norm_g'], 'v_c_ctx': out['v_c_ctx'], 'v_w_ada': out['v_w_ada'], 'v_b_ada': out['v_b_ada'], 'v_norm1_g': out['v_norm1_g'], 'v_w_in': out['v_w_in'], 'v_ret_decay_logit': out['v_ret_decay_logit'], 'v_ret_norm_g': out['v_ret_norm_g'], 'v_pool_w': out['v_pool_w'], 'v_pool_scale': out['v_pool_scale'], 'v_sg_norm_g': out['v_sg_norm_g'], 'v_sg_w': out['v_sg_w'], 'v_sg_b': out['v_sg_b'], 'v_w_br': out['v_w_br'], 'v_w_bp': out['v_w_bp'], 'v_w_bs': out['v_w_bs'], 'v_w_out': out['v_w_out'], 'v_norm2_g': out['v_norm2_g'], 'v_w1': out['v_w1'], 'v_w2': out['v_w2'], 'v_final_norm_g': out['v_final_norm_g']}


def _loss(weights, diff, rest, loss_target):
    with _jax.named_scope("forward"):
        args = {**rest, TWIN_DIFF_INPUT: diff, **{k: w.astype(_WEIGHT_DTYPES[k]) for k, w in weights.items()}}
        y = _forward(args)
    with _jax.named_scope("loss_head"):
        err = _jnp.square(y.astype(_jnp.float32) - loss_target)
        return 0.5 * _jnp.sum(_jnp.mean(err, axis=-1)) if err.ndim else 0.5 * err


def _adamw(w, g, m, v):
    m = ADAM_B1 * m + (1.0 - ADAM_B1) * g
    v = ADAM_B2 * v + (1.0 - ADAM_B2) * _jnp.square(g)
    m_hat = m / (1.0 - ADAM_B1 ** ADAM_STEP)
    v_hat = v / (1.0 - ADAM_B2 ** ADAM_STEP)
    delta = -ADAM_LR * (m_hat / (_jnp.sqrt(v_hat) + ADAM_EPS) + ADAM_WD * w)
    return delta, m, v


def reference(x, c, ctx, c_ctx, w_ada, b_ada, norm1_g, w_in, ret_decay_logit, ret_norm_g, pool_w, pool_scale, sg_norm_g, sg_w, sg_b, w_br, w_bp, w_bs, w_out, norm2_g, w1, w2, final_norm_g, loss_target, m_c_ctx, m_w_ada, m_b_ada, m_norm1_g, m_w_in, m_ret_decay_logit, m_ret_norm_g, m_pool_w, m_pool_scale, m_sg_norm_g, m_sg_w, m_sg_b, m_w_br, m_w_bp, m_w_bs, m_w_out, m_norm2_g, m_w1, m_w2, m_final_norm_g, v_c_ctx, v_w_ada, v_b_ada, v_norm1_g, v_w_in, v_ret_decay_logit, v_ret_norm_g, v_pool_w, v_pool_scale, v_sg_norm_g, v_sg_w, v_sg_b, v_w_br, v_w_bp, v_w_bs, v_w_out, v_norm2_g, v_w1, v_w2, v_final_norm_g):
    given = dict(x=x, c=c, ctx=ctx, c_ctx=c_ctx, w_ada=w_ada, b_ada=b_ada, norm1_g=norm1_g, w_in=w_in, ret_decay_logit=ret_decay_logit, ret_norm_g=ret_norm_g, pool_w=pool_w, pool_scale=pool_scale, sg_norm_g=sg_norm_g, sg_w=sg_w, sg_b=sg_b, w_br=w_br, w_bp=w_bp, w_bs=w_bs, w_out=w_out, norm2_g=norm2_g, w1=w1, w2=w2, final_norm_g=final_norm_g, loss_target=loss_target, m_c_ctx=m_c_ctx, m_w_ada=m_w_ada, m_b_ada=m_b_ada, m_norm1_g=m_norm1_g, m_w_in=m_w_in, m_ret_decay_logit=m_ret_decay_logit, m_ret_norm_g=m_ret_norm_g, m_pool_w=m_pool_w, m_pool_scale=m_pool_scale, m_sg_norm_g=m_sg_norm_g, m_sg_w=m_sg_w, m_sg_b=m_sg_b, m_w_br=m_w_br, m_w_bp=m_w_bp, m_w_bs=m_w_bs, m_w_out=m_w_out, m_norm2_g=m_norm2_g, m_w1=m_w1, m_w2=m_w2, m_final_norm_g=m_final_norm_g, v_c_ctx=v_c_ctx, v_w_ada=v_w_ada, v_b_ada=v_b_ada, v_norm1_g=v_norm1_g, v_w_in=v_w_in, v_ret_decay_logit=v_ret_decay_logit, v_ret_norm_g=v_ret_norm_g, v_pool_w=v_pool_w, v_pool_scale=v_pool_scale, v_sg_norm_g=v_sg_norm_g, v_sg_w=v_sg_w, v_sg_b=v_sg_b, v_w_br=v_w_br, v_w_bp=v_w_bp, v_w_bs=v_w_bs, v_w_out=v_w_out, v_norm2_g=v_norm2_g, v_w1=v_w1, v_w2=v_w2, v_final_norm_g=v_final_norm_g)
    weights = {n: given[n] for n in TWIN_WEIGHTS}
    shared = {n: given[n] for n in SHARED_INPUTS}
    per_example = {n: given[n] for n in ['x', 'c', 'ctx']}
    grad_fn = _jax.value_and_grad(_loss, argnums=(0, 1))

    def one_microbatch(ex, loss_target):
        ex = dict(ex)
        diff = ex.pop(TWIN_DIFF_INPUT)
        return grad_fn(weights, diff, {**shared, **ex}, loss_target)

    if N_MICROBATCH == 1:
        loss, (grad_w, grad_x) = one_microbatch(per_example, given["loss_target"])
    else:
        def body(carry, xs):
            loss_sum, grad_sum = carry
            l_k, (gw_k, gx_k) = one_microbatch(xs[0], xs[1])
            with _jax.named_scope("update"):
                return (loss_sum + l_k, _jax.tree.map(_jnp.add, grad_sum, gw_k)), gx_k

        init = (_jnp.zeros((), _jnp.float32), _jax.tree.map(_jnp.zeros_like, weights))
        (loss, grad_w), grad_x = _jax.lax.scan(body, init, (per_example, given["loss_target"]))
    with _jax.named_scope("update"):
        delta_w, new_m, new_v = {}, {}, {}
        for n in TWIN_WEIGHTS:
            delta_w[n], new_m[n], new_v[n] = _adamw(weights[n], grad_w[n], given["m_" + n], given["v_" + n])
    return (loss, grad_x, *[grad_w[n] for n in TWIN_WEIGHTS], *[delta_w[n] for n in TWIN_WEIGHTS],
            *[new_m[n] for n in TWIN_WEIGHTS], *[new_v[n] for n in TWIN_WEIGHTS])
```

```python
import functools

import numpy as np
import jax
import jax.numpy as jnp
from jax import lax
from jax.experimental import pallas as pl
from jax.experimental.pallas import tpu as pltpu

F32 = jnp.float32
BF16 = jnp.bfloat16
EPS = 1e-6
CH = 128
GRID_W = 64
ROPE_THETA = 10000.0
POOL_WINDOWS = (2, 4, 8, 16)
PT = 256
VMEM_LIMIT = 56 * 1024 * 1024
MESH = pl.DeviceIdType.MESH
ANY = pl.BlockSpec(memory_space=pl.ANY)

ADAM_LR = 0.001
ADAM_B1 = 0.9
ADAM_B2 = 0.999
ADAM_EPS = 1e-08
ADAM_WD = 0.01
ADAM_STEP = 10

BIG = ("w_in", "w_br", "w_bp", "w_bs", "w_out", "w1", "w2")
SHARD_AXIS = {"w_in": 1, "w_br": 1, "w_bp": 1, "w_bs": 1, "w_out": 0, "w1": 1, "w2": 0}


def _pick(dim, pref, mult=128):
    best = None
    for t in range(mult, min(dim, pref) + 1, mult):
        if dim % t == 0:
            best = t
    return dim if best is None else best


def _cparams(sem=None):
    return pltpu.CompilerParams(dimension_semantics=sem, vmem_limit_bytes=VMEM_LIMIT)


def _rows(i, tr):
    return i * tr + lax.broadcasted_iota(jnp.int32, (tr, 1), 0)


def _sel(i, tr, n_ctx, v2):
    return jnp.where(_rows(i, tr) < n_ctx, v2[0:1, :], v2[1:2, :])


def _seg_sums(i, tr, n_ctx, d):
    is_ctx = _rows(i, tr) < n_ctx
    s_c = jnp.sum(jnp.where(is_ctx, d, 0.0), axis=0, keepdims=True)
    s_x = jnp.sum(jnp.where(is_ctx, 0.0, d), axis=0, keepdims=True)
    two = lax.broadcasted_iota(jnp.int32, (2, d.shape[1]), 0)
    return jnp.where(two == 0, s_c, s_x)


def _dot(a, b, form="NN"):
    dims = {"NN": (((1,), (0,)), ((), ())), "NT": (((1,), (1,)), ((), ())), "TN": (((0,), (0,)), ((), ()))}[form]
    return lax.dot_general(a, b, dims, preferred_element_type=F32)


def _gelu(x):
    return 0.5 * x * (1.0 + jnp.tanh(0.7978845608028654 * (x + 0.044715 * x * x * x)))


def _matmul(name, form, pairs, *, R, C, tr, tc, tk, nk, out_dtypes, epi, extras=(), a_pro=None):
    npair, nex, nout = len(pairs), len(extras), len(out_dtypes)
    in_specs, args = [], []
    for a, b in pairs:
        if nk == 1:
            ka = a.shape[0] if form == "TN" else a.shape[1]
        else:
            ka = tk
        if form == "NN":
            in_specs += [pl.BlockSpec((tr, ka), lambda i, j, k: (i, k)), pl.BlockSpec((ka, tc), lambda i, j, k: (k, j))]
        elif form == "NT":
            in_specs += [pl.BlockSpec((tr, ka), lambda i, j, k: (i, k)), pl.BlockSpec((tc, ka), lambda i, j, k: (j, k))]
        else:
            in_specs += [pl.BlockSpec((ka, tr), lambda i, j, k: (k, i)), pl.BlockSpec((ka, tc), lambda i, j, k: (k, j))]
        args += [a, b]
    for kind, arr, off in extras:
        if kind == "tile":
            in_specs.append(pl.BlockSpec((tr, tc), lambda i, j, k, off=off: (i, j + off)))
        else:
            in_specs.append(pl.BlockSpec((2, tc), lambda i, j, k, off=off: (0, j + off)))
        args.append(arr)

    def body(*refs):
        ab = refs[:2 * npair]
        ex = refs[2 * npair:2 * npair + nex]
        outs = refs[2 * npair + nex:2 * npair + nex + nout]
        accs = refs[2 * npair + nex + nout:]
        i, k = pl.program_id(0), pl.program_id(2)

        def products():
            res = []
            for p in range(npair):
                a = ab[2 * p][...]
                if a_pro is not None:
                    a = a_pro(a)
                res.append(_dot(a, ab[2 * p + 1][...], form))
            return res

        def finish(vals):
            res = epi(vals, [e[...] for e in ex], i)
            for o, v in zip(outs, res):
                o[...] = v.astype(o.dtype)

        if nk == 1:
            finish(products())
        else:
            prods = products()

            @pl.when(k == 0)
            def _():
                for acc, v in zip(accs, prods):
                    acc[...] = v

            @pl.when(k > 0)
            def _():
                for acc, v in zip(accs, prods):
                    acc[...] += v

            @pl.when(k == nk - 1)
            def _():
                finish([acc[...] for acc in accs])

    return pl.pallas_call(
        body, name=name, grid=(R // tr, C // tc, nk),
        in_specs=in_specs,
        out_specs=[pl.BlockSpec((tr, tc), lambda i, j, k: (i, j)) for _ in out_dtypes],
        out_shape=[jax.ShapeDtypeStruct((R, C), dt) for dt in out_dtypes],
        scratch_shapes=[pltpu.VMEM((tr, tc), F32) for _ in range(npair if nk > 1 else 0)],
        compiler_params=_cparams(("parallel", "parallel", "arbitrary")),
    )(*args)


def _rowcall(name, fn, *, nrows, tr, ins, outs, accs=(), ncol=1):
    n_in, n_out, n_acc = len(ins), len(outs), len(accs)
    in_specs, args = [], []
    for arr, kind, w, off in ins:
        if kind == "rows":
            in_specs.append(pl.BlockSpec((tr, w), lambda i, j, off=off: (i, off + j)))
        elif kind == "full":
            in_specs.append(pl.BlockSpec(arr.shape, lambda i, j, nd=arr.ndim: (0,) * nd))
        else:
            in_specs.append(pl.BlockSpec((None,) + arr.shape[1:], lambda i, j, f=off, nd=arr.ndim: (f(i),) + (0,) * (nd - 1)))
        args.append(arr)
    aliases = {}
    out_specs, out_shape = [], []
    for o_idx, (dt, total, w, off, alias) in enumerate(outs):
        out_specs.append(pl.BlockSpec((tr, w), lambda i, j, off=off: (i, off + j)))
        out_shape.append(jax.ShapeDtypeStruct((nrows, total), dt))
        if alias is not None:
            aliases[len(args)] = o_idx
            in_specs.append(ANY)
            args.append(alias)
    n_alias = len(aliases)
    for shp in accs:
        out_specs.append(pl.BlockSpec(shp, lambda i, j, nd=len(shp): (0,) * nd))
        out_shape.append(jax.ShapeDtypeStruct(shp, F32))

    def body(*refs):
        in_refs = refs[:n_in]
        out_refs = refs[n_in + n_alias:n_in + n_alias + n_out]
        acc_refs = refs[n_in + n_alias + n_out:]
        i, j = pl.program_id(0), pl.program_id(1)
        res = fn(i, j, *[r[...] for r in in_refs])
        for o, v in zip(out_refs, res[:n_out]):
            o[...] = v.astype(o.dtype)
        first = jnp.logical_and(i == 0, j == 0)
        for acc, v in zip(acc_refs, res[n_out:]):
            @pl.when(first)
            def _(acc=acc, v=v):
                acc[...] = v

            @pl.when(jnp.logical_not(first))
            def _(acc=acc, v=v):
                acc[...] += v

    res = pl.pallas_call(
        body, name=name, grid=(nrows // tr, ncol),
        in_specs=in_specs, out_specs=out_specs, out_shape=out_shape,
        input_output_aliases=aliases,
        compiler_params=_cparams(("arbitrary", "arbitrary")),
    )(*args)
    return res


def _f_normmod(x, g, shift, scale):
    r = lax.rsqrt(jnp.mean(x * x, axis=-1, keepdims=True) + EPS)
    return (x * r * g) * (1.0 + scale) + shift


def _f_headnorm_gate(o, zg, ng):
    r = lax.rsqrt(jnp.mean(o * o, axis=-1, keepdims=True) + EPS)
    return (o * r * ng) * (zg * jax.nn.sigmoid(zg))


def _f_sgv(sv, g):
    v = _gelu(sv)
    r = lax.rsqrt(jnp.mean(v * v, axis=-1, keepdims=True) + EPS)
    return v * r * g


def _rope(t, cos, sin):
    w = t.shape[1]
    lane = lax.broadcasted_iota(jnp.int32, t.shape, 1)
    swapped = jnp.where(jnp.bitwise_and(lane, 63) < 32,pltpu.roll(t, w - 32, 1), pltpu.roll(t, 32, 1))
    return t * cos + swapped * sin


def _rope_t(d, cos, sin):
    w = d.shape[1]
    lane = lax.broadcasted_iota(jnp.int32, d.shape, 1)
    ds = d * sin
    swapped = jnp.where(jnp.bitwise_and(lane, 63) < 32,pltpu.roll(ds, w - 32, 1), pltpu.roll(ds, 32, 1))
    return d * cos + swapped


class _Cfg:
    def __init__(self, D, T, CTX, DFF):
        self.D, self.T, self.CTX, self.DFF = D, T, CTX, DFF
        self.TA = T + CTX
        self.RW = D // 2
        self.H = self.RW // CH
        self.PW = D // 4
        self.SW = D // 4
        self.NIN = 4 * self.RW + self.PW + 2 * self.SW + 3 * D
        self.NC = self.TA // CH
        self.NCC = CTX // CH
        self.k_scale = float(CH) ** -0.5
        self.tr = _pick(self.TA, 768)
        self.o_g = 3 * self.RW
        self.o_p = 4 * self.RW
        self.o_u = self.o_p + self.PW
        self.o_sv = self.o_u + self.SW
        self.o_gate = self.o_sv + self.SW


def _normmod_fwd(cfg, X, g, mods, i_shift, i_scale, name):
    D = cfg.D
    tr = _pick(cfg.TA, 384)

    def fn(i, j, x, g, m):
        sh = _sel(i, tr, cfg.CTX, m[:, i_shift * D:(i_shift + 1) * D])
        sc = _sel(i, tr, cfg.CTX, m[:, i_scale * D:(i_scale + 1) * D])
        return (_f_normmod(x, g, sh, sc),)

    return _rowcall(name, fn, nrows=cfg.TA, tr=tr, ins=[(X, "rows", D, 0), (g, "full", 0, 0), (mods, "full", 0, 0)],
                    outs=[(BF16, D, D, 0, None)])[0]


def _normmod_bwd(cfg, X, dH, dres, g, mods, i_shift, i_scale, name):
    D = cfg.D
    tr = _pick(cfg.TA, 384)

    def fn(i, j, x, dh, dr, g, m):
        sh = _sel(i, tr, cfg.CTX, m[:, i_shift * D:(i_shift + 1) * D])
        sc = _sel(i, tr, cfg.CTX, m[:, i_scale * D:(i_scale + 1) * D])
        _, vjp = jax.vjp(_f_normmod, x, g, sh, sc)
        dx, dg, dsh, dsc = vjp(dh)
        return dr + dx, dg, _seg_sums(i, tr, cfg.CTX, dsh), _seg_sums(i, tr, cfg.CTX, dsc)

    return _rowcall(name, fn, nrows=cfg.TA, tr=tr,
                    ins=[(X, "rows", D, 0), (dH, "rows", D, 0), (dres, "rows", D, 0), (g, "full", 0, 0), (mods, "full", 0, 0)],
                    outs=[(F32, D, D, 0, None)], accs=[(1, D), (2, D), (2, D)])


def _resgate_bwd(cfg, dX, M, mods, i_gate, name):
    D = cfg.D
    tr = _pick(cfg.TA, 384)

    def fn(i, j, dx, m, mm):
        gate = _sel(i, tr, cfg.CTX, mm[:, i_gate * D:(i_gate + 1) * D])
        return dx * gate, _seg_sums(i, tr, cfg.CTX, dx * m.astype(F32))

    return _rowcall(name, fn, nrows=cfg.TA, tr=tr, ins=[(dX, "rows", D, 0), (M, "rows", D, 0), (mods, "full", 0, 0)],
                    outs=[(BF16, D, D, 0, None)], accs=[(2, D)])


def _chunk_of(cfg, d, t):
    fwd = t
    bwd = jnp.where(t < cfg.NCC, cfg.NCC - 1 - t, cfg.NC - 1 - t + cfg.NCC)
    return jnp.where(d == 0, fwd, bwd)


def _ret_specs(cfg, cm):
    RW, H = cfg.RW, cfg.H
    return [
        pl.BlockSpec((CH, RW), lambda d, t: (cm(d, t), 0)),
        pl.BlockSpec((CH, RW), lambda d, t: (cm(d, t), 1)),
        pl.BlockSpec((CH, RW), lambda d, t: (cm(d, t), 2)),
        pl.BlockSpec((CH, CH), lambda d, t: (cm(d, t), 0)),
        pl.BlockSpec((CH, CH), lambda d, t: (cm(d, t), 0)),
        pl.BlockSpec((None, H, CH, CH), lambda d, t: (d, 0, 0, 0)),
        pl.BlockSpec((None, CH, RW), lambda d, t: (d, 0, 0)),
        pl.BlockSpec((None, CH, RW), lambda d, t: (d, 0, 0)),
        pl.BlockSpec((None, 1, RW), lambda d, t: (d, 0, 0)),
    ]


def _ret_prep(cfg, q_ref, k_ref, ct_ref, st_ref, qd_ref, kd_ref):
    cos = jnp.tile(ct_ref[...], (1, cfg.H))
    sin = jnp.tile(st_ref[...], (1, cfg.H))
    qr = _rope(q_ref[...].astype(F32), cos, sin)
    kr = _rope(k_ref[...].astype(F32) * cfg.k_scale, cos, sin)
    return qr, kr, (qr * qd_ref[...]).astype(BF16), (kr * kd_ref[...]).astype(BF16)


def _ret_fwd(cfg, Z, tabs, decs):
    RW, H, TA, NC = cfg.RW, cfg.H, cfg.TA, cfg.NC
    cm = functools.partial(_chunk_of, cfg)

    def body(q_ref, k_ref, v_ref, ct_ref, st_ref, dm_ref, qd_ref, kd_ref, cd_ref, o_ref, so_ref, S):
        @pl.when(pl.program_id(1) == 0)
        def _():
            S[...] = jnp.zeros_like(S)

        qr, kr, qd, kd = _ret_prep(cfg, q_ref, k_ref, ct_ref, st_ref, qd_ref, kd_ref)
        qb, kb = qr.astype(BF16), kr.astype(BF16)
        v = v_ref[...].astype(BF16)
        for h in range(H):
            sl = slice(h * CH, (h + 1) * CH)
            p = (_dot(qb[:, sl], kb[:, sl], "NT") * dm_ref[h]).astype(BF16)
            s_h = S[h]
            so_ref[h] = s_h
            o_ref[:, sl] = _dot(p, v[:, sl]) + _dot(qd[:, sl], s_h.astype(BF16))
            S[h] = s_h * cd_ref[:, sl] + _dot(kd[:, sl], v[:, sl], "TN")

    return pl.pallas_call(
        body, name="ret_fwd", grid=(2, NC),
        in_specs=_ret_specs(cfg, cm),
        out_specs=[pl.BlockSpec((None, CH, RW), lambda d, t: (d, cm(d, t), 0)),
                   pl.BlockSpec((None, None, H, CH, CH), lambda d, t: (d, cm(d, t), 0, 0, 0))],
        out_shape=[jax.ShapeDtypeStruct((2, TA, RW), F32), jax.ShapeDtypeStruct((2, NC, H, CH, CH), F32)],
        scratch_shapes=[pltpu.VMEM((H, CH, CH), F32)],
        compiler_params=_cparams(("arbitrary", "arbitrary")),
    )(Z, Z, Z, tabs["cos"], tabs["sin"], decs["dmat"], decs["qdec"], decs["kdec"], decs["cdec"])


def _ret_bwd(cfg, Z, tabs, decs, states, dO):
    RW, H, TA, NC = cfg.RW, cfg.H, cfg.TA, cfg.NC

    def cm(d, t):
        return _chunk_of(cfg, d, NC - 1 - t)

    def body(q_ref, k_ref, v_ref, ct_ref, st_ref, dm_ref, qd_ref, kd_ref, cd_ref, s_ref, do_ref, w_ref,
             dqkv_ref, dl_ref, dS):
        t = pl.program_id(1)

        @pl.when(t == 0)
        def _():
            dS[...] = jnp.zeros_like(dS)
            dl_ref[...] = jnp.zeros_like(dl_ref)

        qr, kr, qd, kd = _ret_prep(cfg, q_ref, k_ref, ct_ref, st_ref, qd_ref, kd_ref)
        qb, kb = qr.astype(BF16), kr.astype(BF16)
        v = v_ref[...].astype(BF16)
        dob = do_ref[...].astype(BF16)
        for h in range(H):
            sl = slice(h * CH, (h + 1) * CH)
            dm = dm_ref[h]
            p = (_dot(qb[:, sl], kb[:, sl], "NT") * dm).astype(BF16)
            dp = (_dot(dob[:, sl], v[:, sl], "NT") * dm).astype(BF16)
            s_h = s_ref[h]
            ds_h = dS[h]
            sb, dsb = s_h.astype(BF16), ds_h.astype(BF16)
            dq_i = _dot(dp, kb[:, sl])
            dk_i = _dot(dp, qb[:, sl], "TN")
            dq_c = _dot(dob[:, sl], sb, "NT") * qd_ref[:, sl]
            dk_s = _dot(v[:, sl], dsb, "NT") * kd_ref[:, sl]
            dqkv_ref[:, sl] = dq_i + dq_c
            dqkv_ref[:, RW + h * CH:RW + (h + 1) * CH] = dk_i + dk_s
            dqkv_ref[:, 2 * RW + h * CH:2 * RW + (h + 1) * CH] = _dot(p, dob[:, sl], "TN") + _dot(kd[:, sl], dsb)
            qh, kh = qr[:, sl], kr[:, sl]
            lam = w_ref[0] * (qh * dq_i) + w_ref[1] * (qh * dq_c) + w_ref[2] * (kh * dk_i) + w_ref[3] * (kh * dk_s)
            lam_s = float(CH) * cd_ref[:, sl] * jnp.sum(ds_h * s_h, axis=0, keepdims=True)
            dl_ref[:, sl] += jnp.sum(lam, axis=0, keepdims=True) + lam_s
            dS[h] = ds_h * cd_ref[:, sl] + _dot(qd[:, sl], dob[:, sl], "TN")

    in_specs = _ret_specs(cfg, cm) + [
        pl.BlockSpec((None, None, H, CH, CH), lambda d, t: (d, cm(d, t), 0, 0, 0)),
        pl.BlockSpec((CH, RW), lambda d, t: (cm(d, t), 0)),
        pl.BlockSpec((None, 4, CH, CH), lambda d, t: (d, 0, 0, 0)),
    ]
    return pl.pallas_call(
        body, name="ret_bwd", grid=(2, NC),
        in_specs=in_specs,
        out_specs=[pl.BlockSpec((None, CH, 3 * RW), lambda d, t: (d, cm(d, t), 0)),
                   pl.BlockSpec((None, 1, RW), lambda d, t: (d, 0, 0))],
        out_shape=[jax.ShapeDtypeStruct((2, TA, 3 * RW), F32), jax.ShapeDtypeStruct((2, 1, RW), F32)],
        scratch_shapes=[pltpu.VMEM((H, CH, CH), F32)],
        compiler_params=_cparams(("arbitrary", "arbitrary")),
    )(Z, Z, Z, tabs["cos"], tabs["sin"], decs["dmat"], decs["qdec"], decs["kdec"], decs["cdec"], states, dO,
      tabs["lamw"])


def _rope_bwd(cfg, dqkv2, tabs, dz):
    RW, H = cfg.RW, cfg.H
    tr = PT

    def body(d0, d1, ct, st, dz_in, o):
        cos, sin = jnp.tile(ct[...], (1, H)), jnp.tile(st[...], (1, H))
        d = d0[...] + d1[...]
        o[:, :RW] = _rope_t(d[:, :RW], cos, sin).astype(o.dtype)
        o[:, RW:2 * RW] = (_rope_t(d[:, RW:2 * RW], cos, sin) * cfg.k_scale).astype(o.dtype)
        o[:, 2 * RW:] = d[:, 2 * RW:].astype(o.dtype)

    return pl.pallas_call(
        body, name="rope_bwd", grid=(cfg.TA // tr,),
        in_specs=[pl.BlockSpec((None, tr, 3 * RW), lambda i: (0, i, 0)), pl.BlockSpec((None, tr, 3 * RW), lambda i: (1, i, 0)),
                  pl.BlockSpec((tr, CH), lambda i: (i, 0)), pl.BlockSpec((tr, CH), lambda i: (i, 0)), ANY],
        out_specs=pl.BlockSpec((tr, 3 * RW), lambda i: (i, 0)),
        out_shape=jax.ShapeDtypeStruct((cfg.TA, cfg.NIN), BF16),
        input_output_aliases={4: 0},
        compiler_params=_cparams(("arbitrary",)),
    )(dqkv2, dqkv2, tabs["cos"], tabs["sin"], dz)


def _retout_fwd(cfg, o2, Z, ng):
    RW, H = cfg.RW, cfg.H
    tr = PT

    def body(o0, o1, zg, ng, out):
        o = o0[...] + o1[...]
        z = zg[...].astype(F32)
        for h in range(H):
            sl = slice(h * CH, (h + 1) * CH)
            out[:, sl] = _f_headnorm_gate(o[:, sl], z[:, sl], ng[:, sl]).astype(out.dtype)

    return pl.pallas_call(
        body, name="retout_fwd", grid=(cfg.TA // tr,),
        in_specs=[pl.BlockSpec((None, tr, RW), lambda i: (0, i, 0)), pl.BlockSpec((None, tr, RW), lambda i: (1, i, 0)),
                  pl.BlockSpec((tr, RW), lambda i: (i, 3)), pl.BlockSpec((1, RW), lambda i: (0, 0))],
        out_specs=pl.BlockSpec((tr, RW), lambda i: (i, 0)),
        out_shape=jax.ShapeDtypeStruct((cfg.TA, RW), BF16),
        compiler_params=_cparams(("arbitrary",)),
    )(o2, o2, Z, ng)


def _retout_bwd(cfg, o2, Z, ng, dret, dz):
    RW, H = cfg.RW, cfg.H
    tr = PT

    def body(o0, o1, zg, ng, dr, dz_in, do_out, dz_out, dng):
        i = pl.program_id(0)
        o = o0[...] + o1[...]
        z = zg[...].astype(F32)
        d = dr[...]
        acc = []
        for h in range(H):
            sl = slice(h * CH, (h + 1) * CH)
            _, vjp = jax.vjp(_f_headnorm_gate, o[:, sl], z[:, sl], ng[:, sl])
            do_h, dz_h, dg_h = vjp(d[:, sl])
            do_out[:, sl] = do_h
            dz_out[:, sl] = dz_h.astype(dz_out.dtype)
            acc.append(dg_h)

        @pl.when(i == 0)
        def _():
            for h in range(H):
                dng[:, h * CH:(h + 1) * CH] = acc[h]

        @pl.when(i > 0)
        def _():
            for h in range(H):
                dng[:, h * CH:(h + 1) * CH] += acc[h]

    return pl.pallas_call(
        body, name="retout_bwd", grid=(cfg.TA // tr,),
        in_specs=[pl.BlockSpec((None, tr, RW), lambda i: (0, i, 0)), pl.BlockSpec((None, tr, RW), lambda i: (1, i, 0)),
                  pl.BlockSpec((tr, RW), lambda i: (i, 3)), pl.BlockSpec((1, RW), lambda i: (0, 0)),
                  pl.BlockSpec((tr, RW), lambda i: (i, 0)), ANY],
        out_specs=[pl.BlockSpec((tr, RW), lambda i: (i, 0)), pl.BlockSpec((tr, RW), lambda i: (i, 3)),
                   pl.BlockSpec((1, RW), lambda i: (0, 0))],
        out_shape=[jax.ShapeDtypeStruct((cfg.TA, RW), F32), jax.ShapeDtypeStruct((cfg.TA, cfg.NIN), BF16),
                   jax.ShapeDtypeStruct((1, RW), F32)],
        input_output_aliases={5: 1},
        compiler_params=_cparams(("arbitrary",)),
    )(o2, o2, Z, ng, dret, dz)


def _pool_consts(ctx_len):
    assert ctx_len == PT
    bm = np.zeros((2, len(POOL_WINDOWS), PT, PT), np.float32)
    ic = np.zeros((2, len(POOL_WINDOWS), PT, CH), np.float32)
    for ty, seg in enumerate((ctx_len, GRID_W)):
        for gi, w in enumerate(POOL_WINDOWS):
            for r in range(PT):
                s0, pos = (r // seg) * seg, r % seg
                lo, hi = max(pos - w // 2, 0), min(pos + w // 2 - 1, seg - 1)
                bm[ty, gi, r, s0 + lo:s0 + hi + 1] = 1.0
                ic[ty, gi, r, :] = 1.0 / (hi - lo + 1)
    return jnp.asarray(bm, BF16), jnp.asarray(ic, F32)


def _pool_tile(p, bm, ic, pw, g):
    sl = slice(g * CH, (g + 1) * CH)
    pg = p[:, sl].astype(F32)
    hi = pg.astype(BF16)
    lo = (pg - hi.astype(F32)).astype(BF16)
    y = (_dot(bm[g], hi) + _dot(bm[g], lo)) * ic[g] - pg
    return y, _dot(y.astype(BF16), pw[g].astype(BF16))


def _pool_fwd(cfg, Z, consts, pool_w, pool_scale):
    PW = cfg.PW
    G = PW // CH
    nct = cfg.CTX // PT
    ty = lambda i: jnp.where(i < nct, 0, 1)

    def fn(i, j, p, bm, ic, pw, ps):
        outs = [_pool_tile(p, bm, ic, pw, g)[1] for g in range(G)]
        return (jnp.concatenate(outs, axis=1) * ps,)

    return _rowcall("pool_fwd", fn, nrows=cfg.TA, tr=PT,
                    ins=[(Z, "rows", PW, cfg.o_p // PW), (consts[0], "sel", 0, ty), (consts[1], "sel", 0, ty),
                         (pool_w, "full", 0, 0), (pool_scale, "full", 0, 0)],
                    outs=[(BF16, PW, PW, 0, None)])[0]


def _pool_bwd(cfg, Z, consts, pool_w, pool_scale, dpool, dz):
    PW = cfg.PW
    G = PW // CH
    nct = cfg.CTX // PT
    ty = lambda i: jnp.where(i < nct, 0, 1)

    def fn(i, j, p, bm, ic, pw, ps, dout):
        dps, dps_acc, dpw = [], [], []
        for g in range(G):
            sl = slice(g * CH, (g + 1) * CH)
            y, lin = _pool_tile(p, bm, ic, pw, g)
            dlin = (dout[:, sl] * ps[:, sl]).astype(BF16)
            dps_acc.append(jnp.sum(dout[:, sl] * lin, axis=0, keepdims=True))
            dy = _dot(dlin, pw[g].astype(BF16), "NT")
            dpw.append(_dot(y.astype(BF16), dlin, "TN"))
            t = dy * ic[g]
            hi = t.astype(BF16)
            lo = (t - hi.astype(F32)).astype(BF16)
            dps.append(_dot(bm[g], hi, "TN") + _dot(bm[g], lo, "TN") - dy)
        return (jnp.concatenate(dps, axis=1), jnp.stack(dpw), jnp.concatenate(dps_acc, axis=1))

    return _rowcall("pool_bwd", fn, nrows=cfg.TA, tr=PT,
                    ins=[(Z, "rows", PW, cfg.o_p // PW), (consts[0], "sel", 0, ty), (consts[1], "sel", 0, ty),
                         (pool_w, "full", 0, 0), (pool_scale, "full", 0, 0), (dpool, "rows", PW, 0)],
                    outs=[(BF16, cfg.NIN, PW, cfg.o_p // PW, dz)], accs=[(G, CH, CH), (1, PW)])


def _sg_mixed(vn, sw, sbb, g, c):
    rows = slice(c * CH, (c + 1) * CH)
    sl = slice(g * CH, (g + 1) * CH)
    return _dot(sw[g].astype(BF16), vn[rows, sl].astype(BF16)) + sbb[g]


def _sg_fwd(cfg, Z, sng, sw, sbb):
    SW = cfg.SW
    G = SW // CH

    def fn(i, j, u, sv, sng, sw, sbb):
        ug = _gelu(u.astype(F32))
        vn = _f_sgv(sv.astype(F32), sng)
        rows = []
        for c in range(PT // CH):
            mixed = jnp.concatenate([_sg_mixed(vn, sw, sbb, g, c) for g in range(G)], axis=1)
            rows.append(ug[c * CH:(c + 1) * CH, :] * mixed)
        return (jnp.concatenate(rows, axis=0),)

    return _rowcall("sg_fwd", fn, nrows=cfg.TA, tr=PT,
                    ins=[(Z, "rows", SW, cfg.o_u // SW), (Z, "rows", SW, cfg.o_sv // SW), (sng, "full", 0, 0),
                         (sw, "full", 0, 0), (sbb, "full", 0, 0)],
                    outs=[(BF16, SW, SW, 0, None)])[0]


def _sg_bwd(cfg, Z, sng, sw, sbb, dsg, dz):
    SW = cfg.SW
    G = SW // CH

    def fn(i, j, u, sv, sng, sw, sbb, dout):
        uf, svf = u.astype(F32), sv.astype(F32)
        ug, vjp_u = jax.vjp(_gelu, uf)
        vn, vjp_v = jax.vjp(_f_sgv, svf, sng)
        dug_rows, dvn_rows = [], []
        dsw = [jnp.zeros((CH, CH), F32) for _ in range(G)]
        dsb = [jnp.zeros((CH, CH), F32) for _ in range(G)]
        for c in range(PT // CH):
            rows = slice(c * CH, (c + 1) * CH)
            dug_g, dvn_g = [], []
            for g in range(G):
                sl = slice(g * CH, (g + 1) * CH)
                mixed = _sg_mixed(vn, sw, sbb, g, c)
                dmixed = dout[rows, sl] * ug[rows, sl]
                dug_g.append(dout[rows, sl] * mixed)
                dmb = dmixed.astype(BF16)
                dvn_g.append(_dot(sw[g].astype(BF16), dmb, "TN"))
                dsw[g] = dsw[g] + _dot(dmb, vn[rows, sl].astype(BF16), "NT")
                dsb[g] = dsb[g] + jnp.broadcast_to(jnp.sum(dmixed, axis=1, keepdims=True), (CH, CH))
            dug_rows.append(jnp.concatenate(dug_g, axis=1))
            dvn_rows.append(jnp.concatenate(dvn_g, axis=1))
        (du,) = vjp_u(jnp.concatenate(dug_rows, axis=0))
        dsv, dsng = vjp_v(jnp.concatenate(dvn_rows, axis=0))
        return du, dsv, jnp.stack(dsw), jnp.stack(dsb), dsng

    return _rowcall("sg_bwd", fn, nrows=cfg.TA, tr=PT,
                    ins=[(Z, "rows", SW, cfg.o_u // SW), (Z, "rows", SW, cfg.o_sv // SW), (sng, "full", 0, 0),
                         (sw, "full", 0, 0), (sbb, "full", 0, 0), (dsg, "rows", SW, 0)],
                    outs=[(BF16, cfg.NIN, SW, cfg.o_u // SW, dz), (BF16, SW, SW, 0, None)],
                    accs=[(G, CH, CH), (G, CH, CH), (1, SW)])


def _gate_bwd(cfg, b, dY, L, Z, dz):
    D = cfg.D
    tc = 512
    nj = D // tc
    off = cfg.o_gate // tc + b * nj

    def fn(i, j, dy, l, z):
        s = jax.nn.sigmoid(z.astype(F32))
        return dy * s, dy * l.astype(F32) * s * (1.0 - s)

    return _rowcall("gate_bwd", fn, nrows=cfg.TA, tr=cfg.tr,
                    ins=[(dY, "rows", tc, 0), (L, "rows", tc, 0), (Z, "rows", tc, off)],
                    outs=[(BF16, D, tc, 0, None), (BF16, cfg.NIN, tc, off, dz)], ncol=nj)


def _copy_cols(cfg, src, dz, col0):
    w = src.shape[1]
    return _rowcall("copy_cols", lambda i, j, s: (s,), nrows=cfg.TA, tr=cfg.tr, ins=[(src, "rows", w, 0)],
                    outs=[(dz.dtype, dz.shape[1], w, col0 // w, dz)])[0]


def _final(cfg, X, g, target):
    D = cfg.D
    tr = PT
    nct = cfg.CTX // tr

    def body(x_ref, g_ref, t_ref, dx_ref, loss_ref, dg_ref):
        i = pl.program_id(0)

        def f(x, g):
            r = lax.rsqrt(jnp.mean(x * x, axis=-1, keepdims=True) + EPS)
            return x * r * g

        y, vjp = jax.vjp(f, x_ref[...], g_ref[...])
        err = y - t_ref[...]
        dx, dg = vjp(err * (1.0 / D))
        part = 0.5 * jnp.sum(jnp.mean(err * err, axis=-1, keepdims=True), axis=0, keepdims=True)

        @pl.when(i == 0)
        def _():
            loss_ref[...] = jnp.zeros_like(loss_ref)
            dg_ref[...] = jnp.zeros_like(dg_ref)

        @pl.when(i < nct)
        def _():
            dx_ref[...] = jnp.zeros_like(dx_ref)

        @pl.when(i >= nct)
        def _():
            dx_ref[...] = dx
            loss_ref[...] += jnp.broadcast_to(part, loss_ref.shape)
            dg_ref[...] += dg

    return pl.pallas_call(
        body, name="final", grid=(cfg.TA // tr,),
        in_specs=[pl.BlockSpec((tr, D), lambda i: (i, 0)), pl.BlockSpec((1, D), lambda i: (0, 0)),
                  pl.BlockSpec((tr, D), lambda i: (jnp.maximum(i - nct, 0), 0))],
        out_specs=[pl.BlockSpec((tr, D), lambda i: (i, 0)), pl.BlockSpec((8, CH), lambda i: (0, 0)),
                   pl.BlockSpec((1, D), lambda i: (0, 0))],
        out_shape=[jax.ShapeDtypeStruct((cfg.TA, D), F32), jax.ShapeDtypeStruct((8, CH), F32),
                   jax.ShapeDtypeStruct((1, D), F32)],
        compiler_params=_cparams(("arbitrary",)),
    )(X, g, target)


def _layer_fwd(cfg, X, mods, W, sm, tabs, decs, consts):
    D, TA, tr, RW, PW, SW, DFF, NIN = cfg.D, cfg.TA, cfg.tr, cfg.RW, cfg.PW, cfg.SW, cfg.DFF, cfg.NIN
    ident = lambda accs, ex, i: (accs[0],)
    H1 = _normmod_fwd(cfg, X, sm["norm1_g"], mods, 0, 1, "normmod1_fwd")
    (Z,) = _matmul("z_mm", "NN", [(H1, W["w_in"])], R=TA, C=NIN, tr=tr, tc=512, tk=D, nk=1, out_dtypes=[F32], epi=ident)
    o2, states = _ret_fwd(cfg, Z, tabs, decs)
    ret = _retout_fwd(cfg, o2, Z, sm["ret_norm_g"])
    pool = _pool_fwd(cfg, Z, consts, sm["pool_w"], sm["pool_scale"])
    sg = _sg_fwd(cfg, Z, sm["sg_norm_g"], sm["sg_w"], sm["sg_bb"])

    tc = 512
    goff = cfg.o_gate // tc

    def epi_branch(accs, ex, i):
        y = sum(jax.nn.sigmoid(z.astype(F32)) * a for a, z in zip(accs, ex))
        return (y, accs[0], accs[1], accs[2])

    Y, Lr, Lp, Ls = _matmul("branch_mm", "NN", [(ret, W["w_br"]), (pool, W["w_bp"]), (sg, W["w_bs"])], R=TA, C=D, tr=tr,
                            tc=tc, tk=0, nk=1, out_dtypes=[BF16] * 4, epi=epi_branch,
                            extras=[("tile", Z, goff + b * (D // tc)) for b in range(3)])

    def epi_res(accs, ex, i):
        return (ex[0] + _sel(i, tr, cfg.CTX, ex[1]) * accs[0], accs[0])

    X2, O = _matmul("out_mm", "NN", [(Y, W["w_out"])], R=TA, C=D, tr=tr, tc=tc, tk=D, nk=1, out_dtypes=[F32, BF16],
                    epi=epi_res, extras=[("tile", X, 0), ("rows2", mods, 2 * (D // tc))])
    H2 = _normmod_fwd(cfg, X2, sm["norm2_g"], mods, 3, 4, "normmod2_fwd")
    tcf = _pick(DFF, 1024)
    def epi_relu(accs, ex, i):
        r = jnp.maximum(accs[0], 0.0)
        return (r * r, r)

    A2, Rr = _matmul("w1_mm", "NN", [(H2, W["w1"])], R=TA, C=DFF, tr=tr, tc=tcf, tk=D, nk=1, out_dtypes=[BF16, BF16],
                     epi=epi_relu)
    tkf = _pick(DFF, 2048)
    X3, M = _matmul("w2_mm", "NN", [(A2, W["w2"])], R=TA, C=D, tr=tr, tc=1024, tk=tkf, nk=DFF // tkf, out_dtypes=[F32, BF16],
                    epi=epi_res, extras=[("tile", X2, 0), ("rows2", mods, 5 * (D // 1024))])
    saved = dict(X=X, H1=H1, Z=Z, o2=o2, states=states, ret=ret, pool=pool, sg=sg, Y=Y, L=(Lr, Lp, Ls), O=O, X2=X2,
                 H2=H2, R=Rr, A2=A2, M=M)
    return X3, saved


def _layer_bwd(cfg, dX3, sv, mods, W, sm, tabs, decs, consts):
    D, TA, tr, RW, PW, SW, DFF, NIN = cfg.D, cfg.TA, cfg.tr, cfg.RW, cfg.PW, cfg.SW, cfg.DFF, cfg.NIN
    ident = lambda accs, ex, i: (accs[0],)
    tkr = _pick(TA, 768)
    nkr = TA // tkr
    g = {}

    dM, dgate2 = _resgate_bwd(cfg, dX3, sv["M"], mods, 5, "resgate2_bwd")
    tcf = _pick(DFF, 1024)
    (dPre,) = _matmul("dpre_mm", "NT", [(dM, W["w2"])], R=TA, C=DFF, tr=tr, tc=tcf, tk=D, nk=1, out_dtypes=[BF16],
                      epi=lambda accs, ex, i: (accs[0] * (2.0 * ex[0].astype(F32)),), extras=[("tile", sv["R"], 0)])
    (g["w2"],) = _matmul("dw2_mm", "TN", [(sv["A2"], dM)], R=DFF, C=D, tr=tcf, tc=1024, tk=tkr, nk=nkr, out_dtypes=[F32],
                         epi=ident)
    tkf = _pick(DFF, 2048)
    (dH2,) = _matmul("dh2_mm", "NT", [(dPre, W["w1"])], R=TA, C=D, tr=tr, tc=1024, tk=tkf, nk=DFF // tkf, out_dtypes=[F32],
                     epi=ident)
    (g["w1"],) = _matmul("dw1_mm", "TN", [(sv["H2"], dPre)], R=D, C=DFF, tr=1024, tc=tcf, tk=tkr, nk=nkr, out_dtypes=[F32],
                         epi=ident)
    dX2, dn2, dsh2, dsc2 = _normmod_bwd(cfg, sv["X2"], dH2, dX3, sm["norm2_g"], mods, 3, 4, "normmod2_bwd")

    dO, dgate1 = _resgate_bwd(cfg, dX2, sv["O"], mods, 2, "resgate1_bwd")
    (dY,) = _matmul("dy_mm", "NT", [(dO, W["w_out"])], R=TA, C=D, tr=tr, tc=1024, tk=D, nk=1, out_dtypes=[F32], epi=ident)
    (g["w_out"],) = _matmul("dwout_mm", "TN", [(sv["Y"], dO)], R=D, C=D, tr=1024, tc=1024, tk=tkr, nk=nkr,
                            out_dtypes=[F32], epi=ident)
    dLr, dz = _gate_bwd(cfg, 0, dY, sv["L"][0], sv["Z"], None)
    dLp, dz = _gate_bwd(cfg, 1, dY, sv["L"][1], sv["Z"], dz)
    dLs, dz = _gate_bwd(cfg, 2, dY, sv["L"][2], sv["Z"], dz)

    (dret,) = _matmul("dret_mm", "NT", [(dLr, W["w_br"])], R=TA, C=RW, tr=tr, tc=RW, tk=D, nk=1, out_dtypes=[F32], epi=ident)
    (dpool,) = _matmul("dpool_mm", "NT", [(dLp, W["w_bp"])], R=TA, C=PW, tr=tr, tc=PW, tk=D, nk=1, out_dtypes=[F32], epi=ident)
    (dsg,) = _matmul("dsg_mm", "NT", [(dLs, W["w_bs"])], R=TA, C=SW, tr=tr, tc=SW, tk=D, nk=1, out_dtypes=[F32], epi=ident)
    (g["w_br"],) = _matmul("dwbr_mm", "TN", [(sv["ret"], dLr)], R=RW, C=D, tr=RW, tc=1024, tk=tkr, nk=nkr, out_dtypes=[F32],
                           epi=ident)
    (g["w_bp"],) = _matmul("dwbp_mm", "TN", [(sv["pool"], dLp)], R=PW, C=D, tr=PW, tc=1024, tk=tkr, nk=nkr, out_dtypes=[F32],
                           epi=ident)
    (g["w_bs"],) = _matmul("dwbs_mm", "TN", [(sv["sg"], dLs)], R=SW, C=D, tr=SW, tc=1024, tk=tkr, nk=nkr, out_dtypes=[F32],
                           epi=ident)
    dOr, dz, dretng = _retout_bwd(cfg, sv["o2"], sv["Z"], sm["ret_norm_g"], dret, dz)
    dqkv2, dlam = _ret_bwd(cfg, sv["Z"], tabs, decs, sv["states"], dOr)
    dz = _rope_bwd(cfg, dqkv2, tabs, dz)
    dz, dpw, dps = _pool_bwd(cfg, sv["Z"], consts, sm["pool_w"], sm["pool_scale"], dpool, dz)
    dz, dz_sv, dsw, dsb, dsng = _sg_bwd(cfg, sv["Z"], sm["sg_norm_g"], sm["sg_w"], sm["sg_bb"], dsg, dz)
    dz = _copy_cols(cfg, dz_sv, dz, cfg.o_sv)

    tkz = _pick(NIN, 2944)
    (dH1,) = _matmul("dh1_mm", "NT", [(dz, W["w_in"])], R=TA, C=D, tr=tr, tc=1024, tk=tkz, nk=NIN // tkz, out_dtypes=[F32],
                     epi=ident)
    (g["w_in"],) = _matmul("dwin_mm", "TN", [(sv["H1"], dz)], R=D, C=NIN, tr=1024, tc=512, tk=tkr, nk=nkr, out_dtypes=[F32],
                           epi=ident)
    dX, dn1, dsh1, dsc1 = _normmod_bwd(cfg, sv["X"], dH1, dX2, sm["norm1_g"], mods, 0, 1, "normmod1_bwd")
    dmods = jnp.concatenate([dsh1, dsc1, dgate1, dsh2, dsc2, dgate2], axis=1)
    small = dict(norm1_g=dn1, norm2_g=dn2, ret_norm_g=dretng, pool_w=dpw, pool_scale=dps, sg_norm_g=dsng, sg_w=dsw,
                 sg_b=dsb[:, :, 0], dlam=dlam)
    return dX, g, small, dmods


def _tables(cfg):
    nf = CH // 4
    inv = ROPE_THETA ** (-jnp.arange(nf, dtype=F32) / nf)
    tok = jnp.arange(cfg.T)
    ar = (tok // GRID_W).astype(F32)[:, None] * inv[None]
    ac = (tok % GRID_W).astype(F32)[:, None] * inv[None]
    cos = jnp.concatenate([jnp.cos(ar), jnp.cos(ar), jnp.cos(ac), jnp.cos(ac)], axis=1)
    sin = jnp.concatenate([-jnp.sin(ar), jnp.sin(ar), -jnp.sin(ac), jnp.sin(ac)], axis=1)
    cos = jnp.concatenate([jnp.ones((cfg.CTX, CH), F32), cos], axis=0)
    sin = jnp.concatenate([jnp.zeros((cfg.CTX, CH), F32), sin], axis=0)
    idx = np.broadcast_to(np.arange(CH, dtype=np.float32)[:, None], (CH, CH))
    lamw = np.stack([np.stack([idx, idx + 1.0, -idx, CH - 1.0 - idx]), np.stack([-idx, CH - idx, idx, idx])])
    return dict(cos=cos, sin=sin, lamw=jnp.asarray(lamw, F32))


def _decays(cfg, logit):
    H, RW = cfg.H, cfg.RW
    lam = jax.nn.log_sigmoid(logit.astype(F32))
    idx = jnp.arange(CH, dtype=F32)
    dist = idx[:, None] - idx[None, :]
    d0 = jnp.where(dist >= 0, jnp.exp(lam[0][:, None, None] * jnp.maximum(dist, 0.0)), 0.0)
    d1 = jnp.where(dist <= 0, jnp.exp(lam[1][:, None, None] * jnp.maximum(-dist, 0.0)), 0.0)
    lanes = lambda a: jnp.repeat(a.T, CH, axis=1)
    qdec = jnp.stack([lanes(jnp.exp(lam[0][:, None] * (idx + 1.0)[None])), lanes(jnp.exp(lam[1][:, None] * (CH - idx)[None]))])
    kdec = jnp.stack([lanes(jnp.exp(lam[0][:, None] * (CH - 1.0 - idx)[None])), lanes(jnp.exp(lam[1][:, None] * idx[None]))])
    cdec = jnp.repeat(jnp.exp(lam * CH), CH, axis=1)[:, None, :]
    return dict(dmat=jnp.stack([d0, d1]), qdec=qdec, kdec=kdec, cdec=cdec)


def _small_of_layer(small_w, l):
    sm = {k: v[l] for k, v in small_w.items()}
    sm["norm1_g"] = sm["norm1_g"][None]
    sm["norm2_g"] = sm["norm2_g"][None]
    sm["ret_norm_g"] = sm["ret_norm_g"][None]
    sm["pool_scale"] = sm["pool_scale"][None]
    sm["sg_norm_g"] = sm["sg_norm_g"][None]
    sm["sg_bb"] = jnp.broadcast_to(sm["sg_b"][:, :, None], sm["sg_b"].shape + (CH,))
    return sm


def _local_fwd_bwd(cfg, X0, target, mods, weights_of, small_w, final_g):
    depth = len(mods)
    tabs = _tables(cfg)
    consts = _pool_consts(cfg.CTX)
    X, saved, Ws, sms, decs = X0, [], [], [], []
    for l in range(depth):
        Ws.append(weights_of(l))
        sms.append(_small_of_layer(small_w, l))
        decs.append(_decays(cfg, small_w["ret_decay_logit"][l]))
        X, sv = _layer_fwd(cfg, X, mods[l], Ws[l], sms[l], tabs, decs[l], consts)
        saved.append(sv)
    dX, loss_acc, dfinal = _final(cfg, X, final_g[None], target)
    big, small, dmods = [None] * depth, [None] * depth, [None] * depth
    for l in reversed(range(depth)):
        dX, big[l], small[l], dmods[l] = _layer_bwd(cfg, dX, saved[l], mods[l], Ws[l], sms[l], tabs, decs[l], consts)
        lam_grad = jnp.sum(small[l].pop("dlam").reshape(2, cfg.H, CH), axis=-1)
        small[l]["ret_decay_logit"] = lam_grad * jax.nn.sigmoid(-small_w["ret_decay_logit"][l].astype(F32))
    return loss_acc[0, 0], dX, big, small, dmods, dfinal


def _me():
    return lax.axis_index("x"), lax.axis_index("y"), lax.axis_index("c")


def _other_chips(x, y):
    return [(1 - x, y), (x, 1 - y), (1 - x, 1 - y)]


def _rcopy(src, dst, send_sem, recv_sem, dev):
    return pltpu.make_async_remote_copy(src_ref=src, dst_ref=dst, send_sem=send_sem, recv_sem=recv_sem,
                                        device_id=dev, device_id_type=MESH)


def _half(ref, axis, c):
    k, n = ref.shape
    if axis == 1:
        return ref.at[pl.ds(c * (k // 2), k // 2), :]
    return ref.at[:, pl.ds(c * (n // 2), n // 2)]


def _chip_part(ref, axis, j):
    k, n = ref.shape
    if axis == 1:
        return ref.at[:, pl.ds(j * (n // 4), n // 4)]
    return ref.at[pl.ds(j * (k // 4), k // 4), :]


def _piece(ref, axis, j, c):
    k, n = ref.shape
    if axis == 1:
        return ref.at[pl.ds(c * (k // 2), k // 2), pl.ds(j * (n // 4), n // 4)]
    return ref.at[pl.ds(j * (k // 4), k // 4), pl.ds(c * (n // 2), n // 2)]


def _gather_weights(shards, axes):
    n = len(shards)
    full_shapes = [(s.shape[0], 4 * s.shape[1]) if a == 1 else (4 * s.shape[0], s.shape[1]) for s, a in zip(shards, axes)]

    def body(*refs):
        ins, outs = refs[:n], refs[n:2 * n]
        send, recv, loc = refs[2 * n:]
        x, y, c = _me()
        j = 2 * x + y
        sib = (x, y, 1 - c)
        chips = _other_chips(x, y)
        mine = [pltpu.make_async_copy(ins[t], _chip_part(outs[t], axes[t], j), loc.at[t]) for t in range(n)]
        for cp in mine:
            cp.start()
        first = []
        for t in range(n):
            for k, chip in enumerate(chips):
                first.append(_rcopy(_half(ins[t], axes[t], c), _piece(outs[t], axes[t], j, c), send.at[6 * t + k],
                                    recv.at[6 * t + k], (*chip, c)))
                first[-1].start()
        passed = []
        for t in range(n):
            for k, chip in enumerate(chips):
                landed = _piece(outs[t], axes[t], 2 * chip[0] + chip[1], c)
                _rcopy(landed, landed, send.at[6 * t + k], recv.at[6 * t + k], sib).wait_recv()
                passed.append(_rcopy(landed, landed, send.at[6 * t + 3 + k], recv.at[6 * t + 3 + k], sib))
                passed[-1].start()
        for t in range(n):
            for k, chip in enumerate(chips):
                theirs = _piece(outs[t], axes[t], 2 * chip[0] + chip[1], 1 - c)
                _rcopy(theirs, theirs, send.at[6 * t + 3 + k], recv.at[6 * t + 3 + k], sib).wait_recv()
        for cp in first + passed:
            cp.wait_send()
        for cp in mine:
            cp.wait()

    return pl.pallas_call(
        body, name="gather_weights",
        in_specs=[ANY] * n, out_specs=[ANY] * n,
        out_shape=[jax.ShapeDtypeStruct(s, BF16) for s in full_shapes],
        scratch_shapes=[pltpu.SemaphoreType.DMA((6 * n,)), pltpu.SemaphoreType.DMA((6 * n,)), pltpu.SemaphoreType.DMA((n,))],
    )(*shards)


def _rs_swap(grads, axes):
    n = len(grads)
    half_shapes = [(g.shape[0] // 2, g.shape[1]) if a == 1 else (g.shape[0], g.shape[1] // 2) for g, a in zip(grads, axes)]

    def body(*refs):
        ins, own, got = refs[:n], refs[n:2 * n], refs[2 * n:3 * n]
        send, recv, loc = refs[3 * n:]
        x, y, c = _me()
        sib = (x, y, 1 - c)
        local = [pltpu.make_async_copy(_half(ins[t], axes[t], c), own[t], loc.at[t]) for t in range(n)]
        out = [_rcopy(_half(ins[t], axes[t], 1 - c), got[t], send.at[t], recv.at[t], sib) for t in range(n)]
        for cp in local + out:
            cp.start()
        for cp in out:
            cp.wait()
        for cp in local:
            cp.wait()

    res = pl.pallas_call(
        body, name="rs_swap",
        in_specs=[ANY] * n, out_specs=[ANY] * (2 * n),
        out_shape=[jax.ShapeDtypeStruct(s, F32) for s in half_shapes] * 2,
        scratch_shapes=[pltpu.SemaphoreType.DMA((n,)), pltpu.SemaphoreType.DMA((n,)), pltpu.SemaphoreType.DMA((n,))],
    )(*grads)
    return res[:n], res[n:]


def _rs_scatter(parts, axes):
    n = len(parts)
    q_shapes = [(p.shape[0], p.shape[1] // 4) if a == 1 else (p.shape[0] // 4, p.shape[1]) for p, a in zip(parts, axes)]

    def body(*refs):
        ins, mine = refs[:n], refs[n:2 * n]
        got = refs[2 * n:5 * n]
        send, recv, loc = refs[5 * n:]
        x, y, c = _me()
        j = 2 * x + y
        chips = _other_chips(x, y)
        local = [pltpu.make_async_copy(_chip_part(ins[t], axes[t], j), mine[t], loc.at[t]) for t in range(n)]
        out = []
        for t in range(n):
            for k, chip in enumerate(chips):
                out.append(_rcopy(_chip_part(ins[t], axes[t], 2 * chip[0] + chip[1]), got[3 * t + k], send.at[3 * t + k],
                                  recv.at[3 * t + k], (*chip, c)))
        for cp in local + out:
            cp.start()
        for cp in out:
            cp.wait()
        for cp in local:
            cp.wait()

    res = pl.pallas_call(
        body, name="rs_scatter",
        in_specs=[ANY] * n, out_specs=[ANY] * (4 * n),
        out_shape=[jax.ShapeDtypeStruct(s, parts[0].dtype) for s in q_shapes] + [jax.ShapeDtypeStruct(s, parts[0].dtype) for s in q_shapes for _ in range(3)],
        scratch_shapes=[pltpu.SemaphoreType.DMA((3 * n,)), pltpu.SemaphoreType.DMA((3 * n,)), pltpu.SemaphoreType.DMA((n,))],
    )(*parts)
    return res[:n], [res[n + 3 * t:n + 3 * t + 3] for t in range(n)]


def _rs_share(halves, axes):
    n = len(halves)
    shard_shapes = [(2 * h.shape[0], h.shape[1]) if a == 1 else (h.shape[0], 2 * h.shape[1]) for h, a in zip(halves, axes)]

    def body(*refs):
        ins, outs = refs[:n], refs[n:2 * n]
        send, recv, loc = refs[2 * n:]
        x, y, c = _me()
        sib = (x, y, 1 - c)
        local = [pltpu.make_async_copy(ins[t], _half(outs[t], axes[t], c), loc.at[t]) for t in range(n)]
        out = [_rcopy(ins[t], _half(outs[t], axes[t], c), send.at[t], recv.at[t], sib) for t in range(n)]
        for cp in local + out:
            cp.start()
        for t in range(n):
            out[t].wait_send()
            theirs = _half(outs[t], axes[t], 1 - c)
            _rcopy(theirs, theirs, send.at[t], recv.at[t], sib).wait_recv()
        for cp in local:
            cp.wait()

    return pl.pallas_call(
        body, name="rs_share",
        in_specs=[ANY] * n, out_specs=[ANY] * n,
        out_shape=[jax.ShapeDtypeStruct(s, F32) for s in shard_shapes],
        scratch_shapes=[pltpu.SemaphoreType.DMA((n,)), pltpu.SemaphoreType.DMA((n,)), pltpu.SemaphoreType.DMA((n,))],
    )(*halves)


def _gather_small(v):
    def body(v_ref, out_ref, send, recv, loc):
        x, y, c = _me()
        sib = (x, y, 1 - c)
        chips = _other_chips(x, y)
        slot = lambda px, py, pc: out_ref.at[4 * px + 2 * py + pc]
        mine = pltpu.make_async_copy(v_ref, slot(x, y, c), loc)
        mine.start()
        first = [_rcopy(v_ref, slot(x, y, c), send.at[0], recv.at[0], sib)]
        first += [_rcopy(v_ref, slot(x, y, c), send.at[1 + k], recv.at[1 + k], (*chip, c)) for k, chip in enumerate(chips)]
        for cp in first:
            cp.start()
        passed = []
        for k, chip in enumerate(chips):
            landed = slot(*chip, c)
            _rcopy(landed, landed, send.at[1 + k], recv.at[1 + k], sib).wait_recv()
            passed.append(_rcopy(landed, landed, send.at[4 + k], recv.at[4 + k], sib))
            passed[-1].start()
        theirs = slot(x, y, 1 - c)
        _rcopy(theirs, theirs, send.at[0], recv.at[0], sib).wait_recv()
        for k, chip in enumerate(chips):
            theirs = slot(*chip, 1 - c)
            _rcopy(theirs, theirs, send.at[4 + k], recv.at[4 + k], sib).wait_recv()
        for cp in first + passed:
            cp.wait_send()
        mine.wait()

    return pl.pallas_call(
        body, name="gather_small",
        in_specs=[ANY], out_specs=ANY,
        out_shape=jax.ShapeDtypeStruct((8,) + v.shape, v.dtype),
        scratch_shapes=[pltpu.SemaphoreType.DMA((7,)), pltpu.SemaphoreType.DMA((7,)), pltpu.SemaphoreType.DMA],
    )(v)


def _cast_layer(w, l):
    _, k, n = w.shape
    tr = _pick(k, 256, 16)

    def body(w_ref, o_ref):
        o_ref[...] = w_ref[...].astype(BF16)

    return pl.pallas_call(
        body, name="cast_bf16", grid=(k // tr,),
        in_specs=[pl.BlockSpec((None, tr, n), lambda i: (l, i, 0))], out_specs=pl.BlockSpec((tr, n), lambda i: (i, 0)),
        out_shape=jax.ShapeDtypeStruct((k, n), BF16), compiler_params=_cparams(("arbitrary",)),
    )(w)


def _sum_arrays(arrs, name):
    k, n = arrs[0].shape
    tr = _pick(k, max(8, (1 << 19) // n), 8)

    def fn(i, j, *vals):
        acc = vals[0].astype(F32)
        for v in vals[1:]:
            acc = acc + v.astype(F32)
        return (acc,)

    return _rowcall(name, fn, nrows=k, tr=tr, ins=[(a, "rows", n, 0) for a in arrs], outs=[(F32, n, n, 0, None)])[0]


def _adam_math(w, g, m, v):
    m = ADAM_B1 * m + (1.0 - ADAM_B1) * g
    v = ADAM_B2 * v + (1.0 - ADAM_B2) * (g * g)
    m_hat = m / (1.0 - ADAM_B1 ** ADAM_STEP)
    v_hat = v / (1.0 - ADAM_B2 ** ADAM_STEP)
    delta = -ADAM_LR * (m_hat / (jnp.sqrt(v_hat) + ADAM_EPS) + ADAM_WD * w)
    return delta, m, v


def _adam_layer(w, m, v, l, g, prev):
    L, k, n = w.shape
    tr = _pick(k, 128, 8)
    blk = pl.BlockSpec((None, tr, n), lambda i: (l, i, 0))

    def body(*refs):
        w_ref, m_ref, v_ref, g_ref = refs[:4]
        go, do, mo, vo = refs[-4:]
        gv = g_ref[...]
        d, m2, v2 = _adam_math(w_ref[...], gv, m_ref[...], v_ref[...])
        go[...] = gv
        do[...] = d
        mo[...] = m2
        vo[...] = v2

    args = [w, m, v, g]
    in_specs = [blk, blk, blk, pl.BlockSpec((tr, n), lambda i: (i, 0))]
    aliases = {}
    if prev is not None:
        for q, p in enumerate(prev):
            aliases[len(args)] = q
            in_specs.append(ANY)
            args.append(p)
    return pl.pallas_call(
        body, name="adam_layer", grid=(k // tr,),
        in_specs=in_specs, out_specs=[blk] * 4, out_shape=[jax.ShapeDtypeStruct((L, k, n), F32)] * 4,
        input_output_aliases=aliases, compiler_params=_cparams(("arbitrary",)),
    )(*args)


def _adam_flat(w, g, m, v):
    r = w.shape[0]
    tr = _pick(r, 512, 8)
    fn = lambda i, j, w, g, m, v: _adam_math(w, g, m, v)
    return _rowcall("adam_flat", fn, nrows=r, tr=tr, ins=[(a, "rows", 128, 0) for a in (w, g, m, v)],
                    outs=[(F32, 128, 128, 0, None)] * 3)


def _sum8(gathered):
    _, r, _ = gathered.shape
    tr = _pick(r, 512, 8)

    def body(g_ref, o_ref):
        acc = g_ref[0]
        for d in range(1, 8):
            acc = acc + g_ref[d]
        o_ref[...] = acc

    return pl.pallas_call(
        body, name="sum8", grid=(r // tr,),
        in_specs=[pl.BlockSpec((8, tr, 128), lambda i: (0, i, 0))], out_specs=pl.BlockSpec((tr, 128), lambda i: (i, 0)),
        out_shape=jax.ShapeDtypeStruct((r, 128), F32), compiler_params=_cparams(("arbitrary",)),
    )(gathered)


def _hdot(a, b, form="NN"):
    return _dot(a.astype(BF16), b.astype(BF16), form)


def _ada_fwd(s16, w_ada, l):
    _, d, ns = w_ada.shape
    tc = _pick(ns, 512)

    def body(s_ref, w_ref, o_ref):
        o_ref[...] = _hdot(s_ref[...], w_ref[...])

    return pl.pallas_call(
        body, name="ada_fwd", grid=(ns // tc,),
        in_specs=[pl.BlockSpec((16, d), lambda j: (0, 0)), pl.BlockSpec((None, d, tc), lambda j: (l, 0, j))],
        out_specs=pl.BlockSpec((16, tc), lambda j: (0, j)),
        out_shape=jax.ShapeDtypeStruct((16, ns), F32), compiler_params=_cparams(("arbitrary",)),
    )(s16, w_ada)


def _ada_bwd(s16t, dm, w_ada, l):
    _, d, ns = w_ada.shape
    tc = _pick(ns, 512)

    def body(st_ref, dm_ref, w_ref, dw_ref, ds_ref):
        j = pl.program_id(0)
        dw_ref[...] = _hdot(st_ref[...], dm_ref[...])
        part = _hdot(dm_ref[...], w_ref[...], "NT")

        @pl.when(j == 0)
        def _():
            ds_ref[...] = part

        @pl.when(j > 0)
        def _():
            ds_ref[...] += part

    return pl.pallas_call(
        body, name="ada_bwd", grid=(ns // tc,),
        in_specs=[pl.BlockSpec((d, 16), lambda j: (0, 0)), pl.BlockSpec((16, tc), lambda j: (0, j)),
                  pl.BlockSpec((None, d, tc), lambda j: (l, 0, j))],
        out_specs=[pl.BlockSpec((d, tc), lambda j: (0, j)), pl.BlockSpec((16, d), lambda j: (0, 0))],
        out_shape=[jax.ShapeDtypeStruct((d, ns), F32), jax.ShapeDtypeStruct((16, d), F32)],
        compiler_params=_cparams(("arbitrary",)),
    )(s16t, dm, w_ada)


def _pack(arrs):
    flat = jnp.concatenate([a.reshape(-1).astype(F32) for a in arrs])
    pad = (-flat.shape[0]) % 1024
    return jnp.pad(flat, (0, pad)).reshape(-1, 128)


def _unpack(flat2d, shapes):
    flat = flat2d.reshape(-1)
    out, pos = [], 0
    for s in shapes:
        size = int(np.prod(s))
        out.append(flat[pos:pos + size].reshape(s))
        pos += size
    return out


SMALL = ("norm1_g", "norm2_g", "ret_decay_logit", "ret_norm_g", "pool_w", "pool_scale", "sg_norm_g", "sg_w", "sg_b")


def kernel(x, c, ctx, c_ctx, w_ada, b_ada, norm1_g, w_in, ret_decay_logit, ret_norm_g, pool_w, pool_scale, sg_norm_g, sg_w, sg_b, w_br, w_bp, w_bs, w_out, norm2_g, w1, w2, final_norm_g, loss_target, m_c_ctx, m_w_ada, m_b_ada, m_norm1_g, m_w_in, m_ret_decay_logit, m_ret_norm_g, m_pool_w, m_pool_scale, m_sg_norm_g, m_sg_w, m_sg_b, m_w_br, m_w_bp, m_w_bs, m_w_out, m_norm2_g, m_w1, m_w2, m_final_norm_g, v_c_ctx, v_w_ada, v_b_ada, v_norm1_g, v_w_in, v_ret_decay_logit, v_ret_norm_g, v_pool_w, v_pool_scale, v_sg_norm_g, v_sg_w, v_sg_b, v_w_br, v_w_bp, v_w_bs, v_w_out, v_norm2_g, v_w1, v_w2, v_final_norm_g):
    P = dict(c_ctx=c_ctx, w_ada=w_ada, b_ada=b_ada, norm1_g=norm1_g, w_in=w_in, ret_decay_logit=ret_decay_logit,
             ret_norm_g=ret_norm_g, pool_w=pool_w, pool_scale=pool_scale, sg_norm_g=sg_norm_g, sg_w=sg_w, sg_b=sg_b, w_br=w_br,
             w_bp=w_bp, w_bs=w_bs, w_out=w_out, norm2_g=norm2_g, w1=w1, w2=w2, final_norm_g=final_norm_g)
    Mo = dict(c_ctx=m_c_ctx, w_ada=m_w_ada, b_ada=m_b_ada, norm1_g=m_norm1_g, w_in=m_w_in, ret_decay_logit=m_ret_decay_logit,
              ret_norm_g=m_ret_norm_g, pool_w=m_pool_w, pool_scale=m_pool_scale, sg_norm_g=m_sg_norm_g, sg_w=m_sg_w, sg_b=m_sg_b,
              w_br=m_w_br, w_bp=m_w_bp, w_bs=m_w_bs, w_out=m_w_out, norm2_g=m_norm2_g, w1=m_w1, w2=m_w2,
              final_norm_g=m_final_norm_g)
    Vo = dict(c_ctx=v_c_ctx, w_ada=v_w_ada, b_ada=v_b_ada, norm1_g=v_norm1_g, w_in=v_w_in, ret_decay_logit=v_ret_decay_logit,
              ret_norm_g=v_ret_norm_g, pool_w=v_pool_w, pool_scale=v_pool_scale, sg_norm_g=v_sg_norm_g, sg_w=v_sg_w, sg_b=v_sg_b,
              w_br=v_w_br, w_bp=v_w_bp, w_bs=v_w_bs, w_out=v_w_out, norm2_g=v_norm2_g, w1=v_w1, w2=v_w2,
              final_norm_g=v_final_norm_g)
    names = ("c_ctx", "w_ada", "b_ada", "norm1_g", "w_in", "ret_decay_logit", "ret_norm_g", "pool_w", "pool_scale", "sg_norm_g",
             "sg_w", "sg_b", "w_br", "w_bp", "w_bs", "w_out", "norm2_g", "w1", "w2", "final_norm_g")
    L, D = w_in.shape[0], x.shape[-1]
    T, CTX = x.shape[1], ctx.shape[1]
    cfg = _Cfg(D, T, CTX, 4 * w1.shape[2])
    ns_ada = w_ada.shape[2]
    mx, my, mc = _me()
    dev = 4 * mx + 2 * my + mc
    chip = 2 * mx + my

    silu_cc = jax.nn.silu(c_ctx)
    silu_all = _gather_small(jax.nn.silu(c).reshape(-1, 128)).reshape(8, D)
    s16 = jnp.concatenate([silu_cc[None], silu_all, jnp.zeros((7, D), F32)], axis=0)
    proj = jnp.stack([_ada_fwd(s16, w_ada, l) for l in range(L)])
    proj_all = _gather_small(proj.reshape(-1, 128)).reshape(8, L, 16, ns_ada)
    mods_full = jnp.concatenate([proj_all[2 * j] for j in range(4)], axis=-1) + b_ada[:, None, :]
    mods = [jnp.concatenate([mods_full[l, 0:1], lax.dynamic_slice_in_dim(mods_full[l], 1 + dev, 1, axis=0)], axis=0)
            for l in range(L)]

    axes = [SHARD_AXIS[k] for k in BIG]

    def weights_of(l):
        full = _gather_weights([_cast_layer(P[k], l) for k in BIG], axes)
        return dict(zip(BIG, full))

    X0 = jnp.concatenate([ctx[0], x[0]], axis=0)
    small_w = {k: P[k] for k in SMALL}
    loss_part, dX, big, small, dmods, dfinal = _local_fwd_bwd(cfg, X0, loss_target[0], mods, weights_of, small_w, final_norm_g)
    loss = lax.psum(loss_part, ("x", "y", "c"))
    grad_x = dX[CTX:][None]

    outs = {k: None for k in BIG}
    for l in range(L):
        own, got = _rs_swap([big[l][k] for k in BIG], axes)
        parts = [_sum_arrays([a, b], "rs_add2") for a, b in zip(own, got)]
        mine, theirs = _rs_scatter(parts, axes)
        halves = [_sum_arrays([m0] + list(th), "rs_add4") for m0, th in zip(mine, theirs)]
        shard_g = _rs_share(halves, axes)
        for k, g in zip(BIG, shard_g):
            outs[k] = _adam_layer(P[k], Mo[k], Vo[k], l, g, outs[k])

    per_layer = [[small[l][k] for k in SMALL] + [dmods[l][1], dmods[l][0]] for l in range(L)]
    payload = _pack([a for lay in per_layer for a in lay] + [dfinal])
    gathered = _gather_small(payload)
    total = _sum8(gathered)
    shapes = [P[k].shape[1:] for k in SMALL] + [(6 * D,), (6 * D,)]
    tot = _unpack(total, shapes * L + [(D,)])
    per = len(shapes)
    g_small = {k: jnp.stack([tot[l * per + q] for l in range(L)]) for q, k in enumerate(SMALL)}
    dmx_sum = jnp.stack([tot[l * per + per - 2] for l in range(L)])
    dmc_sum = jnp.stack([tot[l * per + per - 1] for l in range(L)])
    g_small["b_ada"] = dmx_sum + dmc_sum
    g_small["final_norm_g"] = tot[-1]
    offs = np.cumsum([0] + [int(np.prod(s)) for s in shapes])
    lay_size = int(offs[-1])
    gflat = gathered.reshape(8, -1)
    s16t = s16.T
    ada_out, ds_part = None, jnp.zeros((16, D), F32)
    for l in range(L):
        dmx_all = gflat[:, l * lay_size + int(offs[per - 2]):l * lay_size + int(offs[per - 1])]
        dm_full = jnp.concatenate([dmc_sum[l][None], dmx_all, jnp.zeros((7, 6 * D), F32)], axis=0)
        dm = lax.dynamic_slice_in_dim(dm_full, chip * ns_ada, ns_ada, axis=1)
        dw, ds = _ada_bwd(s16t, dm, w_ada, l)
        ds_part = ds_part + ds
        ada_out = _adam_layer(w_ada, Mo["w_ada"], Vo["w_ada"], l, dw, ada_out)
    outs["w_ada"] = ada_out
    ds_all = _gather_small(ds_part[0].reshape(-1, 128)).reshape(8, D)
    d_silu_cc = ds_all[0] + ds_all[2] + ds_all[4] + ds_all[6]
    g_small["c_ctx"] = jax.vjp(jax.nn.silu, c_ctx)[1](d_silu_cc)[0]

    small_names = [k for k in names if k not in BIG and k != "w_ada"]
    sm_shapes = [P[k].shape for k in small_names]
    res = _adam_flat(_pack([P[k] for k in small_names]), _pack([g_small[k] for k in small_names]),
                     _pack([Mo[k] for k in small_names]), _pack([Vo[k] for k in small_names]))
    d_s, m_s, v_s = [_unpack(r, sm_shapes) for r in res]
    for q, k in enumerate(small_names):
        outs[k] = (g_small[k].reshape(P[k].shape), d_s[q], m_s[q], v_s[q])

    return (loss, grad_x, *[outs[k][0] for k in names], *[outs[k][1] for k in names], *[outs[k][2] for k in names],
            *[outs[k][3] for k in names])
```

```python
import functools

import numpy as np
import jax
import jax.numpy as jnp
from jax import lax
from jax.experimental import pallas as pl
from jax.experimental.pallas import tpu as pltpu

F32 = jnp.float32
BF16 = jnp.bfloat16
EPS = 1e-6
CH = 128
GRID_W = 64
ROPE_THETA = 10000.0
POOL_WINDOWS = (2, 4, 8, 16)
PT = 256
VMEM_LIMIT = 56 * 1024 * 1024
MESH = pl.DeviceIdType.MESH
ANY = pl.BlockSpec(memory_space=pl.ANY)

ADAM_LR = 0.001
ADAM_B1 = 0.9
ADAM_B2 = 0.999
ADAM_EPS = 1e-08
ADAM_WD = 0.01
ADAM_STEP = 10

BIG = ("w_in", "w_br", "w_bp", "w_bs", "w_out", "w1", "w2")
SHARD_AXIS = {"w_in": 1, "w_br": 1, "w_bp": 1, "w_bs": 1, "w_out": 0, "w1": 1, "w2": 0}


def _pick(dim, pref, mult=128):
    best = None
    for t in range(mult, min(dim, pref) + 1, mult):
        if dim % t == 0:
            best = t
    return dim if best is None else best


def _cparams(sem=None):
    return pltpu.CompilerParams(dimension_semantics=sem, vmem_limit_bytes=VMEM_LIMIT)


def _rows(i, tr):
    return i * tr + lax.broadcasted_iota(jnp.int32, (tr, 1), 0)


def _sel(i, tr, n_ctx, v2):
    return jnp.where(_rows(i, tr) < n_ctx, v2[0:1, :], v2[1:2, :])


def _seg_sums(i, tr, n_ctx, d):
    is_ctx = _rows(i, tr) < n_ctx
    s_c = jnp.sum(jnp.where(is_ctx, d, 0.0), axis=0, keepdims=True)
    s_x = jnp.sum(jnp.where(is_ctx, 0.0, d), axis=0, keepdims=True)
    two = lax.broadcasted_iota(jnp.int32, (2, d.shape[1]), 0)
    return jnp.where(two == 0, s_c, s_x)


def _dot(a, b, form="NN"):
    dims = {"NN": (((1,), (0,)), ((), ())), "NT": (((1,), (1,)), ((), ())), "TN": (((0,), (0,)), ((), ()))}[form]
    return lax.dot_general(a, b, dims, preferred_element_type=F32)


def _gelu(x):
    return 0.5 * x * (1.0 + jnp.tanh(0.7978845608028654 * (x + 0.044715 * x * x * x)))


def _matmul(name, form, pairs, *, R, C, tr, tc, tk, nk, out_dtypes, epi, extras=(), a_pro=None):
    npair, nex, nout = len(pairs), len(extras), len(out_dtypes)
    in_specs, args = [], []
    for a, b in pairs:
        if nk == 1:
            ka = a.shape[0] if form == "TN" else a.shape[1]
        else:
            ka = tk
        if form == "NN":
            in_specs += [pl.BlockSpec((tr, ka), lambda i, j, k: (i, k)), pl.BlockSpec((ka, tc), lambda i, j, k: (k, j))]
        elif form == "NT":
            in_specs += [pl.BlockSpec((tr, ka), lambda i, j, k: (i, k)), pl.BlockSpec((tc, ka), lambda i, j, k: (j, k))]
        else:
            in_specs += [pl.BlockSpec((ka, tr), lambda i, j, k: (k, i)), pl.BlockSpec((ka, tc), lambda i, j, k: (k, j))]
        args += [a, b]
    for kind, arr, off in extras:
        if kind == "tile":
            in_specs.append(pl.BlockSpec((tr, tc), lambda i, j, k, off=off: (i, j + off)))
        else:
            in_specs.append(pl.BlockSpec((2, tc), lambda i, j, k, off=off: (0, j + off)))
        args.append(arr)

    def body(*refs):
        ab = refs[:2 * npair]
        ex = refs[2 * npair:2 * npair + nex]
        outs = refs[2 * npair + nex:2 * npair + nex + nout]
        accs = refs[2 * npair + nex + nout:]
        i, k = pl.program_id(0), pl.program_id(2)

        def products():
            res = []
            for p in range(npair):
                a = ab[2 * p][...]
                if a_pro is not None:
                    a = a_pro(a)
                res.append(_dot(a, ab[2 * p + 1][...], form))
            return res

        def finish(vals):
            res = epi(vals, [e[...] for e in ex], i)
            for o, v in zip(outs, res):
                o[...] = v.astype(o.dtype)

        if nk == 1:
            finish(products())
        else:
            prods = products()

            @pl.when(k == 0)
            def _():
                for acc, v in zip(accs, prods):
                    acc[...] = v

            @pl.when(k > 0)
            def _():
                for acc, v in zip(accs, prods):
                    acc[...] += v

            @pl.when(k == nk - 1)
            def _():
                finish([acc[...] for acc in accs])

    return pl.pallas_call(
        body, name=name, grid=(R // tr, C // tc, nk),
        in_specs=in_specs,
        out_specs=[pl.BlockSpec((tr, tc), lambda i, j, k: (i, j)) for _ in out_dtypes],
        out_shape=[jax.ShapeDtypeStruct((R, C), dt) for dt in out_dtypes],
        scratch_shapes=[pltpu.VMEM((tr, tc), F32) for _ in range(npair if nk > 1 else 0)],
        compiler_params=_cparams(("parallel", "parallel", "arbitrary")),
    )(*args)


def _rowcall(name, fn, *, nrows, tr, ins, outs, accs=(), ncol=1):
    n_in, n_out, n_acc = len(ins), len(outs), len(accs)
    in_specs, args = [], []
    for arr, kind, w, off in ins:
        if kind == "rows":
            in_specs.append(pl.BlockSpec((tr, w), lambda i, j, off=off: (i, off + j)))
        elif kind == "full":
            in_specs.append(pl.BlockSpec(arr.shape, lambda i, j, nd=arr.ndim: (0,) * nd))
        else:
            in_specs.append(pl.BlockSpec((None,) + arr.shape[1:], lambda i, j, f=off, nd=arr.ndim: (f(i),) + (0,) * (nd - 1)))
        args.append(arr)
    aliases = {}
    out_specs, out_shape = [], []
    for o_idx, (dt, total, w, off, alias) in enumerate(outs):
        out_specs.append(pl.BlockSpec((tr, w), lambda i, j, off=off: (i, off + j)))
        out_shape.append(jax.ShapeDtypeStruct((nrows, total), dt))
        if alias is not None:
            aliases[len(args)] = o_idx
            in_specs.append(ANY)
            args.append(alias)
    n_alias = len(aliases)
    for shp in accs:
        out_specs.append(pl.BlockSpec(shp, lambda i, j, nd=len(shp): (0,) * nd))
        out_shape.append(jax.ShapeDtypeStruct(shp, F32))

    def body(*refs):
        in_refs = refs[:n_in]
        out_refs = refs[n_in + n_alias:n_in + n_alias + n_out]
        acc_refs = refs[n_in + n_alias + n_out:]
        i, j = pl.program_id(0), pl.program_id(1)
        res = fn(i, j, *[r[...] for r in in_refs])
        for o, v in zip(out_refs, res[:n_out]):
            o[...] = v.astype(o.dtype)
        first = jnp.logical_and(i == 0, j == 0)
        for acc, v in zip(acc_refs, res[n_out:]):
            @pl.when(first)
            def _(acc=acc, v=v):
                acc[...] = v

            @pl.when(jnp.logical_not(first))
            def _(acc=acc, v=v):
                acc[...] += v

    res = pl.pallas_call(
        body, name=name, grid=(nrows // tr, ncol),
        in_specs=in_specs, out_specs=out_specs, out_shape=out_shape,
        input_output_aliases=aliases,
        compiler_params=_cparams(("arbitrary", "arbitrary")),
    )(*args)
    return res


def _f_normmod(x, g, shift, scale):
    r = lax.rsqrt(jnp.mean(x * x, axis=-1, keepdims=True) + EPS)
    return (x * r * g) * (1.0 + scale) + shift


def _f_headnorm_gate(o, zg, ng):
    r = lax.rsqrt(jnp.mean(o * o, axis=-1, keepdims=True) + EPS)
    return (o * r * ng) * (zg * jax.nn.sigmoid(zg))


def _f_sgv(sv, g):
    v = _gelu(sv)
    r = lax.rsqrt(jnp.mean(v * v, axis=-1, keepdims=True) + EPS)
    return v * r * g


def _rope(t, cos, sin):
    w = t.shape[1]
    lane = lax.broadcasted_iota(jnp.int32, t.shape, 1)
    swapped = jnp.where(jnp.bitwise_and(lane, 63) < 32,pltpu.roll(t, w - 32, 1), pltpu.roll(t, 32, 1))
    return t * cos + swapped * sin


def _rope_t(d, cos, sin):
    w = d.shape[1]
    lane = lax.broadcasted_iota(jnp.int32, d.shape, 1)
    ds = d * sin
    swapped = jnp.where(jnp.bitwise_and(lane, 63) < 32,pltpu.roll(ds, w - 32, 1), pltpu.roll(ds, 32, 1))
    return d * cos + swapped


class _Cfg:
    def __init__(self, D, T, CTX, DFF):
        self.D, self.T, self.CTX, self.DFF = D, T, CTX, DFF
        self.TA = T + CTX
        self.RW = D // 2
        self.H = self.RW // CH
        self.PW = D // 4
        self.SW = D // 4
        self.NIN = 4 * self.RW + self.PW + 2 * self.SW + 3 * D
        self.NC = self.TA // CH
        self.NCC = CTX // CH
        self.k_scale = float(CH) ** -0.5
        self.tr = _pick(self.TA, 768)
        self.o_g = 3 * self.RW
        self.o_p = 4 * self.RW
        self.o_u = self.o_p + self.PW
        self.o_sv = self.o_u + self.SW
        self.o_gate = self.o_sv + self.SW


def _normmod_fwd(cfg, X, g, mods, i_shift, i_scale, name):
    D = cfg.D
    tr = _pick(cfg.TA, 384)

    def fn(i, j, x, g, m):
        sh = _sel(i, tr, cfg.CTX, m[:, i_shift * D:(i_shift + 1) * D])
        sc = _sel(i, tr, cfg.CTX, m[:, i_scale * D:(i_scale + 1) * D])
        return (_f_normmod(x, g, sh, sc),)

    return _rowcall(name, fn, nrows=cfg.TA, tr=tr, ins=[(X, "rows", D, 0), (g, "full", 0, 0), (mods, "full", 0, 0)],
                    outs=[(BF16, D, D, 0, None)])[0]


def _normmod_bwd(cfg, X, dH, dres, g, mods, i_shift, i_scale, name):
    D = cfg.D
    tr = _pick(cfg.TA, 384)

    def fn(i, j, x, dh, dr, g, m):
        sh = _sel(i, tr, cfg.CTX, m[:, i_shift * D:(i_shift + 1) * D])
        sc = _sel(i, tr, cfg.CTX, m[:, i_scale * D:(i_scale + 1) * D])
        _, vjp = jax.vjp(_f_normmod, x, g, sh, sc)
        dx, dg, dsh, dsc = vjp(dh)
        return dr + dx, dg, _seg_sums(i, tr, cfg.CTX, dsh), _seg_sums(i, tr, cfg.CTX, dsc)

    return _rowcall(name, fn, nrows=cfg.TA, tr=tr,
                    ins=[(X, "rows", D, 0), (dH, "rows", D, 0), (dres, "rows", D, 0), (g, "full", 0, 0), (mods, "full", 0, 0)],
                    outs=[(F32, D, D, 0, None)], accs=[(1, D), (2, D), (2, D)])


def _resgate_bwd(cfg, dX, M, mods, i_gate, name):
    D = cfg.D
    tr = _pick(cfg.TA, 384)

    def fn(i, j, dx, m, mm):
        gate = _sel(i, tr, cfg.CTX, mm[:, i_gate * D:(i_gate + 1) * D])
        return dx * gate, _seg_sums(i, tr, cfg.CTX, dx * m.astype(F32))

    return _rowcall(name, fn, nrows=cfg.TA, tr=tr, ins=[(dX, "rows", D, 0), (M, "rows", D, 0), (mods, "full", 0, 0)],
                    outs=[(BF16, D, D, 0, None)], accs=[(2, D)])


def _chunk_of(cfg, d, t):
    fwd = t
    bwd = jnp.where(t < cfg.NCC, cfg.NCC - 1 - t, cfg.NC - 1 - t + cfg.NCC)
    return jnp.where(d == 0, fwd, bwd)


def _ret_specs(cfg, cm):
    RW, H = cfg.RW, cfg.H
    return [
        pl.BlockSpec((CH, RW), lambda d, t: (cm(d, t), 0)),
        pl.BlockSpec((CH, RW), lambda d, t: (cm(d, t), 1)),
        pl.BlockSpec((CH, RW), lambda d, t: (cm(d, t), 2)),
        pl.BlockSpec((CH, CH), lambda d, t: (cm(d, t), 0)),
        pl.BlockSpec((CH, CH), lambda d, t: (cm(d, t), 0)),
        pl.BlockSpec((None, H, CH, CH), lambda d, t: (d, 0, 0, 0)),
        pl.BlockSpec((None, CH, RW), lambda d, t: (d, 0, 0)),
        pl.BlockSpec((None, CH, RW), lambda d, t: (d, 0, 0)),
        pl.BlockSpec((None, 1, RW), lambda d, t: (d, 0, 0)),
    ]


def _ret_prep(cfg, q_ref, k_ref, ct_ref, st_ref, qd_ref, kd_ref):
    cos = jnp.tile(ct_ref[...], (1, cfg.H))
    sin = jnp.tile(st_ref[...], (1, cfg.H))
    qr = _rope(q_ref[...].astype(F32), cos, sin)
    kr = _rope(k_ref[...].astype(F32) * cfg.k_scale, cos, sin)
    return qr, kr, (qr * qd_ref[...]).astype(BF16), (kr * kd_ref[...]).astype(BF16)


def _ret_fwd(cfg, Z, tabs, decs):
    RW, H, TA, NC = cfg.RW, cfg.H, cfg.TA, cfg.NC
    cm = functools.partial(_chunk_of, cfg)

    def body(q_ref, k_ref, v_ref, ct_ref, st_ref, dm_ref, qd_ref, kd_ref, cd_ref, o_ref, so_ref, S):
        @pl.when(pl.program_id(1) == 0)
        def _():
            S[...] = jnp.zeros_like(S)

        qr, kr, qd, kd = _ret_prep(cfg, q_ref, k_ref, ct_ref, st_ref, qd_ref, kd_ref)
        qb, kb = qr.astype(BF16), kr.astype(BF16)
        v = v_ref[...].astype(BF16)
        for h in range(H):
            sl = slice(h * CH, (h + 1) * CH)
            p = (_dot(qb[:, sl], kb[:, sl], "NT") * dm_ref[h]).astype(BF16)
            s_h = S[h]
            so_ref[h] = s_h
            o_ref[:, sl] = _dot(p, v[:, sl]) + _dot(qd[:, sl], s_h.astype(BF16))
            S[h] = s_h * cd_ref[:, sl] + _dot(kd[:, sl], v[:, sl], "TN")

    return pl.pallas_call(
        body, name="ret_fwd", grid=(2, NC),
        in_specs=_ret_specs(cfg, cm),
        out_specs=[pl.BlockSpec((None, CH, RW), lambda d, t: (d, cm(d, t), 0)),
                   pl.BlockSpec((None, None, H, CH, CH), lambda d, t: (d, cm(d, t), 0, 0, 0))],
        out_shape=[jax.ShapeDtypeStruct((2, TA, RW), F32), jax.ShapeDtypeStruct((2, NC, H, CH, CH), F32)],
        scratch_shapes=[pltpu.VMEM((H, CH, CH), F32)],
        compiler_params=_cparams(("arbitrary", "arbitrary")),
    )(Z, Z, Z, tabs["cos"], tabs["sin"], decs["dmat"], decs["qdec"], decs["kdec"], decs["cdec"])


def _ret_bwd(cfg, Z, tabs, decs, states, dO):
    RW, H, TA, NC = cfg.RW, cfg.H, cfg.TA, cfg.NC

    def cm(d, t):
        return _chunk_of(cfg, d, NC - 1 - t)

    def body(q_ref, k_ref, v_ref, ct_ref, st_ref, dm_ref, qd_ref, kd_ref, cd_ref, s_ref, do_ref, w_ref,
             dqkv_ref, dl_ref, dS):
        t = pl.program_id(1)

        @pl.when(t == 0)
        def _():
            dS[...] = jnp.zeros_like(dS)
            dl_ref[...] = jnp.zeros_like(dl_ref)

        qr, kr, qd, kd = _ret_prep(cfg, q_ref, k_ref, ct_ref, st_ref, qd_ref, kd_ref)
        qb, kb = qr.astype(BF16), kr.astype(BF16)
        v = v_ref[...].astype(BF16)
        dob = do_ref[...].astype(BF16)
        for h in range(H):
            sl = slice(h * CH, (h + 1) * CH)
            dm = dm_ref[h]
            p = (_dot(qb[:, sl], kb[:, sl], "NT") * dm).astype(BF16)
            dp = (_dot(dob[:, sl], v[:, sl], "NT") * dm).astype(BF16)
            s_h = s_ref[h]
            ds_h = dS[h]
            sb, dsb = s_h.astype(BF16), ds_h.astype(BF16)
            dq_i = _dot(dp, kb[:, sl])
            dk_i = _dot(dp, qb[:, sl], "TN")
            dq_c = _dot(dob[:, sl], sb, "NT") * qd_ref[:, sl]
            dk_s = _dot(v[:, sl], dsb, "NT") * kd_ref[:, sl]
            dqkv_ref[:, sl] = dq_i + dq_c
            dqkv_ref[:, RW + h * CH:RW + (h + 1) * CH] = dk_i + dk_s
            dqkv_ref[:, 2 * RW + h * CH:2 * RW + (h + 1) * CH] = _dot(p, dob[:, sl], "TN") + _dot(kd[:, sl], dsb)
            qh, kh = qr[:, sl], kr[:, sl]
            lam = w_ref[0] * (qh * dq_i) + w_ref[1] * (qh * dq_c) + w_ref[2] * (kh * dk_i) + w_ref[3] * (kh * dk_s)
            lam_s = float(CH) * cd_ref[:, sl] * jnp.sum(ds_h * s_h, axis=0, keepdims=True)
            dl_ref[:, sl] += jnp.sum(lam, axis=0, keepdims=True) + lam_s
            dS[h] = ds_h * cd_ref[:, sl] + _dot(qd[:, sl], dob[:, sl], "TN")

    in_specs = _ret_specs(cfg, cm) + [
        pl.BlockSpec((None, None, H, CH, CH), lambda d, t: (d, cm(d, t), 0, 0, 0)),
        pl.BlockSpec((CH, RW), lambda d, t: (cm(d, t), 0)),
        pl.BlockSpec((None, 4, CH, CH), lambda d, t: (d, 0, 0, 0)),
    ]
    return pl.pallas_call(
        body, name="ret_bwd", grid=(2, NC),
        in_specs=in_specs,
        out_specs=[pl.BlockSpec((None, CH, 3 * RW), lambda d, t: (d, cm(d, t), 0)),
                   pl.BlockSpec((None, 1, RW), lambda d, t: (d, 0, 0))],
        out_shape=[jax.ShapeDtypeStruct((2, TA, 3 * RW), F32), jax.ShapeDtypeStruct((2, 1, RW), F32)],
        scratch_shapes=[pltpu.VMEM((H, CH, CH), F32)],
        compiler_params=_cparams(("arbitrary", "arbitrary")),
    )(Z, Z, Z, tabs["cos"], tabs["sin"], decs["dmat"], decs["qdec"], decs["kdec"], decs["cdec"], states, dO,
      tabs["lamw"])


def _rope_bwd(cfg, dqkv2, tabs, dz):
    RW, H = cfg.RW, cfg.H
    tr = PT

    def body(d0, d1, ct, st, dz_in, o):
        cos, sin = jnp.tile(ct[...], (1, H)), jnp.tile(st[...], (1, H))
        d = d0[...] + d1[...]
        o[:, :RW] = _rope_t(d[:, :RW], cos, sin).astype(o.dtype)
        o[:, RW:2 * RW] = (_rope_t(d[:, RW:2 * RW], cos, sin) * cfg.k_scale).astype(o.dtype)
        o[:, 2 * RW:] = d[:, 2 * RW:].astype(o.dtype)

    return pl.pallas_call(
        body, name="rope_bwd", grid=(cfg.TA // tr,),
        in_specs=[pl.BlockSpec((None, tr, 3 * RW), lambda i: (0, i, 0)), pl.BlockSpec((None, tr, 3 * RW), lambda i: (1, i, 0)),
                  pl.BlockSpec((tr, CH), lambda i: (i, 0)), pl.BlockSpec((tr, CH), lambda i: (i, 0)), ANY],
        out_specs=pl.BlockSpec((tr, 3 * RW), lambda i: (i, 0)),
        out_shape=jax.ShapeDtypeStruct((cfg.TA, cfg.NIN), BF16),
        input_output_aliases={4: 0},
        compiler_params=_cparams(("arbitrary",)),
    )(dqkv2, dqkv2, tabs["cos"], tabs["sin"], dz)


def _retout_fwd(cfg, o2, Z, ng):
    RW, H = cfg.RW, cfg.H
    tr = PT

    def body(o0, o1, zg, ng, out):
        o = o0[...] + o1[...]
        z = zg[...].astype(F32)
        for h in range(H):
            sl = slice(h * CH, (h + 1) * CH)
            out[:, sl] = _f_headnorm_gate(o[:, sl], z[:, sl], ng[:, sl]).astype(out.dtype)

    return pl.pallas_call(
        body, name="retout_fwd", grid=(cfg.TA // tr,),
        in_specs=[pl.BlockSpec((None, tr, RW), lambda i: (0, i, 0)), pl.BlockSpec((None, tr, RW), lambda i: (1, i, 0)),
                  pl.BlockSpec((tr, RW), lambda i: (i, 3)), pl.BlockSpec((1, RW), lambda i: (0, 0))],
        out_specs=pl.BlockSpec((tr, RW), lambda i: (i, 0)),
        out_shape=jax.ShapeDtypeStruct((cfg.TA, RW), BF16),
        compiler_params=_cparams(("arbitrary",)),
    )(o2, o2, Z, ng)


def _retout_bwd(cfg, o2, Z, ng, dret, dz):
    RW, H = cfg.RW, cfg.H
    tr = PT

    def body(o0, o1, zg, ng, dr, dz_in, do_out, dz_out, dng):
        i = pl.program_id(0)
        o = o0[...] + o1[...]
        z = zg[...].astype(F32)
        d = dr[...]
        acc = []
        for h in range(H):
            sl = slice(h * CH, (h + 1) * CH)
            _, vjp = jax.vjp(_f_headnorm_gate, o[:, sl], z[:, sl], ng[:, sl])
            do_h, dz_h, dg_h = vjp(d[:, sl])
            do_out[:, sl] = do_h
            dz_out[:, sl] = dz_h.astype(dz_out.dtype)
            acc.append(dg_h)

        @pl.when(i == 0)
        def _():
            for h in range(H):
                dng[:, h * CH:(h + 1) * CH] = acc[h]

        @pl.when(i > 0)
        def _():
            for h in range(H):
                dng[:, h * CH:(h + 1) * CH] += acc[h]

    return pl.pallas_call(
        body, name="retout_bwd", grid=(cfg.TA // tr,),
        in_specs=[pl.BlockSpec((None, tr, RW), lambda i: (0, i, 0)), pl.BlockSpec((None, tr, RW), lambda i: (1, i, 0)),
                  pl.BlockSpec((tr, RW), lambda i: (i, 3)), pl.BlockSpec((1, RW), lambda i: (0, 0)),
                  pl.BlockSpec((tr, RW), lambda i: (i, 0)), ANY],
        out_specs=[pl.BlockSpec((tr, RW), lambda i: (i, 0)), pl.BlockSpec((tr, RW), lambda i: (i, 3)),
                   pl.BlockSpec((1, RW), lambda i: (0, 0))],
        out_shape=[jax.ShapeDtypeStruct((cfg.TA, RW), F32), jax.ShapeDtypeStruct((cfg.TA, cfg.NIN), BF16),
                   jax.ShapeDtypeStruct((1, RW), F32)],
        input_output_aliases={5: 1},
        compiler_params=_cparams(("arbitrary",)),
    )(o2, o2, Z, ng, dret, dz)


def _pool_consts(ctx_len):
    assert ctx_len == PT
    bm = np.zeros((2, len(POOL_WINDOWS), PT, PT), np.float32)
    ic = np.zeros((2, len(POOL_WINDOWS), PT, CH), np.float32)
    for ty, seg in enumerate((ctx_len, GRID_W)):
        for gi, w in enumerate(POOL_WINDOWS):
            for r in range(PT):
                s0, pos = (r // seg) * seg, r % seg
                lo, hi = max(pos - w // 2, 0), min(pos + w // 2 - 1, seg - 1)
                bm[ty, gi, r, s0 + lo:s0 + hi + 1] = 1.0
                ic[ty, gi, r, :] = 1.0 / (hi - lo + 1)
    return jnp.asarray(bm, BF16), jnp.asarray(ic, F32)


def _pool_tile(p, bm, ic, pw, g):
    sl = slice(g * CH, (g + 1) * CH)
    pg = p[:, sl].astype(F32)
    hi = pg.astype(BF16)
    lo = (pg - hi.astype(F32)).astype(BF16)
    y = (_dot(bm[g], hi) + _dot(bm[g], lo)) * ic[g] - pg
    return y, _dot(y.astype(BF16), pw[g].astype(BF16))


def _pool_fwd(cfg, Z, consts, pool_w, pool_scale):
    PW = cfg.PW
    G = PW // CH
    nct = cfg.CTX // PT
    ty = lambda i: jnp.where(i < nct, 0, 1)

    def fn(i, j, p, bm, ic, pw, ps):
        outs = [_pool_tile(p, bm, ic, pw, g)[1] for g in range(G)]
        return (jnp.concatenate(outs, axis=1) * ps,)

    return _rowcall("pool_fwd", fn, nrows=cfg.TA, tr=PT,
                    ins=[(Z, "rows", PW, cfg.o_p // PW), (consts[0], "sel", 0, ty), (consts[1], "sel", 0, ty),
                         (pool_w, "full", 0, 0), (pool_scale, "full", 0, 0)],
                    outs=[(BF16, PW, PW, 0, None)])[0]


def _pool_bwd(cfg, Z, consts, pool_w, pool_scale, dpool, dz):
    PW = cfg.PW
    G = PW // CH
    nct = cfg.CTX // PT
    ty = lambda i: jnp.where(i < nct, 0, 1)

    def fn(i, j, p, bm, ic, pw, ps, dout):
        dps, dps_acc, dpw = [], [], []
        for g in range(G):
            sl = slice(g * CH, (g + 1) * CH)
            y, lin = _pool_tile(p, bm, ic, pw, g)
            dlin = (dout[:, sl] * ps[:, sl]).astype(BF16)
            dps_acc.append(jnp.sum(dout[:, sl] * lin, axis=0, keepdims=True))
            dy = _dot(dlin, pw[g].astype(BF16), "NT")
            dpw.append(_dot(y.astype(BF16), dlin, "TN"))
            t = dy * ic[g]
            hi = t.astype(BF16)
            lo = (t - hi.astype(F32)).astype(BF16)
            dps.append(_dot(bm[g], hi, "TN") + _dot(bm[g], lo, "TN") - dy)
        return (jnp.concatenate(dps, axis=1), jnp.stack(dpw), jnp.concatenate(dps_acc, axis=1))

    return _rowcall("pool_bwd", fn, nrows=cfg.TA, tr=PT,
                    ins=[(Z, "rows", PW, cfg.o_p // PW), (consts[0], "sel", 0, ty), (consts[1], "sel", 0, ty),
                         (pool_w, "full", 0, 0), (pool_scale, "full", 0, 0), (dpool, "rows", PW, 0)],
                    outs=[(BF16, cfg.NIN, PW, cfg.o_p // PW, dz)], accs=[(G, CH, CH), (1, PW)])


def _sg_mixed(vn, sw, sbb, g, c):
    rows = slice(c * CH, (c + 1) * CH)
    sl = slice(g * CH, (g + 1) * CH)
    return _dot(sw[g].astype(BF16), vn[rows, sl].astype(BF16)) + sbb[g]


def _sg_fwd(cfg, Z, sng, sw, sbb):
    SW = cfg.SW
    G = SW // CH

    def fn(i, j, u, sv, sng, sw, sbb):
        ug = _gelu(u.astype(F32))
        vn = _f_sgv(sv.astype(F32), sng)
        rows = []
        for c in range(PT // CH):
            mixed = jnp.concatenate([_sg_mixed(vn, sw, sbb, g, c) for g in range(G)], axis=1)
            rows.append(ug[c * CH:(c + 1) * CH, :] * mixed)
        return (jnp.concatenate(rows, axis=0),)

    return _rowcall("sg_fwd", fn, nrows=cfg.TA, tr=PT,
                    ins=[(Z, "rows", SW, cfg.o_u // SW), (Z, "rows", SW, cfg.o_sv // SW), (sng, "full", 0, 0),
                         (sw, "full", 0, 0), (sbb, "full", 0, 0)],
                    outs=[(BF16, SW, SW, 0, None)])[0]


def _sg_bwd(cfg, Z, sng, sw, sbb, dsg, dz):
    SW = cfg.SW
    G = SW // CH

    def fn(i, j, u, sv, sng, sw, sbb, dout):
        uf, svf = u.astype(F32), sv.astype(F32)
        ug, vjp_u = jax.vjp(_gelu, uf)
        vn, vjp_v = jax.vjp(_f_sgv, svf, sng)
        dug_rows, dvn_rows = [], []
        dsw = [jnp.zeros((CH, CH), F32) for _ in range(G)]
        dsb = [jnp.zeros((CH, CH), F32) for _ in range(G)]
        for c in range(PT // CH):
            rows = slice(c * CH, (c + 1) * CH)
            dug_g, dvn_g = [], []
            for g in range(G):
                sl = slice(g * CH, (g + 1) * CH)
                mixed = _sg_mixed(vn, sw, sbb, g, c)
                dmixed = dout[rows, sl] * ug[rows, sl]
                dug_g.append(dout[rows, sl] * mixed)
                dmb = dmixed.astype(BF16)
                dvn_g.append(_dot(sw[g].astype(BF16), dmb, "TN"))
                dsw[g] = dsw[g] + _dot(dmb, vn[rows, sl].astype(BF16), "NT")
                dsb[g] = dsb[g] + jnp.broadcast_to(jnp.sum(dmixed, axis=1, keepdims=True), (CH, CH))
            dug_rows.append(jnp.concatenate(dug_g, axis=1))
            dvn_rows.append(jnp.concatenate(dvn_g, axis=1))
        (du,) = vjp_u(jnp.concatenate(dug_rows, axis=0))
        dsv, dsng = vjp_v(jnp.concatenate(dvn_rows, axis=0))
        return du, dsv, jnp.stack(dsw), jnp.stack(dsb), dsng

    return _rowcall("sg_bwd", fn, nrows=cfg.TA, tr=PT,
                    ins=[(Z, "rows", SW, cfg.o_u // SW), (Z, "rows", SW, cfg.o_sv // SW), (sng, "full", 0, 0),
                         (sw, "full", 0, 0), (sbb, "full", 0, 0), (dsg, "rows", SW, 0)],
                    outs=[(BF16, cfg.NIN, SW, cfg.o_u // SW, dz), (BF16, SW, SW, 0, None)],
                    accs=[(G, CH, CH), (G, CH, CH), (1, SW)])


def _gate_bwd(cfg, b, dY, L, Z, dz):
    D = cfg.D
    tc = 512
    nj = D // tc
    off = cfg.o_gate // tc + b * nj

    def fn(i, j, dy, l, z):
        s = jax.nn.sigmoid(z.astype(F32))
        return dy * s, dy * l.astype(F32) * s * (1.0 - s)

    return _rowcall("gate_bwd", fn, nrows=cfg.TA, tr=cfg.tr,
                    ins=[(dY, "rows", tc, 0), (L, "rows", tc, 0), (Z, "rows", tc, off)],
                    outs=[(BF16, D, tc, 0, None), (BF16, cfg.NIN, tc, off, dz)], ncol=nj)


def _copy_cols(cfg, src, dz, col0):
    w = src.shape[1]
    return _rowcall("copy_cols", lambda i, j, s: (s,), nrows=cfg.TA, tr=cfg.tr, ins=[(src, "rows", w, 0)],
                    outs=[(dz.dtype, dz.shape[1], w, col0 // w, dz)])[0]


def _final(cfg, X, g, target):
    D = cfg.D
    tr = PT
    nct = cfg.CTX // tr

    def body(x_ref, g_ref, t_ref, dx_ref, loss_ref, dg_ref):
        i = pl.program_id(0)

        def f(x, g):
            r = lax.rsqrt(jnp.mean(x * x, axis=-1, keepdims=True) + EPS)
            return x * r * g

        y, vjp = jax.vjp(f, x_ref[...], g_ref[...])
        err = y - t_ref[...]
        dx, dg = vjp(err * (1.0 / D))
        part = 0.5 * jnp.sum(jnp.mean(err * err, axis=-1, keepdims=True), axis=0, keepdims=True)

        @pl.when(i == 0)
        def _():
            loss_ref[...] = jnp.zeros_like(loss_ref)
            dg_ref[...] = jnp.zeros_like(dg_ref)

        @pl.when(i < nct)
        def _():
            dx_ref[...] = jnp.zeros_like(dx_ref)

        @pl.when(i >= nct)
        def _():
            dx_ref[...] = dx
            loss_ref[...] += jnp.broadcast_to(part, loss_ref.shape)
            dg_ref[...] += dg

    return pl.pallas_call(
        body, name="final", grid=(cfg.TA // tr,),
        in_specs=[pl.BlockSpec((tr, D), lambda i: (i, 0)), pl.BlockSpec((1, D), lambda i: (0, 0)),
                  pl.BlockSpec((tr, D), lambda i: (jnp.maximum(i - nct, 0), 0))],
        out_specs=[pl.BlockSpec((tr, D), lambda i: (i, 0)), pl.BlockSpec((8, CH), lambda i: (0, 0)),
                   pl.BlockSpec((1, D), lambda i: (0, 0))],
        out_shape=[jax.ShapeDtypeStruct((cfg.TA, D), F32), jax.ShapeDtypeStruct((8, CH), F32),
                   jax.ShapeDtypeStruct((1, D), F32)],
        compiler_params=_cparams(("arbitrary",)),
    )(X, g, target)


def _layer_fwd(cfg, X, mods, W, sm, tabs, decs, consts):
    D, TA, tr, RW, PW, SW, DFF, NIN = cfg.D, cfg.TA, cfg.tr, cfg.RW, cfg.PW, cfg.SW, cfg.DFF, cfg.NIN
    ident = lambda accs, ex, i: (accs[0],)
    H1 = _normmod_fwd(cfg, X, sm["norm1_g"], mods, 0, 1, "normmod1_fwd")
    (Z,) = _matmul("z_mm", "NN", [(H1, W["w_in"])], R=TA, C=NIN, tr=tr, tc=512, tk=D, nk=1, out_dtypes=[F32], epi=ident)
    o2, states = _ret_fwd(cfg, Z, tabs, decs)
    ret = _retout_fwd(cfg, o2, Z, sm["ret_norm_g"])
    pool = _pool_fwd(cfg, Z, consts, sm["pool_w"], sm["pool_scale"])
    sg = _sg_fwd(cfg, Z, sm["sg_norm_g"], sm["sg_w"], sm["sg_bb"])

    tc = 512
    goff = cfg.o_gate // tc

    def epi_branch(accs, ex, i):
        y = sum(jax.nn.sigmoid(z.astype(F32)) * a for a, z in zip(accs, ex))
        return (y, accs[0], accs[1], accs[2])

    Y, Lr, Lp, Ls = _matmul("branch_mm", "NN", [(ret, W["w_br"]), (pool, W["w_bp"]), (sg, W["w_bs"])], R=TA, C=D, tr=tr,
                            tc=tc, tk=0, nk=1, out_dtypes=[BF16] * 4, epi=epi_branch,
                            extras=[("tile", Z, goff + b * (D // tc)) for b in range(3)])

    def epi_res(accs, ex, i):
        return (ex[0] + _sel(i, tr, cfg.CTX, ex[1]) * accs[0], accs[0])

    X2, O = _matmul("out_mm", "NN", [(Y, W["w_out"])], R=TA, C=D, tr=tr, tc=tc, tk=D, nk=1, out_dtypes=[F32, BF16],
                    epi=epi_res, extras=[("tile", X, 0), ("rows2", mods, 2 * (D // tc))])
    H2 = _normmod_fwd(cfg, X2, sm["norm2_g"], mods, 3, 4, "normmod2_fwd")
    tcf = _pick(DFF, 1024)
    def epi_relu(accs, ex, i):
        r = jnp.maximum(accs[0], 0.0)
        return (r * r, r)

    A2, Rr = _matmul("w1_mm", "NN", [(H2, W["w1"])], R=TA, C=DFF, tr=tr, tc=tcf, tk=D, nk=1, out_dtypes=[BF16, BF16],
                     epi=epi_relu)
    tkf = _pick(DFF, 2048)
    X3, M = _matmul("w2_mm", "NN", [(A2, W["w2"])], R=TA, C=D, tr=tr, tc=1024, tk=tkf, nk=DFF // tkf, out_dtypes=[F32, BF16],
                    epi=epi_res, extras=[("tile", X2, 0), ("rows2", mods, 5 * (D // 1024))])
    saved = dict(X=X, H1=H1, Z=Z, o2=o2, states=states, ret=ret, pool=pool, sg=sg, Y=Y, L=(Lr, Lp, Ls), O=O, X2=X2,
                 H2=H2, R=Rr, A2=A2, M=M)
    return X3, saved


def _layer_bwd(cfg, dX3, sv, mods, W, sm, tabs, decs, consts):
    D, TA, tr, RW, PW, SW, DFF, NIN = cfg.D, cfg.TA, cfg.tr, cfg.RW, cfg.PW, cfg.SW, cfg.DFF, cfg.NIN
    ident = lambda accs, ex, i: (accs[0],)
    tkr = _pick(TA, 768)
    nkr = TA // tkr
    g = {}

    dM, dgate2 = _resgate_bwd(cfg, dX3, sv["M"], mods, 5, "resgate2_bwd")
    tcf = _pick(DFF, 1024)
    (dPre,) = _matmul("dpre_mm", "NT", [(dM, W["w2"])], R=TA, C=DFF, tr=tr, tc=tcf, tk=D, nk=1, out_dtypes=[BF16],
                      epi=lambda accs, ex, i: (accs[0] * (2.0 * ex[0].astype(F32)),), extras=[("tile", sv["R"], 0)])
    (g["w2"],) = _matmul("dw2_mm", "TN", [(sv["A2"], dM)], R=DFF, C=D, tr=tcf, tc=1024, tk=tkr, nk=nkr, out_dtypes=[F32],
                         epi=ident)
    tkf = _pick(DFF, 2048)
    (dH2,) = _matmul("dh2_mm", "NT", [(dPre, W["w1"])], R=TA, C=D, tr=tr, tc=1024, tk=tkf, nk=DFF // tkf, out_dtypes=[F32],
                     epi=ident)
    (g["w1"],) = _matmul("dw1_mm", "TN", [(sv["H2"], dPre)], R=D, C=DFF, tr=1024, tc=tcf, tk=tkr, nk=nkr, out_dtypes=[F32],
                         epi=ident)
    dX2, dn2, dsh2, dsc2 = _normmod_bwd(cfg, sv["X2"], dH2, dX3, sm["norm2_g"], mods, 3, 4, "normmod2_bwd")

    dO, dgate1 = _resgate_bwd(cfg, dX2, sv["O"], mods, 2, "resgate1_bwd")
    (dY,) = _matmul("dy_mm", "NT", [(dO, W["w_out"])], R=TA, C=D, tr=tr, tc=1024, tk=D, nk=1, out_dtypes=[F32], epi=ident)
    (g["w_out"],) = _matmul("dwout_mm", "TN", [(sv["Y"], dO)], R=D, C=D, tr=1024, tc=1024, tk=tkr, nk=nkr,
                            out_dtypes=[F32], epi=ident)
    dLr, dz = _gate_bwd(cfg, 0, dY, sv["L"][0], sv["Z"], None)
    dLp, dz = _gate_bwd(cfg, 1, dY, sv["L"][1], sv["Z"], dz)
    dLs, dz = _gate_bwd(cfg, 2, dY, sv["L"][2], sv["Z"], dz)

    (dret,) = _matmul("dret_mm", "NT", [(dLr, W["w_br"])], R=TA, C=RW, tr=tr, tc=RW, tk=D, nk=1, out_dtypes=[F32], epi=ident)
    (dpool,) = _matmul("dpool_mm", "NT", [(dLp, W["w_bp"])], R=TA, C=PW, tr=tr, tc=PW, tk=D, nk=1, out_dtypes=[F32], epi=ident)
    (dsg,) = _matmul("dsg_mm", "NT", [(dLs, W["w_bs"])], R=TA, C=SW, tr=tr, tc=SW, tk=D, nk=1, out_dtypes=[F32], epi=ident)
    (g["w_br"],) = _matmul("dwbr_mm", "TN", [(sv["ret"], dLr)], R=RW, C=D, tr=RW, tc=1024, tk=tkr, nk=nkr, out_dtypes=[F32],
                           epi=ident)
    (g["w_bp"],) = _matmul("dwbp_mm", "TN", [(sv["pool"], dLp)], R=PW, C=D, tr=PW, tc=1024, tk=tkr, nk=nkr, out_dtypes=[F32],
                           epi=ident)
    (g["w_bs"],) = _matmul("dwbs_mm", "TN", [(sv["sg"], dLs)], R=SW, C=D, tr=SW, tc=1024, tk=tkr, nk=nkr, out_dtypes=[F32],
                           epi=ident)
    dOr, dz, dretng = _retout_bwd(cfg, sv["o2"], sv["Z"], sm["ret_norm_g"], dret, dz)
    dqkv2, dlam = _ret_bwd(cfg, sv["Z"], tabs, decs, sv["states"], dOr)
    dz = _rope_bwd(cfg, dqkv2, tabs, dz)
    dz, dpw, dps = _pool_bwd(cfg, sv["Z"], consts, sm["pool_w"], sm["pool_scale"], dpool, dz)
    dz, dz_sv, dsw, dsb, dsng = _sg_bwd(cfg, sv["Z"], sm["sg_norm_g"], sm["sg_w"], sm["sg_bb"], dsg, dz)
    dz = _copy_cols(cfg, dz_sv, dz, cfg.o_sv)

    tkz = _pick(NIN, 2944)
    (dH1,) = _matmul("dh1_mm", "NT", [(dz, W["w_in"])], R=TA, C=D, tr=tr, tc=1024, tk=tkz, nk=NIN // tkz, out_dtypes=[F32],
                     epi=ident)
    (g["w_in"],) = _matmul("dwin_mm", "TN", [(sv["H1"], dz)], R=D, C=NIN, tr=1024, tc=512, tk=tkr, nk=nkr, out_dtypes=[F32],
                           epi=ident)
    dX, dn1, dsh1, dsc1 = _normmod_bwd(cfg, sv["X"], dH1, dX2, sm["norm1_g"], mods, 0, 1, "normmod1_bwd")
    dmods = jnp.concatenate([dsh1, dsc1, dgate1, dsh2, dsc2, dgate2], axis=1)
    small = dict(norm1_g=dn1, norm2_g=dn2, ret_norm_g=dretng, pool_w=dpw, pool_scale=dps, sg_norm_g=dsng, sg_w=dsw,
                 sg_b=dsb[:, :, 0], dlam=dlam)
    return dX, g, small, dmods


def _tables(cfg):
    nf = CH // 4
    inv = ROPE_THETA ** (-jnp.arange(nf, dtype=F32) / nf)
    tok = jnp.arange(cfg.T)
    ar = (tok // GRID_W).astype(F32)[:, None] * inv[None]
    ac = (tok % GRID_W).astype(F32)[:, None] * inv[None]
    cos = jnp.concatenate([jnp.cos(ar), jnp.cos(ar), jnp.cos(ac), jnp.cos(ac)], axis=1)
    sin = jnp.concatenate([-jnp.sin(ar), jnp.sin(ar), -jnp.sin(ac), jnp.sin(ac)], axis=1)
    cos = jnp.concatenate([jnp.ones((cfg.CTX, CH), F32), cos], axis=0)
    sin = jnp.concatenate([jnp.zeros((cfg.CTX, CH), F32), sin], axis=0)
    idx = np.broadcast_to(np.arange(CH, dtype=np.float32)[:, None], (CH, CH))
    lamw = np.stack([np.stack([idx, idx + 1.0, -idx, CH - 1.0 - idx]), np.stack([-idx, CH - idx, idx, idx])])
    return dict(cos=cos, sin=sin, lamw=jnp.asarray(lamw, F32))


def _decays(cfg, logit):
    H, RW = cfg.H, cfg.RW
    lam = jax.nn.log_sigmoid(logit.astype(F32))
    idx = jnp.arange(CH, dtype=F32)
    dist = idx[:, None] - idx[None, :]
    d0 = jnp.where(dist >= 0, jnp.exp(lam[0][:, None, None] * jnp.maximum(dist, 0.0)), 0.0)
    d1 = jnp.where(dist <= 0, jnp.exp(lam[1][:, None, None] * jnp.maximum(-dist, 0.0)), 0.0)
    lanes = lambda a: jnp.repeat(a.T, CH, axis=1)
    qdec = jnp.stack([lanes(jnp.exp(lam[0][:, None] * (idx + 1.0)[None])), lanes(jnp.exp(lam[1][:, None] * (CH - idx)[None]))])
    kdec = jnp.stack([lanes(jnp.exp(lam[0][:, None] * (CH - 1.0 - idx)[None])), lanes(jnp.exp(lam[1][:, None] * idx[None]))])
    cdec = jnp.repeat(jnp.exp(lam * CH), CH, axis=1)[:, None, :]
    return dict(dmat=jnp.stack([d0, d1]), qdec=qdec, kdec=kdec, cdec=cdec)


def _small_of_layer(small_w, l):
    sm = {k: v[l] for k, v in small_w.items()}
    sm["norm1_g"] = sm["norm1_g"][None]
    sm["norm2_g"] = sm["norm2_g"][None]
    sm["ret_norm_g"] = sm["ret_norm_g"][None]
    sm["pool_scale"] = sm["pool_scale"][None]
    sm["sg_norm_g"] = sm["sg_norm_g"][None]
    sm["sg_bb"] = jnp.broadcast_to(sm["sg_b"][:, :, None], sm["sg_b"].shape + (CH,))
    return sm


def _local_fwd_bwd(cfg, X0, target, mods, weights_of, small_w, final_g):
    depth = len(mods)
    tabs = _tables(cfg)
    consts = _pool_consts(cfg.CTX)
    X, saved, Ws, sms, decs = X0, [], [], [], []
    for l in range(depth):
        Ws.append(weights_of(l))
        sms.append(_small_of_layer(small_w, l))
        decs.append(_decays(cfg, small_w["ret_decay_logit"][l]))
        X, sv = _layer_fwd(cfg, X, mods[l], Ws[l], sms[l], tabs, decs[l], consts)
        saved.append(sv)
    dX, loss_acc, dfinal = _final(cfg, X, final_g[None], target)
    big, small, dmods = [None] * depth, [None] * depth, [None] * depth
    for l in reversed(range(depth)):
        dX, big[l], small[l], dmods[l] = _layer_bwd(cfg, dX, saved[l], mods[l], Ws[l], sms[l], tabs, decs[l], consts)
        lam_grad = jnp.sum(small[l].pop("dlam").reshape(2, cfg.H, CH), axis=-1)
        small[l]["ret_decay_logit"] = lam_grad * jax.nn.sigmoid(-small_w["ret_decay_logit"][l].astype(F32))
    return loss_acc[0, 0], dX, big, small, dmods, dfinal


def _me():
    return lax.axis_index("x"), lax.axis_index("y"), lax.axis_index("c")


def _other_chips(x, y):
    return [(1 - x, y), (x, 1 - y), (1 - x, 1 - y)]


def _rcopy(src, dst, send_sem, recv_sem, dev):
    return pltpu.make_async_remote_copy(src_ref=src, dst_ref=dst, send_sem=send_sem, recv_sem=recv_sem,
                                        device_id=dev, device_id_type=MESH)


def _half(ref, axis, c):
    k, n = ref.shape
    if axis == 1:
        return ref.at[pl.ds(c * (k // 2), k // 2), :]
    return ref.at[:, pl.ds(c * (n // 2), n // 2)]


def _chip_part(ref, axis, j):
    k, n = ref.shape
    if axis == 1:
        return ref.at[:, pl.ds(j * (n // 4), n // 4)]
    return ref.at[pl.ds(j * (k // 4), k // 4), :]


def _piece(ref, axis, j, c):
    k, n = ref.shape
    if axis == 1:
        return ref.at[pl.ds(c * (k // 2), k // 2), pl.ds(j * (n // 4), n // 4)]
    return ref.at[pl.ds(j * (k // 4), k // 4), pl.ds(c * (n // 2), n // 2)]


def _gather_weights(fulls, axes):
    n = len(fulls)

    def body(*refs):
        outs = refs[n:2 * n]
        send, recv = refs[2 * n:]
        x, y, c = _me()
        j = 2 * x + y
        sib = (x, y, 1 - c)
        chips = _other_chips(x, y)
        first = []
        for t in range(n):
            for k, chip in enumerate(chips):
                own = _piece(outs[t], axes[t], j, c)
                first.append(_rcopy(own, own, send.at[6 * t + k], recv.at[6 * t + k], (*chip, c)))
                first[-1].start()
        passed = []
        for t in range(n):
            for k, chip in enumerate(chips):
                landed = _piece(outs[t], axes[t], 2 * chip[0] + chip[1], c)
                _rcopy(landed, landed, send.at[6 * t + k], recv.at[6 * t + k], sib).wait_recv()
                passed.append(_rcopy(landed, landed, send.at[6 * t + 3 + k], recv.at[6 * t + 3 + k], sib))
                passed[-1].start()
        for t in range(n):
            for k, chip in enumerate(chips):
                theirs = _piece(outs[t], axes[t], 2 * chip[0] + chip[1], 1 - c)
                _rcopy(theirs, theirs, send.at[6 * t + 3 + k], recv.at[6 * t + 3 + k], sib).wait_recv()
        for cp in first + passed:
            cp.wait_send()

    return pl.pallas_call(
        body, name="gather_weights",
        in_specs=[ANY] * n, out_specs=[ANY] * n,
        out_shape=[jax.ShapeDtypeStruct(f.shape, f.dtype) for f in fulls],
        input_output_aliases={t: t for t in range(n)},
        scratch_shapes=[pltpu.SemaphoreType.DMA((6 * n,)), pltpu.SemaphoreType.DMA((6 * n,))],
    )(*fulls)


def _rs_swap(grads, axes):
    n = len(grads)
    half_shapes = [(g.shape[0] // 2, g.shape[1]) if a == 1 else (g.shape[0], g.shape[1] // 2) for g, a in zip(grads, axes)]

    def body(*refs):
        ins, got = refs[:n], refs[n:2 * n]
        send, recv = refs[2 * n:]
        x, y, c = _me()
        out = [_rcopy(_half(ins[t], axes[t], 1 - c), got[t], send.at[t], recv.at[t], (x, y, 1 - c)) for t in range(n)]
        for cp in out:
            cp.start()
        for cp in out:
            cp.wait()

    return pl.pallas_call(
        body, name="rs_swap",
        in_specs=[ANY] * n, out_specs=[ANY] * n,
        out_shape=[jax.ShapeDtypeStruct(s, F32) for s in half_shapes],
        scratch_shapes=[pltpu.SemaphoreType.DMA((n,)), pltpu.SemaphoreType.DMA((n,))],
    )(*grads)


def _rs_scatter(parts, axes):
    n = len(parts)
    q_shapes = [(p.shape[0], p.shape[1] // 4) if a == 1 else (p.shape[0] // 4, p.shape[1]) for p, a in zip(parts, axes)]

    def body(*refs):
        ins, got = refs[:n], refs[n:4 * n]
        send, recv = refs[4 * n:]
        x, y, c = _me()
        out = []
        for t in range(n):
            for k, chip in enumerate(_other_chips(x, y)):
                out.append(_rcopy(_chip_part(ins[t], axes[t], 2 * chip[0] + chip[1]), got[3 * t + k], send.at[3 * t + k],
                                  recv.at[3 * t + k], (*chip, c)))
        for cp in out:
            cp.start()
        for cp in out:
            cp.wait()

    res = pl.pallas_call(
        body, name="rs_scatter",
        in_specs=[ANY] * n, out_specs=[ANY] * (3 * n),
        out_shape=[jax.ShapeDtypeStruct(s, p.dtype) for s, p in zip(q_shapes, parts) for _ in range(3)],
        scratch_shapes=[pltpu.SemaphoreType.DMA((3 * n,)), pltpu.SemaphoreType.DMA((3 * n,))],
    )(*parts)
    return [res[3 * t:3 * t + 3] for t in range(n)]


def _rs_share(shards, axes):
    n = len(shards)

    def body(*refs):
        outs = refs[n:2 * n]
        send, recv = refs[2 * n:]
        x, y, c = _me()
        sib = (x, y, 1 - c)
        out = []
        for t in range(n):
            mine = _half(outs[t], axes[t], c)
            out.append(_rcopy(mine, mine, send.at[t], recv.at[t], sib))
            out[-1].start()
        for t in range(n):
            out[t].wait_send()
            theirs = _half(outs[t], axes[t], 1 - c)
            _rcopy(theirs, theirs, send.at[t], recv.at[t], sib).wait_recv()

    return pl.pallas_call(
        body, name="rs_share",
        in_specs=[ANY] * n, out_specs=[ANY] * n,
        out_shape=[jax.ShapeDtypeStruct(s.shape, s.dtype) for s in shards],
        input_output_aliases={t: t for t in range(n)},
        scratch_shapes=[pltpu.SemaphoreType.DMA((n,)), pltpu.SemaphoreType.DMA((n,))],
    )(*shards)


def _cast_into_full(w, l, axis, sc):
    _, k, n = w.shape
    tr = _pick(k, 256, 16)
    if axis == 1:
        full, out_spec = (k, 4 * n), pl.BlockSpec((tr, n), lambda i, s: (i, s[1]))
    else:
        full, out_spec = (4 * k, n), pl.BlockSpec((tr, n), lambda i, s: (s[1] * (k // tr) + i, 0))

    def body(s_ref, w_ref, o_ref):
        o_ref[...] = w_ref[...].astype(BF16)

    return pl.pallas_call(
        body, name="cast_into_full",
        grid_spec=pltpu.PrefetchScalarGridSpec(
            num_scalar_prefetch=1, grid=(k // tr,),
            in_specs=[pl.BlockSpec((None, tr, n), lambda i, s: (l, i, 0))], out_specs=out_spec),
        out_shape=jax.ShapeDtypeStruct(full, BF16), compiler_params=_cparams(("arbitrary",)),
    )(sc, w)


def _rs_add2(g, got, axis, sc):
    k, n = g.shape
    hk, hn = (k // 2, n) if axis == 1 else (k, n // 2)
    tr = _pick(hk, max(16, (1 << 18) // hn), 16)
    if axis == 1:
        g_spec = pl.BlockSpec((tr, hn), lambda i, s: (s[0] * (hk // tr) + i, 0))
    else:
        g_spec = pl.BlockSpec((tr, hn), lambda i, s: (i, s[0]))
    blk = pl.BlockSpec((tr, hn), lambda i, s: (i, 0))

    def body(s_ref, a_ref, b_ref, o_ref):
        o_ref[...] = (a_ref[...] + b_ref[...]).astype(o_ref.dtype)

    return pl.pallas_call(
        body, name="rs_add2",
        grid_spec=pltpu.PrefetchScalarGridSpec(num_scalar_prefetch=1, grid=(hk // tr,), in_specs=[g_spec, blk], out_specs=blk),
        out_shape=jax.ShapeDtypeStruct((hk, hn), BF16), compiler_params=_cparams(("arbitrary",)),
    )(sc, g, got)


def _rs_add4(part, got3, axis, sc):
    k, n = part.shape
    qk, qn = (k, n // 4) if axis == 1 else (k // 4, n)
    tr = _pick(qk, max(16, (1 << 18) // qn), 16)
    if axis == 1:
        p_spec = pl.BlockSpec((tr, qn), lambda i, s: (i, s[1]))
        shard, o_spec = (2 * qk, qn), pl.BlockSpec((tr, qn), lambda i, s: (s[0] * (qk // tr) + i, 0))
    else:
        p_spec = pl.BlockSpec((tr, qn), lambda i, s: (s[1] * (qk // tr) + i, 0))
        shard, o_spec = (qk, 2 * qn), pl.BlockSpec((tr, qn), lambda i, s: (i, s[0]))
    blk = pl.BlockSpec((tr, qn), lambda i, s: (i, 0))

    def body(s_ref, p_ref, a_ref, b_ref, c_ref, o_ref):
        o_ref[...] = ((p_ref[...].astype(F32) + a_ref[...].astype(F32)) + b_ref[...].astype(F32)) + c_ref[...].astype(F32)

    return pl.pallas_call(
        body, name="rs_add4",
        grid_spec=pltpu.PrefetchScalarGridSpec(num_scalar_prefetch=1, grid=(qk // tr,), in_specs=[p_spec, blk, blk, blk],
                                               out_specs=o_spec),
        out_shape=jax.ShapeDtypeStruct(shard, F32), compiler_params=_cparams(("arbitrary",)),
    )(sc, part, *got3)


def _gather_small(v):
    def body(v_ref, out_ref, send, recv, loc):
        x, y, c = _me()
        sib = (x, y, 1 - c)
        chips = _other_chips(x, y)
        slot = lambda px, py, pc: out_ref.at[4 * px + 2 * py + pc]
        mine = pltpu.make_async_copy(v_ref, slot(x, y, c), loc)
        mine.start()
        first = [_rcopy(v_ref, slot(x, y, c), send.at[0], recv.at[0], sib)]
        first += [_rcopy(v_ref, slot(x, y, c), send.at[1 + k], recv.at[1 + k], (*chip, c)) for k, chip in enumerate(chips)]
        for cp in first:
            cp.start()
        passed = []
        for k, chip in enumerate(chips):
            landed = slot(*chip, c)
            _rcopy(landed, landed, send.at[1 + k], recv.at[1 + k], sib).wait_recv()
            passed.append(_rcopy(landed, landed, send.at[4 + k], recv.at[4 + k], sib))
            passed[-1].start()
        theirs = slot(x, y, 1 - c)
        _rcopy(theirs, theirs, send.at[0], recv.at[0], sib).wait_recv()
        for k, chip in enumerate(chips):
            theirs = slot(*chip, 1 - c)
            _rcopy(theirs, theirs, send.at[4 + k], recv.at[4 + k], sib).wait_recv()
        for cp in first + passed:
            cp.wait_send()
        mine.wait()

    return pl.pallas_call(
        body, name="gather_small",
        in_specs=[ANY], out_specs=ANY,
        out_shape=jax.ShapeDtypeStruct((8,) + v.shape, v.dtype),
        scratch_shapes=[pltpu.SemaphoreType.DMA((7,)), pltpu.SemaphoreType.DMA((7,)), pltpu.SemaphoreType.DMA],
    )(v)


def _adam_math(w, g, m, v):
    m = ADAM_B1 * m + (1.0 - ADAM_B1) * g
    v = ADAM_B2 * v + (1.0 - ADAM_B2) * (g * g)
    m_hat = m / (1.0 - ADAM_B1 ** ADAM_STEP)
    v_hat = v / (1.0 - ADAM_B2 ** ADAM_STEP)
    delta = -ADAM_LR * (m_hat / (jnp.sqrt(v_hat) + ADAM_EPS) + ADAM_WD * w)
    return delta, m, v


def _adam_layer(w, m, v, l, g, prev):
    L, k, n = w.shape
    tr = _pick(k, 128, 8)
    blk = pl.BlockSpec((None, tr, n), lambda i: (l, i, 0))

    def body(*refs):
        w_ref, m_ref, v_ref, g_ref = refs[:4]
        go, do, mo, vo = refs[-4:]
        gv = g_ref[...]
        d, m2, v2 = _adam_math(w_ref[...], gv, m_ref[...], v_ref[...])
        go[...] = gv
        do[...] = d
        mo[...] = m2
        vo[...] = v2

    args = [w, m, v, g]
    in_specs = [blk, blk, blk, pl.BlockSpec((tr, n), lambda i: (i, 0))]
    aliases = {}
    if prev is not None:
        for q, p in enumerate(prev):
            aliases[len(args)] = q
            in_specs.append(ANY)
            args.append(p)
    return pl.pallas_call(
        body, name="adam_layer", grid=(k // tr,),
        in_specs=in_specs, out_specs=[blk] * 4, out_shape=[jax.ShapeDtypeStruct((L, k, n), F32)] * 4,
        input_output_aliases=aliases, compiler_params=_cparams(("arbitrary",)),
    )(*args)


def _adam_flat(w, g, m, v):
    r = w.shape[0]
    tr = _pick(r, 512, 8)
    fn = lambda i, j, w, g, m, v: _adam_math(w, g, m, v)
    return _rowcall("adam_flat", fn, nrows=r, tr=tr, ins=[(a, "rows", 128, 0) for a in (w, g, m, v)],
                    outs=[(F32, 128, 128, 0, None)] * 3)


def _sum8(gathered):
    _, r, _ = gathered.shape
    tr = _pick(r, 512, 8)

    def body(g_ref, o_ref):
        acc = g_ref[0]
        for d in range(1, 8):
            acc = acc + g_ref[d]
        o_ref[...] = acc

    return pl.pallas_call(
        body, name="sum8", grid=(r // tr,),
        in_specs=[pl.BlockSpec((8, tr, 128), lambda i: (0, i, 0))], out_specs=pl.BlockSpec((tr, 128), lambda i: (i, 0)),
        out_shape=jax.ShapeDtypeStruct((r, 128), F32), compiler_params=_cparams(("arbitrary",)),
    )(gathered)


def _hdot(a, b, form="NN"):
    return _dot(a.astype(BF16), b.astype(BF16), form)


def _ada_fwd(s16, w_ada, l):
    _, d, ns = w_ada.shape
    tc = _pick(ns, 512)

    def body(s_ref, w_ref, o_ref):
        o_ref[...] = _hdot(s_ref[...], w_ref[...])

    return pl.pallas_call(
        body, name="ada_fwd", grid=(ns // tc,),
        in_specs=[pl.BlockSpec((16, d), lambda j: (0, 0)), pl.BlockSpec((None, d, tc), lambda j: (l, 0, j))],
        out_specs=pl.BlockSpec((16, tc), lambda j: (0, j)),
        out_shape=jax.ShapeDtypeStruct((16, ns), F32), compiler_params=_cparams(("arbitrary",)),
    )(s16, w_ada)


def _ada_bwd(s16t, dm, w_ada, l):
    _, d, ns = w_ada.shape
    tc = _pick(ns, 512)

    def body(st_ref, dm_ref, w_ref, dw_ref, ds_ref):
        j = pl.program_id(0)
        dw_ref[...] = _hdot(st_ref[...], dm_ref[...])
        part = _hdot(dm_ref[...], w_ref[...], "NT")

        @pl.when(j == 0)
        def _():
            ds_ref[...] = part

        @pl.when(j > 0)
        def _():
            ds_ref[...] += part

    return pl.pallas_call(
        body, name="ada_bwd", grid=(ns // tc,),
        in_specs=[pl.BlockSpec((d, 16), lambda j: (0, 0)), pl.BlockSpec((16, tc), lambda j: (0, j)),
                  pl.BlockSpec((None, d, tc), lambda j: (l, 0, j))],
        out_specs=[pl.BlockSpec((d, tc), lambda j: (0, j)), pl.BlockSpec((16, d), lambda j: (0, 0))],
        out_shape=[jax.ShapeDtypeStruct((d, ns), F32), jax.ShapeDtypeStruct((16, d), F32)],
        compiler_params=_cparams(("arbitrary",)),
    )(s16t, dm, w_ada)


def _pack(arrs):
    flat = jnp.concatenate([a.reshape(-1).astype(F32) for a in arrs])
    pad = (-flat.shape[0]) % 1024
    return jnp.pad(flat, (0, pad)).reshape(-1, 128)


def _unpack(flat2d, shapes):
    flat = flat2d.reshape(-1)
    out, pos = [], 0
    for s in shapes:
        size = int(np.prod(s))
        out.append(flat[pos:pos + size].reshape(s))
        pos += size
    return out


SMALL = ("norm1_g", "norm2_g", "ret_decay_logit", "ret_norm_g", "pool_w", "pool_scale", "sg_norm_g", "sg_w", "sg_b")


def kernel(x, c, ctx, c_ctx, w_ada, b_ada, norm1_g, w_in, ret_decay_logit, ret_norm_g, pool_w, pool_scale, sg_norm_g, sg_w, sg_b, w_br, w_bp, w_bs, w_out, norm2_g, w1, w2, final_norm_g, loss_target, m_c_ctx, m_w_ada, m_b_ada, m_norm1_g, m_w_in, m_ret_decay_logit, m_ret_norm_g, m_pool_w, m_pool_scale, m_sg_norm_g, m_sg_w, m_sg_b, m_w_br, m_w_bp, m_w_bs, m_w_out, m_norm2_g, m_w1, m_w2, m_final_norm_g, v_c_ctx, v_w_ada, v_b_ada, v_norm1_g, v_w_in, v_ret_decay_logit, v_ret_norm_g, v_pool_w, v_pool_scale, v_sg_norm_g, v_sg_w, v_sg_b, v_w_br, v_w_bp, v_w_bs, v_w_out, v_norm2_g, v_w1, v_w2, v_final_norm_g):
    P = dict(c_ctx=c_ctx, w_ada=w_ada, b_ada=b_ada, norm1_g=norm1_g, w_in=w_in, ret_decay_logit=ret_decay_logit,
             ret_norm_g=ret_norm_g, pool_w=pool_w, pool_scale=pool_scale, sg_norm_g=sg_norm_g, sg_w=sg_w, sg_b=sg_b, w_br=w_br,
             w_bp=w_bp, w_bs=w_bs, w_out=w_out, norm2_g=norm2_g, w1=w1, w2=w2, final_norm_g=final_norm_g)
    Mo = dict(c_ctx=m_c_ctx, w_ada=m_w_ada, b_ada=m_b_ada, norm1_g=m_norm1_g, w_in=m_w_in, ret_decay_logit=m_ret_decay_logit,
              ret_norm_g=m_ret_norm_g, pool_w=m_pool_w, pool_scale=m_pool_scale, sg_norm_g=m_sg_norm_g, sg_w=m_sg_w, sg_b=m_sg_b,
              w_br=m_w_br, w_bp=m_w_bp, w_bs=m_w_bs, w_out=m_w_out, norm2_g=m_norm2_g, w1=m_w1, w2=m_w2,
              final_norm_g=m_final_norm_g)
    Vo = dict(c_ctx=v_c_ctx, w_ada=v_w_ada, b_ada=v_b_ada, norm1_g=v_norm1_g, w_in=v_w_in, ret_decay_logit=v_ret_decay_logit,
              ret_norm_g=v_ret_norm_g, pool_w=v_pool_w, pool_scale=v_pool_scale, sg_norm_g=v_sg_norm_g, sg_w=v_sg_w, sg_b=v_sg_b,
              w_br=v_w_br, w_bp=v_w_bp, w_bs=v_w_bs, w_out=v_w_out, norm2_g=v_norm2_g, w1=v_w1, w2=v_w2,
              final_norm_g=v_final_norm_g)
    names = ("c_ctx", "w_ada", "b_ada", "norm1_g", "w_in", "ret_decay_logit", "ret_norm_g", "pool_w", "pool_scale", "sg_norm_g",
             "sg_w", "sg_b", "w_br", "w_bp", "w_bs", "w_out", "norm2_g", "w1", "w2", "final_norm_g")
    L, D = w_in.shape[0], x.shape[-1]
    T, CTX = x.shape[1], ctx.shape[1]
    cfg = _Cfg(D, T, CTX, 4 * w1.shape[2])
    ns_ada = w_ada.shape[2]
    mx, my, mc = _me()
    dev = 4 * mx + 2 * my + mc
    chip = 2 * mx + my

    silu_cc = jax.nn.silu(c_ctx)
    silu_all = _gather_small(jax.nn.silu(c).reshape(-1, 128)).reshape(8, D)
    s16 = jnp.concatenate([silu_cc[None], silu_all, jnp.zeros((7, D), F32)], axis=0)
    proj = jnp.stack([_ada_fwd(s16, w_ada, l) for l in range(L)])
    proj_all = _gather_small(proj.reshape(-1, 128)).reshape(8, L, 16, ns_ada)
    mods_full = jnp.concatenate([proj_all[2 * j] for j in range(4)], axis=-1) + b_ada[:, None, :]
    mods = [jnp.concatenate([mods_full[l, 0:1], lax.dynamic_slice_in_dim(mods_full[l], 1 + dev, 1, axis=0)], axis=0)
            for l in range(L)]

    axes = [SHARD_AXIS[k] for k in BIG]

    sc = jnp.stack([mc, chip]).astype(jnp.int32)

    def weights_of(l):
        full = _gather_weights([_cast_into_full(P[k], l, SHARD_AXIS[k], sc) for k in BIG], axes)
        return dict(zip(BIG, full))

    X0 = jnp.concatenate([ctx[0], x[0]], axis=0)
    small_w = {k: P[k] for k in SMALL}
    loss_part, dX, big, small, dmods, dfinal = _local_fwd_bwd(cfg, X0, loss_target[0], mods, weights_of, small_w, final_norm_g)
    loss = lax.psum(loss_part, ("x", "y", "c"))
    grad_x = dX[CTX:][None]

    outs = {k: None for k in BIG}
    for l in range(L):
        grads = [big[l][k] for k in BIG]
        got = _rs_swap(grads, axes)
        parts = [_rs_add2(g, s, a, sc) for g, s, a in zip(grads, got, axes)]
        theirs = _rs_scatter(parts, axes)
        shard_g = _rs_share([_rs_add4(p, th, a, sc) for p, th, a in zip(parts, theirs, axes)], axes)
        for k, g in zip(BIG, shard_g):
            outs[k] = _adam_layer(P[k], Mo[k], Vo[k], l, g, outs[k])

    per_layer = [[small[l][k] for k in SMALL] + [dmods[l][1], dmods[l][0]] for l in range(L)]
    payload = _pack([a for lay in per_layer for a in lay] + [dfinal])
    gathered = _gather_small(payload)
    total = _sum8(gathered)
    shapes = [P[k].shape[1:] for k in SMALL] + [(6 * D,), (6 * D,)]
    tot = _unpack(total, shapes * L + [(D,)])
    per = len(shapes)
    g_small = {k: jnp.stack([tot[l * per + q] for l in range(L)]) for q, k in enumerate(SMALL)}
    dmx_sum = jnp.stack([tot[l * per + per - 2] for l in range(L)])
    dmc_sum = jnp.stack([tot[l * per + per - 1] for l in range(L)])
    g_small["b_ada"] = dmx_sum + dmc_sum
    g_small["final_norm_g"] = tot[-1]
    offs = np.cumsum([0] + [int(np.prod(s)) for s in shapes])
    lay_size = int(offs[-1])
    gflat = gathered.reshape(8, -1)
    s16t = s16.T
    ada_out, ds_part = None, jnp.zeros((16, D), F32)
    for l in range(L):
        dmx_all = gflat[:, l * lay_size + int(offs[per - 2]):l * lay_size + int(offs[per - 1])]
        dm_full = jnp.concatenate([dmc_sum[l][None], dmx_all, jnp.zeros((7, 6 * D), F32)], axis=0)
        dm = lax.dynamic_slice_in_dim(dm_full, chip * ns_ada, ns_ada, axis=1)
        dw, ds = _ada_bwd(s16t, dm, w_ada, l)
        ds_part = ds_part + ds
        ada_out = _adam_layer(w_ada, Mo["w_ada"], Vo["w_ada"], l, dw, ada_out)
    outs["w_ada"] = ada_out
    ds_all = _gather_small(ds_part[0].reshape(-1, 128)).reshape(8, D)
    d_silu_cc = ds_all[0] + ds_all[2] + ds_all[4] + ds_all[6]
    g_small["c_ctx"] = jax.vjp(jax.nn.silu, c_ctx)[1](d_silu_cc)[0]

    small_names = [k for k in names if k not in BIG and k != "w_ada"]
    sm_shapes = [P[k].shape for k in small_names]
    res = _adam_flat(_pack([P[k] for k in small_names]), _pack([g_small[k] for k in small_names]),
                     _pack([Mo[k] for k in small_names]), _pack([Vo[k] for k in small_names]))
    d_s, m_s, v_s = [_unpack(r, sm_shapes) for r in res]
    for q, k in enumerate(small_names):
        outs[k] = (g_small[k].reshape(P[k].shape), d_s[q], m_s[q], v_s[q])

    return (loss, grad_x, *[outs[k][0] for k in names], *[outs[k][1] for k in names], *[outs[k][2] for k in names],
            *[outs[k][3] for k in names])
```

```python
import functools

import numpy as np
import jax
import jax.numpy as jnp
from jax import lax
from jax.experimental import pallas as pl
from jax.experimental.pallas import tpu as pltpu

F32 = jnp.float32
BF16 = jnp.bfloat16
EPS = 1e-6
CH = 128
GRID_W = 64
ROPE_THETA = 10000.0
POOL_WINDOWS = (2, 4, 8, 16)
PT = 256
VMEM_LIMIT = 56 * 1024 * 1024
MESH = pl.DeviceIdType.MESH
ANY = pl.BlockSpec(memory_space=pl.ANY)

ADAM_LR = 0.001
ADAM_B1 = 0.9
ADAM_B2 = 0.999
ADAM_EPS = 1e-08
ADAM_WD = 0.01
ADAM_STEP = 10

BIG = ("w_in", "w_br", "w_bp", "w_bs", "w_out", "w1", "w2")
SHARD_AXIS = {"w_in": 1, "w_br": 1, "w_bp": 1, "w_bs": 1, "w_out": 0, "w1": 1, "w2": 0}


def _pick(dim, pref, mult=128):
    best = None
    for t in range(mult, min(dim, pref) + 1, mult):
        if dim % t == 0:
            best = t
    return dim if best is None else best


def _cparams(sem=None):
    return pltpu.CompilerParams(dimension_semantics=sem, vmem_limit_bytes=VMEM_LIMIT)


def _rows(i, tr):
    return i * tr + lax.broadcasted_iota(jnp.int32, (tr, 1), 0)


def _sel(i, tr, n_ctx, v2):
    return jnp.where(_rows(i, tr) < n_ctx, v2[0:1, :], v2[1:2, :])


def _seg_sums(i, tr, n_ctx, d):
    is_ctx = _rows(i, tr) < n_ctx
    s_c = jnp.sum(jnp.where(is_ctx, d, 0.0), axis=0, keepdims=True)
    s_x = jnp.sum(jnp.where(is_ctx, 0.0, d), axis=0, keepdims=True)
    two = lax.broadcasted_iota(jnp.int32, (2, d.shape[1]), 0)
    return jnp.where(two == 0, s_c, s_x)


def _dot(a, b, form="NN"):
    dims = {"NN": (((1,), (0,)), ((), ())), "NT": (((1,), (1,)), ((), ())), "TN": (((0,), (0,)), ((), ()))}[form]
    return lax.dot_general(a, b, dims, preferred_element_type=F32)


def _gelu(x):
    return 0.5 * x * (1.0 + jnp.tanh(0.7978845608028654 * (x + 0.044715 * x * x * x)))


def _matmul(name, form, pairs, *, R, C, tr, tc, tk, nk, out_dtypes, epi, extras=(), a_pro=None, comm=None):
    npair, nex, nout = len(pairs), len(extras), len(out_dtypes)
    in_specs, args = [], []
    for a, b in pairs:
        if nk == 1:
            ka = a.shape[0] if form == "TN" else a.shape[1]
        else:
            ka = tk
        if form == "NN":
            in_specs += [pl.BlockSpec((tr, ka), lambda i, j, k: (i, k)), pl.BlockSpec((ka, tc), lambda i, j, k: (k, j))]
        elif form == "NT":
            in_specs += [pl.BlockSpec((tr, ka), lambda i, j, k: (i, k)), pl.BlockSpec((tc, ka), lambda i, j, k: (j, k))]
        else:
            in_specs += [pl.BlockSpec((ka, tr), lambda i, j, k: (k, i)), pl.BlockSpec((ka, tc), lambda i, j, k: (k, j))]
        args += [a, b]
    for kind, arr, off in extras:
        if kind == "tile":
            in_specs.append(pl.BlockSpec((tr, tc), lambda i, j, k, off=off: (i, j + off)))
        else:
            in_specs.append(pl.BlockSpec((2, tc), lambda i, j, k, off=off: (0, j + off)))
        args.append(arr)

    direct = epi is None
    n_acc = 0 if (nk == 1 or direct) else npair
    n_main = len(args)
    ni, nj = R // tr, C // tc
    aliases = {}
    out_specs = [pl.BlockSpec((tr, tc), lambda i, j, k: (i, j)) for _ in out_dtypes]
    out_shape = [jax.ShapeDtypeStruct((R, C), dt) for dt in out_dtypes]
    scratch = [pltpu.VMEM((tr, tc), F32) for _ in range(n_acc)]
    n_cin = n_cout = 0
    if comm is not None:
        n_cin, n_cout = len(comm.ins), len(comm.outs)
        in_specs = in_specs + [ANY] * n_cin
        args = args + list(comm.ins)
        out_specs = out_specs + [ANY] * n_cout
        out_shape = out_shape + list(comm.outs)
        aliases = {n_main + a: nout + b for a, b in comm.aliases.items()}
        scratch = scratch + [pltpu.SemaphoreType.DMA((comm.nsem,)), pltpu.SemaphoreType.DMA((comm.nsem,))]

    def body(*refs):
        ab = refs[:2 * npair]
        ex = refs[2 * npair:n_main]
        cin = refs[n_main:n_main + n_cin]
        outs = refs[n_main + n_cin:n_main + n_cin + nout]
        cout = refs[n_main + n_cin + nout:n_main + n_cin + nout + n_cout]
        accs = refs[n_main + n_cin + nout + n_cout:n_main + n_cin + nout + n_cout + n_acc]
        sems = refs[n_main + n_cin + nout + n_cout + n_acc:]
        i, j, k = pl.program_id(0), pl.program_id(1), pl.program_id(2)

        if comm is not None:
            @pl.when(jnp.logical_and(jnp.logical_and(i == 0, j == 0), k == 0))
            def _():
                comm.start(cin, cout, *sems)

        def products():
            res = []
            for p in range(npair):
                a = ab[2 * p][...]
                if a_pro is not None:
                    a = a_pro(a)
                res.append(_dot(a, ab[2 * p + 1][...], form))
            return res

        def finish(vals):
            res = epi(vals, [e[...] for e in ex], i)
            for o, v in zip(outs, res):
                o[...] = v.astype(o.dtype)

        if direct:
            prod = products()[0]
            if nk == 1:
                outs[0][...] = prod
            else:
                @pl.when(k == 0)
                def _():
                    outs[0][...] = prod

                @pl.when(k > 0)
                def _():
                    outs[0][...] += prod
        elif nk == 1:
            finish(products())
        else:
            prods = products()

            @pl.when(k == 0)
            def _():
                for acc, v in zip(accs, prods):
                    acc[...] = v

            @pl.when(k > 0)
            def _():
                for acc, v in zip(accs, prods):
                    acc[...] += v

            @pl.when(k == nk - 1)
            def _():
                finish([acc[...] for acc in accs])

        if comm is not None:
            @pl.when(jnp.logical_and(jnp.logical_and(i == ni - 1, j == nj - 1), k == nk - 1))
            def _():
                comm.finish(cin, cout, *sems)

    res = pl.pallas_call(
        body, name=name, grid=(ni, nj, nk),
        in_specs=in_specs, out_specs=out_specs, out_shape=out_shape, scratch_shapes=scratch,
        input_output_aliases=aliases,
        compiler_params=_cparams(("arbitrary", "arbitrary", "arbitrary")),
    )(*args)
    return res if comm is None else (res[:nout], res[nout:])


def _rowcall(name, fn, *, nrows, tr, ins, outs, accs=(), ncol=1):
    n_in, n_out, n_acc = len(ins), len(outs), len(accs)
    in_specs, args = [], []
    for arr, kind, w, off in ins:
        if kind == "rows":
            in_specs.append(pl.BlockSpec((tr, w), lambda i, j, off=off: (i, off + j)))
        elif kind == "full":
            in_specs.append(pl.BlockSpec(arr.shape, lambda i, j, nd=arr.ndim: (0,) * nd))
        else:
            in_specs.append(pl.BlockSpec((None,) + arr.shape[1:], lambda i, j, f=off, nd=arr.ndim: (f(i),) + (0,) * (nd - 1)))
        args.append(arr)
    aliases = {}
    out_specs, out_shape = [], []
    for o_idx, (dt, total, w, off, alias) in enumerate(outs):
        out_specs.append(pl.BlockSpec((tr, w), lambda i, j, off=off: (i, off + j)))
        out_shape.append(jax.ShapeDtypeStruct((nrows, total), dt))
        if alias is not None:
            aliases[len(args)] = o_idx
            in_specs.append(ANY)
            args.append(alias)
    n_alias = len(aliases)
    for shp in accs:
        out_specs.append(pl.BlockSpec(shp, lambda i, j, nd=len(shp): (0,) * nd))
        out_shape.append(jax.ShapeDtypeStruct(shp, F32))

    def body(*refs):
        in_refs = refs[:n_in]
        out_refs = refs[n_in + n_alias:n_in + n_alias + n_out]
        acc_refs = refs[n_in + n_alias + n_out:]
        i, j = pl.program_id(0), pl.program_id(1)
        res = fn(i, j, *[r[...] for r in in_refs])
        for o, v in zip(out_refs, res[:n_out]):
            o[...] = v.astype(o.dtype)
        first = jnp.logical_and(i == 0, j == 0)
        for acc, v in zip(acc_refs, res[n_out:]):
            @pl.when(first)
            def _(acc=acc, v=v):
                acc[...] = v

            @pl.when(jnp.logical_not(first))
            def _(acc=acc, v=v):
                acc[...] += v

    res = pl.pallas_call(
        body, name=name, grid=(nrows // tr, ncol),
        in_specs=in_specs, out_specs=out_specs, out_shape=out_shape,
        input_output_aliases=aliases,
        compiler_params=_cparams(("arbitrary", "arbitrary")),
    )(*args)
    return res


def _f_normmod(x, g, shift, scale):
    r = lax.rsqrt(jnp.mean(x * x, axis=-1, keepdims=True) + EPS)
    return (x * r * g) * (1.0 + scale) + shift


def _f_headnorm_gate(o, zg, ng):
    r = lax.rsqrt(jnp.mean(o * o, axis=-1, keepdims=True) + EPS)
    return (o * r * ng) * (zg * jax.nn.sigmoid(zg))


def _f_sgv(sv, g):
    v = _gelu(sv)
    r = lax.rsqrt(jnp.mean(v * v, axis=-1, keepdims=True) + EPS)
    return v * r * g


def _rope(t, cos, sin):
    w = t.shape[1]
    lane = lax.broadcasted_iota(jnp.int32, t.shape, 1)
    swapped = jnp.where(jnp.bitwise_and(lane, 63) < 32,pltpu.roll(t, w - 32, 1), pltpu.roll(t, 32, 1))
    return t * cos + swapped * sin


def _rope_t(d, cos, sin):
    w = d.shape[1]
    lane = lax.broadcasted_iota(jnp.int32, d.shape, 1)
    ds = d * sin
    swapped = jnp.where(jnp.bitwise_and(lane, 63) < 32,pltpu.roll(ds, w - 32, 1), pltpu.roll(ds, 32, 1))
    return d * cos + swapped


class _Cfg:
    def __init__(self, D, T, CTX, DFF):
        self.D, self.T, self.CTX, self.DFF = D, T, CTX, DFF
        self.TA = T + CTX
        self.RW = D // 2
        self.H = self.RW // CH
        self.PW = D // 4
        self.SW = D // 4
        self.NIN = 4 * self.RW + self.PW + 2 * self.SW + 3 * D
        self.NC = self.TA // CH
        self.NCC = CTX // CH
        self.k_scale = float(CH) ** -0.5
        self.tr = _pick(self.TA, 768)
        self.o_g = 3 * self.RW
        self.o_p = 4 * self.RW
        self.o_u = self.o_p + self.PW
        self.o_sv = self.o_u + self.SW
        self.o_gate = self.o_sv + self.SW


def _normmod_fwd(cfg, X, g, mods, i_shift, i_scale, name):
    D = cfg.D
    tr = _pick(cfg.TA, 384)

    def fn(i, j, x, g, m):
        sh = _sel(i, tr, cfg.CTX, m[:, i_shift * D:(i_shift + 1) * D])
        sc = _sel(i, tr, cfg.CTX, m[:, i_scale * D:(i_scale + 1) * D])
        return (_f_normmod(x, g, sh, sc),)

    return _rowcall(name, fn, nrows=cfg.TA, tr=tr, ins=[(X, "rows", D, 0), (g, "full", 0, 0), (mods, "full", 0, 0)],
                    outs=[(BF16, D, D, 0, None)])[0]


def _normmod_bwd(cfg, X, dH, dres, g, mods, i_shift, i_scale, name):
    D = cfg.D
    tr = _pick(cfg.TA, 384)

    def fn(i, j, x, dh, dr, g, m):
        sh = _sel(i, tr, cfg.CTX, m[:, i_shift * D:(i_shift + 1) * D])
        sc = _sel(i, tr, cfg.CTX, m[:, i_scale * D:(i_scale + 1) * D])
        _, vjp = jax.vjp(_f_normmod, x, g, sh, sc)
        dx, dg, dsh, dsc = vjp(dh)
        return dr + dx, dg, _seg_sums(i, tr, cfg.CTX, dsh), _seg_sums(i, tr, cfg.CTX, dsc)

    return _rowcall(name, fn, nrows=cfg.TA, tr=tr,
                    ins=[(X, "rows", D, 0), (dH, "rows", D, 0), (dres, "rows", D, 0), (g, "full", 0, 0), (mods, "full", 0, 0)],
                    outs=[(F32, D, D, 0, None)], accs=[(1, D), (2, D), (2, D)])


def _resgate_bwd(cfg, dX, M, mods, i_gate, name):
    D = cfg.D
    tr = _pick(cfg.TA, 384)

    def fn(i, j, dx, m, mm):
        gate = _sel(i, tr, cfg.CTX, mm[:, i_gate * D:(i_gate + 1) * D])
        return dx * gate, _seg_sums(i, tr, cfg.CTX, dx * m.astype(F32))

    return _rowcall(name, fn, nrows=cfg.TA, tr=tr, ins=[(dX, "rows", D, 0), (M, "rows", D, 0), (mods, "full", 0, 0)],
                    outs=[(BF16, D, D, 0, None)], accs=[(2, D)])


def _chunk_of(cfg, d, t):
    fwd = t
    bwd = jnp.where(t < cfg.NCC, cfg.NCC - 1 - t, cfg.NC - 1 - t + cfg.NCC)
    return jnp.where(d == 0, fwd, bwd)


def _ret_specs(cfg, cm):
    RW, H = cfg.RW, cfg.H
    return [
        pl.BlockSpec((CH, RW), lambda d, t: (cm(d, t), 0)),
        pl.BlockSpec((CH, RW), lambda d, t: (cm(d, t), 1)),
        pl.BlockSpec((CH, RW), lambda d, t: (cm(d, t), 2)),
        pl.BlockSpec((CH, CH), lambda d, t: (cm(d, t), 0)),
        pl.BlockSpec((CH, CH), lambda d, t: (cm(d, t), 0)),
        pl.BlockSpec((None, H, CH, CH), lambda d, t: (d, 0, 0, 0)),
        pl.BlockSpec((None, CH, RW), lambda d, t: (d, 0, 0)),
        pl.BlockSpec((None, CH, RW), lambda d, t: (d, 0, 0)),
        pl.BlockSpec((None, 1, RW), lambda d, t: (d, 0, 0)),
    ]


def _ret_prep(cfg, q_ref, k_ref, ct_ref, st_ref, qd_ref, kd_ref):
    cos = jnp.tile(ct_ref[...], (1, cfg.H))
    sin = jnp.tile(st_ref[...], (1, cfg.H))
    qr = _rope(q_ref[...].astype(F32), cos, sin)
    kr = _rope(k_ref[...].astype(F32) * cfg.k_scale, cos, sin)
    return qr, kr, (qr * qd_ref[...]).astype(BF16), (kr * kd_ref[...]).astype(BF16)


def _ret_fwd(cfg, Z, tabs, decs):
    RW, H, TA, NC = cfg.RW, cfg.H, cfg.TA, cfg.NC
    cm = functools.partial(_chunk_of, cfg)

    def body(q_ref, k_ref, v_ref, ct_ref, st_ref, dm_ref, qd_ref, kd_ref, cd_ref, o_ref, so_ref, S):
        @pl.when(pl.program_id(1) == 0)
        def _():
            S[...] = jnp.zeros_like(S)

        qr, kr, qd, kd = _ret_prep(cfg, q_ref, k_ref, ct_ref, st_ref, qd_ref, kd_ref)
        qb, kb = qr.astype(BF16), kr.astype(BF16)
        v = v_ref[...].astype(BF16)
        for h in range(H):
            sl = slice(h * CH, (h + 1) * CH)
            p = (_dot(qb[:, sl], kb[:, sl], "NT") * dm_ref[h]).astype(BF16)
            s_h = S[h]
            so_ref[h] = s_h
            o_ref[:, sl] = _dot(p, v[:, sl]) + _dot(qd[:, sl], s_h.astype(BF16))
            S[h] = s_h * cd_ref[:, sl] + _dot(kd[:, sl], v[:, sl], "TN")

    return pl.pallas_call(
        body, name="ret_fwd", grid=(2, NC),
        in_specs=_ret_specs(cfg, cm),
        out_specs=[pl.BlockSpec((None, CH, RW), lambda d, t: (d, cm(d, t), 0)),
                   pl.BlockSpec((None, None, H, CH, CH), lambda d, t: (d, cm(d, t), 0, 0, 0))],
        out_shape=[jax.ShapeDtypeStruct((2, TA, RW), F32), jax.ShapeDtypeStruct((2, NC, H, CH, CH), F32)],
        scratch_shapes=[pltpu.VMEM((H, CH, CH), F32)],
        compiler_params=_cparams(("arbitrary", "arbitrary")),
    )(Z, Z, Z, tabs["cos"], tabs["sin"], decs["dmat"], decs["qdec"], decs["kdec"], decs["cdec"])


def _ret_bwd(cfg, Z, tabs, decs, states, dO):
    RW, H, TA, NC = cfg.RW, cfg.H, cfg.TA, cfg.NC

    def cm(d, t):
        return _chunk_of(cfg, d, NC - 1 - t)

    def body(q_ref, k_ref, v_ref, ct_ref, st_ref, dm_ref, qd_ref, kd_ref, cd_ref, s_ref, do_ref, w_ref,
             dqkv_ref, dl_ref, dS):
        t = pl.program_id(1)

        @pl.when(t == 0)
        def _():
            dS[...] = jnp.zeros_like(dS)
            dl_ref[...] = jnp.zeros_like(dl_ref)

        qr, kr, qd, kd = _ret_prep(cfg, q_ref, k_ref, ct_ref, st_ref, qd_ref, kd_ref)
        qb, kb = qr.astype(BF16), kr.astype(BF16)
        v = v_ref[...].astype(BF16)
        dob = do_ref[...].astype(BF16)
        for h in range(H):
            sl = slice(h * CH, (h + 1) * CH)
            dm = dm_ref[h]
            p = (_dot(qb[:, sl], kb[:, sl], "NT") * dm).astype(BF16)
            dp = (_dot(dob[:, sl], v[:, sl], "NT") * dm).astype(BF16)
            s_h = s_ref[h]
            ds_h = dS[h]
            sb, dsb = s_h.astype(BF16), ds_h.astype(BF16)
            dq_i = _dot(dp, kb[:, sl])
            dk_i = _dot(dp, qb[:, sl], "TN")
            dq_c = _dot(dob[:, sl], sb, "NT") * qd_ref[:, sl]
            dk_s = _dot(v[:, sl], dsb, "NT") * kd_ref[:, sl]
            dqkv_ref[:, sl] = dq_i + dq_c
            dqkv_ref[:, RW + h * CH:RW + (h + 1) * CH] = dk_i + dk_s
            dqkv_ref[:, 2 * RW + h * CH:2 * RW + (h + 1) * CH] = _dot(p, dob[:, sl], "TN") + _dot(kd[:, sl], dsb)
            qh, kh = qr[:, sl], kr[:, sl]
            lam = w_ref[0] * (qh * dq_i) + w_ref[1] * (qh * dq_c) + w_ref[2] * (kh * dk_i) + w_ref[3] * (kh * dk_s)
            lam_s = float(CH) * cd_ref[:, sl] * jnp.sum(ds_h * s_h, axis=0, keepdims=True)
            dl_ref[:, sl] += jnp.sum(lam, axis=0, keepdims=True) + lam_s
            dS[h] = ds_h * cd_ref[:, sl] + _dot(qd[:, sl], dob[:, sl], "TN")

    in_specs = _ret_specs(cfg, cm) + [
        pl.BlockSpec((None, None, H, CH, CH), lambda d, t: (d, cm(d, t), 0, 0, 0)),
        pl.BlockSpec((CH, RW), lambda d, t: (cm(d, t), 0)),
        pl.BlockSpec((None, 4, CH, CH), lambda d, t: (d, 0, 0, 0)),
    ]
    return pl.pallas_call(
        body, name="ret_bwd", grid=(2, NC),
        in_specs=in_specs,
        out_specs=[pl.BlockSpec((None, CH, 3 * RW), lambda d, t: (d, cm(d, t), 0)),
                   pl.BlockSpec((None, 1, RW), lambda d, t: (d, 0, 0))],
        out_shape=[jax.ShapeDtypeStruct((2, TA, 3 * RW), F32), jax.ShapeDtypeStruct((2, 1, RW), F32)],
        scratch_shapes=[pltpu.VMEM((H, CH, CH), F32)],
        compiler_params=_cparams(("arbitrary", "arbitrary")),
    )(Z, Z, Z, tabs["cos"], tabs["sin"], decs["dmat"], decs["qdec"], decs["kdec"], decs["cdec"], states, dO,
      tabs["lamw"])


def _rope_bwd(cfg, dqkv2, tabs, dz):
    RW, H = cfg.RW, cfg.H
    tr = PT

    def body(d0, d1, ct, st, dz_in, o):
        cos, sin = jnp.tile(ct[...], (1, H)), jnp.tile(st[...], (1, H))
        d = d0[...] + d1[...]
        o[:, :RW] = _rope_t(d[:, :RW], cos, sin).astype(o.dtype)
        o[:, RW:2 * RW] = (_rope_t(d[:, RW:2 * RW], cos, sin) * cfg.k_scale).astype(o.dtype)
        o[:, 2 * RW:] = d[:, 2 * RW:].astype(o.dtype)

    return pl.pallas_call(
        body, name="rope_bwd", grid=(cfg.TA // tr,),
        in_specs=[pl.BlockSpec((None, tr, 3 * RW), lambda i: (0, i, 0)), pl.BlockSpec((None, tr, 3 * RW), lambda i: (1, i, 0)),
                  pl.BlockSpec((tr, CH), lambda i: (i, 0)), pl.BlockSpec((tr, CH), lambda i: (i, 0)), ANY],
        out_specs=pl.BlockSpec((tr, 3 * RW), lambda i: (i, 0)),
        out_shape=jax.ShapeDtypeStruct((cfg.TA, cfg.NIN), BF16),
        input_output_aliases={4: 0},
        compiler_params=_cparams(("arbitrary",)),
    )(dqkv2, dqkv2, tabs["cos"], tabs["sin"], dz)


def _retout_fwd(cfg, o2, Z, ng):
    RW, H = cfg.RW, cfg.H
    tr = PT

    def body(o0, o1, zg, ng, out):
        o = o0[...] + o1[...]
        z = zg[...].astype(F32)
        for h in range(H):
            sl = slice(h * CH, (h + 1) * CH)
            out[:, sl] = _f_headnorm_gate(o[:, sl], z[:, sl], ng[:, sl]).astype(out.dtype)

    return pl.pallas_call(
        body, name="retout_fwd", grid=(cfg.TA // tr,),
        in_specs=[pl.BlockSpec((None, tr, RW), lambda i: (0, i, 0)), pl.BlockSpec((None, tr, RW), lambda i: (1, i, 0)),
                  pl.BlockSpec((tr, RW), lambda i: (i, 3)), pl.BlockSpec((1, RW), lambda i: (0, 0))],
        out_specs=pl.BlockSpec((tr, RW), lambda i: (i, 0)),
        out_shape=jax.ShapeDtypeStruct((cfg.TA, RW), BF16),
        compiler_params=_cparams(("arbitrary",)),
    )(o2, o2, Z, ng)


def _retout_bwd(cfg, o2, Z, ng, dret, dz):
    RW, H = cfg.RW, cfg.H
    tr = PT

    def body(o0, o1, zg, ng, dr, dz_in, do_out, dz_out, dng):
        i = pl.program_id(0)
        o = o0[...] + o1[...]
        z = zg[...].astype(F32)
        d = dr[...]
        acc = []
        for h in range(H):
            sl = slice(h * CH, (h + 1) * CH)
            _, vjp = jax.vjp(_f_headnorm_gate, o[:, sl], z[:, sl], ng[:, sl])
            do_h, dz_h, dg_h = vjp(d[:, sl])
            do_out[:, sl] = do_h
            dz_out[:, sl] = dz_h.astype(dz_out.dtype)
            acc.append(dg_h)

        @pl.when(i == 0)
        def _():
            for h in range(H):
                dng[:, h * CH:(h + 1) * CH] = acc[h]

        @pl.when(i > 0)
        def _():
            for h in range(H):
                dng[:, h * CH:(h + 1) * CH] += acc[h]

    return pl.pallas_call(
        body, name="retout_bwd", grid=(cfg.TA // tr,),
        in_specs=[pl.BlockSpec((None, tr, RW), lambda i: (0, i, 0)), pl.BlockSpec((None, tr, RW), lambda i: (1, i, 0)),
                  pl.BlockSpec((tr, RW), lambda i: (i, 3)), pl.BlockSpec((1, RW), lambda i: (0, 0)),
                  pl.BlockSpec((tr, RW), lambda i: (i, 0)), ANY],
        out_specs=[pl.BlockSpec((tr, RW), lambda i: (i, 0)), pl.BlockSpec((tr, RW), lambda i: (i, 3)),
                   pl.BlockSpec((1, RW), lambda i: (0, 0))],
        out_shape=[jax.ShapeDtypeStruct((cfg.TA, RW), F32), jax.ShapeDtypeStruct((cfg.TA, cfg.NIN), BF16),
                   jax.ShapeDtypeStruct((1, RW), F32)],
        input_output_aliases={5: 1},
        compiler_params=_cparams(("arbitrary",)),
    )(o2, o2, Z, ng, dret, dz)


def _pool_consts(ctx_len):
    assert ctx_len == PT
    bm = np.zeros((2, len(POOL_WINDOWS), PT, PT), np.float32)
    ic = np.zeros((2, len(POOL_WINDOWS), PT, CH), np.float32)
    for ty, seg in enumerate((ctx_len, GRID_W)):
        for gi, w in enumerate(POOL_WINDOWS):
            for r in range(PT):
                s0, pos = (r // seg) * seg, r % seg
                lo, hi = max(pos - w // 2, 0), min(pos + w // 2 - 1, seg - 1)
                bm[ty, gi, r, s0 + lo:s0 + hi + 1] = 1.0
                ic[ty, gi, r, :] = 1.0 / (hi - lo + 1)
    return jnp.asarray(bm, BF16), jnp.asarray(ic, F32)


def _pool_tile(p, bm, ic, pw, g):
    sl = slice(g * CH, (g + 1) * CH)
    pg = p[:, sl].astype(F32)
    hi = pg.astype(BF16)
    lo = (pg - hi.astype(F32)).astype(BF16)
    y = (_dot(bm[g], hi) + _dot(bm[g], lo)) * ic[g] - pg
    return y, _dot(y.astype(BF16), pw[g].astype(BF16))


def _pool_fwd(cfg, Z, consts, pool_w, pool_scale):
    PW = cfg.PW
    G = PW // CH
    nct = cfg.CTX // PT
    ty = lambda i: jnp.where(i < nct, 0, 1)

    def fn(i, j, p, bm, ic, pw, ps):
        outs = [_pool_tile(p, bm, ic, pw, g)[1] for g in range(G)]
        return (jnp.concatenate(outs, axis=1) * ps,)

    return _rowcall("pool_fwd", fn, nrows=cfg.TA, tr=PT,
                    ins=[(Z, "rows", PW, cfg.o_p // PW), (consts[0], "sel", 0, ty), (consts[1], "sel", 0, ty),
                         (pool_w, "full", 0, 0), (pool_scale, "full", 0, 0)],
                    outs=[(BF16, PW, PW, 0, None)])[0]


def _pool_bwd(cfg, Z, consts, pool_w, pool_scale, dpool, dz):
    PW = cfg.PW
    G = PW // CH
    nct = cfg.CTX // PT
    ty = lambda i: jnp.where(i < nct, 0, 1)

    def fn(i, j, p, bm, ic, pw, ps, dout):
        dps, dps_acc, dpw = [], [], []
        for g in range(G):
            sl = slice(g * CH, (g + 1) * CH)
            y, lin = _pool_tile(p, bm, ic, pw, g)
            dlin = (dout[:, sl] * ps[:, sl]).astype(BF16)
            dps_acc.append(jnp.sum(dout[:, sl] * lin, axis=0, keepdims=True))
            dy = _dot(dlin, pw[g].astype(BF16), "NT")
            dpw.append(_dot(y.astype(BF16), dlin, "TN"))
            t = dy * ic[g]
            hi = t.astype(BF16)
            lo = (t - hi.astype(F32)).astype(BF16)
            dps.append(_dot(bm[g], hi, "TN") + _dot(bm[g], lo, "TN") - dy)
        return (jnp.concatenate(dps, axis=1), jnp.stack(dpw), jnp.concatenate(dps_acc, axis=1))

    return _rowcall("pool_bwd", fn, nrows=cfg.TA, tr=PT,
                    ins=[(Z, "rows", PW, cfg.o_p // PW), (consts[0], "sel", 0, ty), (consts[1], "sel", 0, ty),
                         (pool_w, "full", 0, 0), (pool_scale, "full", 0, 0), (dpool, "rows", PW, 0)],
                    outs=[(BF16, cfg.NIN, PW, cfg.o_p // PW, dz)], accs=[(G, CH, CH), (1, PW)])


def _sg_mixed(vn, sw, sbb, g, c):
    rows = slice(c * CH, (c + 1) * CH)
    sl = slice(g * CH, (g + 1) * CH)
    return _dot(sw[g].astype(BF16), vn[rows, sl].astype(BF16)) + sbb[g]


def _sg_fwd(cfg, Z, sng, sw, sbb):
    SW = cfg.SW
    G = SW // CH

    def fn(i, j, u, sv, sng, sw, sbb):
        ug = _gelu(u.astype(F32))
        vn = _f_sgv(sv.astype(F32), sng)
        rows = []
        for c in range(PT // CH):
            mixed = jnp.concatenate([_sg_mixed(vn, sw, sbb, g, c) for g in range(G)], axis=1)
            rows.append(ug[c * CH:(c + 1) * CH, :] * mixed)
        return (jnp.concatenate(rows, axis=0),)

    return _rowcall("sg_fwd", fn, nrows=cfg.TA, tr=PT,
                    ins=[(Z, "rows", SW, cfg.o_u // SW), (Z, "rows", SW, cfg.o_sv // SW), (sng, "full", 0, 0),
                         (sw, "full", 0, 0), (sbb, "full", 0, 0)],
                    outs=[(BF16, SW, SW, 0, None)])[0]


def _sg_bwd(cfg, Z, sng, sw, sbb, dsg, dz):
    SW = cfg.SW
    G = SW // CH

    def fn(i, j, u, sv, sng, sw, sbb, dout):
        uf, svf = u.astype(F32), sv.astype(F32)
        ug, vjp_u = jax.vjp(_gelu, uf)
        vn, vjp_v = jax.vjp(_f_sgv, svf, sng)
        dug_rows, dvn_rows = [], []
        dsw = [jnp.zeros((CH, CH), F32) for _ in range(G)]
        dsb = [jnp.zeros((CH, CH), F32) for _ in range(G)]
        for c in range(PT // CH):
            rows = slice(c * CH, (c + 1) * CH)
            dug_g, dvn_g = [], []
            for g in range(G):
                sl = slice(g * CH, (g + 1) * CH)
                mixed = _sg_mixed(vn, sw, sbb, g, c)
                dmixed = dout[rows, sl] * ug[rows, sl]
                dug_g.append(dout[rows, sl] * mixed)
                dmb = dmixed.astype(BF16)
                dvn_g.append(_dot(sw[g].astype(BF16), dmb, "TN"))
                dsw[g] = dsw[g] + _dot(dmb, vn[rows, sl].astype(BF16), "NT")
                dsb[g] = dsb[g] + jnp.broadcast_to(jnp.sum(dmixed, axis=1, keepdims=True), (CH, CH))
            dug_rows.append(jnp.concatenate(dug_g, axis=1))
            dvn_rows.append(jnp.concatenate(dvn_g, axis=1))
        (du,) = vjp_u(jnp.concatenate(dug_rows, axis=0))
        dsv, dsng = vjp_v(jnp.concatenate(dvn_rows, axis=0))
        return du, dsv, jnp.stack(dsw), jnp.stack(dsb), dsng

    return _rowcall("sg_bwd", fn, nrows=cfg.TA, tr=PT,
                    ins=[(Z, "rows", SW, cfg.o_u // SW), (Z, "rows", SW, cfg.o_sv // SW), (sng, "full", 0, 0),
                         (sw, "full", 0, 0), (sbb, "full", 0, 0), (dsg, "rows", SW, 0)],
                    outs=[(BF16, cfg.NIN, SW, cfg.o_u // SW, dz), (BF16, SW, SW, 0, None)],
                    accs=[(G, CH, CH), (G, CH, CH), (1, SW)])


def _gate_bwd(cfg, b, dY, L, Z, dz):
    D = cfg.D
    tc = 512
    nj = D // tc
    off = cfg.o_gate // tc + b * nj

    def fn(i, j, dy, l, z):
        s = jax.nn.sigmoid(z.astype(F32))
        return dy * s, dy * l.astype(F32) * s * (1.0 - s)

    return _rowcall("gate_bwd", fn, nrows=cfg.TA, tr=cfg.tr,
                    ins=[(dY, "rows", tc, 0), (L, "rows", tc, 0), (Z, "rows", tc, off)],
                    outs=[(BF16, D, tc, 0, None), (BF16, cfg.NIN, tc, off, dz)], ncol=nj)


def _copy_cols(cfg, src, dz, col0):
    w = src.shape[1]
    return _rowcall("copy_cols", lambda i, j, s: (s,), nrows=cfg.TA, tr=cfg.tr, ins=[(src, "rows", w, 0)],
                    outs=[(dz.dtype, dz.shape[1], w, col0 // w, dz)])[0]


def _final(cfg, X, g, target):
    D = cfg.D
    tr = PT
    nct = cfg.CTX // tr

    def body(x_ref, g_ref, t_ref, dx_ref, loss_ref, dg_ref):
        i = pl.program_id(0)

        def f(x, g):
            r = lax.rsqrt(jnp.mean(x * x, axis=-1, keepdims=True) + EPS)
            return x * r * g

        y, vjp = jax.vjp(f, x_ref[...], g_ref[...])
        err = y - t_ref[...]
        dx, dg = vjp(err * (1.0 / D))
        part = 0.5 * jnp.sum(jnp.mean(err * err, axis=-1, keepdims=True), axis=0, keepdims=True)

        @pl.when(i == 0)
        def _():
            loss_ref[...] = jnp.zeros_like(loss_ref)
            dg_ref[...] = jnp.zeros_like(dg_ref)

        @pl.when(i < nct)
        def _():
            dx_ref[...] = jnp.zeros_like(dx_ref)

        @pl.when(i >= nct)
        def _():
            dx_ref[...] = dx
            loss_ref[...] += jnp.broadcast_to(part, loss_ref.shape)
            dg_ref[...] += dg

    return pl.pallas_call(
        body, name="final", grid=(cfg.TA // tr,),
        in_specs=[pl.BlockSpec((tr, D), lambda i: (i, 0)), pl.BlockSpec((1, D), lambda i: (0, 0)),
                  pl.BlockSpec((tr, D), lambda i: (jnp.maximum(i - nct, 0), 0))],
        out_specs=[pl.BlockSpec((tr, D), lambda i: (i, 0)), pl.BlockSpec((8, CH), lambda i: (0, 0)),
                   pl.BlockSpec((1, D), lambda i: (0, 0))],
        out_shape=[jax.ShapeDtypeStruct((cfg.TA, D), F32), jax.ShapeDtypeStruct((8, CH), F32),
                   jax.ShapeDtypeStruct((1, D), F32)],
        compiler_params=_cparams(("arbitrary",)),
    )(X, g, target)


GATHER_A = ("w_in", "w1")
GATHER_B = ("w2", "w_out", "w_br", "w_bp", "w_bs")
SCATTER_A = ("w_in", "w_out", "w_br", "w_bp", "w_bs")
SCATTER_B = ("w1", "w2")


def _hosted(res, comm, sink):
    if comm is None:
        return res
    main, outs = res
    sink(outs)
    return main


def _layer_fwd(cfg, X, mods, W, sm, tabs, decs, consts, nxt=None):
    D, TA, tr, RW, PW, SW, DFF, NIN = cfg.D, cfg.TA, cfg.tr, cfg.RW, cfg.PW, cfg.SW, cfg.DFF, cfg.NIN
    ident = lambda accs, ex, i: (accs[0],)
    H1 = _normmod_fwd(cfg, X, sm["norm1_g"], mods, 0, 1, "normmod1_fwd")
    comm = nxt.ici(GATHER_A) if nxt else None
    (Z,) = _hosted(_matmul("z_mm", "NN", [(H1, W["w_in"])], R=TA, C=NIN, tr=tr, tc=512, tk=D, nk=1, out_dtypes=[F32],
                           epi=ident, comm=comm), comm, lambda o: nxt.update(GATHER_A, o))
    o2, states = _ret_fwd(cfg, Z, tabs, decs)
    ret = _retout_fwd(cfg, o2, Z, sm["ret_norm_g"])
    pool = _pool_fwd(cfg, Z, consts, sm["pool_w"], sm["pool_scale"])
    sg = _sg_fwd(cfg, Z, sm["sg_norm_g"], sm["sg_w"], sm["sg_bb"])

    tc = 512
    goff = cfg.o_gate // tc

    def epi_branch(accs, ex, i):
        y = sum(jax.nn.sigmoid(z.astype(F32)) * a for a, z in zip(accs, ex))
        return (y, accs[0], accs[1], accs[2])

    Y, Lr, Lp, Ls = _matmul("branch_mm", "NN", [(ret, W["w_br"]), (pool, W["w_bp"]), (sg, W["w_bs"])], R=TA, C=D, tr=tr,
                            tc=tc, tk=0, nk=1, out_dtypes=[BF16] * 4, epi=epi_branch,
                            extras=[("tile", Z, goff + b * (D // tc)) for b in range(3)])

    def epi_res(accs, ex, i):
        return (ex[0] + _sel(i, tr, cfg.CTX, ex[1]) * accs[0], accs[0])

    X2, O = _matmul("out_mm", "NN", [(Y, W["w_out"])], R=TA, C=D, tr=tr, tc=tc, tk=D, nk=1, out_dtypes=[F32, BF16],
                    epi=epi_res, extras=[("tile", X, 0), ("rows2", mods, 2 * (D // tc))])
    H2 = _normmod_fwd(cfg, X2, sm["norm2_g"], mods, 3, 4, "normmod2_fwd")
    tcf = _pick(DFF, 1024)
    def epi_relu(accs, ex, i):
        r = jnp.maximum(accs[0], 0.0)
        return (r * r, r)

    comm = nxt.ici(GATHER_B) if nxt else None
    A2, Rr = _hosted(_matmul("w1_mm", "NN", [(H2, W["w1"])], R=TA, C=DFF, tr=tr, tc=tcf, tk=D, nk=1, out_dtypes=[BF16, BF16],
                             epi=epi_relu, comm=comm), comm, lambda o: nxt.update(GATHER_B, o))
    tkf = _pick(DFF, 2048)
    comm = nxt.d2d(BIG) if nxt else None
    X3, M = _hosted(_matmul("w2_mm", "NN", [(A2, W["w2"])], R=TA, C=D, tr=tr, tc=1024, tk=tkf, nk=DFF // tkf,
                            out_dtypes=[F32, BF16], epi=epi_res, extras=[("tile", X2, 0), ("rows2", mods, 5 * (D // 1024))],
                            comm=comm), comm, lambda o: nxt.update(BIG, o))
    saved = dict(X=X, H1=H1, Z=Z, o2=o2, states=states, ret=ret, pool=pool, sg=sg, Y=Y, L=(Lr, Lp, Ls), O=O, X2=X2,
                 H2=H2, R=Rr, A2=A2, M=M)
    return X3, saved


def _layer_bwd(cfg, dX3, sv, mods, W, sm, tabs, decs, consts, rs=None):
    D, TA, tr, RW, PW, SW, DFF, NIN = cfg.D, cfg.TA, cfg.tr, cfg.RW, cfg.PW, cfg.SW, cfg.DFF, cfg.NIN
    ident = lambda accs, ex, i: (accs[0],)
    tkr = _pick(TA, 768)
    nkr = TA // tkr
    tw = 512
    g = {}

    dM, dgate2 = _resgate_bwd(cfg, dX3, sv["M"], mods, 5, "resgate2_bwd")
    tcf = _pick(DFF, 1024)
    comm = rs.swap() if rs else None
    (dPre,) = _hosted(_matmul("dpre_mm", "NT", [(dM, W["w2"])], R=TA, C=DFF, tr=tr, tc=tcf, tk=D, nk=1, out_dtypes=[BF16],
                              epi=lambda accs, ex, i: (accs[0] * (2.0 * ex[0].astype(F32)),), extras=[("tile", sv["R"], 0)],
                              comm=comm), comm, lambda o: rs.swapped(o))
    comm = rs.scatter(SCATTER_A) if rs else None
    (g["w2"],) = _hosted(_matmul("dw2_mm", "TN", [(sv["A2"], dM)], R=DFF, C=D, tr=tw, tc=D, tk=tkr, nk=nkr, out_dtypes=[F32],
                                 epi=None, comm=comm), comm, lambda o: rs.scattered(SCATTER_A, o))
    tkf = _pick(DFF, 2048)
    (dH2,) = _matmul("dh2_mm", "NT", [(dPre, W["w1"])], R=TA, C=D, tr=tr, tc=1024, tk=tkf, nk=DFF // tkf, out_dtypes=[F32],
                     epi=ident)
    comm = rs.scatter(SCATTER_B) if rs else None
    (g["w1"],) = _hosted(_matmul("dw1_mm", "TN", [(sv["H2"], dPre)], R=D, C=DFF, tr=tw, tc=_pick(DFF, 2048), tk=tkr, nk=nkr,
                                 out_dtypes=[F32], epi=None, comm=comm), comm, lambda o: rs.scattered(SCATTER_B, o))
    dX2, dn2, dsh2, dsc2 = _normmod_bwd(cfg, sv["X2"], dH2, dX3, sm["norm2_g"], mods, 3, 4, "normmod2_bwd")

    dO, dgate1 = _resgate_bwd(cfg, dX2, sv["O"], mods, 2, "resgate1_bwd")
    (dY,) = _matmul("dy_mm", "NT", [(dO, W["w_out"])], R=TA, C=D, tr=tr, tc=1024, tk=D, nk=1, out_dtypes=[F32], epi=ident)
    (g["w_out"],) = _matmul("dwout_mm", "TN", [(sv["Y"], dO)], R=D, C=D, tr=tw, tc=D, tk=tkr, nk=nkr,
                            out_dtypes=[F32], epi=None)
    dLr, dz = _gate_bwd(cfg, 0, dY, sv["L"][0], sv["Z"], None)
    dLp, dz = _gate_bwd(cfg, 1, dY, sv["L"][1], sv["Z"], dz)
    dLs, dz = _gate_bwd(cfg, 2, dY, sv["L"][2], sv["Z"], dz)

    (dret,) = _matmul("dret_mm", "NT", [(dLr, W["w_br"])], R=TA, C=RW, tr=tr, tc=RW, tk=D, nk=1, out_dtypes=[F32], epi=ident)
    (dpool,) = _matmul("dpool_mm", "NT", [(dLp, W["w_bp"])], R=TA, C=PW, tr=tr, tc=PW, tk=D, nk=1, out_dtypes=[F32], epi=ident)
    (dsg,) = _matmul("dsg_mm", "NT", [(dLs, W["w_bs"])], R=TA, C=SW, tr=tr, tc=SW, tk=D, nk=1, out_dtypes=[F32], epi=ident)
    (g["w_br"],) = _matmul("dwbr_mm", "TN", [(sv["ret"], dLr)], R=RW, C=D, tr=tw, tc=D, tk=tkr, nk=nkr, out_dtypes=[F32],
                           epi=None)
    (g["w_bp"],) = _matmul("dwbp_mm", "TN", [(sv["pool"], dLp)], R=PW, C=D, tr=tw, tc=D, tk=tkr, nk=nkr, out_dtypes=[F32],
                           epi=None)
    (g["w_bs"],) = _matmul("dwbs_mm", "TN", [(sv["sg"], dLs)], R=SW, C=D, tr=tw, tc=D, tk=tkr, nk=nkr, out_dtypes=[F32],
                           epi=None)
    dOr, dz, dretng = _retout_bwd(cfg, sv["o2"], sv["Z"], sm["ret_norm_g"], dret, dz)
    dqkv2, dlam = _ret_bwd(cfg, sv["Z"], tabs, decs, sv["states"], dOr)
    dz = _rope_bwd(cfg, dqkv2, tabs, dz)
    dz, dpw, dps = _pool_bwd(cfg, sv["Z"], consts, sm["pool_w"], sm["pool_scale"], dpool, dz)
    dz, dz_sv, dsw, dsb, dsng = _sg_bwd(cfg, sv["Z"], sm["sg_norm_g"], sm["sg_w"], sm["sg_bb"], dsg, dz)
    dz = _copy_cols(cfg, dz_sv, dz, cfg.o_sv)

    tkz = _pick(NIN, 2944)
    (dH1,) = _matmul("dh1_mm", "NT", [(dz, W["w_in"])], R=TA, C=D, tr=tr, tc=1024, tk=tkz, nk=NIN // tkz, out_dtypes=[F32],
                     epi=ident)
    (g["w_in"],) = _matmul("dwin_mm", "TN", [(sv["H1"], dz)], R=D, C=NIN, tr=tw, tc=_pick(NIN, 2944), tk=tkr, nk=nkr,
                           out_dtypes=[F32], epi=None)
    dX, dn1, dsh1, dsc1 = _normmod_bwd(cfg, sv["X"], dH1, dX2, sm["norm1_g"], mods, 0, 1, "normmod1_bwd")
    dmods = jnp.concatenate([dsh1, dsc1, dgate1, dsh2, dsc2, dgate2], axis=1)
    small = dict(norm1_g=dn1, norm2_g=dn2, ret_norm_g=dretng, pool_w=dpw, pool_scale=dps, sg_norm_g=dsng, sg_w=dsw,
                 sg_b=dsb[:, :, 0], dlam=dlam)
    return dX, g, small, dmods


def _tables(cfg):
    nf = CH // 4
    inv = ROPE_THETA ** (-jnp.arange(nf, dtype=F32) / nf)
    tok = jnp.arange(cfg.T)
    ar = (tok // GRID_W).astype(F32)[:, None] * inv[None]
    ac = (tok % GRID_W).astype(F32)[:, None] * inv[None]
    cos = jnp.concatenate([jnp.cos(ar), jnp.cos(ar), jnp.cos(ac), jnp.cos(ac)], axis=1)
    sin = jnp.concatenate([-jnp.sin(ar), jnp.sin(ar), -jnp.sin(ac), jnp.sin(ac)], axis=1)
    cos = jnp.concatenate([jnp.ones((cfg.CTX, CH), F32), cos], axis=0)
    sin = jnp.concatenate([jnp.zeros((cfg.CTX, CH), F32), sin], axis=0)
    idx = np.broadcast_to(np.arange(CH, dtype=np.float32)[:, None], (CH, CH))
    lamw = np.stack([np.stack([idx, idx + 1.0, -idx, CH - 1.0 - idx]), np.stack([-idx, CH - idx, idx, idx])])
    return dict(cos=cos, sin=sin, lamw=jnp.asarray(lamw, F32))


def _decays(cfg, logit):
    H, RW = cfg.H, cfg.RW
    lam = jax.nn.log_sigmoid(logit.astype(F32))
    idx = jnp.arange(CH, dtype=F32)
    dist = idx[:, None] - idx[None, :]
    d0 = jnp.where(dist >= 0, jnp.exp(lam[0][:, None, None] * jnp.maximum(dist, 0.0)), 0.0)
    d1 = jnp.where(dist <= 0, jnp.exp(lam[1][:, None, None] * jnp.maximum(-dist, 0.0)), 0.0)
    lanes = lambda a: jnp.repeat(a.T, CH, axis=1)
    qdec = jnp.stack([lanes(jnp.exp(lam[0][:, None] * (idx + 1.0)[None])), lanes(jnp.exp(lam[1][:, None] * (CH - idx)[None]))])
    kdec = jnp.stack([lanes(jnp.exp(lam[0][:, None] * (CH - 1.0 - idx)[None])), lanes(jnp.exp(lam[1][:, None] * idx[None]))])
    cdec = jnp.repeat(jnp.exp(lam * CH), CH, axis=1)[:, None, :]
    return dict(dmat=jnp.stack([d0, d1]), qdec=qdec, kdec=kdec, cdec=cdec)


def _small_of_layer(small_w, l):
    sm = {k: v[l] for k, v in small_w.items()}
    sm["norm1_g"] = sm["norm1_g"][None]
    sm["norm2_g"] = sm["norm2_g"][None]
    sm["ret_norm_g"] = sm["ret_norm_g"][None]
    sm["pool_scale"] = sm["pool_scale"][None]
    sm["sg_norm_g"] = sm["sg_norm_g"][None]
    sm["sg_bb"] = jnp.broadcast_to(sm["sg_b"][:, :, None], sm["sg_b"].shape + (CH,))
    return sm


def _local_fwd_bwd(cfg, X0, target, mods, gathers, make_reduce, small_w, final_g):
    depth = len(mods)
    tabs = _tables(cfg)
    consts = _pool_consts(cfg.CTX)
    X, saved, Ws, sms, decs = X0, [], [], [], []
    gathers[0].run()
    for l in range(depth):
        Ws.append(gathers[l].weights())
        sms.append(_small_of_layer(small_w, l))
        decs.append(_decays(cfg, small_w["ret_decay_logit"][l]))
        X, sv = _layer_fwd(cfg, X, mods[l], Ws[l], sms[l], tabs, decs[l], consts, gathers[l + 1] if l + 1 < depth else None)
        saved.append(sv)
    dX, loss_acc, dfinal = _final(cfg, X, final_g[None], target)
    shard_g, small, dmods = [None] * depth, [None] * depth, [None] * depth
    pending = None
    for l in reversed(range(depth)):
        dX, big, small[l], dmods[l] = _layer_bwd(cfg, dX, saved[l], mods[l], Ws[l], sms[l], tabs, decs[l], consts, pending)
        if pending is not None:
            shard_g[l + 1] = pending.finish()
        pending = make_reduce(l, big)
        lam_grad = jnp.sum(small[l].pop("dlam").reshape(2, cfg.H, CH), axis=-1)
        small[l]["ret_decay_logit"] = lam_grad * jax.nn.sigmoid(-small_w["ret_decay_logit"][l].astype(F32))
    shard_g[0] = pending.run()
    return loss_acc[0, 0], dX, shard_g, small, dmods, dfinal


def _me():
    return lax.axis_index("x"), lax.axis_index("y"), lax.axis_index("c")


def _other_chips(x, y):
    return [(1 - x, y), (x, 1 - y), (1 - x, 1 - y)]


def _rcopy(src, dst, send_sem, recv_sem, dev):
    return pltpu.make_async_remote_copy(src_ref=src, dst_ref=dst, send_sem=send_sem, recv_sem=recv_sem,
                                        device_id=dev, device_id_type=MESH)


def _half(ref, axis, c):
    k, n = ref.shape
    if axis == 1:
        return ref.at[pl.ds(c * (k // 2), k // 2), :]
    return ref.at[:, pl.ds(c * (n // 2), n // 2)]


def _chip_part(ref, axis, j):
    k, n = ref.shape
    if axis == 1:
        return ref.at[:, pl.ds(j * (n // 4), n // 4)]
    return ref.at[pl.ds(j * (k // 4), k // 4), :]


def _piece(ref, axis, j, c):
    k, n = ref.shape
    if axis == 1:
        return ref.at[pl.ds(c * (k // 2), k // 2), pl.ds(j * (n // 4), n // 4)]
    return ref.at[pl.ds(j * (k // 4), k // 4), pl.ds(c * (n // 2), n // 2)]


def _gather_weights(fulls, axes):
    n = len(fulls)

    def body(*refs):
        outs = refs[n:2 * n]
        send, recv = refs[2 * n:]
        x, y, c = _me()
        j = 2 * x + y
        sib = (x, y, 1 - c)
        chips = _other_chips(x, y)
        first = []
        for t in range(n):
            for k, chip in enumerate(chips):
                own = _piece(outs[t], axes[t], j, c)
                first.append(_rcopy(own, own, send.at[6 * t + k], recv.at[6 * t + k], (*chip, c)))
                first[-1].start()
        passed = []
        for t in range(n):
            for k, chip in enumerate(chips):
                landed = _piece(outs[t], axes[t], 2 * chip[0] + chip[1], c)
                _rcopy(landed, landed, send.at[6 * t + k], recv.at[6 * t + k], sib).wait_recv()
                passed.append(_rcopy(landed, landed, send.at[6 * t + 3 + k], recv.at[6 * t + 3 + k], sib))
                passed[-1].start()
        for t in range(n):
            for k, chip in enumerate(chips):
                theirs = _piece(outs[t], axes[t], 2 * chip[0] + chip[1], 1 - c)
                _rcopy(theirs, theirs, send.at[6 * t + 3 + k], recv.at[6 * t + 3 + k], sib).wait_recv()
        for cp in first + passed:
            cp.wait_send()

    return pl.pallas_call(
        body, name="gather_weights",
        in_specs=[ANY] * n, out_specs=[ANY] * n,
        out_shape=[jax.ShapeDtypeStruct(f.shape, f.dtype) for f in fulls],
        input_output_aliases={t: t for t in range(n)},
        scratch_shapes=[pltpu.SemaphoreType.DMA((6 * n,)), pltpu.SemaphoreType.DMA((6 * n,))],
    )(*fulls)


class _Comm:
    def __init__(self, ins, outs, aliases, nsem, start, finish):
        self.ins, self.outs, self.aliases, self.nsem, self.start, self.finish = ins, outs, aliases, nsem, start, finish


def _run_comm(name, comm):
    n_in = len(comm.ins)

    def body(*refs):
        ins, outs = refs[:n_in], refs[n_in:n_in + len(comm.outs)]
        send, recv = refs[n_in + len(comm.outs):]
        comm.start(ins, outs, send, recv)
        comm.finish(ins, outs, send, recv)

    return pl.pallas_call(
        body, name=name, in_specs=[ANY] * n_in, out_specs=[ANY] * len(comm.outs), out_shape=list(comm.outs),
        input_output_aliases=dict(comm.aliases),
        scratch_shapes=[pltpu.SemaphoreType.DMA((comm.nsem,)), pltpu.SemaphoreType.DMA((comm.nsem,))],
    )(*comm.ins)


def _like(arrs):
    return [jax.ShapeDtypeStruct(a.shape, a.dtype) for a in arrs]


def _gather_ici_comm(fulls, axes):
    n = len(fulls)

    def copies(outs, send, recv):
        x, y, c = _me()
        own_j = 2 * x + y
        res = []
        for t in range(n):
            for k, chip in enumerate(_other_chips(x, y)):
                own = _piece(outs[t], axes[t], own_j, c)
                landed = _piece(outs[t], axes[t], 2 * chip[0] + chip[1], c)
                res.append((_rcopy(own, own, send.at[3 * t + k], recv.at[3 * t + k], (*chip, c)),
                            _rcopy(landed, landed, send.at[3 * t + k], recv.at[3 * t + k], (*chip, c))))
        return res

    def start(ins, outs, send, recv):
        for out, _ in copies(outs, send, recv):
            out.start()

    def finish(ins, outs, send, recv):
        for out, arrival in copies(outs, send, recv):
            out.wait_send()
            arrival.wait_recv()

    return _Comm(fulls, _like(fulls), {t: t for t in range(n)}, 3 * n, start, finish)


def _gather_d2d_comm(fulls, axes):
    n = len(fulls)

    def copies(outs, send, recv):
        x, y, c = _me()
        res = []
        for t in range(n):
            for k, chip in enumerate(_other_chips(x, y)):
                landed = _piece(outs[t], axes[t], 2 * chip[0] + chip[1], c)
                theirs = _piece(outs[t], axes[t], 2 * chip[0] + chip[1], 1 - c)
                res.append((_rcopy(landed, landed, send.at[3 * t + k], recv.at[3 * t + k], (x, y, 1 - c)),
                            _rcopy(theirs, theirs, send.at[3 * t + k], recv.at[3 * t + k], (x, y, 1 - c))))
        return res

    def start(ins, outs, send, recv):
        for out, _ in copies(outs, send, recv):
            out.start()

    def finish(ins, outs, send, recv):
        for out, arrival in copies(outs, send, recv):
            out.wait_send()
            arrival.wait_recv()

    return _Comm(fulls, _like(fulls), {t: t for t in range(n)}, 3 * n, start, finish)


def _swap_comm(grads, axes):
    n = len(grads)
    half_shapes = [(g.shape[0] // 2, g.shape[1]) if a == 1 else (g.shape[0], g.shape[1] // 2) for g, a in zip(grads, axes)]

    def copies(ins, outs, send, recv):
        x, y, c = _me()
        return [_rcopy(_half(ins[t], axes[t], 1 - c), outs[t], send.at[t], recv.at[t], (x, y, 1 - c)) for t in range(n)]

    def start(ins, outs, send, recv):
        for cp in copies(ins, outs, send, recv):
            cp.start()

    def finish(ins, outs, send, recv):
        for cp in copies(ins, outs, send, recv):
            cp.wait()

    return _Comm(grads, [jax.ShapeDtypeStruct(s, F32) for s in half_shapes], {}, n, start, finish)


def _scatter_comm(parts, axes):
    n = len(parts)
    q_shapes = [(p.shape[0], p.shape[1] // 4) if a == 1 else (p.shape[0] // 4, p.shape[1]) for p, a in zip(parts, axes)]

    def copies(ins, outs, send, recv):
        x, y, c = _me()
        res = []
        for t in range(n):
            for k, chip in enumerate(_other_chips(x, y)):
                res.append(_rcopy(_chip_part(ins[t], axes[t], 2 * chip[0] + chip[1]), outs[3 * t + k], send.at[3 * t + k],
                                  recv.at[3 * t + k], (*chip, c)))
        return res

    def start(ins, outs, send, recv):
        for cp in copies(ins, outs, send, recv):
            cp.start()

    def finish(ins, outs, send, recv):
        for cp in copies(ins, outs, send, recv):
            cp.wait()

    return _Comm(parts, [jax.ShapeDtypeStruct(s, p.dtype) for s, p in zip(q_shapes, parts) for _ in range(3)], {}, 3 * n,
                 start, finish)


def _rs_share(shards, axes):
    n = len(shards)

    def body(*refs):
        outs = refs[n:2 * n]
        send, recv = refs[2 * n:]
        x, y, c = _me()
        sib = (x, y, 1 - c)
        out = []
        for t in range(n):
            mine = _half(outs[t], axes[t], c)
            out.append(_rcopy(mine, mine, send.at[t], recv.at[t], sib))
            out[-1].start()
        for t in range(n):
            out[t].wait_send()
            theirs = _half(outs[t], axes[t], 1 - c)
            _rcopy(theirs, theirs, send.at[t], recv.at[t], sib).wait_recv()

    return pl.pallas_call(
        body, name="rs_share",
        in_specs=[ANY] * n, out_specs=[ANY] * n,
        out_shape=[jax.ShapeDtypeStruct(s.shape, s.dtype) for s in shards],
        input_output_aliases={t: t for t in range(n)},
        scratch_shapes=[pltpu.SemaphoreType.DMA((n,)), pltpu.SemaphoreType.DMA((n,))],
    )(*shards)


def _cast_into_full(w, l, axis, sc):
    _, k, n = w.shape
    tr = _pick(k, 256, 16)
    if axis == 1:
        full, out_spec = (k, 4 * n), pl.BlockSpec((tr, n), lambda i, s: (i, s[1]))
    else:
        full, out_spec = (4 * k, n), pl.BlockSpec((tr, n), lambda i, s: (s[1] * (k // tr) + i, 0))

    def body(s_ref, w_ref, o_ref):
        o_ref[...] = w_ref[...].astype(BF16)

    return pl.pallas_call(
        body, name="cast_into_full",
        grid_spec=pltpu.PrefetchScalarGridSpec(
            num_scalar_prefetch=1, grid=(k // tr,),
            in_specs=[pl.BlockSpec((None, tr, n), lambda i, s: (l, i, 0))], out_specs=out_spec),
        out_shape=jax.ShapeDtypeStruct(full, BF16), compiler_params=_cparams(("arbitrary",)),
    )(sc, w)


def _rs_add2(g, got, axis, sc):
    k, n = g.shape
    hk, hn = (k // 2, n) if axis == 1 else (k, n // 2)
    tr = _pick(hk, max(16, (1 << 18) // hn), 16)
    if axis == 1:
        g_spec = pl.BlockSpec((tr, hn), lambda i, s: (s[0] * (hk // tr) + i, 0))
    else:
        g_spec = pl.BlockSpec((tr, hn), lambda i, s: (i, s[0]))
    blk = pl.BlockSpec((tr, hn), lambda i, s: (i, 0))

    def body(s_ref, a_ref, b_ref, o_ref):
        o_ref[...] = (a_ref[...] + b_ref[...]).astype(o_ref.dtype)

    return pl.pallas_call(
        body, name="rs_add2",
        grid_spec=pltpu.PrefetchScalarGridSpec(num_scalar_prefetch=1, grid=(hk // tr,), in_specs=[g_spec, blk], out_specs=blk),
        out_shape=jax.ShapeDtypeStruct((hk, hn), BF16), compiler_params=_cparams(("arbitrary",)),
    )(sc, g, got)


def _rs_add4(part, got3, axis, sc):
    k, n = part.shape
    qk, qn = (k, n // 4) if axis == 1 else (k // 4, n)
    tr = _pick(qk, max(16, (1 << 18) // qn), 16)
    if axis == 1:
        p_spec = pl.BlockSpec((tr, qn), lambda i, s: (i, s[1]))
        shard, o_spec = (2 * qk, qn), pl.BlockSpec((tr, qn), lambda i, s: (s[0] * (qk // tr) + i, 0))
    else:
        p_spec = pl.BlockSpec((tr, qn), lambda i, s: (s[1] * (qk // tr) + i, 0))
        shard, o_spec = (qk, 2 * qn), pl.BlockSpec((tr, qn), lambda i, s: (i, s[0]))
    blk = pl.BlockSpec((tr, qn), lambda i, s: (i, 0))

    def body(s_ref, p_ref, a_ref, b_ref, c_ref, o_ref):
        o_ref[...] = ((p_ref[...].astype(F32) + a_ref[...].astype(F32)) + b_ref[...].astype(F32)) + c_ref[...].astype(F32)

    return pl.pallas_call(
        body, name="rs_add4",
        grid_spec=pltpu.PrefetchScalarGridSpec(num_scalar_prefetch=1, grid=(qk // tr,), in_specs=[p_spec, blk, blk, blk],
                                               out_specs=o_spec),
        out_shape=jax.ShapeDtypeStruct(shard, F32), compiler_params=_cparams(("arbitrary",)),
    )(sc, part, *got3)


def _gather_small(v):
    def body(v_ref, out_ref, send, recv, loc):
        x, y, c = _me()
        sib = (x, y, 1 - c)
        chips = _other_chips(x, y)
        slot = lambda px, py, pc: out_ref.at[4 * px + 2 * py + pc]
        mine = pltpu.make_async_copy(v_ref, slot(x, y, c), loc)
        mine.start()
        first = [_rcopy(v_ref, slot(x, y, c), send.at[0], recv.at[0], sib)]
        first += [_rcopy(v_ref, slot(x, y, c), send.at[1 + k], recv.at[1 + k], (*chip, c)) for k, chip in enumerate(chips)]
        for cp in first:
            cp.start()
        passed = []
        for k, chip in enumerate(chips):
            landed = slot(*chip, c)
            _rcopy(landed, landed, send.at[1 + k], recv.at[1 + k], sib).wait_recv()
            passed.append(_rcopy(landed, landed, send.at[4 + k], recv.at[4 + k], sib))
            passed[-1].start()
        theirs = slot(x, y, 1 - c)
        _rcopy(theirs, theirs, send.at[0], recv.at[0], sib).wait_recv()
        for k, chip in enumerate(chips):
            theirs = slot(*chip, 1 - c)
            _rcopy(theirs, theirs, send.at[4 + k], recv.at[4 + k], sib).wait_recv()
        for cp in first + passed:
            cp.wait_send()
        mine.wait()

    return pl.pallas_call(
        body, name="gather_small",
        in_specs=[ANY], out_specs=ANY,
        out_shape=jax.ShapeDtypeStruct((8,) + v.shape, v.dtype),
        scratch_shapes=[pltpu.SemaphoreType.DMA((7,)), pltpu.SemaphoreType.DMA((7,)), pltpu.SemaphoreType.DMA],
    )(v)


class _WeightGather:
    def __init__(self, params, l, sc):
        self.bufs = {k: _cast_into_full(params[k], l, SHARD_AXIS[k], sc) for k in BIG}

    def run(self):
        self.update(BIG, _gather_weights([self.bufs[k] for k in BIG], [SHARD_AXIS[k] for k in BIG]))

    def ici(self, names):
        return _gather_ici_comm([self.bufs[k] for k in names], [SHARD_AXIS[k] for k in names])

    def d2d(self, names):
        return _gather_d2d_comm([self.bufs[k] for k in names], [SHARD_AXIS[k] for k in names])

    def update(self, names, outs):
        self.bufs.update(zip(names, outs))

    def weights(self):
        return self.bufs


class _GradReduce:
    def __init__(self, grads, sc):
        self.grads, self.sc, self.parts, self.theirs = grads, sc, None, {}

    def swap(self):
        return _swap_comm([self.grads[k] for k in BIG], [SHARD_AXIS[k] for k in BIG])

    def swapped(self, got):
        self.parts = {k: _rs_add2(self.grads[k], s, SHARD_AXIS[k], self.sc) for k, s in zip(BIG, got)}

    def scatter(self, names):
        return _scatter_comm([self.parts[k] for k in names], [SHARD_AXIS[k] for k in names])

    def scattered(self, names, outs):
        for q, k in enumerate(names):
            self.theirs[k] = outs[3 * q:3 * q + 3]

    def finish(self):
        halves = [_rs_add4(self.parts[k], self.theirs[k], SHARD_AXIS[k], self.sc) for k in BIG]
        return dict(zip(BIG, _rs_share(halves, [SHARD_AXIS[k] for k in BIG])))

    def run(self):
        self.swapped(_run_comm("rs_swap", self.swap()))
        self.scattered(BIG, _run_comm("rs_scatter", self.scatter(BIG)))
        return self.finish()


def _adam_math(w, g, m, v):
    m = ADAM_B1 * m + (1.0 - ADAM_B1) * g
    v = ADAM_B2 * v + (1.0 - ADAM_B2) * (g * g)
    m_hat = m / (1.0 - ADAM_B1 ** ADAM_STEP)
    v_hat = v / (1.0 - ADAM_B2 ** ADAM_STEP)
    delta = -ADAM_LR * (m_hat / (jnp.sqrt(v_hat) + ADAM_EPS) + ADAM_WD * w)
    return delta, m, v


def _adam_layer(w, m, v, l, g, prev):
    L, k, n = w.shape
    tr = _pick(k, 128, 8)
    blk = pl.BlockSpec((None, tr, n), lambda i: (l, i, 0))

    def body(*refs):
        w_ref, m_ref, v_ref, g_ref = refs[:4]
        go, do, mo, vo = refs[-4:]
        gv = g_ref[...]
        d, m2, v2 = _adam_math(w_ref[...], gv, m_ref[...], v_ref[...])
        go[...] = gv
        do[...] = d
        mo[...] = m2
        vo[...] = v2

    args = [w, m, v, g]
    in_specs = [blk, blk, blk, pl.BlockSpec((tr, n), lambda i: (i, 0))]
    aliases = {}
    if prev is not None:
        for q, p in enumerate(prev):
            aliases[len(args)] = q
            in_specs.append(ANY)
            args.append(p)
    return pl.pallas_call(
        body, name="adam_layer", grid=(k // tr,),
        in_specs=in_specs, out_specs=[blk] * 4, out_shape=[jax.ShapeDtypeStruct((L, k, n), F32)] * 4,
        input_output_aliases=aliases, compiler_params=_cparams(("arbitrary",)),
    )(*args)


def _adam_flat(w, g, m, v):
    r = w.shape[0]
    tr = _pick(r, 512, 8)
    fn = lambda i, j, w, g, m, v: _adam_math(w, g, m, v)
    return _rowcall("adam_flat", fn, nrows=r, tr=tr, ins=[(a, "rows", 128, 0) for a in (w, g, m, v)],
                    outs=[(F32, 128, 128, 0, None)] * 3)


def _sum8(gathered):
    _, r, _ = gathered.shape
    tr = _pick(r, 512, 8)

    def body(g_ref, o_ref):
        acc = g_ref[0]
        for d in range(1, 8):
            acc = acc + g_ref[d]
        o_ref[...] = acc

    return pl.pallas_call(
        body, name="sum8", grid=(r // tr,),
        in_specs=[pl.BlockSpec((8, tr, 128), lambda i: (0, i, 0))], out_specs=pl.BlockSpec((tr, 128), lambda i: (i, 0)),
        out_shape=jax.ShapeDtypeStruct((r, 128), F32), compiler_params=_cparams(("arbitrary",)),
    )(gathered)


def _hdot(a, b, form="NN"):
    return _dot(a.astype(BF16), b.astype(BF16), form)


def _ada_fwd(s16, w_ada, l):
    _, d, ns = w_ada.shape
    tc = _pick(ns, 512)

    def body(s_ref, w_ref, o_ref):
        o_ref[...] = _hdot(s_ref[...], w_ref[...])

    return pl.pallas_call(
        body, name="ada_fwd", grid=(ns // tc,),
        in_specs=[pl.BlockSpec((16, d), lambda j: (0, 0)), pl.BlockSpec((None, d, tc), lambda j: (l, 0, j))],
        out_specs=pl.BlockSpec((16, tc), lambda j: (0, j)),
        out_shape=jax.ShapeDtypeStruct((16, ns), F32), compiler_params=_cparams(("arbitrary",)),
    )(s16, w_ada)


def _ada_bwd(s16t, dm, w_ada, l):
    _, d, ns = w_ada.shape
    tc = _pick(ns, 512)

    def body(st_ref, dm_ref, w_ref, dw_ref, ds_ref):
        j = pl.program_id(0)
        dw_ref[...] = _hdot(st_ref[...], dm_ref[...])
        part = _hdot(dm_ref[...], w_ref[...], "NT")

        @pl.when(j == 0)
        def _():
            ds_ref[...] = part

        @pl.when(j > 0)
        def _():
            ds_ref[...] += part

    return pl.pallas_call(
        body, name="ada_bwd", grid=(ns // tc,),
        in_specs=[pl.BlockSpec((d, 16), lambda j: (0, 0)), pl.BlockSpec((16, tc), lambda j: (0, j)),
                  pl.BlockSpec((None, d, tc), lambda j: (l, 0, j))],
        out_specs=[pl.BlockSpec((d, tc), lambda j: (0, j)), pl.BlockSpec((16, d), lambda j: (0, 0))],
        out_shape=[jax.ShapeDtypeStruct((d, ns), F32), jax.ShapeDtypeStruct((16, d), F32)],
        compiler_params=_cparams(("arbitrary",)),
    )(s16t, dm, w_ada)


def _pack(arrs):
    flat = jnp.concatenate([a.reshape(-1).astype(F32) for a in arrs])
    pad = (-flat.shape[0]) % 1024
    return jnp.pad(flat, (0, pad)).reshape(-1, 128)


def _unpack(flat2d, shapes):
    flat = flat2d.reshape(-1)
    out, pos = [], 0
    for s in shapes:
        size = int(np.prod(s))
        out.append(flat[pos:pos + size].reshape(s))
        pos += size
    return out


SMALL = ("norm1_g", "norm2_g", "ret_decay_logit", "ret_norm_g", "pool_w", "pool_scale", "sg_norm_g", "sg_w", "sg_b")


def kernel(x, c, ctx, c_ctx, w_ada, b_ada, norm1_g, w_in, ret_decay_logit, ret_norm_g, pool_w, pool_scale, sg_norm_g, sg_w, sg_b, w_br, w_bp, w_bs, w_out, norm2_g, w1, w2, final_norm_g, loss_target, m_c_ctx, m_w_ada, m_b_ada, m_norm1_g, m_w_in, m_ret_decay_logit, m_ret_norm_g, m_pool_w, m_pool_scale, m_sg_norm_g, m_sg_w, m_sg_b, m_w_br, m_w_bp, m_w_bs, m_w_out, m_norm2_g, m_w1, m_w2, m_final_norm_g, v_c_ctx, v_w_ada, v_b_ada, v_norm1_g, v_w_in, v_ret_decay_logit, v_ret_norm_g, v_pool_w, v_pool_scale, v_sg_norm_g, v_sg_w, v_sg_b, v_w_br, v_w_bp, v_w_bs, v_w_out, v_norm2_g, v_w1, v_w2, v_final_norm_g):
    P = dict(c_ctx=c_ctx, w_ada=w_ada, b_ada=b_ada, norm1_g=norm1_g, w_in=w_in, ret_decay_logit=ret_decay_logit,
             ret_norm_g=ret_norm_g, pool_w=pool_w, pool_scale=pool_scale, sg_norm_g=sg_norm_g, sg_w=sg_w, sg_b=sg_b, w_br=w_br,
             w_bp=w_bp, w_bs=w_bs, w_out=w_out, norm2_g=norm2_g, w1=w1, w2=w2, final_norm_g=final_norm_g)
    Mo = dict(c_ctx=m_c_ctx, w_ada=m_w_ada, b_ada=m_b_ada, norm1_g=m_norm1_g, w_in=m_w_in, ret_decay_logit=m_ret_decay_logit,
              ret_norm_g=m_ret_norm_g, pool_w=m_pool_w, pool_scale=m_pool_scale, sg_norm_g=m_sg_norm_g, sg_w=m_sg_w, sg_b=m_sg_b,
              w_br=m_w_br, w_bp=m_w_bp, w_bs=m_w_bs, w_out=m_w_out, norm2_g=m_norm2_g, w1=m_w1, w2=m_w2,
              final_norm_g=m_final_norm_g)
    Vo = dict(c_ctx=v_c_ctx, w_ada=v_w_ada, b_ada=v_b_ada, norm1_g=v_norm1_g, w_in=v_w_in, ret_decay_logit=v_ret_decay_logit,
              ret_norm_g=v_ret_norm_g, pool_w=v_pool_w, pool_scale=v_pool_scale, sg_norm_g=v_sg_norm_g, sg_w=v_sg_w, sg_b=v_sg_b,
              w_br=v_w_br, w_bp=v_w_bp, w_bs=v_w_bs, w_out=v_w_out, norm2_g=v_norm2_g, w1=v_w1, w2=v_w2,
              final_norm_g=v_final_norm_g)
    names = ("c_ctx", "w_ada", "b_ada", "norm1_g", "w_in", "ret_decay_logit", "ret_norm_g", "pool_w", "pool_scale", "sg_norm_g",
             "sg_w", "sg_b", "w_br", "w_bp", "w_bs", "w_out", "norm2_g", "w1", "w2", "final_norm_g")
    L, D = w_in.shape[0], x.shape[-1]
    T, CTX = x.shape[1], ctx.shape[1]
    cfg = _Cfg(D, T, CTX, 4 * w1.shape[2])
    ns_ada = w_ada.shape[2]
    mx, my, mc = _me()
    dev = 4 * mx + 2 * my + mc
    chip = 2 * mx + my

    silu_cc = jax.nn.silu(c_ctx)
    silu_all = _gather_small(jax.nn.silu(c).reshape(-1, 128)).reshape(8, D)
    s16 = jnp.concatenate([silu_cc[None], silu_all, jnp.zeros((7, D), F32)], axis=0)
    proj = jnp.stack([_ada_fwd(s16, w_ada, l) for l in range(L)])
    proj_all = _gather_small(proj.reshape(-1, 128)).reshape(8, L, 16, ns_ada)
    mods_full = jnp.concatenate([proj_all[2 * j] for j in range(4)], axis=-1) + b_ada[:, None, :]
    mods = [jnp.concatenate([mods_full[l, 0:1], lax.dynamic_slice_in_dim(mods_full[l], 1 + dev, 1, axis=0)], axis=0)
            for l in range(L)]

    sc = jnp.stack([mc, chip]).astype(jnp.int32)
    gathers = [_WeightGather(P, l, sc) for l in range(L)]
    X0 = jnp.concatenate([ctx[0], x[0]], axis=0)
    small_w = {k: P[k] for k in SMALL}
    loss_part, dX, shard_g, small, dmods, dfinal = _local_fwd_bwd(
        cfg, X0, loss_target[0], mods, gathers, lambda l, grads: _GradReduce(grads, sc), small_w, final_norm_g)
    loss = lax.psum(loss_part, ("x", "y", "c"))
    grad_x = dX[CTX:][None]

    outs = {k: None for k in BIG}
    for l in range(L):
        for k in BIG:
            outs[k] = _adam_layer(P[k], Mo[k], Vo[k], l, shard_g[l][k], outs[k])

    per_layer = [[small[l][k] for k in SMALL] + [dmods[l][1], dmods[l][0]] for l in range(L)]
    payload = _pack([a for lay in per_layer for a in lay] + [dfinal])
    gathered = _gather_small(payload)
    total = _sum8(gathered)
    shapes = [P[k].shape[1:] for k in SMALL] + [(6 * D,), (6 * D,)]
    tot = _unpack(total, shapes * L + [(D,)])
    per = len(shapes)
    g_small = {k: jnp.stack([tot[l * per + q] for l in range(L)]) for q, k in enumerate(SMALL)}
    dmx_sum = jnp.stack([tot[l * per + per - 2] for l in range(L)])
    dmc_sum = jnp.stack([tot[l * per + per - 1] for l in range(L)])
    g_small["b_ada"] = dmx_sum + dmc_sum
    g_small["final_norm_g"] = tot[-1]
    offs = np.cumsum([0] + [int(np.prod(s)) for s in shapes])
    lay_size = int(offs[-1])
    gflat = gathered.reshape(8, -1)
    s16t = s16.T
    ada_out, ds_part = None, jnp.zeros((16, D), F32)
    for l in range(L):
        dmx_all = gflat[:, l * lay_size + int(offs[per - 2]):l * lay_size + int(offs[per - 1])]
        dm_full = jnp.concatenate([dmc_sum[l][None], dmx_all, jnp.zeros((7, 6 * D), F32)], axis=0)
        dm = lax.dynamic_slice_in_dim(dm_full, chip * ns_ada, ns_ada, axis=1)
        dw, ds = _ada_bwd(s16t, dm, w_ada, l)
        ds_part = ds_part + ds
        ada_out = _adam_layer(w_ada, Mo["w_ada"], Vo["w_ada"], l, dw, ada_out)
    outs["w_ada"] = ada_out
    ds_all = _gather_small(ds_part[0].reshape(-1, 128)).reshape(8, D)
    d_silu_cc = ds_all[0] + ds_all[2] + ds_all[4] + ds_all[6]
    g_small["c_ctx"] = jax.vjp(jax.nn.silu, c_ctx)[1](d_silu_cc)[0]

    small_names = [k for k in names if k not in BIG and k != "w_ada"]
    sm_shapes = [P[k].shape for k in small_names]
    res = _adam_flat(_pack([P[k] for k in small_names]), _pack([g_small[k] for k in small_names]),
                     _pack([Mo[k] for k in small_names]), _pack([Vo[k] for k in small_names]))
    d_s, m_s, v_s = [_unpack(r, sm_shapes) for r in res]
    for q, k in enumerate(small_names):
        outs[k] = (g_small[k].reshape(P[k].shape), d_s[q], m_s[q], v_s[q])

    return (loss, grad_x, *[outs[k][0] for k in names], *[outs[k][1] for k in names], *[outs[k][2] for k in names],
            *[outs[k][3] for k in names])
```

```python
import functools

import numpy as np
import jax
import jax.numpy as jnp
from jax import lax
from jax.experimental import pallas as pl
from jax.experimental.pallas import tpu as pltpu

F32 = jnp.float32
BF16 = jnp.bfloat16
EPS = 1e-6
CH = 128
GRID_W = 64
ROPE_THETA = 10000.0
POOL_WINDOWS = (2, 4, 8, 16)
PT = 256
VMEM_LIMIT = 56 * 1024 * 1024
MESH = pl.DeviceIdType.MESH
ANY = pl.BlockSpec(memory_space=pl.ANY)

ADAM_LR = 0.001
ADAM_B1 = 0.9
ADAM_B2 = 0.999
ADAM_EPS = 1e-08
ADAM_WD = 0.01
ADAM_STEP = 10

BIG = ("w_in", "w_br", "w_bp", "w_bs", "w_out", "w1", "w2")
SHARD_AXIS = {"w_in": 1, "w_br": 1, "w_bp": 1, "w_bs": 1, "w_out": 0, "w1": 1, "w2": 0}


def _pick(dim, pref, mult=128):
    best = None
    for t in range(mult, min(dim, pref) + 1, mult):
        if dim % t == 0:
            best = t
    return dim if best is None else best


def _cparams(sem=None):
    return pltpu.CompilerParams(dimension_semantics=sem, vmem_limit_bytes=VMEM_LIMIT)


def _rows(i, tr):
    return i * tr + lax.broadcasted_iota(jnp.int32, (tr, 1), 0)


def _sel(i, tr, n_ctx, v2):
    return jnp.where(_rows(i, tr) < n_ctx, v2[0:1, :], v2[1:2, :])


def _seg_sums(i, tr, n_ctx, d):
    is_ctx = _rows(i, tr) < n_ctx
    s_c = jnp.sum(jnp.where(is_ctx, d, 0.0), axis=0, keepdims=True)
    s_x = jnp.sum(jnp.where(is_ctx, 0.0, d), axis=0, keepdims=True)
    two = lax.broadcasted_iota(jnp.int32, (2, d.shape[1]), 0)
    return jnp.where(two == 0, s_c, s_x)


def _dot(a, b, form="NN"):
    dims = {"NN": (((1,), (0,)), ((), ())), "NT": (((1,), (1,)), ((), ())), "TN": (((0,), (0,)), ((), ()))}[form]
    return lax.dot_general(a, b, dims, preferred_element_type=F32)


def _gelu(x):
    return 0.5 * x * (1.0 + jnp.tanh(0.7978845608028654 * (x + 0.044715 * x * x * x)))


def _matmul(name, form, pairs, *, R, C, tr, tc, tk, nk, out_dtypes, epi, extras=(), a_pro=None, comm=None):
    npair, nex, nout = len(pairs), len(extras), len(out_dtypes)
    in_specs, args = [], []
    for a, b in pairs:
        if nk == 1:
            ka = a.shape[0] if form == "TN" else a.shape[1]
        else:
            ka = tk
        if form == "NN":
            in_specs += [pl.BlockSpec((tr, ka), lambda i, j, k: (i, k)), pl.BlockSpec((ka, tc), lambda i, j, k: (k, j))]
        elif form == "NT":
            in_specs += [pl.BlockSpec((tr, ka), lambda i, j, k: (i, k)), pl.BlockSpec((tc, ka), lambda i, j, k: (j, k))]
        else:
            in_specs += [pl.BlockSpec((ka, tr), lambda i, j, k: (k, i)), pl.BlockSpec((ka, tc), lambda i, j, k: (k, j))]
        args += [a, b]
    for kind, arr, off in extras:
        if kind == "tile":
            in_specs.append(pl.BlockSpec((tr, tc), lambda i, j, k, off=off: (i, j + off)))
        else:
            in_specs.append(pl.BlockSpec((2, tc), lambda i, j, k, off=off: (0, j + off)))
        args.append(arr)

    direct = epi is None
    n_acc = 0 if (nk == 1 or direct) else npair
    n_main = len(args)
    ni, nj = R // tr, C // tc
    aliases = {}
    out_specs = [pl.BlockSpec((tr, tc), lambda i, j, k: (i, j)) for _ in out_dtypes]
    out_shape = [jax.ShapeDtypeStruct((R, C), dt) for dt in out_dtypes]
    scratch = [pltpu.VMEM((tr, tc), F32) for _ in range(n_acc)]
    n_cin = n_cout = 0
    if comm is not None:
        n_cin, n_cout = len(comm.ins), len(comm.outs)
        in_specs = in_specs + [ANY] * n_cin
        args = args + list(comm.ins)
        out_specs = out_specs + [ANY] * n_cout
        out_shape = out_shape + list(comm.outs)
        aliases = {n_main + a: nout + b for a, b in comm.aliases.items()}
        scratch = scratch + [pltpu.SemaphoreType.DMA((comm.nsem,)), pltpu.SemaphoreType.DMA((comm.nsem,))]

    def body(*refs):
        ab = refs[:2 * npair]
        ex = refs[2 * npair:n_main]
        cin = refs[n_main:n_main + n_cin]
        outs = refs[n_main + n_cin:n_main + n_cin + nout]
        cout = refs[n_main + n_cin + nout:n_main + n_cin + nout + n_cout]
        accs = refs[n_main + n_cin + nout + n_cout:n_main + n_cin + nout + n_cout + n_acc]
        sems = refs[n_main + n_cin + nout + n_cout + n_acc:]
        i, j, k = pl.program_id(0), pl.program_id(1), pl.program_id(2)

        if comm is not None:
            @pl.when(jnp.logical_and(jnp.logical_and(i == 0, j == 0), k == 0))
            def _():
                comm.start(cin, cout, *sems)

        def products():
            res = []
            for p in range(npair):
                a = ab[2 * p][...]
                if a_pro is not None:
                    a = a_pro(a)
                res.append(_dot(a, ab[2 * p + 1][...], form))
            return res

        def finish(vals):
            res = epi(vals, [e[...] for e in ex], i)
            for o, v in zip(outs, res):
                o[...] = v.astype(o.dtype)

        if direct:
            prod = products()[0]
            if nk == 1:
                outs[0][...] = prod
            else:
                @pl.when(k == 0)
                def _():
                    outs[0][...] = prod

                @pl.when(k > 0)
                def _():
                    outs[0][...] += prod
        elif nk == 1:
            finish(products())
        else:
            prods = products()

            @pl.when(k == 0)
            def _():
                for acc, v in zip(accs, prods):
                    acc[...] = v

            @pl.when(k > 0)
            def _():
                for acc, v in zip(accs, prods):
                    acc[...] += v

            @pl.when(k == nk - 1)
            def _():
                finish([acc[...] for acc in accs])

        if comm is not None:
            @pl.when(jnp.logical_and(jnp.logical_and(i == ni - 1, j == nj - 1), k == nk - 1))
            def _():
                comm.finish(cin, cout, *sems)

    res = pl.pallas_call(
        body, name=name, grid=(ni, nj, nk),
        in_specs=in_specs, out_specs=out_specs, out_shape=out_shape, scratch_shapes=scratch,
        input_output_aliases=aliases,
        compiler_params=_cparams(("arbitrary", "arbitrary", "arbitrary")),
    )(*args)
    return res if comm is None else (res[:nout], res[nout:])


def _rowcall(name, fn, *, nrows, tr, ins, outs, accs=(), ncol=1):
    n_in, n_out, n_acc = len(ins), len(outs), len(accs)
    in_specs, args = [], []
    for arr, kind, w, off in ins:
        if kind == "rows":
            in_specs.append(pl.BlockSpec((tr, w), lambda i, j, off=off: (i, off + j)))
        elif kind == "full":
            in_specs.append(pl.BlockSpec(arr.shape, lambda i, j, nd=arr.ndim: (0,) * nd))
        else:
            in_specs.append(pl.BlockSpec((None,) + arr.shape[1:], lambda i, j, f=off, nd=arr.ndim: (f(i),) + (0,) * (nd - 1)))
        args.append(arr)
    aliases = {}
    out_specs, out_shape = [], []
    for o_idx, (dt, total, w, off, alias) in enumerate(outs):
        out_specs.append(pl.BlockSpec((tr, w), lambda i, j, off=off: (i, off + j)))
        out_shape.append(jax.ShapeDtypeStruct((nrows, total), dt))
        if alias is not None:
            aliases[len(args)] = o_idx
            in_specs.append(ANY)
            args.append(alias)
    n_alias = len(aliases)
    for shp in accs:
        out_specs.append(pl.BlockSpec(shp, lambda i, j, nd=len(shp): (0,) * nd))
        out_shape.append(jax.ShapeDtypeStruct(shp, F32))

    def body(*refs):
        in_refs = refs[:n_in]
        out_refs = refs[n_in + n_alias:n_in + n_alias + n_out]
        acc_refs = refs[n_in + n_alias + n_out:]
        i, j = pl.program_id(0), pl.program_id(1)
        res = fn(i, j, *[r[...] for r in in_refs])
        for o, v in zip(out_refs, res[:n_out]):
            o[...] = v.astype(o.dtype)
        first = jnp.logical_and(i == 0, j == 0)
        for acc, v in zip(acc_refs, res[n_out:]):
            @pl.when(first)
            def _(acc=acc, v=v):
                acc[...] = v

            @pl.when(jnp.logical_not(first))
            def _(acc=acc, v=v):
                acc[...] += v

    res = pl.pallas_call(
        body, name=name, grid=(nrows // tr, ncol),
        in_specs=in_specs, out_specs=out_specs, out_shape=out_shape,
        input_output_aliases=aliases,
        compiler_params=_cparams(("arbitrary", "arbitrary")),
    )(*args)
    return res


def _f_normmod(x, g, shift, scale):
    r = lax.rsqrt(jnp.mean(x * x, axis=-1, keepdims=True) + EPS)
    return (x * r * g) * (1.0 + scale) + shift


def _f_headnorm_gate(o, zg, ng):
    r = lax.rsqrt(jnp.mean(o * o, axis=-1, keepdims=True) + EPS)
    return (o * r * ng) * (zg * jax.nn.sigmoid(zg))


def _f_sgv(sv, g):
    v = _gelu(sv)
    r = lax.rsqrt(jnp.mean(v * v, axis=-1, keepdims=True) + EPS)
    return v * r * g


def _rope(t, cos, sin):
    w = t.shape[1]
    lane = lax.broadcasted_iota(jnp.int32, t.shape, 1)
    swapped = jnp.where(jnp.bitwise_and(lane, 63) < 32,pltpu.roll(t, w - 32, 1), pltpu.roll(t, 32, 1))
    return t * cos + swapped * sin


def _rope_t(d, cos, sin):
    w = d.shape[1]
    lane = lax.broadcasted_iota(jnp.int32, d.shape, 1)
    ds = d * sin
    swapped = jnp.where(jnp.bitwise_and(lane, 63) < 32,pltpu.roll(ds, w - 32, 1), pltpu.roll(ds, 32, 1))
    return d * cos + swapped


class _Cfg:
    def __init__(self, D, T, CTX, DFF):
        self.D, self.T, self.CTX, self.DFF = D, T, CTX, DFF
        self.TA = T + CTX
        self.RW = D // 2
        self.H = self.RW // CH
        self.PW = D // 4
        self.SW = D // 4
        self.NIN = 4 * self.RW + self.PW + 2 * self.SW + 3 * D
        self.NC = self.TA // CH
        self.NCC = CTX // CH
        self.k_scale = float(CH) ** -0.5
        self.tr = _pick(self.TA, 1408)
        self.tr_small = _pick(self.TA, 768)
        self.o_g = 3 * self.RW
        self.o_p = 4 * self.RW
        self.o_u = self.o_p + self.PW
        self.o_sv = self.o_u + self.SW
        self.o_gate = self.o_sv + self.SW


def _normmod_fwd(cfg, X, g, mods, i_shift, i_scale, name):
    D = cfg.D
    tr = _pick(cfg.TA, 384)

    def fn(i, j, x, g, m):
        sh = _sel(i, tr, cfg.CTX, m[:, i_shift * D:(i_shift + 1) * D])
        sc = _sel(i, tr, cfg.CTX, m[:, i_scale * D:(i_scale + 1) * D])
        return (_f_normmod(x, g, sh, sc),)

    return _rowcall(name, fn, nrows=cfg.TA, tr=tr, ins=[(X, "rows", D, 0), (g, "full", 0, 0), (mods, "full", 0, 0)],
                    outs=[(BF16, D, D, 0, None)])[0]


def _normmod_bwd(cfg, X, dH, dres, g, mods, i_shift, i_scale, name):
    D = cfg.D
    tr = _pick(cfg.TA, 384)

    def fn(i, j, x, dh, dr, g, m):
        sh = _sel(i, tr, cfg.CTX, m[:, i_shift * D:(i_shift + 1) * D])
        sc = _sel(i, tr, cfg.CTX, m[:, i_scale * D:(i_scale + 1) * D])
        _, vjp = jax.vjp(_f_normmod, x, g, sh, sc)
        dx, dg, dsh, dsc = vjp(dh)
        return dr + dx, dg, _seg_sums(i, tr, cfg.CTX, dsh), _seg_sums(i, tr, cfg.CTX, dsc)

    return _rowcall(name, fn, nrows=cfg.TA, tr=tr,
                    ins=[(X, "rows", D, 0), (dH, "rows", D, 0), (dres, "rows", D, 0), (g, "full", 0, 0), (mods, "full", 0, 0)],
                    outs=[(F32, D, D, 0, None)], accs=[(1, D), (2, D), (2, D)])


def _resgate_bwd(cfg, dX, M, mods, i_gate, name):
    D = cfg.D
    tr = _pick(cfg.TA, 384)

    def fn(i, j, dx, m, mm):
        gate = _sel(i, tr, cfg.CTX, mm[:, i_gate * D:(i_gate + 1) * D])
        return dx * gate, _seg_sums(i, tr, cfg.CTX, dx * m.astype(F32))

    return _rowcall(name, fn, nrows=cfg.TA, tr=tr, ins=[(dX, "rows", D, 0), (M, "rows", D, 0), (mods, "full", 0, 0)],
                    outs=[(BF16, D, D, 0, None)], accs=[(2, D)])


def _chunk_of(cfg, d, t):
    fwd = t
    bwd = jnp.where(t < cfg.NCC, cfg.NCC - 1 - t, cfg.NC - 1 - t + cfg.NCC)
    return jnp.where(d == 0, fwd, bwd)


def _ret_specs(cfg, cm):
    RW, H = cfg.RW, cfg.H
    return [
        pl.BlockSpec((CH, RW), lambda d, t: (cm(d, t), 0)),
        pl.BlockSpec((CH, RW), lambda d, t: (cm(d, t), 1)),
        pl.BlockSpec((CH, RW), lambda d, t: (cm(d, t), 2)),
        pl.BlockSpec((CH, CH), lambda d, t: (cm(d, t), 0)),
        pl.BlockSpec((CH, CH), lambda d, t: (cm(d, t), 0)),
        pl.BlockSpec((None, H, CH, CH), lambda d, t: (d, 0, 0, 0)),
        pl.BlockSpec((None, CH, RW), lambda d, t: (d, 0, 0)),
        pl.BlockSpec((None, CH, RW), lambda d, t: (d, 0, 0)),
        pl.BlockSpec((None, 1, RW), lambda d, t: (d, 0, 0)),
    ]


def _ret_prep(cfg, q_ref, k_ref, ct_ref, st_ref, qd_ref, kd_ref):
    cos = jnp.tile(ct_ref[...], (1, cfg.H))
    sin = jnp.tile(st_ref[...], (1, cfg.H))
    qr = _rope(q_ref[...].astype(F32), cos, sin)
    kr = _rope(k_ref[...].astype(F32) * cfg.k_scale, cos, sin)
    return qr, kr, (qr * qd_ref[...]).astype(BF16), (kr * kd_ref[...]).astype(BF16)


def _ret_fwd(cfg, Z, tabs, decs):
    RW, H, TA, NC = cfg.RW, cfg.H, cfg.TA, cfg.NC
    cm = functools.partial(_chunk_of, cfg)

    def body(q_ref, k_ref, v_ref, ct_ref, st_ref, dm_ref, qd_ref, kd_ref, cd_ref, o_ref, so_ref, S):
        @pl.when(pl.program_id(1) == 0)
        def _():
            S[...] = jnp.zeros_like(S)

        qr, kr, qd, kd = _ret_prep(cfg, q_ref, k_ref, ct_ref, st_ref, qd_ref, kd_ref)
        qb, kb = qr.astype(BF16), kr.astype(BF16)
        v = v_ref[...].astype(BF16)
        for h in range(H):
            sl = slice(h * CH, (h + 1) * CH)
            p = (_dot(qb[:, sl], kb[:, sl], "NT") * dm_ref[h]).astype(BF16)
            s_h = S[h]
            so_ref[h] = s_h
            o_ref[:, sl] = _dot(p, v[:, sl]) + _dot(qd[:, sl], s_h.astype(BF16))
            S[h] = s_h * cd_ref[:, sl] + _dot(kd[:, sl], v[:, sl], "TN")

    return pl.pallas_call(
        body, name="ret_fwd", grid=(2, NC),
        in_specs=_ret_specs(cfg, cm),
        out_specs=[pl.BlockSpec((None, CH, RW), lambda d, t: (d, cm(d, t), 0)),
                   pl.BlockSpec((None, None, H, CH, CH), lambda d, t: (d, cm(d, t), 0, 0, 0))],
        out_shape=[jax.ShapeDtypeStruct((2, TA, RW), F32), jax.ShapeDtypeStruct((2, NC, H, CH, CH), F32)],
        scratch_shapes=[pltpu.VMEM((H, CH, CH), F32)],
        compiler_params=_cparams(("arbitrary", "arbitrary")),
    )(Z, Z, Z, tabs["cos"], tabs["sin"], decs["dmat"], decs["qdec"], decs["kdec"], decs["cdec"])


def _ret_bwd(cfg, Z, tabs, decs, states, dO):
    RW, H, TA, NC = cfg.RW, cfg.H, cfg.TA, cfg.NC

    def cm(d, t):
        return _chunk_of(cfg, d, NC - 1 - t)

    def body(q_ref, k_ref, v_ref, ct_ref, st_ref, dm_ref, qd_ref, kd_ref, cd_ref, s_ref, do_ref, w_ref,
             dqkv_ref, dl_ref, dS):
        t = pl.program_id(1)

        @pl.when(t == 0)
        def _():
            dS[...] = jnp.zeros_like(dS)
            dl_ref[...] = jnp.zeros_like(dl_ref)

        qr, kr, qd, kd = _ret_prep(cfg, q_ref, k_ref, ct_ref, st_ref, qd_ref, kd_ref)
        qb, kb = qr.astype(BF16), kr.astype(BF16)
        v = v_ref[...].astype(BF16)
        dob = do_ref[...].astype(BF16)
        for h in range(H):
            sl = slice(h * CH, (h + 1) * CH)
            dm = dm_ref[h]
            p = (_dot(qb[:, sl], kb[:, sl], "NT") * dm).astype(BF16)
            dp = (_dot(dob[:, sl], v[:, sl], "NT") * dm).astype(BF16)
            s_h = s_ref[h]
            ds_h = dS[h]
            sb, dsb = s_h.astype(BF16), ds_h.astype(BF16)
            dq_i = _dot(dp, kb[:, sl])
            dk_i = _dot(dp, qb[:, sl], "TN")
            dq_c = _dot(dob[:, sl], sb, "NT") * qd_ref[:, sl]
            dk_s = _dot(v[:, sl], dsb, "NT") * kd_ref[:, sl]
            dqkv_ref[:, sl] = dq_i + dq_c
            dqkv_ref[:, RW + h * CH:RW + (h + 1) * CH] = dk_i + dk_s
            dqkv_ref[:, 2 * RW + h * CH:2 * RW + (h + 1) * CH] = _dot(p, dob[:, sl], "TN") + _dot(kd[:, sl], dsb)
            qh, kh = qr[:, sl], kr[:, sl]
            lam = w_ref[0] * (qh * dq_i) + w_ref[1] * (qh * dq_c) + w_ref[2] * (kh * dk_i) + w_ref[3] * (kh * dk_s)
            lam_s = float(CH) * cd_ref[:, sl] * jnp.sum(ds_h * s_h, axis=0, keepdims=True)
            dl_ref[:, sl] += jnp.sum(lam, axis=0, keepdims=True) + lam_s
            dS[h] = ds_h * cd_ref[:, sl] + _dot(qd[:, sl], dob[:, sl], "TN")

    in_specs = _ret_specs(cfg, cm) + [
        pl.BlockSpec((None, None, H, CH, CH), lambda d, t: (d, cm(d, t), 0, 0, 0)),
        pl.BlockSpec((CH, RW), lambda d, t: (cm(d, t), 0)),
        pl.BlockSpec((None, 4, CH, CH), lambda d, t: (d, 0, 0, 0)),
    ]
    return pl.pallas_call(
        body, name="ret_bwd", grid=(2, NC),
        in_specs=in_specs,
        out_specs=[pl.BlockSpec((None, CH, 3 * RW), lambda d, t: (d, cm(d, t), 0)),
                   pl.BlockSpec((None, 1, RW), lambda d, t: (d, 0, 0))],
        out_shape=[jax.ShapeDtypeStruct((2, TA, 3 * RW), F32), jax.ShapeDtypeStruct((2, 1, RW), F32)],
        scratch_shapes=[pltpu.VMEM((H, CH, CH), F32)],
        compiler_params=_cparams(("arbitrary", "arbitrary")),
    )(Z, Z, Z, tabs["cos"], tabs["sin"], decs["dmat"], decs["qdec"], decs["kdec"], decs["cdec"], states, dO,
      tabs["lamw"])


def _rope_bwd(cfg, dqkv2, tabs, dz):
    RW, H = cfg.RW, cfg.H
    tr = PT

    def body(d0, d1, ct, st, dz_in, o):
        cos, sin = jnp.tile(ct[...], (1, H)), jnp.tile(st[...], (1, H))
        d = d0[...] + d1[...]
        o[:, :RW] = _rope_t(d[:, :RW], cos, sin).astype(o.dtype)
        o[:, RW:2 * RW] = (_rope_t(d[:, RW:2 * RW], cos, sin) * cfg.k_scale).astype(o.dtype)
        o[:, 2 * RW:] = d[:, 2 * RW:].astype(o.dtype)

    return pl.pallas_call(
        body, name="rope_bwd", grid=(cfg.TA // tr,),
        in_specs=[pl.BlockSpec((None, tr, 3 * RW), lambda i: (0, i, 0)), pl.BlockSpec((None, tr, 3 * RW), lambda i: (1, i, 0)),
                  pl.BlockSpec((tr, CH), lambda i: (i, 0)), pl.BlockSpec((tr, CH), lambda i: (i, 0)), ANY],
        out_specs=pl.BlockSpec((tr, 3 * RW), lambda i: (i, 0)),
        out_shape=jax.ShapeDtypeStruct((cfg.TA, cfg.NIN), BF16),
        input_output_aliases={4: 0},
        compiler_params=_cparams(("arbitrary",)),
    )(dqkv2, dqkv2, tabs["cos"], tabs["sin"], dz)


def _retout_fwd(cfg, o2, Z, ng):
    RW, H = cfg.RW, cfg.H
    tr = PT

    def body(o0, o1, zg, ng, out):
        o = o0[...] + o1[...]
        z = zg[...].astype(F32)
        for h in range(H):
            sl = slice(h * CH, (h + 1) * CH)
            out[:, sl] = _f_headnorm_gate(o[:, sl], z[:, sl], ng[:, sl]).astype(out.dtype)

    return pl.pallas_call(
        body, name="retout_fwd", grid=(cfg.TA // tr,),
        in_specs=[pl.BlockSpec((None, tr, RW), lambda i: (0, i, 0)), pl.BlockSpec((None, tr, RW), lambda i: (1, i, 0)),
                  pl.BlockSpec((tr, RW), lambda i: (i, 3)), pl.BlockSpec((1, RW), lambda i: (0, 0))],
        out_specs=pl.BlockSpec((tr, RW), lambda i: (i, 0)),
        out_shape=jax.ShapeDtypeStruct((cfg.TA, RW), BF16),
        compiler_params=_cparams(("arbitrary",)),
    )(o2, o2, Z, ng)


def _retout_bwd(cfg, o2, Z, ng, dret, dz):
    RW, H = cfg.RW, cfg.H
    tr = PT

    def body(o0, o1, zg, ng, dr, dz_in, do_out, dz_out, dng):
        i = pl.program_id(0)
        o = o0[...] + o1[...]
        z = zg[...].astype(F32)
        d = dr[...]
        acc = []
        for h in range(H):
            sl = slice(h * CH, (h + 1) * CH)
            _, vjp = jax.vjp(_f_headnorm_gate, o[:, sl], z[:, sl], ng[:, sl])
            do_h, dz_h, dg_h = vjp(d[:, sl])
            do_out[:, sl] = do_h
            dz_out[:, sl] = dz_h.astype(dz_out.dtype)
            acc.append(dg_h)

        @pl.when(i == 0)
        def _():
            for h in range(H):
                dng[:, h * CH:(h + 1) * CH] = acc[h]

        @pl.when(i > 0)
        def _():
            for h in range(H):
                dng[:, h * CH:(h + 1) * CH] += acc[h]

    return pl.pallas_call(
        body, name="retout_bwd", grid=(cfg.TA // tr,),
        in_specs=[pl.BlockSpec((None, tr, RW), lambda i: (0, i, 0)), pl.BlockSpec((None, tr, RW), lambda i: (1, i, 0)),
                  pl.BlockSpec((tr, RW), lambda i: (i, 3)), pl.BlockSpec((1, RW), lambda i: (0, 0)),
                  pl.BlockSpec((tr, RW), lambda i: (i, 0)), ANY],
        out_specs=[pl.BlockSpec((tr, RW), lambda i: (i, 0)), pl.BlockSpec((tr, RW), lambda i: (i, 3)),
                   pl.BlockSpec((1, RW), lambda i: (0, 0))],
        out_shape=[jax.ShapeDtypeStruct((cfg.TA, RW), F32), jax.ShapeDtypeStruct((cfg.TA, cfg.NIN), BF16),
                   jax.ShapeDtypeStruct((1, RW), F32)],
        input_output_aliases={5: 1},
        compiler_params=_cparams(("arbitrary",)),
    )(o2, o2, Z, ng, dret, dz)


def _pool_consts(ctx_len):
    assert ctx_len == PT
    bm = np.zeros((2, len(POOL_WINDOWS), PT, PT), np.float32)
    ic = np.zeros((2, len(POOL_WINDOWS), PT, CH), np.float32)
    for ty, seg in enumerate((ctx_len, GRID_W)):
        for gi, w in enumerate(POOL_WINDOWS):
            for r in range(PT):
                s0, pos = (r // seg) * seg, r % seg
                lo, hi = max(pos - w // 2, 0), min(pos + w // 2 - 1, seg - 1)
                bm[ty, gi, r, s0 + lo:s0 + hi + 1] = 1.0
                ic[ty, gi, r, :] = 1.0 / (hi - lo + 1)
    return jnp.asarray(bm, BF16), jnp.asarray(ic, F32)


def _pool_tile(p, bm, ic, pw, g):
    sl = slice(g * CH, (g + 1) * CH)
    pg = p[:, sl].astype(F32)
    hi = pg.astype(BF16)
    lo = (pg - hi.astype(F32)).astype(BF16)
    y = (_dot(bm[g], hi) + _dot(bm[g], lo)) * ic[g] - pg
    return y, _dot(y.astype(BF16), pw[g].astype(BF16))


def _pool_fwd(cfg, Z, consts, pool_w, pool_scale):
    PW = cfg.PW
    G = PW // CH
    nct = cfg.CTX // PT
    ty = lambda i: jnp.where(i < nct, 0, 1)

    def fn(i, j, p, bm, ic, pw, ps):
        outs = [_pool_tile(p, bm, ic, pw, g)[1] for g in range(G)]
        return (jnp.concatenate(outs, axis=1) * ps,)

    return _rowcall("pool_fwd", fn, nrows=cfg.TA, tr=PT,
                    ins=[(Z, "rows", PW, cfg.o_p // PW), (consts[0], "sel", 0, ty), (consts[1], "sel", 0, ty),
                         (pool_w, "full", 0, 0), (pool_scale, "full", 0, 0)],
                    outs=[(BF16, PW, PW, 0, None)])[0]


def _pool_bwd(cfg, Z, consts, pool_w, pool_scale, dpool, dz):
    PW = cfg.PW
    G = PW // CH
    nct = cfg.CTX // PT
    ty = lambda i: jnp.where(i < nct, 0, 1)

    def fn(i, j, p, bm, ic, pw, ps, dout):
        dps, dps_acc, dpw = [], [], []
        for g in range(G):
            sl = slice(g * CH, (g + 1) * CH)
            y, lin = _pool_tile(p, bm, ic, pw, g)
            dlin = (dout[:, sl] * ps[:, sl]).astype(BF16)
            dps_acc.append(jnp.sum(dout[:, sl] * lin, axis=0, keepdims=True))
            dy = _dot(dlin, pw[g].astype(BF16), "NT")
            dpw.append(_dot(y.astype(BF16), dlin, "TN"))
            t = dy * ic[g]
            hi = t.astype(BF16)
            lo = (t - hi.astype(F32)).astype(BF16)
            dps.append(_dot(bm[g], hi, "TN") + _dot(bm[g], lo, "TN") - dy)
        return (jnp.concatenate(dps, axis=1), jnp.stack(dpw), jnp.concatenate(dps_acc, axis=1))

    return _rowcall("pool_bwd", fn, nrows=cfg.TA, tr=PT,
                    ins=[(Z, "rows", PW, cfg.o_p // PW), (consts[0], "sel", 0, ty), (consts[1], "sel", 0, ty),
                         (pool_w, "full", 0, 0), (pool_scale, "full", 0, 0), (dpool, "rows", PW, 0)],
                    outs=[(BF16, cfg.NIN, PW, cfg.o_p // PW, dz)], accs=[(G, CH, CH), (1, PW)])


def _sg_mixed(vn, sw, sbb, g, c):
    rows = slice(c * CH, (c + 1) * CH)
    sl = slice(g * CH, (g + 1) * CH)
    return _dot(sw[g].astype(BF16), vn[rows, sl].astype(BF16)) + sbb[g]


def _sg_fwd(cfg, Z, sng, sw, sbb):
    SW = cfg.SW
    G = SW // CH

    def fn(i, j, u, sv, sng, sw, sbb):
        ug = _gelu(u.astype(F32))
        vn = _f_sgv(sv.astype(F32), sng)
        rows = []
        for c in range(PT // CH):
            mixed = jnp.concatenate([_sg_mixed(vn, sw, sbb, g, c) for g in range(G)], axis=1)
            rows.append(ug[c * CH:(c + 1) * CH, :] * mixed)
        return (jnp.concatenate(rows, axis=0),)

    return _rowcall("sg_fwd", fn, nrows=cfg.TA, tr=PT,
                    ins=[(Z, "rows", SW, cfg.o_u // SW), (Z, "rows", SW, cfg.o_sv // SW), (sng, "full", 0, 0),
                         (sw, "full", 0, 0), (sbb, "full", 0, 0)],
                    outs=[(BF16, SW, SW, 0, None)])[0]


def _sg_bwd(cfg, Z, sng, sw, sbb, dsg, dz):
    SW = cfg.SW
    G = SW // CH

    def fn(i, j, u, sv, sng, sw, sbb, dout):
        uf, svf = u.astype(F32), sv.astype(F32)
        ug, vjp_u = jax.vjp(_gelu, uf)
        vn, vjp_v = jax.vjp(_f_sgv, svf, sng)
        dug_rows, dvn_rows = [], []
        dsw = [jnp.zeros((CH, CH), F32) for _ in range(G)]
        dsb = [jnp.zeros((CH, CH), F32) for _ in range(G)]
        for c in range(PT // CH):
            rows = slice(c * CH, (c + 1) * CH)
            dug_g, dvn_g = [], []
            for g in range(G):
                sl = slice(g * CH, (g + 1) * CH)
                mixed = _sg_mixed(vn, sw, sbb, g, c)
                dmixed = dout[rows, sl] * ug[rows, sl]
                dug_g.append(dout[rows, sl] * mixed)
                dmb = dmixed.astype(BF16)
                dvn_g.append(_dot(sw[g].astype(BF16), dmb, "TN"))
                dsw[g] = dsw[g] + _dot(dmb, vn[rows, sl].astype(BF16), "NT")
                dsb[g] = dsb[g] + jnp.broadcast_to(jnp.sum(dmixed, axis=1, keepdims=True), (CH, CH))
            dug_rows.append(jnp.concatenate(dug_g, axis=1))
            dvn_rows.append(jnp.concatenate(dvn_g, axis=1))
        (du,) = vjp_u(jnp.concatenate(dug_rows, axis=0))
        dsv, dsng = vjp_v(jnp.concatenate(dvn_rows, axis=0))
        return du, dsv, jnp.stack(dsw), jnp.stack(dsb), dsng

    return _rowcall("sg_bwd", fn, nrows=cfg.TA, tr=PT,
                    ins=[(Z, "rows", SW, cfg.o_u // SW), (Z, "rows", SW, cfg.o_sv // SW), (sng, "full", 0, 0),
                         (sw, "full", 0, 0), (sbb, "full", 0, 0), (dsg, "rows", SW, 0)],
                    outs=[(BF16, cfg.NIN, SW, cfg.o_u // SW, dz), (BF16, SW, SW, 0, None)],
                    accs=[(G, CH, CH), (G, CH, CH), (1, SW)])


def _gate_bwd(cfg, b, dY, L, Z, dz):
    D = cfg.D
    tc = 512
    nj = D // tc
    off = cfg.o_gate // tc + b * nj

    def fn(i, j, dy, l, z):
        s = jax.nn.sigmoid(z.astype(F32))
        return dy * s, dy * l.astype(F32) * s * (1.0 - s)

    return _rowcall("gate_bwd", fn, nrows=cfg.TA, tr=cfg.tr,
                    ins=[(dY, "rows", tc, 0), (L, "rows", tc, 0), (Z, "rows", tc, off)],
                    outs=[(BF16, D, tc, 0, None), (BF16, cfg.NIN, tc, off, dz)], ncol=nj)


def _copy_cols(cfg, src, dz, col0):
    w = src.shape[1]
    return _rowcall("copy_cols", lambda i, j, s: (s,), nrows=cfg.TA, tr=cfg.tr, ins=[(src, "rows", w, 0)],
                    outs=[(dz.dtype, dz.shape[1], w, col0 // w, dz)])[0]


def _final(cfg, X, g, target):
    D = cfg.D
    tr = PT
    nct = cfg.CTX // tr

    def body(x_ref, g_ref, t_ref, dx_ref, loss_ref, dg_ref):
        i = pl.program_id(0)

        def f(x, g):
            r = lax.rsqrt(jnp.mean(x * x, axis=-1, keepdims=True) + EPS)
            return x * r * g

        y, vjp = jax.vjp(f, x_ref[...], g_ref[...])
        err = y - t_ref[...]
        dx, dg = vjp(err * (1.0 / D))
        part = 0.5 * jnp.sum(jnp.mean(err * err, axis=-1, keepdims=True), axis=0, keepdims=True)

        @pl.when(i == 0)
        def _():
            loss_ref[...] = jnp.zeros_like(loss_ref)
            dg_ref[...] = jnp.zeros_like(dg_ref)

        @pl.when(i < nct)
        def _():
            dx_ref[...] = jnp.zeros_like(dx_ref)

        @pl.when(i >= nct)
        def _():
            dx_ref[...] = dx
            loss_ref[...] += jnp.broadcast_to(part, loss_ref.shape)
            dg_ref[...] += dg

    return pl.pallas_call(
        body, name="final", grid=(cfg.TA // tr,),
        in_specs=[pl.BlockSpec((tr, D), lambda i: (i, 0)), pl.BlockSpec((1, D), lambda i: (0, 0)),
                  pl.BlockSpec((tr, D), lambda i: (jnp.maximum(i - nct, 0), 0))],
        out_specs=[pl.BlockSpec((tr, D), lambda i: (i, 0)), pl.BlockSpec((8, CH), lambda i: (0, 0)),
                   pl.BlockSpec((1, D), lambda i: (0, 0))],
        out_shape=[jax.ShapeDtypeStruct((cfg.TA, D), F32), jax.ShapeDtypeStruct((8, CH), F32),
                   jax.ShapeDtypeStruct((1, D), F32)],
        compiler_params=_cparams(("arbitrary",)),
    )(X, g, target)


GATHER_A = ("w_in", "w1")
GATHER_B = ("w2", "w_out", "w_br", "w_bp", "w_bs")
REDUCE_MLP = ("w2", "w1")
REDUCE_MIX = ("w_out", "w_br", "w_bp", "w_bs", "w_in")


def _hosted(res, comm, sink):
    if comm is None:
        return res
    main, outs = res
    sink(outs)
    return main


def _layer_fwd(cfg, X, mods, W, sm, tabs, decs, consts, nxt=None):
    D, TA, tr, RW, PW, SW, DFF, NIN = cfg.D, cfg.TA, cfg.tr, cfg.RW, cfg.PW, cfg.SW, cfg.DFF, cfg.NIN
    ident = lambda accs, ex, i: (accs[0],)
    H1 = _normmod_fwd(cfg, X, sm["norm1_g"], mods, 0, 1, "normmod1_fwd")
    comm = nxt.ici(GATHER_A) if nxt else None
    (Z,) = _hosted(_matmul("z_mm", "NN", [(H1, W["w_in"])], R=TA, C=NIN, tr=tr, tc=512, tk=D, nk=1, out_dtypes=[F32],
                           epi=ident, comm=comm), comm, lambda o: nxt.update(GATHER_A, o))
    o2, states = _ret_fwd(cfg, Z, tabs, decs)
    ret = _retout_fwd(cfg, o2, Z, sm["ret_norm_g"])
    pool = _pool_fwd(cfg, Z, consts, sm["pool_w"], sm["pool_scale"])
    sg = _sg_fwd(cfg, Z, sm["sg_norm_g"], sm["sg_w"], sm["sg_bb"])

    tc = 512
    goff = cfg.o_gate // tc

    def epi_branch(accs, ex, i):
        y = sum(jax.nn.sigmoid(z.astype(F32)) * a for a, z in zip(accs, ex))
        return (y, accs[0], accs[1], accs[2])

    Y, Lr, Lp, Ls = _matmul("branch_mm", "NN", [(ret, W["w_br"]), (pool, W["w_bp"]), (sg, W["w_bs"])], R=TA, C=D, tr=cfg.tr_small,
                            tc=tc, tk=0, nk=1, out_dtypes=[BF16] * 4, epi=epi_branch,
                            extras=[("tile", Z, goff + b * (D // tc)) for b in range(3)])

    def epi_res(accs, ex, i):
        return (ex[0] + _sel(i, tr, cfg.CTX, ex[1]) * accs[0], accs[0])

    X2, O = _matmul("out_mm", "NN", [(Y, W["w_out"])], R=TA, C=D, tr=tr, tc=tc, tk=D, nk=1, out_dtypes=[F32, BF16],
                    epi=epi_res, extras=[("tile", X, 0), ("rows2", mods, 2 * (D // tc))])
    H2 = _normmod_fwd(cfg, X2, sm["norm2_g"], mods, 3, 4, "normmod2_fwd")
    tcf = _pick(DFF, 1024)
    def epi_relu(accs, ex, i):
        r = jnp.maximum(accs[0], 0.0)
        return (r * r, r)

    comm = nxt.ici(GATHER_B) if nxt else None
    A2, Rr = _hosted(_matmul("w1_mm", "NN", [(H2, W["w1"])], R=TA, C=DFF, tr=tr, tc=tcf, tk=D, nk=1, out_dtypes=[BF16, BF16],
                             epi=epi_relu, comm=comm), comm, lambda o: nxt.update(GATHER_B, o))
    tkf = _pick(DFF, 2048)
    comm = nxt.d2d(BIG) if nxt else None
    X3, M = _hosted(_matmul("w2_mm", "NN", [(A2, W["w2"])], R=TA, C=D, tr=tr, tc=tc, tk=tkf, nk=DFF // tkf,
                            out_dtypes=[F32, BF16], epi=epi_res, extras=[("tile", X2, 0), ("rows2", mods, 5 * (D // tc))],
                            comm=comm), comm, lambda o: nxt.update(BIG, o))
    saved = dict(X=X, H1=H1, Z=Z, o2=o2, states=states, ret=ret, pool=pool, sg=sg, Y=Y, L=(Lr, Lp, Ls), O=O, X2=X2,
                 H2=H2, R=Rr, A2=A2, M=M)
    return X3, saved


def _layer_bwd(cfg, dX3, sv, mods, W, sm, tabs, decs, consts, rs, make_reduce):
    D, TA, tr, RW, PW, SW, DFF, NIN = cfg.D, cfg.TA, cfg.tr, cfg.RW, cfg.PW, cfg.SW, cfg.DFF, cfg.NIN
    ident = lambda accs, ex, i: (accs[0],)
    tw = 512
    g = {}

    dM, dgate2 = _resgate_bwd(cfg, dX3, sv["M"], mods, 5, "resgate2_bwd")
    tcf = _pick(DFF, 1024)
    comm = rs.swap() if rs else None
    (dPre,) = _hosted(_matmul("dpre_mm", "NT", [(dM, W["w2"])], R=TA, C=DFF, tr=tr, tc=tcf, tk=D, nk=1, out_dtypes=[BF16],
                              epi=lambda accs, ex, i: (accs[0] * (2.0 * ex[0].astype(F32)),), extras=[("tile", sv["R"], 0)],
                              comm=comm), comm, lambda o: rs.swapped(o))
    comm = rs.scatter() if rs else None
    (g["w2"],) = _hosted(_matmul("dw2_mm", "TN", [(sv["A2"], dM)], R=DFF, C=D, tr=tw, tc=tw, tk=TA, nk=1, out_dtypes=[F32],
                                 epi=None, comm=comm), comm, lambda o: rs.scattered(o))
    tkf = _pick(DFF, 2048)
    (dH2,) = _matmul("dh2_mm", "NT", [(dPre, W["w1"])], R=TA, C=D, tr=tr, tc=1024, tk=tkf, nk=DFF // tkf, out_dtypes=[F32],
                     epi=ident)
    (g["w1"],) = _matmul("dw1_mm", "TN", [(sv["H2"], dPre)], R=D, C=DFF, tr=tw, tc=tw, tk=TA, nk=1, out_dtypes=[F32], epi=None)
    mlp = make_reduce(REDUCE_MLP, {k: g.pop(k) for k in REDUCE_MLP})
    dX2, dn2, dsh2, dsc2 = _normmod_bwd(cfg, sv["X2"], dH2, dX3, sm["norm2_g"], mods, 3, 4, "normmod2_bwd")

    dO, dgate1 = _resgate_bwd(cfg, dX2, sv["O"], mods, 2, "resgate1_bwd")
    comm = mlp.swap()
    (dY,) = _hosted(_matmul("dy_mm", "NT", [(dO, W["w_out"])], R=TA, C=D, tr=tr, tc=1024, tk=D, nk=1, out_dtypes=[F32],
                            epi=ident, comm=comm), comm, lambda o: mlp.swapped(o))
    (g["w_out"],) = _matmul("dwout_mm", "TN", [(sv["Y"], dO)], R=D, C=D, tr=tw, tc=tw, tk=TA, nk=1,
                            out_dtypes=[F32], epi=None)
    dLr, dz = _gate_bwd(cfg, 0, dY, sv["L"][0], sv["Z"], None)
    dLp, dz = _gate_bwd(cfg, 1, dY, sv["L"][1], sv["Z"], dz)
    dLs, dz = _gate_bwd(cfg, 2, dY, sv["L"][2], sv["Z"], dz)

    (dret,) = _matmul("dret_mm", "NT", [(dLr, W["w_br"])], R=TA, C=RW, tr=tr, tc=RW, tk=D, nk=1, out_dtypes=[F32], epi=ident)
    (dpool,) = _matmul("dpool_mm", "NT", [(dLp, W["w_bp"])], R=TA, C=PW, tr=tr, tc=PW, tk=D, nk=1, out_dtypes=[F32], epi=ident)
    (dsg,) = _matmul("dsg_mm", "NT", [(dLs, W["w_bs"])], R=TA, C=SW, tr=tr, tc=SW, tk=D, nk=1, out_dtypes=[F32], epi=ident)
    (g["w_br"],) = _matmul("dwbr_mm", "TN", [(sv["ret"], dLr)], R=RW, C=D, tr=tw, tc=tw, tk=TA, nk=1, out_dtypes=[F32],
                           epi=None)
    (g["w_bp"],) = _matmul("dwbp_mm", "TN", [(sv["pool"], dLp)], R=PW, C=D, tr=tw, tc=tw, tk=TA, nk=1, out_dtypes=[F32],
                           epi=None)
    (g["w_bs"],) = _matmul("dwbs_mm", "TN", [(sv["sg"], dLs)], R=SW, C=D, tr=tw, tc=tw, tk=TA, nk=1, out_dtypes=[F32],
                           epi=None)
    dOr, dz, dretng = _retout_bwd(cfg, sv["o2"], sv["Z"], sm["ret_norm_g"], dret, dz)
    dqkv2, dlam = _ret_bwd(cfg, sv["Z"], tabs, decs, sv["states"], dOr)
    dz = _rope_bwd(cfg, dqkv2, tabs, dz)
    dz, dpw, dps = _pool_bwd(cfg, sv["Z"], consts, sm["pool_w"], sm["pool_scale"], dpool, dz)
    dz, dz_sv, dsw, dsb, dsng = _sg_bwd(cfg, sv["Z"], sm["sg_norm_g"], sm["sg_w"], sm["sg_bb"], dsg, dz)
    dz = _copy_cols(cfg, dz_sv, dz, cfg.o_sv)

    tkz = _pick(NIN, 2944)
    comm = mlp.scatter()
    (dH1,) = _hosted(_matmul("dh1_mm", "NT", [(dz, W["w_in"])], R=TA, C=D, tr=tr, tc=512, tk=tkz, nk=NIN // tkz,
                             out_dtypes=[F32], epi=ident, comm=comm), comm, lambda o: mlp.scattered(o))
    (g["w_in"],) = _matmul("dwin_mm", "TN", [(sv["H1"], dz)], R=D, C=NIN, tr=tw, tc=tw, tk=TA, nk=1,
                           out_dtypes=[F32], epi=None)
    dX, dn1, dsh1, dsc1 = _normmod_bwd(cfg, sv["X"], dH1, dX2, sm["norm1_g"], mods, 0, 1, "normmod1_bwd")
    dmods = jnp.concatenate([dsh1, dsc1, dgate1, dsh2, dsc2, dgate2], axis=1)
    small = dict(norm1_g=dn1, norm2_g=dn2, ret_norm_g=dretng, pool_w=dpw, pool_scale=dps, sg_norm_g=dsng, sg_w=dsw,
                 sg_b=dsb[:, :, 0], dlam=dlam)
    return dX, g, mlp.finish(), small, dmods


def _tables(cfg):
    nf = CH // 4
    inv = ROPE_THETA ** (-jnp.arange(nf, dtype=F32) / nf)
    tok = jnp.arange(cfg.T)
    ar = (tok // GRID_W).astype(F32)[:, None] * inv[None]
    ac = (tok % GRID_W).astype(F32)[:, None] * inv[None]
    cos = jnp.concatenate([jnp.cos(ar), jnp.cos(ar), jnp.cos(ac), jnp.cos(ac)], axis=1)
    sin = jnp.concatenate([-jnp.sin(ar), jnp.sin(ar), -jnp.sin(ac), jnp.sin(ac)], axis=1)
    cos = jnp.concatenate([jnp.ones((cfg.CTX, CH), F32), cos], axis=0)
    sin = jnp.concatenate([jnp.zeros((cfg.CTX, CH), F32), sin], axis=0)
    idx = np.broadcast_to(np.arange(CH, dtype=np.float32)[:, None], (CH, CH))
    lamw = np.stack([np.stack([idx, idx + 1.0, -idx, CH - 1.0 - idx]), np.stack([-idx, CH - idx, idx, idx])])
    return dict(cos=cos, sin=sin, lamw=jnp.asarray(lamw, F32))


def _decays(cfg, logit):
    H, RW = cfg.H, cfg.RW
    lam = jax.nn.log_sigmoid(logit.astype(F32))
    idx = jnp.arange(CH, dtype=F32)
    dist = idx[:, None] - idx[None, :]
    d0 = jnp.where(dist >= 0, jnp.exp(lam[0][:, None, None] * jnp.maximum(dist, 0.0)), 0.0)
    d1 = jnp.where(dist <= 0, jnp.exp(lam[1][:, None, None] * jnp.maximum(-dist, 0.0)), 0.0)
    lanes = lambda a: jnp.repeat(a.T, CH, axis=1)
    qdec = jnp.stack([lanes(jnp.exp(lam[0][:, None] * (idx + 1.0)[None])), lanes(jnp.exp(lam[1][:, None] * (CH - idx)[None]))])
    kdec = jnp.stack([lanes(jnp.exp(lam[0][:, None] * (CH - 1.0 - idx)[None])), lanes(jnp.exp(lam[1][:, None] * idx[None]))])
    cdec = jnp.repeat(jnp.exp(lam * CH), CH, axis=1)[:, None, :]
    return dict(dmat=jnp.stack([d0, d1]), qdec=qdec, kdec=kdec, cdec=cdec)


def _small_of_layer(small_w, l):
    sm = {k: v[l] for k, v in small_w.items()}
    sm["norm1_g"] = sm["norm1_g"][None]
    sm["norm2_g"] = sm["norm2_g"][None]
    sm["ret_norm_g"] = sm["ret_norm_g"][None]
    sm["pool_scale"] = sm["pool_scale"][None]
    sm["sg_norm_g"] = sm["sg_norm_g"][None]
    sm["sg_bb"] = jnp.broadcast_to(sm["sg_b"][:, :, None], sm["sg_b"].shape + (CH,))
    return sm


def _local_fwd_bwd(cfg, X0, target, mods, gathers, make_reduce, small_w, final_g):
    depth = len(mods)
    tabs = _tables(cfg)
    consts = _pool_consts(cfg.CTX)
    X, saved, Ws, sms, decs = X0, [], [], [], []
    gathers[0].run()
    for l in range(depth):
        Ws.append(gathers[l].weights())
        sms.append(_small_of_layer(small_w, l))
        decs.append(_decays(cfg, small_w["ret_decay_logit"][l]))
        X, sv = _layer_fwd(cfg, X, mods[l], Ws[l], sms[l], tabs, decs[l], consts, gathers[l + 1] if l + 1 < depth else None)
        saved.append(sv)
    dX, loss_acc, dfinal = _final(cfg, X, final_g[None], target)
    shard_g, small, dmods = [None] * depth, [None] * depth, [None] * depth
    pending = None
    for l in reversed(range(depth)):
        dX, mix, shard_g[l], small[l], dmods[l] = _layer_bwd(cfg, dX, saved[l], mods[l], Ws[l], sms[l], tabs, decs[l], consts,
                                                             pending, make_reduce)
        if pending is not None:
            shard_g[l + 1].update(pending.finish())
        pending = make_reduce(REDUCE_MIX, mix)
        lam_grad = jnp.sum(small[l].pop("dlam").reshape(2, cfg.H, CH), axis=-1)
        small[l]["ret_decay_logit"] = lam_grad * jax.nn.sigmoid(-small_w["ret_decay_logit"][l].astype(F32))
    shard_g[0].update(pending.run())
    return loss_acc[0, 0], dX, shard_g, small, dmods, dfinal


def _me():
    return lax.axis_index("x"), lax.axis_index("y"), lax.axis_index("c")


def _other_chips(x, y):
    return [(1 - x, y), (x, 1 - y), (1 - x, 1 - y)]


def _rcopy(src, dst, send_sem, recv_sem, dev):
    return pltpu.make_async_remote_copy(src_ref=src, dst_ref=dst, send_sem=send_sem, recv_sem=recv_sem,
                                        device_id=dev, device_id_type=MESH)


def _half(ref, axis, c):
    k, n = ref.shape
    if axis == 1:
        return ref.at[pl.ds(c * (k // 2), k // 2), :]
    return ref.at[:, pl.ds(c * (n // 2), n // 2)]


def _chip_part(ref, axis, j):
    k, n = ref.shape
    if axis == 1:
        return ref.at[:, pl.ds(j * (n // 4), n // 4)]
    return ref.at[pl.ds(j * (k // 4), k // 4), :]


def _piece(ref, axis, j, c):
    k, n = ref.shape
    if axis == 1:
        return ref.at[pl.ds(c * (k // 2), k // 2), pl.ds(j * (n // 4), n // 4)]
    return ref.at[pl.ds(j * (k // 4), k // 4), pl.ds(c * (n // 2), n // 2)]


def _gather_weights(fulls, axes):
    n = len(fulls)

    def body(*refs):
        outs = refs[n:2 * n]
        send, recv = refs[2 * n:]
        x, y, c = _me()
        j = 2 * x + y
        sib = (x, y, 1 - c)
        chips = _other_chips(x, y)
        first = []
        for t in range(n):
            for k, chip in enumerate(chips):
                own = _piece(outs[t], axes[t], j, c)
                first.append(_rcopy(own, own, send.at[6 * t + k], recv.at[6 * t + k], (*chip, c)))
                first[-1].start()
        passed = []
        for t in range(n):
            for k, chip in enumerate(chips):
                landed = _piece(outs[t], axes[t], 2 * chip[0] + chip[1], c)
                _rcopy(landed, landed, send.at[6 * t + k], recv.at[6 * t + k], sib).wait_recv()
                passed.append(_rcopy(landed, landed, send.at[6 * t + 3 + k], recv.at[6 * t + 3 + k], sib))
                passed[-1].start()
        for t in range(n):
            for k, chip in enumerate(chips):
                theirs = _piece(outs[t], axes[t], 2 * chip[0] + chip[1], 1 - c)
                _rcopy(theirs, theirs, send.at[6 * t + 3 + k], recv.at[6 * t + 3 + k], sib).wait_recv()
        for cp in first + passed:
            cp.wait_send()

    return pl.pallas_call(
        body, name="gather_weights",
        in_specs=[ANY] * n, out_specs=[ANY] * n,
        out_shape=[jax.ShapeDtypeStruct(f.shape, f.dtype) for f in fulls],
        input_output_aliases={t: t for t in range(n)},
        scratch_shapes=[pltpu.SemaphoreType.DMA((6 * n,)), pltpu.SemaphoreType.DMA((6 * n,))],
    )(*fulls)


class _Comm:
    def __init__(self, ins, outs, aliases, nsem, start, finish):
        self.ins, self.outs, self.aliases, self.nsem, self.start, self.finish = ins, outs, aliases, nsem, start, finish


def _run_comm(name, comm):
    n_in = len(comm.ins)

    def body(*refs):
        ins, outs = refs[:n_in], refs[n_in:n_in + len(comm.outs)]
        send, recv = refs[n_in + len(comm.outs):]
        comm.start(ins, outs, send, recv)
        comm.finish(ins, outs, send, recv)

    return pl.pallas_call(
        body, name=name, in_specs=[ANY] * n_in, out_specs=[ANY] * len(comm.outs), out_shape=list(comm.outs),
        input_output_aliases=dict(comm.aliases),
        scratch_shapes=[pltpu.SemaphoreType.DMA((comm.nsem,)), pltpu.SemaphoreType.DMA((comm.nsem,))],
    )(*comm.ins)


def _like(arrs):
    return [jax.ShapeDtypeStruct(a.shape, a.dtype) for a in arrs]


def _gather_ici_comm(fulls, axes):
    n = len(fulls)

    def copies(outs, send, recv):
        x, y, c = _me()
        own_j = 2 * x + y
        res = []
        for t in range(n):
            for k, chip in enumerate(_other_chips(x, y)):
                own = _piece(outs[t], axes[t], own_j, c)
                landed = _piece(outs[t], axes[t], 2 * chip[0] + chip[1], c)
                res.append((_rcopy(own, own, send.at[3 * t + k], recv.at[3 * t + k], (*chip, c)),
                            _rcopy(landed, landed, send.at[3 * t + k], recv.at[3 * t + k], (*chip, c))))
        return res

    def start(ins, outs, send, recv):
        for out, _ in copies(outs, send, recv):
            out.start()

    def finish(ins, outs, send, recv):
        for out, arrival in copies(outs, send, recv):
            out.wait_send()
            arrival.wait_recv()

    return _Comm(fulls, _like(fulls), {t: t for t in range(n)}, 3 * n, start, finish)


def _gather_d2d_comm(fulls, axes):
    n = len(fulls)

    def copies(outs, send, recv):
        x, y, c = _me()
        res = []
        for t in range(n):
            for k, chip in enumerate(_other_chips(x, y)):
                landed = _piece(outs[t], axes[t], 2 * chip[0] + chip[1], c)
                theirs = _piece(outs[t], axes[t], 2 * chip[0] + chip[1], 1 - c)
                res.append((_rcopy(landed, landed, send.at[3 * t + k], recv.at[3 * t + k], (x, y, 1 - c)),
                            _rcopy(theirs, theirs, send.at[3 * t + k], recv.at[3 * t + k], (x, y, 1 - c))))
        return res

    def start(ins, outs, send, recv):
        for out, _ in copies(outs, send, recv):
            out.start()

    def finish(ins, outs, send, recv):
        for out, arrival in copies(outs, send, recv):
            out.wait_send()
            arrival.wait_recv()

    return _Comm(fulls, _like(fulls), {t: t for t in range(n)}, 3 * n, start, finish)


def _swap_comm(grads, axes):
    n = len(grads)
    half_shapes = [(g.shape[0] // 2, g.shape[1]) if a == 1 else (g.shape[0], g.shape[1] // 2) for g, a in zip(grads, axes)]

    def copies(ins, outs, send, recv):
        x, y, c = _me()
        return [_rcopy(_half(ins[t], axes[t], 1 - c), outs[t], send.at[t], recv.at[t], (x, y, 1 - c)) for t in range(n)]

    def start(ins, outs, send, recv):
        for cp in copies(ins, outs, send, recv):
            cp.start()

    def finish(ins, outs, send, recv):
        for cp in copies(ins, outs, send, recv):
            cp.wait()

    return _Comm(grads, [jax.ShapeDtypeStruct(s, F32) for s in half_shapes], {}, n, start, finish)


def _scatter_comm(parts, axes):
    n = len(parts)
    q_shapes = [(p.shape[0], p.shape[1] // 4) if a == 1 else (p.shape[0] // 4, p.shape[1]) for p, a in zip(parts, axes)]

    def copies(ins, outs, send, recv):
        x, y, c = _me()
        res = []
        for t in range(n):
            for k, chip in enumerate(_other_chips(x, y)):
                res.append(_rcopy(_chip_part(ins[t], axes[t], 2 * chip[0] + chip[1]), outs[3 * t + k], send.at[3 * t + k],
                                  recv.at[3 * t + k], (*chip, c)))
        return res

    def start(ins, outs, send, recv):
        for cp in copies(ins, outs, send, recv):
            cp.start()

    def finish(ins, outs, send, recv):
        for cp in copies(ins, outs, send, recv):
            cp.wait()

    return _Comm(parts, [jax.ShapeDtypeStruct(s, p.dtype) for s, p in zip(q_shapes, parts) for _ in range(3)], {}, 3 * n,
                 start, finish)


def _rs_share(shards, axes):
    n = len(shards)

    def body(*refs):
        outs = refs[n:2 * n]
        send, recv = refs[2 * n:]
        x, y, c = _me()
        sib = (x, y, 1 - c)
        out = []
        for t in range(n):
            mine = _half(outs[t], axes[t], c)
            out.append(_rcopy(mine, mine, send.at[t], recv.at[t], sib))
            out[-1].start()
        for t in range(n):
            out[t].wait_send()
            theirs = _half(outs[t], axes[t], 1 - c)
            _rcopy(theirs, theirs, send.at[t], recv.at[t], sib).wait_recv()

    return pl.pallas_call(
        body, name="rs_share",
        in_specs=[ANY] * n, out_specs=[ANY] * n,
        out_shape=[jax.ShapeDtypeStruct(s.shape, s.dtype) for s in shards],
        input_output_aliases={t: t for t in range(n)},
        scratch_shapes=[pltpu.SemaphoreType.DMA((n,)), pltpu.SemaphoreType.DMA((n,))],
    )(*shards)


def _cast_into_full(w, l, axis, sc):
    _, k, n = w.shape
    tr = _pick(k, 256, 16)
    if axis == 1:
        full, out_spec = (k, 4 * n), pl.BlockSpec((tr, n), lambda i, s: (i, s[1]))
    else:
        full, out_spec = (4 * k, n), pl.BlockSpec((tr, n), lambda i, s: (s[1] * (k // tr) + i, 0))

    def body(s_ref, w_ref, o_ref):
        o_ref[...] = w_ref[...].astype(BF16)

    return pl.pallas_call(
        body, name="cast_into_full",
        grid_spec=pltpu.PrefetchScalarGridSpec(
            num_scalar_prefetch=1, grid=(k // tr,),
            in_specs=[pl.BlockSpec((None, tr, n), lambda i, s: (l, i, 0))], out_specs=out_spec),
        out_shape=jax.ShapeDtypeStruct(full, BF16), compiler_params=_cparams(("arbitrary",)),
    )(sc, w)


def _rs_add2(g, got, axis, sc):
    k, n = g.shape
    hk, hn = (k // 2, n) if axis == 1 else (k, n // 2)
    tr = _pick(hk, max(16, (1 << 18) // hn), 16)
    if axis == 1:
        g_spec = pl.BlockSpec((tr, hn), lambda i, s: (s[0] * (hk // tr) + i, 0))
    else:
        g_spec = pl.BlockSpec((tr, hn), lambda i, s: (i, s[0]))
    blk = pl.BlockSpec((tr, hn), lambda i, s: (i, 0))

    def body(s_ref, a_ref, b_ref, o_ref):
        o_ref[...] = (a_ref[...] + b_ref[...]).astype(o_ref.dtype)

    return pl.pallas_call(
        body, name="rs_add2",
        grid_spec=pltpu.PrefetchScalarGridSpec(num_scalar_prefetch=1, grid=(hk // tr,), in_specs=[g_spec, blk], out_specs=blk),
        out_shape=jax.ShapeDtypeStruct((hk, hn), BF16), compiler_params=_cparams(("arbitrary",)),
    )(sc, g, got)


def _rs_add4(part, got3, axis, sc):
    k, n = part.shape
    qk, qn = (k, n // 4) if axis == 1 else (k // 4, n)
    tr = _pick(qk, max(16, (1 << 18) // qn), 16)
    if axis == 1:
        p_spec = pl.BlockSpec((tr, qn), lambda i, s: (i, s[1]))
        shard, o_spec = (2 * qk, qn), pl.BlockSpec((tr, qn), lambda i, s: (s[0] * (qk // tr) + i, 0))
    else:
        p_spec = pl.BlockSpec((tr, qn), lambda i, s: (s[1] * (qk // tr) + i, 0))
        shard, o_spec = (qk, 2 * qn), pl.BlockSpec((tr, qn), lambda i, s: (i, s[0]))
    blk = pl.BlockSpec((tr, qn), lambda i, s: (i, 0))

    def body(s_ref, p_ref, a_ref, b_ref, c_ref, o_ref):
        o_ref[...] = ((p_ref[...].astype(F32) + a_ref[...].astype(F32)) + b_ref[...].astype(F32)) + c_ref[...].astype(F32)

    return pl.pallas_call(
        body, name="rs_add4",
        grid_spec=pltpu.PrefetchScalarGridSpec(num_scalar_prefetch=1, grid=(qk // tr,), in_specs=[p_spec, blk, blk, blk],
                                               out_specs=o_spec),
        out_shape=jax.ShapeDtypeStruct(shard, F32), compiler_params=_cparams(("arbitrary",)),
    )(sc, part, *got3)


def _gather_small(v):
    def body(v_ref, out_ref, send, recv, loc):
        x, y, c = _me()
        sib = (x, y, 1 - c)
        chips = _other_chips(x, y)
        slot = lambda px, py, pc: out_ref.at[4 * px + 2 * py + pc]
        mine = pltpu.make_async_copy(v_ref, slot(x, y, c), loc)
        mine.start()
        first = [_rcopy(v_ref, slot(x, y, c), send.at[0], recv.at[0], sib)]
        first += [_rcopy(v_ref, slot(x, y, c), send.at[1 + k], recv.at[1 + k], (*chip, c)) for k, chip in enumerate(chips)]
        for cp in first:
            cp.start()
        passed = []
        for k, chip in enumerate(chips):
            landed = slot(*chip, c)
            _rcopy(landed, landed, send.at[1 + k], recv.at[1 + k], sib).wait_recv()
            passed.append(_rcopy(landed, landed, send.at[4 + k], recv.at[4 + k], sib))
            passed[-1].start()
        theirs = slot(x, y, 1 - c)
        _rcopy(theirs, theirs, send.at[0], recv.at[0], sib).wait_recv()
        for k, chip in enumerate(chips):
            theirs = slot(*chip, 1 - c)
            _rcopy(theirs, theirs, send.at[4 + k], recv.at[4 + k], sib).wait_recv()
        for cp in first + passed:
            cp.wait_send()
        mine.wait()

    return pl.pallas_call(
        body, name="gather_small",
        in_specs=[ANY], out_specs=ANY,
        out_shape=jax.ShapeDtypeStruct((8,) + v.shape, v.dtype),
        scratch_shapes=[pltpu.SemaphoreType.DMA((7,)), pltpu.SemaphoreType.DMA((7,)), pltpu.SemaphoreType.DMA],
    )(v)


class _WeightGather:
    def __init__(self, params, l, sc):
        self.bufs = {k: _cast_into_full(params[k], l, SHARD_AXIS[k], sc) for k in BIG}

    def run(self):
        self.update(BIG, _gather_weights([self.bufs[k] for k in BIG], [SHARD_AXIS[k] for k in BIG]))

    def ici(self, names):
        return _gather_ici_comm([self.bufs[k] for k in names], [SHARD_AXIS[k] for k in names])

    def d2d(self, names):
        return _gather_d2d_comm([self.bufs[k] for k in names], [SHARD_AXIS[k] for k in names])

    def update(self, names, outs):
        self.bufs.update(zip(names, outs))

    def weights(self):
        return self.bufs


class _GradReduce:
    def __init__(self, names, grads, sc):
        self.names, self.grads, self.sc, self.parts, self.theirs = names, grads, sc, None, None
        self.axes = [SHARD_AXIS[k] for k in names]

    def swap(self):
        return _swap_comm([self.grads[k] for k in self.names], self.axes)

    def swapped(self, got):
        self.parts = [_rs_add2(self.grads[k], s, a, self.sc) for k, s, a in zip(self.names, got, self.axes)]

    def scatter(self):
        return _scatter_comm(self.parts, self.axes)

    def scattered(self, outs):
        self.theirs = [outs[3 * q:3 * q + 3] for q in range(len(self.names))]

    def finish(self):
        halves = [_rs_add4(p, th, a, self.sc) for p, th, a in zip(self.parts, self.theirs, self.axes)]
        return dict(zip(self.names, _rs_share(halves, self.axes)))

    def run(self):
        self.swapped(_run_comm("rs_swap", self.swap()))
        self.scattered(_run_comm("rs_scatter", self.scatter()))
        return self.finish()


def _adam_math(w, g, m, v):
    m = ADAM_B1 * m + (1.0 - ADAM_B1) * g
    v = ADAM_B2 * v + (1.0 - ADAM_B2) * (g * g)
    m_hat = m / (1.0 - ADAM_B1 ** ADAM_STEP)
    v_hat = v / (1.0 - ADAM_B2 ** ADAM_STEP)
    delta = -ADAM_LR * (m_hat / (jnp.sqrt(v_hat) + ADAM_EPS) + ADAM_WD * w)
    return delta, m, v


def _adam_layer(w, m, v, l, g, prev):
    L, k, n = w.shape
    tr = _pick(k, 128, 8)
    blk = pl.BlockSpec((None, tr, n), lambda i: (l, i, 0))

    def body(*refs):
        w_ref, m_ref, v_ref, g_ref = refs[:4]
        go, do, mo, vo = refs[-4:]
        gv = g_ref[...]
        d, m2, v2 = _adam_math(w_ref[...], gv, m_ref[...], v_ref[...])
        go[...] = gv
        do[...] = d
        mo[...] = m2
        vo[...] = v2

    args = [w, m, v, g]
    in_specs = [blk, blk, blk, pl.BlockSpec((tr, n), lambda i: (i, 0))]
    aliases = {}
    if prev is not None:
        for q, p in enumerate(prev):
            aliases[len(args)] = q
            in_specs.append(ANY)
            args.append(p)
    return pl.pallas_call(
        body, name="adam_layer", grid=(k // tr,),
        in_specs=in_specs, out_specs=[blk] * 4, out_shape=[jax.ShapeDtypeStruct((L, k, n), F32)] * 4,
        input_output_aliases=aliases, compiler_params=_cparams(("arbitrary",)),
    )(*args)


def _adam_flat(w, g, m, v):
    r = w.shape[0]
    tr = _pick(r, 512, 8)
    fn = lambda i, j, w, g, m, v: _adam_math(w, g, m, v)
    return _rowcall("adam_flat", fn, nrows=r, tr=tr, ins=[(a, "rows", 128, 0) for a in (w, g, m, v)],
                    outs=[(F32, 128, 128, 0, None)] * 3)


def _sum8(gathered):
    _, r, _ = gathered.shape
    tr = _pick(r, 512, 8)

    def body(g_ref, o_ref):
        acc = g_ref[0]
        for d in range(1, 8):
            acc = acc + g_ref[d]
        o_ref[...] = acc

    return pl.pallas_call(
        body, name="sum8", grid=(r // tr,),
        in_specs=[pl.BlockSpec((8, tr, 128), lambda i: (0, i, 0))], out_specs=pl.BlockSpec((tr, 128), lambda i: (i, 0)),
        out_shape=jax.ShapeDtypeStruct((r, 128), F32), compiler_params=_cparams(("arbitrary",)),
    )(gathered)


def _hdot(a, b, form="NN"):
    return _dot(a.astype(BF16), b.astype(BF16), form)


def _ada_fwd(s16, w_ada, l):
    _, d, ns = w_ada.shape
    tc = _pick(ns, 512)

    def body(s_ref, w_ref, o_ref):
        o_ref[...] = _hdot(s_ref[...], w_ref[...])

    return pl.pallas_call(
        body, name="ada_fwd", grid=(ns // tc,),
        in_specs=[pl.BlockSpec((16, d), lambda j: (0, 0)), pl.BlockSpec((None, d, tc), lambda j: (l, 0, j))],
        out_specs=pl.BlockSpec((16, tc), lambda j: (0, j)),
        out_shape=jax.ShapeDtypeStruct((16, ns), F32), compiler_params=_cparams(("arbitrary",)),
    )(s16, w_ada)


def _ada_bwd(s16t, dm, w_ada, l):
    _, d, ns = w_ada.shape
    tc = _pick(ns, 512)

    def body(st_ref, dm_ref, w_ref, dw_ref, ds_ref):
        j = pl.program_id(0)
        dw_ref[...] = _hdot(st_ref[...], dm_ref[...])
        part = _hdot(dm_ref[...], w_ref[...], "NT")

        @pl.when(j == 0)
        def _():
            ds_ref[...] = part

        @pl.when(j > 0)
        def _():
            ds_ref[...] += part

    return pl.pallas_call(
        body, name="ada_bwd", grid=(ns // tc,),
        in_specs=[pl.BlockSpec((d, 16), lambda j: (0, 0)), pl.BlockSpec((16, tc), lambda j: (0, j)),
                  pl.BlockSpec((None, d, tc), lambda j: (l, 0, j))],
        out_specs=[pl.BlockSpec((d, tc), lambda j: (0, j)), pl.BlockSpec((16, d), lambda j: (0, 0))],
        out_shape=[jax.ShapeDtypeStruct((d, ns), F32), jax.ShapeDtypeStruct((16, d), F32)],
        compiler_params=_cparams(("arbitrary",)),
    )(s16t, dm, w_ada)


def _pack(arrs):
    flat = jnp.concatenate([a.reshape(-1).astype(F32) for a in arrs])
    pad = (-flat.shape[0]) % 1024
    return jnp.pad(flat, (0, pad)).reshape(-1, 128)


def _unpack(flat2d, shapes):
    flat = flat2d.reshape(-1)
    out, pos = [], 0
    for s in shapes:
        size = int(np.prod(s))
        out.append(flat[pos:pos + size].reshape(s))
        pos += size
    return out


SMALL = ("norm1_g", "norm2_g", "ret_decay_logit", "ret_norm_g", "pool_w", "pool_scale", "sg_norm_g", "sg_w", "sg_b")


def kernel(x, c, ctx, c_ctx, w_ada, b_ada, norm1_g, w_in, ret_decay_logit, ret_norm_g, pool_w, pool_scale, sg_norm_g, sg_w, sg_b, w_br, w_bp, w_bs, w_out, norm2_g, w1, w2, final_norm_g, loss_target, m_c_ctx, m_w_ada, m_b_ada, m_norm1_g, m_w_in, m_ret_decay_logit, m_ret_norm_g, m_pool_w, m_pool_scale, m_sg_norm_g, m_sg_w, m_sg_b, m_w_br, m_w_bp, m_w_bs, m_w_out, m_norm2_g, m_w1, m_w2, m_final_norm_g, v_c_ctx, v_w_ada, v_b_ada, v_norm1_g, v_w_in, v_ret_decay_logit, v_ret_norm_g, v_pool_w, v_pool_scale, v_sg_norm_g, v_sg_w, v_sg_b, v_w_br, v_w_bp, v_w_bs, v_w_out, v_norm2_g, v_w1, v_w2, v_final_norm_g):
    P = dict(c_ctx=c_ctx, w_ada=w_ada, b_ada=b_ada, norm1_g=norm1_g, w_in=w_in, ret_decay_logit=ret_decay_logit,
             ret_norm_g=ret_norm_g, pool_w=pool_w, pool_scale=pool_scale, sg_norm_g=sg_norm_g, sg_w=sg_w, sg_b=sg_b, w_br=w_br,
             w_bp=w_bp, w_bs=w_bs, w_out=w_out, norm2_g=norm2_g, w1=w1, w2=w2, final_norm_g=final_norm_g)
    Mo = dict(c_ctx=m_c_ctx, w_ada=m_w_ada, b_ada=m_b_ada, norm1_g=m_norm1_g, w_in=m_w_in, ret_decay_logit=m_ret_decay_logit,
              ret_norm_g=m_ret_norm_g, pool_w=m_pool_w, pool_scale=m_pool_scale, sg_norm_g=m_sg_norm_g, sg_w=m_sg_w, sg_b=m_sg_b,
              w_br=m_w_br, w_bp=m_w_bp, w_bs=m_w_bs, w_out=m_w_out, norm2_g=m_norm2_g, w1=m_w1, w2=m_w2,
              final_norm_g=m_final_norm_g)
    Vo = dict(c_ctx=v_c_ctx, w_ada=v_w_ada, b_ada=v_b_ada, norm1_g=v_norm1_g, w_in=v_w_in, ret_decay_logit=v_ret_decay_logit,
              ret_norm_g=v_ret_norm_g, pool_w=v_pool_w, pool_scale=v_pool_scale, sg_norm_g=v_sg_norm_g, sg_w=v_sg_w, sg_b=v_sg_b,
              w_br=v_w_br, w_bp=v_w_bp, w_bs=v_w_bs, w_out=v_w_out, norm2_g=v_norm2_g, w1=v_w1, w2=v_w2,
              final_norm_g=v_final_norm_g)
    names = ("c_ctx", "w_ada", "b_ada", "norm1_g", "w_in", "ret_decay_logit", "ret_norm_g", "pool_w", "pool_scale", "sg_norm_g",
             "sg_w", "sg_b", "w_br", "w_bp", "w_bs", "w_out", "norm2_g", "w1", "w2", "final_norm_g")
    L, D = w_in.shape[0], x.shape[-1]
    T, CTX = x.shape[1], ctx.shape[1]
    cfg = _Cfg(D, T, CTX, 4 * w1.shape[2])
    ns_ada = w_ada.shape[2]
    mx, my, mc = _me()
    dev = 4 * mx + 2 * my + mc
    chip = 2 * mx + my

    silu_cc = jax.nn.silu(c_ctx)
    silu_all = _gather_small(jax.nn.silu(c).reshape(-1, 128)).reshape(8, D)
    s16 = jnp.concatenate([silu_cc[None], silu_all, jnp.zeros((7, D), F32)], axis=0)
    proj = jnp.stack([_ada_fwd(s16, w_ada, l) for l in range(L)])
    proj_all = _gather_small(proj.reshape(-1, 128)).reshape(8, L, 16, ns_ada)
    mods_full = jnp.concatenate([proj_all[2 * j] for j in range(4)], axis=-1) + b_ada[:, None, :]
    mods = [jnp.concatenate([mods_full[l, 0:1], lax.dynamic_slice_in_dim(mods_full[l], 1 + dev, 1, axis=0)], axis=0)
            for l in range(L)]

    sc = jnp.stack([mc, chip]).astype(jnp.int32)
    gathers = [_WeightGather(P, l, sc) for l in range(L)]
    X0 = jnp.concatenate([ctx[0], x[0]], axis=0)
    small_w = {k: P[k] for k in SMALL}
    loss_part, dX, shard_g, small, dmods, dfinal = _local_fwd_bwd(
        cfg, X0, loss_target[0], mods, gathers, lambda names, grads: _GradReduce(names, grads, sc), small_w, final_norm_g)
    loss = lax.psum(loss_part, ("x", "y", "c"))
    grad_x = dX[CTX:][None]

    outs = {k: None for k in BIG}
    for l in range(L):
        for k in BIG:
            outs[k] = _adam_layer(P[k], Mo[k], Vo[k], l, shard_g[l][k], outs[k])

    per_layer = [[small[l][k] for k in SMALL] + [dmods[l][1], dmods[l][0]] for l in range(L)]
    payload = _pack([a for lay in per_layer for a in lay] + [dfinal])
    gathered = _gather_small(payload)
    total = _sum8(gathered)
    shapes = [P[k].shape[1:] for k in SMALL] + [(6 * D,), (6 * D,)]
    tot = _unpack(total, shapes * L + [(D,)])
    per = len(shapes)
    g_small = {k: jnp.stack([tot[l * per + q] for l in range(L)]) for q, k in enumerate(SMALL)}
    dmx_sum = jnp.stack([tot[l * per + per - 2] for l in range(L)])
    dmc_sum = jnp.stack([tot[l * per + per - 1] for l in range(L)])
    g_small["b_ada"] = dmx_sum + dmc_sum
    g_small["final_norm_g"] = tot[-1]
    offs = np.cumsum([0] + [int(np.prod(s)) for s in shapes])
    lay_size = int(offs[-1])
    gflat = gathered.reshape(8, -1)
    s16t = s16.T
    ada_out, ds_part = None, jnp.zeros((16, D), F32)
    for l in range(L):
        dmx_all = gflat[:, l * lay_size + int(offs[per - 2]):l * lay_size + int(offs[per - 1])]
        dm_full = jnp.concatenate([dmc_sum[l][None], dmx_all, jnp.zeros((7, 6 * D), F32)], axis=0)
        dm = lax.dynamic_slice_in_dim(dm_full, chip * ns_ada, ns_ada, axis=1)
        dw, ds = _ada_bwd(s16t, dm, w_ada, l)
        ds_part = ds_part + ds
        ada_out = _adam_layer(w_ada, Mo["w_ada"], Vo["w_ada"], l, dw, ada_out)
    outs["w_ada"] = ada_out
    ds_all = _gather_small(ds_part[0].reshape(-1, 128)).reshape(8, D)
    d_silu_cc = ds_all[0] + ds_all[2] + ds_all[4] + ds_all[6]
    g_small["c_ctx"] = jax.vjp(jax.nn.silu, c_ctx)[1](d_silu_cc)[0]

    small_names = [k for k in names if k not in BIG and k != "w_ada"]
    sm_shapes = [P[k].shape for k in small_names]
    res = _adam_flat(_pack([P[k] for k in small_names]), _pack([g_small[k] for k in small_names]),
                     _pack([Mo[k] for k in small_names]), _pack([Vo[k] for k in small_names]))
    d_s, m_s, v_s = [_unpack(r, sm_shapes) for r in res]
    for q, k in enumerate(small_names):
        outs[k] = (g_small[k].reshape(P[k].shape), d_s[q], m_s[q], v_s[q])

    return (loss, grad_x, *[outs[k][0] for k in names], *[outs[k][1] for k in names], *[outs[k][2] for k in names],
            *[outs[k][3] for k in names])
```

```python
import functools

import numpy as np
import jax
import jax.numpy as jnp
from jax import lax
from jax.experimental import pallas as pl
from jax.experimental.pallas import tpu as pltpu

F32 = jnp.float32
BF16 = jnp.bfloat16
EPS = 1e-6
CH = 128
GRID_W = 64
ROPE_THETA = 10000.0
POOL_WINDOWS = (2, 4, 8, 16)
PT = 256
VMEM_LIMIT = 56 * 1024 * 1024
MESH = pl.DeviceIdType.MESH
ANY = pl.BlockSpec(memory_space=pl.ANY)

ADAM_LR = 0.001
ADAM_B1 = 0.9
ADAM_B2 = 0.999
ADAM_EPS = 1e-08
ADAM_WD = 0.01
ADAM_STEP = 10

BIG = ("w_in", "w_br", "w_bp", "w_bs", "w_out", "w1", "w2")
SHARD_AXIS = {"w_in": 1, "w_br": 1, "w_bp": 1, "w_bs": 1, "w_out": 0, "w1": 1, "w2": 0}


def _pick(dim, pref, mult=128):
    best = None
    for t in range(mult, min(dim, pref) + 1, mult):
        if dim % t == 0:
            best = t
    return dim if best is None else best


def _cparams(sem=None):
    return pltpu.CompilerParams(dimension_semantics=sem, vmem_limit_bytes=VMEM_LIMIT)


def _rows(i, tr):
    return i * tr + lax.broadcasted_iota(jnp.int32, (tr, 1), 0)


def _sel(i, tr, n_ctx, v2):
    return jnp.where(_rows(i, tr) < n_ctx, v2[0:1, :], v2[1:2, :])


def _seg_sums(i, tr, n_ctx, d):
    is_ctx = _rows(i, tr) < n_ctx
    s_c = jnp.sum(jnp.where(is_ctx, d, 0.0), axis=0, keepdims=True)
    s_x = jnp.sum(jnp.where(is_ctx, 0.0, d), axis=0, keepdims=True)
    two = lax.broadcasted_iota(jnp.int32, (2, d.shape[1]), 0)
    return jnp.where(two == 0, s_c, s_x)


def _dot(a, b, form="NN"):
    dims = {"NN": (((1,), (0,)), ((), ())), "NT": (((1,), (1,)), ((), ())), "TN": (((0,), (0,)), ((), ()))}[form]
    return lax.dot_general(a, b, dims, preferred_element_type=F32)


def _gelu(x):
    return 0.5 * x * (1.0 + jnp.tanh(0.7978845608028654 * (x + 0.044715 * x * x * x)))


def _matmul(name, form, pairs, *, R, C, tr, tc, tk, nk, out_dtypes, epi, extras=(), a_pro=None, comm=None):
    npair, nex, nout = len(pairs), len(extras), len(out_dtypes)
    in_specs, args = [], []
    for a, b in pairs:
        if nk == 1:
            ka = a.shape[0] if form == "TN" else a.shape[1]
        else:
            ka = tk
        if form == "NN":
            in_specs += [pl.BlockSpec((tr, ka), lambda i, j, k: (i, k)), pl.BlockSpec((ka, tc), lambda i, j, k: (k, j))]
        elif form == "NT":
            in_specs += [pl.BlockSpec((tr, ka), lambda i, j, k: (i, k)), pl.BlockSpec((tc, ka), lambda i, j, k: (j, k))]
        else:
            in_specs += [pl.BlockSpec((ka, tr), lambda i, j, k: (k, i)), pl.BlockSpec((ka, tc), lambda i, j, k: (k, j))]
        args += [a, b]
    for kind, arr, off in extras:
        if kind == "tile":
            in_specs.append(pl.BlockSpec((tr, tc), lambda i, j, k, off=off: (i, j + off)))
        else:
            in_specs.append(pl.BlockSpec((2, tc), lambda i, j, k, off=off: (0, j + off)))
        args.append(arr)

    direct = epi is None
    n_acc = 0 if (nk == 1 or direct) else npair
    n_main = len(args)
    ni, nj = R // tr, C // tc
    aliases = {}
    out_specs = [pl.BlockSpec((tr, tc), lambda i, j, k: (i, j)) for _ in out_dtypes]
    out_shape = [jax.ShapeDtypeStruct((R, C), dt) for dt in out_dtypes]
    scratch = [pltpu.VMEM((tr, tc), F32) for _ in range(n_acc)]
    n_cin = n_cout = 0
    if comm is not None:
        n_cin, n_cout = len(comm.ins), len(comm.outs)
        in_specs = in_specs + [ANY] * n_cin
        args = args + list(comm.ins)
        out_specs = out_specs + [ANY] * n_cout
        out_shape = out_shape + list(comm.outs)
        aliases = {n_main + a: nout + b for a, b in comm.aliases.items()}
        scratch = scratch + [pltpu.SemaphoreType.DMA((comm.nsem,)), pltpu.SemaphoreType.DMA((comm.nsem,))]

    def body(*refs):
        ab = refs[:2 * npair]
        ex = refs[2 * npair:n_main]
        cin = refs[n_main:n_main + n_cin]
        outs = refs[n_main + n_cin:n_main + n_cin + nout]
        cout = refs[n_main + n_cin + nout:n_main + n_cin + nout + n_cout]
        accs = refs[n_main + n_cin + nout + n_cout:n_main + n_cin + nout + n_cout + n_acc]
        sems = refs[n_main + n_cin + nout + n_cout + n_acc:]
        i, j, k = pl.program_id(0), pl.program_id(1), pl.program_id(2)

        if comm is not None:
            @pl.when(jnp.logical_and(jnp.logical_and(i == 0, j == 0), k == 0))
            def _():
                comm.start(cin, cout, *sems)

        def products():
            res = []
            for p in range(npair):
                a = ab[2 * p][...]
                if a_pro is not None:
                    a = a_pro(a)
                res.append(_dot(a, ab[2 * p + 1][...], form))
            return res

        def finish(vals):
            res = epi(vals, [e[...] for e in ex], i)
            for o, v in zip(outs, res):
                o[...] = v.astype(o.dtype)

        if direct:
            prod = products()[0]
            if nk == 1:
                outs[0][...] = prod
            else:
                @pl.when(k == 0)
                def _():
                    outs[0][...] = prod

                @pl.when(k > 0)
                def _():
                    outs[0][...] += prod
        elif nk == 1:
            finish(products())
        else:
            prods = products()

            @pl.when(k == 0)
            def _():
                for acc, v in zip(accs, prods):
                    acc[...] = v

            @pl.when(k > 0)
            def _():
                for acc, v in zip(accs, prods):
                    acc[...] += v

            @pl.when(k == nk - 1)
            def _():
                finish([acc[...] for acc in accs])

        if comm is not None:
            @pl.when(jnp.logical_and(jnp.logical_and(i == ni - 1, j == nj - 1), k == nk - 1))
            def _():
                comm.finish(cin, cout, *sems)

    res = pl.pallas_call(
        body, name=name, grid=(ni, nj, nk),
        in_specs=in_specs, out_specs=out_specs, out_shape=out_shape, scratch_shapes=scratch,
        input_output_aliases=aliases,
        compiler_params=_cparams(("arbitrary", "arbitrary", "arbitrary")),
    )(*args)
    return res if comm is None else (res[:nout], res[nout:])


def _rowcall(name, fn, *, nrows, tr, ins, outs, accs=(), ncol=1):
    n_in, n_out, n_acc = len(ins), len(outs), len(accs)
    in_specs, args = [], []
    for arr, kind, w, off in ins:
        if kind == "rows":
            in_specs.append(pl.BlockSpec((tr, w), lambda i, j, off=off: (i, off + j)))
        elif kind == "full":
            in_specs.append(pl.BlockSpec(arr.shape, lambda i, j, nd=arr.ndim: (0,) * nd))
        else:
            in_specs.append(pl.BlockSpec((None,) + arr.shape[1:], lambda i, j, f=off, nd=arr.ndim: (f(i),) + (0,) * (nd - 1)))
        args.append(arr)
    aliases = {}
    out_specs, out_shape = [], []
    for o_idx, (dt, total, w, off, alias) in enumerate(outs):
        out_specs.append(pl.BlockSpec((tr, w), lambda i, j, off=off: (i, off + j)))
        out_shape.append(jax.ShapeDtypeStruct((nrows, total), dt))
        if alias is not None:
            aliases[len(args)] = o_idx
            in_specs.append(ANY)
            args.append(alias)
    n_alias = len(aliases)
    for shp in accs:
        out_specs.append(pl.BlockSpec(shp, lambda i, j, nd=len(shp): (0,) * nd))
        out_shape.append(jax.ShapeDtypeStruct(shp, F32))

    def body(*refs):
        in_refs = refs[:n_in]
        out_refs = refs[n_in + n_alias:n_in + n_alias + n_out]
        acc_refs = refs[n_in + n_alias + n_out:]
        i, j = pl.program_id(0), pl.program_id(1)
        res = fn(i, j, *[r[...] for r in in_refs])
        for o, v in zip(out_refs, res[:n_out]):
            o[...] = v.astype(o.dtype)
        first = jnp.logical_and(i == 0, j == 0)
        for acc, v in zip(acc_refs, res[n_out:]):
            @pl.when(first)
            def _(acc=acc, v=v):
                acc[...] = v

            @pl.when(jnp.logical_not(first))
            def _(acc=acc, v=v):
                acc[...] += v

    res = pl.pallas_call(
        body, name=name, grid=(nrows // tr, ncol),
        in_specs=in_specs, out_specs=out_specs, out_shape=out_shape,
        input_output_aliases=aliases,
        compiler_params=_cparams(("arbitrary", "arbitrary")),
    )(*args)
    return res


def _f_normmod(x, g, shift, scale):
    r = lax.rsqrt(jnp.mean(x * x, axis=-1, keepdims=True) + EPS)
    return (x * r * g) * (1.0 + scale) + shift


def _f_headnorm_gate(o, zg, ng):
    r = lax.rsqrt(jnp.mean(o * o, axis=-1, keepdims=True) + EPS)
    return (o * r * ng) * (zg * jax.nn.sigmoid(zg))


def _f_sgv(sv, g):
    v = _gelu(sv)
    r = lax.rsqrt(jnp.mean(v * v, axis=-1, keepdims=True) + EPS)
    return v * r * g


def _rope(t, cos, sin):
    w = t.shape[1]
    lane = lax.broadcasted_iota(jnp.int32, t.shape, 1)
    swapped = jnp.where(jnp.bitwise_and(lane, 63) < 32,pltpu.roll(t, w - 32, 1), pltpu.roll(t, 32, 1))
    return t * cos + swapped * sin


def _rope_t(d, cos, sin):
    w = d.shape[1]
    lane = lax.broadcasted_iota(jnp.int32, d.shape, 1)
    ds = d * sin
    swapped = jnp.where(jnp.bitwise_and(lane, 63) < 32,pltpu.roll(ds, w - 32, 1), pltpu.roll(ds, 32, 1))
    return d * cos + swapped


class _Cfg:
    def __init__(self, D, T, CTX, DFF):
        self.D, self.T, self.CTX, self.DFF = D, T, CTX, DFF
        self.TA = T + CTX
        self.RW = D // 2
        self.H = self.RW // CH
        self.PW = D // 4
        self.SW = D // 4
        self.NIN = 4 * self.RW + self.PW + 2 * self.SW + 3 * D
        self.NC = self.TA // CH
        self.NCC = CTX // CH
        self.k_scale = float(CH) ** -0.5
        self.tr = _pick(self.TA, 1408)
        self.tr_small = _pick(self.TA, 768)
        self.o_g = 3 * self.RW
        self.o_p = 4 * self.RW
        self.o_u = self.o_p + self.PW
        self.o_sv = self.o_u + self.SW
        self.o_gate = self.o_sv + self.SW


def _normmod_fwd(cfg, X, g, mods, i_shift, i_scale, name):
    D = cfg.D
    tr = _pick(cfg.TA, 384)

    def fn(i, j, x, g, m):
        sh = _sel(i, tr, cfg.CTX, m[:, i_shift * D:(i_shift + 1) * D])
        sc = _sel(i, tr, cfg.CTX, m[:, i_scale * D:(i_scale + 1) * D])
        return (_f_normmod(x, g, sh, sc),)

    return _rowcall(name, fn, nrows=cfg.TA, tr=tr, ins=[(X, "rows", D, 0), (g, "full", 0, 0), (mods, "full", 0, 0)],
                    outs=[(BF16, D, D, 0, None)])[0]


def _normmod_bwd(cfg, X, dH, dres, g, mods, i_shift, i_scale, name):
    D = cfg.D
    tr = _pick(cfg.TA, 384)

    def fn(i, j, x, dh, dr, g, m):
        sh = _sel(i, tr, cfg.CTX, m[:, i_shift * D:(i_shift + 1) * D])
        sc = _sel(i, tr, cfg.CTX, m[:, i_scale * D:(i_scale + 1) * D])
        _, vjp = jax.vjp(_f_normmod, x, g, sh, sc)
        dx, dg, dsh, dsc = vjp(dh)
        return dr + dx, dg, _seg_sums(i, tr, cfg.CTX, dsh), _seg_sums(i, tr, cfg.CTX, dsc)

    return _rowcall(name, fn, nrows=cfg.TA, tr=tr,
                    ins=[(X, "rows", D, 0), (dH, "rows", D, 0), (dres, "rows", D, 0), (g, "full", 0, 0), (mods, "full", 0, 0)],
                    outs=[(F32, D, D, 0, None)], accs=[(1, D), (2, D), (2, D)])


def _resgate_bwd(cfg, dX, M, mods, i_gate, name):
    D = cfg.D
    tr = _pick(cfg.TA, 384)

    def fn(i, j, dx, m, mm):
        gate = _sel(i, tr, cfg.CTX, mm[:, i_gate * D:(i_gate + 1) * D])
        return dx * gate, _seg_sums(i, tr, cfg.CTX, dx * m.astype(F32))

    return _rowcall(name, fn, nrows=cfg.TA, tr=tr, ins=[(dX, "rows", D, 0), (M, "rows", D, 0), (mods, "full", 0, 0)],
                    outs=[(BF16, D, D, 0, None)], accs=[(2, D)])


def _chunk_of(cfg, d, t):
    fwd = t
    bwd = jnp.where(t < cfg.NCC, cfg.NCC - 1 - t, cfg.NC - 1 - t + cfg.NCC)
    return jnp.where(d == 0, fwd, bwd)


def _ret_specs(cfg, cm):
    RW, H = cfg.RW, cfg.H
    return [
        pl.BlockSpec((CH, RW), lambda d, t: (cm(d, t), 0)),
        pl.BlockSpec((CH, RW), lambda d, t: (cm(d, t), 1)),
        pl.BlockSpec((CH, RW), lambda d, t: (cm(d, t), 2)),
        pl.BlockSpec((CH, CH), lambda d, t: (cm(d, t), 0)),
        pl.BlockSpec((CH, CH), lambda d, t: (cm(d, t), 0)),
        pl.BlockSpec((None, H, CH, CH), lambda d, t: (d, 0, 0, 0)),
        pl.BlockSpec((None, CH, RW), lambda d, t: (d, 0, 0)),
        pl.BlockSpec((None, CH, RW), lambda d, t: (d, 0, 0)),
        pl.BlockSpec((None, 1, RW), lambda d, t: (d, 0, 0)),
    ]


def _ret_prep(cfg, q_ref, k_ref, cos, sin, qd_ref, kd_ref, sl):
    qr = _rope(q_ref[:, sl].astype(F32), cos, sin)
    kr = _rope(k_ref[:, sl].astype(F32) * cfg.k_scale, cos, sin)
    return qr, kr, (qr * qd_ref[:, sl]).astype(BF16), (kr * kd_ref[:, sl]).astype(BF16)


def _ret_fwd(cfg, Z, tabs, decs):
    RW, H, TA, NC = cfg.RW, cfg.H, cfg.TA, cfg.NC
    cm = functools.partial(_chunk_of, cfg)

    def body(q_ref, k_ref, v_ref, ct_ref, st_ref, dm_ref, qd_ref, kd_ref, cd_ref, o_ref, so_ref, S):
        @pl.when(pl.program_id(1) == 0)
        def _():
            S[...] = jnp.zeros_like(S)

        cos, sin = ct_ref[...], st_ref[...]
        for h in range(H):
            sl = slice(h * CH, (h + 1) * CH)
            qr, kr, qd, kd = _ret_prep(cfg, q_ref, k_ref, cos, sin, qd_ref, kd_ref, sl)
            v = v_ref[:, sl].astype(BF16)
            p = (_dot(qr.astype(BF16), kr.astype(BF16), "NT") * dm_ref[h]).astype(BF16)
            s_h = S[h]
            so_ref[h] = s_h
            o_ref[:, sl] = _dot(p, v) + _dot(qd, s_h.astype(BF16))
            S[h] = s_h * cd_ref[:, sl] + _dot(kd, v, "TN")

    return pl.pallas_call(
        body, name="ret_fwd", grid=(2, NC),
        in_specs=_ret_specs(cfg, cm),
        out_specs=[pl.BlockSpec((None, CH, RW), lambda d, t: (d, cm(d, t), 0)),
                   pl.BlockSpec((None, None, H, CH, CH), lambda d, t: (d, cm(d, t), 0, 0, 0))],
        out_shape=[jax.ShapeDtypeStruct((2, TA, RW), F32), jax.ShapeDtypeStruct((2, NC, H, CH, CH), F32)],
        scratch_shapes=[pltpu.VMEM((H, CH, CH), F32)],
        compiler_params=_cparams(("arbitrary", "arbitrary")),
    )(Z, Z, Z, tabs["cos"], tabs["sin"], decs["dmat"], decs["qdec"], decs["kdec"], decs["cdec"])


def _ret_bwd(cfg, Z, tabs, decs, states, dO):
    RW, H, TA, NC = cfg.RW, cfg.H, cfg.TA, cfg.NC

    def cm(d, t):
        return _chunk_of(cfg, d, NC - 1 - t)

    def body(q_ref, k_ref, v_ref, ct_ref, st_ref, dm_ref, qd_ref, kd_ref, cd_ref, s_ref, do_ref, w_ref,
             dqkv_ref, dl_ref, dS):
        t = pl.program_id(1)

        @pl.when(t == 0)
        def _():
            dS[...] = jnp.zeros_like(dS)
            dl_ref[...] = jnp.zeros_like(dl_ref)

        cos, sin = ct_ref[...], st_ref[...]
        for h in range(H):
            sl = slice(h * CH, (h + 1) * CH)
            qh, kh, qd, kd = _ret_prep(cfg, q_ref, k_ref, cos, sin, qd_ref, kd_ref, sl)
            qb, kb = qh.astype(BF16), kh.astype(BF16)
            v = v_ref[:, sl].astype(BF16)
            dob = do_ref[:, sl].astype(BF16)
            dm = dm_ref[h]
            p = (_dot(qb, kb, "NT") * dm).astype(BF16)
            dp = (_dot(dob, v, "NT") * dm).astype(BF16)
            s_h = s_ref[h]
            ds_h = dS[h]
            sb, dsb = s_h.astype(BF16), ds_h.astype(BF16)
            dq_i = _dot(dp, kb)
            dk_i = _dot(dp, qb, "TN")
            dq_c = _dot(dob, sb, "NT") * qd_ref[:, sl]
            dk_s = _dot(v, dsb, "NT") * kd_ref[:, sl]
            dqkv_ref[:, sl] = dq_i + dq_c
            dqkv_ref[:, RW + h * CH:RW + (h + 1) * CH] = dk_i + dk_s
            dqkv_ref[:, 2 * RW + h * CH:2 * RW + (h + 1) * CH] = _dot(p, dob, "TN") + _dot(kd, dsb)
            lam = w_ref[0] * (qh * dq_i) + w_ref[1] * (qh * dq_c) + w_ref[2] * (kh * dk_i) + w_ref[3] * (kh * dk_s)
            lam_s = float(CH) * cd_ref[:, sl] * jnp.sum(ds_h * s_h, axis=0, keepdims=True)
            dl_ref[:, sl] += jnp.sum(lam, axis=0, keepdims=True) + lam_s
            dS[h] = ds_h * cd_ref[:, sl] + _dot(qd, dob, "TN")

    in_specs = _ret_specs(cfg, cm) + [
        pl.BlockSpec((None, None, H, CH, CH), lambda d, t: (d, cm(d, t), 0, 0, 0)),
        pl.BlockSpec((CH, RW), lambda d, t: (cm(d, t), 0)),
        pl.BlockSpec((None, 4, CH, CH), lambda d, t: (d, 0, 0, 0)),
    ]
    return pl.pallas_call(
        body, name="ret_bwd", grid=(2, NC),
        in_specs=in_specs,
        out_specs=[pl.BlockSpec((None, CH, 3 * RW), lambda d, t: (d, cm(d, t), 0)),
                   pl.BlockSpec((None, 1, RW), lambda d, t: (d, 0, 0))],
        out_shape=[jax.ShapeDtypeStruct((2, TA, 3 * RW), F32), jax.ShapeDtypeStruct((2, 1, RW), F32)],
        scratch_shapes=[pltpu.VMEM((H, CH, CH), F32)],
        compiler_params=_cparams(("arbitrary", "arbitrary")),
    )(Z, Z, Z, tabs["cos"], tabs["sin"], decs["dmat"], decs["qdec"], decs["kdec"], decs["cdec"], states, dO,
      tabs["lamw"])


def _rope_bwd(cfg, dqkv2, tabs, dz):
    RW, H = cfg.RW, cfg.H
    tr = PT

    def body(d0, d1, ct, st, dz_in, o):
        cos, sin = jnp.tile(ct[...], (1, H)), jnp.tile(st[...], (1, H))
        d = d0[...] + d1[...]
        o[:, :RW] = _rope_t(d[:, :RW], cos, sin).astype(o.dtype)
        o[:, RW:2 * RW] = (_rope_t(d[:, RW:2 * RW], cos, sin) * cfg.k_scale).astype(o.dtype)
        o[:, 2 * RW:] = d[:, 2 * RW:].astype(o.dtype)

    return pl.pallas_call(
        body, name="rope_bwd", grid=(cfg.TA // tr,),
        in_specs=[pl.BlockSpec((None, tr, 3 * RW), lambda i: (0, i, 0)), pl.BlockSpec((None, tr, 3 * RW), lambda i: (1, i, 0)),
                  pl.BlockSpec((tr, CH), lambda i: (i, 0)), pl.BlockSpec((tr, CH), lambda i: (i, 0)), ANY],
        out_specs=pl.BlockSpec((tr, 3 * RW), lambda i: (i, 0)),
        out_shape=jax.ShapeDtypeStruct((cfg.TA, cfg.NIN), BF16),
        input_output_aliases={4: 0},
        compiler_params=_cparams(("arbitrary",)),
    )(dqkv2, dqkv2, tabs["cos"], tabs["sin"], dz)


def _retout_fwd(cfg, o2, Z, ng):
    RW, H = cfg.RW, cfg.H
    tr = PT

    def body(o0, o1, zg, ng, out):
        o = o0[...] + o1[...]
        z = zg[...].astype(F32)
        for h in range(H):
            sl = slice(h * CH, (h + 1) * CH)
            out[:, sl] = _f_headnorm_gate(o[:, sl], z[:, sl], ng[:, sl]).astype(out.dtype)

    return pl.pallas_call(
        body, name="retout_fwd", grid=(cfg.TA // tr,),
        in_specs=[pl.BlockSpec((None, tr, RW), lambda i: (0, i, 0)), pl.BlockSpec((None, tr, RW), lambda i: (1, i, 0)),
                  pl.BlockSpec((tr, RW), lambda i: (i, 3)), pl.BlockSpec((1, RW), lambda i: (0, 0))],
        out_specs=pl.BlockSpec((tr, RW), lambda i: (i, 0)),
        out_shape=jax.ShapeDtypeStruct((cfg.TA, RW), BF16),
        compiler_params=_cparams(("arbitrary",)),
    )(o2, o2, Z, ng)


def _retout_bwd(cfg, o2, Z, ng, dret, dz):
    RW, H = cfg.RW, cfg.H
    tr = PT

    def body(o0, o1, zg, ng, dr, dz_in, do_out, dz_out, dng):
        i = pl.program_id(0)
        o = o0[...] + o1[...]
        z = zg[...].astype(F32)
        d = dr[...]
        acc = []
        for h in range(H):
            sl = slice(h * CH, (h + 1) * CH)
            _, vjp = jax.vjp(_f_headnorm_gate, o[:, sl], z[:, sl], ng[:, sl])
            do_h, dz_h, dg_h = vjp(d[:, sl])
            do_out[:, sl] = do_h
            dz_out[:, sl] = dz_h.astype(dz_out.dtype)
            acc.append(dg_h)

        @pl.when(i == 0)
        def _():
            for h in range(H):
                dng[:, h * CH:(h + 1) * CH] = acc[h]

        @pl.when(i > 0)
        def _():
            for h in range(H):
                dng[:, h * CH:(h + 1) * CH] += acc[h]

    return pl.pallas_call(
        body, name="retout_bwd", grid=(cfg.TA // tr,),
        in_specs=[pl.BlockSpec((None, tr, RW), lambda i: (0, i, 0)), pl.BlockSpec((None, tr, RW), lambda i: (1, i, 0)),
                  pl.BlockSpec((tr, RW), lambda i: (i, 3)), pl.BlockSpec((1, RW), lambda i: (0, 0)),
                  pl.BlockSpec((tr, RW), lambda i: (i, 0)), ANY],
        out_specs=[pl.BlockSpec((tr, RW), lambda i: (i, 0)), pl.BlockSpec((tr, RW), lambda i: (i, 3)),
                   pl.BlockSpec((1, RW), lambda i: (0, 0))],
        out_shape=[jax.ShapeDtypeStruct((cfg.TA, RW), F32), jax.ShapeDtypeStruct((cfg.TA, cfg.NIN), BF16),
                   jax.ShapeDtypeStruct((1, RW), F32)],
        input_output_aliases={5: 1},
        compiler_params=_cparams(("arbitrary",)),
    )(o2, o2, Z, ng, dret, dz)


def _pool_consts(ctx_len):
    assert ctx_len == PT
    bm = np.zeros((2, len(POOL_WINDOWS), PT, PT), np.float32)
    ic = np.zeros((2, len(POOL_WINDOWS), PT, CH), np.float32)
    for ty, seg in enumerate((ctx_len, GRID_W)):
        for gi, w in enumerate(POOL_WINDOWS):
            for r in range(PT):
                s0, pos = (r // seg) * seg, r % seg
                lo, hi = max(pos - w // 2, 0), min(pos + w // 2 - 1, seg - 1)
                bm[ty, gi, r, s0 + lo:s0 + hi + 1] = 1.0
                ic[ty, gi, r, :] = 1.0 / (hi - lo + 1)
    return jnp.asarray(bm, BF16), jnp.asarray(ic, F32)


def _pool_tile(p, bm, ic, pw, g):
    sl = slice(g * CH, (g + 1) * CH)
    pg = p[:, sl].astype(F32)
    hi = pg.astype(BF16)
    lo = (pg - hi.astype(F32)).astype(BF16)
    y = (_dot(bm[g], hi) + _dot(bm[g], lo)) * ic[g] - pg
    return y, _dot(y.astype(BF16), pw[g].astype(BF16))


def _pool_fwd(cfg, Z, consts, pool_w, pool_scale):
    PW = cfg.PW
    G = PW // CH
    nct = cfg.CTX // PT
    ty = lambda i: jnp.where(i < nct, 0, 1)

    def fn(i, j, p, bm, ic, pw, ps):
        outs = [_pool_tile(p, bm, ic, pw, g)[1] for g in range(G)]
        return (jnp.concatenate(outs, axis=1) * ps,)

    return _rowcall("pool_fwd", fn, nrows=cfg.TA, tr=PT,
                    ins=[(Z, "rows", PW, cfg.o_p // PW), (consts[0], "sel", 0, ty), (consts[1], "sel", 0, ty),
                         (pool_w, "full", 0, 0), (pool_scale, "full", 0, 0)],
                    outs=[(BF16, PW, PW, 0, None)])[0]


def _pool_bwd(cfg, Z, consts, pool_w, pool_scale, dpool, dz):
    PW = cfg.PW
    G = PW // CH
    nct = cfg.CTX // PT
    ty = lambda i: jnp.where(i < nct, 0, 1)

    def fn(i, j, p, bm, ic, pw, ps, dout):
        dps, dps_acc, dpw = [], [], []
        for g in range(G):
            sl = slice(g * CH, (g + 1) * CH)
            y, lin = _pool_tile(p, bm, ic, pw, g)
            dlin = (dout[:, sl] * ps[:, sl]).astype(BF16)
            dps_acc.append(jnp.sum(dout[:, sl] * lin, axis=0, keepdims=True))
            dy = _dot(dlin, pw[g].astype(BF16), "NT")
            dpw.append(_dot(y.astype(BF16), dlin, "TN"))
            t = dy * ic[g]
            hi = t.astype(BF16)
            lo = (t - hi.astype(F32)).astype(BF16)
            dps.append(_dot(bm[g], hi, "TN") + _dot(bm[g], lo, "TN") - dy)
        return (jnp.concatenate(dps, axis=1), jnp.stack(dpw), jnp.concatenate(dps_acc, axis=1))

    return _rowcall("pool_bwd", fn, nrows=cfg.TA, tr=PT,
                    ins=[(Z, "rows", PW, cfg.o_p // PW), (consts[0], "sel", 0, ty), (consts[1], "sel", 0, ty),
                         (pool_w, "full", 0, 0), (pool_scale, "full", 0, 0), (dpool, "rows", PW, 0)],
                    outs=[(BF16, cfg.NIN, PW, cfg.o_p // PW, dz)], accs=[(G, CH, CH), (1, PW)])


def _sg_mixed(vn, sw, sbb, g, c):
    rows = slice(c * CH, (c + 1) * CH)
    sl = slice(g * CH, (g + 1) * CH)
    return _dot(sw[g].astype(BF16), vn[rows, sl].astype(BF16)) + sbb[g]


def _sg_fwd(cfg, Z, sng, sw, sbb):
    SW = cfg.SW
    G = SW // CH

    def fn(i, j, u, sv, sng, sw, sbb):
        ug = _gelu(u.astype(F32))
        vn = _f_sgv(sv.astype(F32), sng)
        rows = []
        for c in range(PT // CH):
            mixed = jnp.concatenate([_sg_mixed(vn, sw, sbb, g, c) for g in range(G)], axis=1)
            rows.append(ug[c * CH:(c + 1) * CH, :] * mixed)
        return (jnp.concatenate(rows, axis=0),)

    return _rowcall("sg_fwd", fn, nrows=cfg.TA, tr=PT,
                    ins=[(Z, "rows", SW, cfg.o_u // SW), (Z, "rows", SW, cfg.o_sv // SW), (sng, "full", 0, 0),
                         (sw, "full", 0, 0), (sbb, "full", 0, 0)],
                    outs=[(BF16, SW, SW, 0, None)])[0]


def _sg_bwd(cfg, Z, sng, sw, sbb, dsg, dz):
    SW = cfg.SW
    G = SW // CH

    def fn(i, j, u, sv, sng, sw, sbb, dout):
        uf, svf = u.astype(F32), sv.astype(F32)
        ug, vjp_u = jax.vjp(_gelu, uf)
        vn, vjp_v = jax.vjp(_f_sgv, svf, sng)
        dug_rows, dvn_rows = [], []
        dsw = [jnp.zeros((CH, CH), F32) for _ in range(G)]
        dsb = [jnp.zeros((CH, CH), F32) for _ in range(G)]
        for c in range(PT // CH):
            rows = slice(c * CH, (c + 1) * CH)
            dug_g, dvn_g = [], []
            for g in range(G):
                sl = slice(g * CH, (g + 1) * CH)
                mixed = _sg_mixed(vn, sw, sbb, g, c)
                dmixed = dout[rows, sl] * ug[rows, sl]
                dug_g.append(dout[rows, sl] * mixed)
                dmb = dmixed.astype(BF16)
                dvn_g.append(_dot(sw[g].astype(BF16), dmb, "TN"))
                dsw[g] = dsw[g] + _dot(dmb, vn[rows, sl].astype(BF16), "NT")
                dsb[g] = dsb[g] + jnp.broadcast_to(jnp.sum(dmixed, axis=1, keepdims=True), (CH, CH))
            dug_rows.append(jnp.concatenate(dug_g, axis=1))
            dvn_rows.append(jnp.concatenate(dvn_g, axis=1))
        (du,) = vjp_u(jnp.concatenate(dug_rows, axis=0))
        dsv, dsng = vjp_v(jnp.concatenate(dvn_rows, axis=0))
        return du, dsv, jnp.stack(dsw), jnp.stack(dsb), dsng

    return _rowcall("sg_bwd", fn, nrows=cfg.TA, tr=PT,
                    ins=[(Z, "rows", SW, cfg.o_u // SW), (Z, "rows", SW, cfg.o_sv // SW), (sng, "full", 0, 0),
                         (sw, "full", 0, 0), (sbb, "full", 0, 0), (dsg, "rows", SW, 0)],
                    outs=[(BF16, cfg.NIN, SW, cfg.o_u // SW, dz), (BF16, SW, SW, 0, None)],
                    accs=[(G, CH, CH), (G, CH, CH), (1, SW)])


def _gate_bwd(cfg, b, dY, L, Z, dz):
    D = cfg.D
    tc = 512
    nj = D // tc
    off = cfg.o_gate // tc + b * nj

    def fn(i, j, dy, l, z):
        s = jax.nn.sigmoid(z.astype(F32))
        return dy * s, dy * l.astype(F32) * s * (1.0 - s)

    return _rowcall("gate_bwd", fn, nrows=cfg.TA, tr=cfg.tr,
                    ins=[(dY, "rows", tc, 0), (L, "rows", tc, 0), (Z, "rows", tc, off)],
                    outs=[(BF16, D, tc, 0, None), (BF16, cfg.NIN, tc, off, dz)], ncol=nj)


def _copy_cols(cfg, src, dz, col0):
    w = src.shape[1]
    return _rowcall("copy_cols", lambda i, j, s: (s,), nrows=cfg.TA, tr=cfg.tr, ins=[(src, "rows", w, 0)],
                    outs=[(dz.dtype, dz.shape[1], w, col0 // w, dz)])[0]


def _final(cfg, X, g, target):
    D = cfg.D
    tr = PT
    nct = cfg.CTX // tr

    def body(x_ref, g_ref, t_ref, dx_ref, loss_ref, dg_ref):
        i = pl.program_id(0)

        def f(x, g):
            r = lax.rsqrt(jnp.mean(x * x, axis=-1, keepdims=True) + EPS)
            return x * r * g

        y, vjp = jax.vjp(f, x_ref[...], g_ref[...])
        err = y - t_ref[...]
        dx, dg = vjp(err * (1.0 / D))
        part = 0.5 * jnp.sum(jnp.mean(err * err, axis=-1, keepdims=True), axis=0, keepdims=True)

        @pl.when(i == 0)
        def _():
            loss_ref[...] = jnp.zeros_like(loss_ref)
            dg_ref[...] = jnp.zeros_like(dg_ref)

        @pl.when(i < nct)
        def _():
            dx_ref[...] = jnp.zeros_like(dx_ref)

        @pl.when(i >= nct)
        def _():
            dx_ref[...] = dx
            loss_ref[...] += jnp.broadcast_to(part, loss_ref.shape)
            dg_ref[...] += dg

    return pl.pallas_call(
        body, name="final", grid=(cfg.TA // tr,),
        in_specs=[pl.BlockSpec((tr, D), lambda i: (i, 0)), pl.BlockSpec((1, D), lambda i: (0, 0)),
                  pl.BlockSpec((tr, D), lambda i: (jnp.maximum(i - nct, 0), 0))],
        out_specs=[pl.BlockSpec((tr, D), lambda i: (i, 0)), pl.BlockSpec((8, CH), lambda i: (0, 0)),
                   pl.BlockSpec((1, D), lambda i: (0, 0))],
        out_shape=[jax.ShapeDtypeStruct((cfg.TA, D), F32), jax.ShapeDtypeStruct((8, CH), F32),
                   jax.ShapeDtypeStruct((1, D), F32)],
        compiler_params=_cparams(("arbitrary",)),
    )(X, g, target)


GATHER_A = ("w_in", "w1")
GATHER_B = ("w2", "w_out", "w_br", "w_bp", "w_bs")
REDUCE_MLP = ("w2", "w1")
REDUCE_MIX = ("w_out", "w_br", "w_bp", "w_bs", "w_in")


def _hosted(res, comm, sink):
    if comm is None:
        return res
    main, outs = res
    sink(outs)
    return main


def _layer_fwd(cfg, X, mods, W, sm, tabs, decs, consts, nxt=None):
    D, TA, tr, RW, PW, SW, DFF, NIN = cfg.D, cfg.TA, cfg.tr, cfg.RW, cfg.PW, cfg.SW, cfg.DFF, cfg.NIN
    ident = lambda accs, ex, i: (accs[0],)
    H1 = _normmod_fwd(cfg, X, sm["norm1_g"], mods, 0, 1, "normmod1_fwd")
    comm = nxt.ici(GATHER_A) if nxt else None
    (Z,) = _hosted(_matmul("z_mm", "NN", [(H1, W["w_in"])], R=TA, C=NIN, tr=tr, tc=512, tk=D, nk=1, out_dtypes=[F32],
                           epi=ident, comm=comm), comm, lambda o: nxt.update(GATHER_A, o))
    o2, states = _ret_fwd(cfg, Z, tabs, decs)
    ret = _retout_fwd(cfg, o2, Z, sm["ret_norm_g"])
    pool = _pool_fwd(cfg, Z, consts, sm["pool_w"], sm["pool_scale"])
    sg = _sg_fwd(cfg, Z, sm["sg_norm_g"], sm["sg_w"], sm["sg_bb"])

    tc = 512
    goff = cfg.o_gate // tc

    def epi_branch(accs, ex, i):
        y = sum(jax.nn.sigmoid(z.astype(F32)) * a for a, z in zip(accs, ex))
        return (y, accs[0], accs[1], accs[2])

    Y, Lr, Lp, Ls = _matmul("branch_mm", "NN", [(ret, W["w_br"]), (pool, W["w_bp"]), (sg, W["w_bs"])], R=TA, C=D, tr=cfg.tr_small,
                            tc=tc, tk=0, nk=1, out_dtypes=[BF16] * 4, epi=epi_branch,
                            extras=[("tile", Z, goff + b * (D // tc)) for b in range(3)])

    def epi_res(rows):
        return lambda accs, ex, i: (ex[0] + _sel(i, rows, cfg.CTX, ex[1]) * accs[0], accs[0])

    X2, O = _matmul("out_mm", "NN", [(Y, W["w_out"])], R=TA, C=D, tr=tr, tc=tc, tk=D, nk=1, out_dtypes=[F32, BF16],
                    epi=epi_res(tr), extras=[("tile", X, 0), ("rows2", mods, 2 * (D // tc))])
    H2 = _normmod_fwd(cfg, X2, sm["norm2_g"], mods, 3, 4, "normmod2_fwd")
    tcf = _pick(DFF, 1024)
    def epi_relu(accs, ex, i):
        r = jnp.maximum(accs[0], 0.0)
        return (r * r, r)

    comm = nxt.ici(GATHER_B) if nxt else None
    A2, Rr = _hosted(_matmul("w1_mm", "NN", [(H2, W["w1"])], R=TA, C=DFF, tr=tr, tc=tcf, tk=D, nk=1, out_dtypes=[BF16, BF16],
                             epi=epi_relu, comm=comm), comm, lambda o: nxt.update(GATHER_B, o))
    comm = nxt.d2d(BIG) if nxt else None
    tr2 = _pick(TA, 704)
    X3, M = _hosted(_matmul("w2_mm", "NN", [(A2, W["w2"])], R=TA, C=D, tr=tr2, tc=tc, tk=DFF, nk=1,
                            out_dtypes=[F32, BF16], epi=epi_res(tr2), extras=[("tile", X2, 0), ("rows2", mods, 5 * (D // tc))],
                            comm=comm), comm, lambda o: nxt.update(BIG, o))
    saved = dict(X=X, H1=H1, Z=Z, o2=o2, states=states, ret=ret, pool=pool, sg=sg, Y=Y, L=(Lr, Lp, Ls), O=O, X2=X2,
                 H2=H2, R=Rr, A2=A2, M=M)
    return X3, saved


def _layer_bwd(cfg, dX3, sv, mods, W, sm, tabs, decs, consts, rs, make_reduce):
    D, TA, tr, RW, PW, SW, DFF, NIN = cfg.D, cfg.TA, cfg.tr, cfg.RW, cfg.PW, cfg.SW, cfg.DFF, cfg.NIN
    ident = lambda accs, ex, i: (accs[0],)
    tw = 512
    g = {}

    dM, dgate2 = _resgate_bwd(cfg, dX3, sv["M"], mods, 5, "resgate2_bwd")
    tcf = _pick(DFF, 1024)
    comm = rs.swap() if rs else None
    (dPre,) = _hosted(_matmul("dpre_mm", "NT", [(dM, W["w2"])], R=TA, C=DFF, tr=tr, tc=tcf, tk=D, nk=1, out_dtypes=[BF16],
                              epi=lambda accs, ex, i: (accs[0] * (2.0 * ex[0].astype(F32)),), extras=[("tile", sv["R"], 0)],
                              comm=comm), comm, lambda o: rs.swapped(o))
    comm = rs.scatter() if rs else None
    (g["w2"],) = _hosted(_matmul("dw2_mm", "TN", [(sv["A2"], dM)], R=DFF, C=D, tr=tw, tc=tw, tk=TA, nk=1, out_dtypes=[F32],
                                 epi=None, comm=comm), comm, lambda o: rs.scattered(o))
    (g["w1"],) = _matmul("dw1_mm", "TN", [(sv["H2"], dPre)], R=D, C=DFF, tr=tw, tc=tw, tk=TA, nk=1, out_dtypes=[F32], epi=None)
    mlp = make_reduce(REDUCE_MLP, {k: g.pop(k) for k in REDUCE_MLP})
    tkf = _pick(DFF, 2048)
    comm = mlp.swap()
    (dH2,) = _hosted(_matmul("dh2_mm", "NT", [(dPre, W["w1"])], R=TA, C=D, tr=tr, tc=1024, tk=tkf, nk=DFF // tkf,
                             out_dtypes=[F32], epi=ident, comm=comm), comm, lambda o: mlp.swapped(o))
    dX2, dn2, dsh2, dsc2 = _normmod_bwd(cfg, sv["X2"], dH2, dX3, sm["norm2_g"], mods, 3, 4, "normmod2_bwd")

    dO, dgate1 = _resgate_bwd(cfg, dX2, sv["O"], mods, 2, "resgate1_bwd")
    (dY,) = _matmul("dy_mm", "NT", [(dO, W["w_out"])], R=TA, C=D, tr=tr, tc=1024, tk=D, nk=1, out_dtypes=[F32], epi=ident)
    (g["w_out"],) = _matmul("dwout_mm", "TN", [(sv["Y"], dO)], R=D, C=D, tr=tw, tc=tw, tk=TA, nk=1,
                            out_dtypes=[F32], epi=None)
    dLr, dz = _gate_bwd(cfg, 0, dY, sv["L"][0], sv["Z"], None)
    dLp, dz = _gate_bwd(cfg, 1, dY, sv["L"][1], sv["Z"], dz)
    dLs, dz = _gate_bwd(cfg, 2, dY, sv["L"][2], sv["Z"], dz)

    (dret,) = _matmul("dret_mm", "NT", [(dLr, W["w_br"])], R=TA, C=RW, tr=tr, tc=RW, tk=D, nk=1, out_dtypes=[F32], epi=ident)
    (dpool,) = _matmul("dpool_mm", "NT", [(dLp, W["w_bp"])], R=TA, C=PW, tr=tr, tc=PW, tk=D, nk=1, out_dtypes=[F32], epi=ident)
    (dsg,) = _matmul("dsg_mm", "NT", [(dLs, W["w_bs"])], R=TA, C=SW, tr=tr, tc=SW, tk=D, nk=1, out_dtypes=[F32], epi=ident)
    (g["w_br"],) = _matmul("dwbr_mm", "TN", [(sv["ret"], dLr)], R=RW, C=D, tr=tw, tc=tw, tk=TA, nk=1, out_dtypes=[F32],
                           epi=None)
    (g["w_bp"],) = _matmul("dwbp_mm", "TN", [(sv["pool"], dLp)], R=PW, C=D, tr=tw, tc=tw, tk=TA, nk=1, out_dtypes=[F32],
                           epi=None)
    (g["w_bs"],) = _matmul("dwbs_mm", "TN", [(sv["sg"], dLs)], R=SW, C=D, tr=tw, tc=tw, tk=TA, nk=1, out_dtypes=[F32],
                           epi=None)
    dOr, dz, dretng = _retout_bwd(cfg, sv["o2"], sv["Z"], sm["ret_norm_g"], dret, dz)
    dqkv2, dlam = _ret_bwd(cfg, sv["Z"], tabs, decs, sv["states"], dOr)
    dz = _rope_bwd(cfg, dqkv2, tabs, dz)
    dz, dpw, dps = _pool_bwd(cfg, sv["Z"], consts, sm["pool_w"], sm["pool_scale"], dpool, dz)
    dz, dz_sv, dsw, dsb, dsng = _sg_bwd(cfg, sv["Z"], sm["sg_norm_g"], sm["sg_w"], sm["sg_bb"], dsg, dz)
    dz = _copy_cols(cfg, dz_sv, dz, cfg.o_sv)

    tkz = _pick(NIN, 2944)
    comm = mlp.scatter()
    (dH1,) = _hosted(_matmul("dh1_mm", "NT", [(dz, W["w_in"])], R=TA, C=D, tr=tr, tc=512, tk=tkz, nk=NIN // tkz,
                             out_dtypes=[F32], epi=ident, comm=comm), comm, lambda o: mlp.scattered(o))
    (g["w_in"],) = _matmul("dwin_mm", "TN", [(sv["H1"], dz)], R=D, C=NIN, tr=tw, tc=tw, tk=TA, nk=1,
                           out_dtypes=[F32], epi=None)
    dX, dn1, dsh1, dsc1 = _normmod_bwd(cfg, sv["X"], dH1, dX2, sm["norm1_g"], mods, 0, 1, "normmod1_bwd")
    dmods = jnp.concatenate([dsh1, dsc1, dgate1, dsh2, dsc2, dgate2], axis=1)
    small = dict(norm1_g=dn1, norm2_g=dn2, ret_norm_g=dretng, pool_w=dpw, pool_scale=dps, sg_norm_g=dsng, sg_w=dsw,
                 sg_b=dsb[:, :, 0], dlam=dlam)
    return dX, g, mlp.finish(), small, dmods


def _tables(cfg):
    nf = CH // 4
    inv = ROPE_THETA ** (-jnp.arange(nf, dtype=F32) / nf)
    tok = jnp.arange(cfg.T)
    ar = (tok // GRID_W).astype(F32)[:, None] * inv[None]
    ac = (tok % GRID_W).astype(F32)[:, None] * inv[None]
    cos = jnp.concatenate([jnp.cos(ar), jnp.cos(ar), jnp.cos(ac), jnp.cos(ac)], axis=1)
    sin = jnp.concatenate([-jnp.sin(ar), jnp.sin(ar), -jnp.sin(ac), jnp.sin(ac)], axis=1)
    cos = jnp.concatenate([jnp.ones((cfg.CTX, CH), F32), cos], axis=0)
    sin = jnp.concatenate([jnp.zeros((cfg.CTX, CH), F32), sin], axis=0)
    idx = np.broadcast_to(np.arange(CH, dtype=np.float32)[:, None], (CH, CH))
    lamw = np.stack([np.stack([idx, idx + 1.0, -idx, CH - 1.0 - idx]), np.stack([-idx, CH - idx, idx, idx])])
    return dict(cos=cos, sin=sin, lamw=jnp.asarray(lamw, F32))


def _decays(cfg, logit):
    H, RW = cfg.H, cfg.RW
    lam = jax.nn.log_sigmoid(logit.astype(F32))
    idx = jnp.arange(CH, dtype=F32)
    dist = idx[:, None] - idx[None, :]
    d0 = jnp.where(dist >= 0, jnp.exp(lam[0][:, None, None] * jnp.maximum(dist, 0.0)), 0.0)
    d1 = jnp.where(dist <= 0, jnp.exp(lam[1][:, None, None] * jnp.maximum(-dist, 0.0)), 0.0)
    lanes = lambda a: jnp.repeat(a.T, CH, axis=1)
    qdec = jnp.stack([lanes(jnp.exp(lam[0][:, None] * (idx + 1.0)[None])), lanes(jnp.exp(lam[1][:, None] * (CH - idx)[None]))])
    kdec = jnp.stack([lanes(jnp.exp(lam[0][:, None] * (CH - 1.0 - idx)[None])), lanes(jnp.exp(lam[1][:, None] * idx[None]))])
    cdec = jnp.repeat(jnp.exp(lam * CH), CH, axis=1)[:, None, :]
    return dict(dmat=jnp.stack([d0, d1]), qdec=qdec, kdec=kdec, cdec=cdec)


def _small_of_layer(small_w, l):
    sm = {k: v[l] for k, v in small_w.items()}
    sm["norm1_g"] = sm["norm1_g"][None]
    sm["norm2_g"] = sm["norm2_g"][None]
    sm["ret_norm_g"] = sm["ret_norm_g"][None]
    sm["pool_scale"] = sm["pool_scale"][None]
    sm["sg_norm_g"] = sm["sg_norm_g"][None]
    sm["sg_bb"] = jnp.broadcast_to(sm["sg_b"][:, :, None], sm["sg_b"].shape + (CH,))
    return sm


def _local_fwd_bwd(cfg, X0, target, mods, gathers, make_reduce, small_w, final_g):
    depth = len(mods)
    tabs = _tables(cfg)
    consts = _pool_consts(cfg.CTX)
    X, saved, Ws, sms, decs = X0, [], [], [], []
    gathers[0].run()
    for l in range(depth):
        Ws.append(gathers[l].weights())
        sms.append(_small_of_layer(small_w, l))
        decs.append(_decays(cfg, small_w["ret_decay_logit"][l]))
        X, sv = _layer_fwd(cfg, X, mods[l], Ws[l], sms[l], tabs, decs[l], consts, gathers[l + 1] if l + 1 < depth else None)
        saved.append(sv)
    dX, loss_acc, dfinal = _final(cfg, X, final_g[None], target)
    shard_g, small, dmods = [None] * depth, [None] * depth, [None] * depth
    pending = None
    for l in reversed(range(depth)):
        dX, mix, shard_g[l], small[l], dmods[l] = _layer_bwd(cfg, dX, saved[l], mods[l], Ws[l], sms[l], tabs, decs[l], consts,
                                                             pending, make_reduce)
        if pending is not None:
            shard_g[l + 1].update(pending.finish())
        pending = make_reduce(REDUCE_MIX, mix)
        lam_grad = jnp.sum(small[l].pop("dlam").reshape(2, cfg.H, CH), axis=-1)
        small[l]["ret_decay_logit"] = lam_grad * jax.nn.sigmoid(-small_w["ret_decay_logit"][l].astype(F32))
    shard_g[0].update(pending.run())
    return loss_acc[0, 0], dX, shard_g, small, dmods, dfinal


def _me():
    return lax.axis_index("x"), lax.axis_index("y"), lax.axis_index("c")


def _other_chips(x, y):
    return [(1 - x, y), (x, 1 - y), (1 - x, 1 - y)]


def _rcopy(src, dst, send_sem, recv_sem, dev):
    return pltpu.make_async_remote_copy(src_ref=src, dst_ref=dst, send_sem=send_sem, recv_sem=recv_sem,
                                        device_id=dev, device_id_type=MESH)


def _half(ref, axis, c):
    k, n = ref.shape
    if axis == 1:
        return ref.at[pl.ds(c * (k // 2), k // 2), :]
    return ref.at[:, pl.ds(c * (n // 2), n // 2)]


def _chip_part(ref, axis, j):
    k, n = ref.shape
    if axis == 1:
        return ref.at[:, pl.ds(j * (n // 4), n // 4)]
    return ref.at[pl.ds(j * (k // 4), k // 4), :]


def _piece(ref, axis, j, c):
    k, n = ref.shape
    if axis == 1:
        return ref.at[pl.ds(c * (k // 2), k // 2), pl.ds(j * (n // 4), n // 4)]
    return ref.at[pl.ds(j * (k // 4), k // 4), pl.ds(c * (n // 2), n // 2)]


def _gather_weights(fulls, axes):
    n = len(fulls)

    def body(*refs):
        outs = refs[n:2 * n]
        send, recv = refs[2 * n:]
        x, y, c = _me()
        j = 2 * x + y
        sib = (x, y, 1 - c)
        chips = _other_chips(x, y)
        first = []
        for t in range(n):
            for k, chip in enumerate(chips):
                own = _piece(outs[t], axes[t], j, c)
                first.append(_rcopy(own, own, send.at[6 * t + k], recv.at[6 * t + k], (*chip, c)))
                first[-1].start()
        passed = []
        for t in range(n):
            for k, chip in enumerate(chips):
                landed = _piece(outs[t], axes[t], 2 * chip[0] + chip[1], c)
                _rcopy(landed, landed, send.at[6 * t + k], recv.at[6 * t + k], sib).wait_recv()
                passed.append(_rcopy(landed, landed, send.at[6 * t + 3 + k], recv.at[6 * t + 3 + k], sib))
                passed[-1].start()
        for t in range(n):
            for k, chip in enumerate(chips):
                theirs = _piece(outs[t], axes[t], 2 * chip[0] + chip[1], 1 - c)
                _rcopy(theirs, theirs, send.at[6 * t + 3 + k], recv.at[6 * t + 3 + k], sib).wait_recv()
        for cp in first + passed:
            cp.wait_send()

    return pl.pallas_call(
        body, name="gather_weights",
        in_specs=[ANY] * n, out_specs=[ANY] * n,
        out_shape=[jax.ShapeDtypeStruct(f.shape, f.dtype) for f in fulls],
        input_output_aliases={t: t for t in range(n)},
        scratch_shapes=[pltpu.SemaphoreType.DMA((6 * n,)), pltpu.SemaphoreType.DMA((6 * n,))],
    )(*fulls)


class _Comm:
    def __init__(self, ins, outs, aliases, nsem, start, finish):
        self.ins, self.outs, self.aliases, self.nsem, self.start, self.finish = ins, outs, aliases, nsem, start, finish


def _run_comm(name, comm):
    n_in = len(comm.ins)

    def body(*refs):
        ins, outs = refs[:n_in], refs[n_in:n_in + len(comm.outs)]
        send, recv = refs[n_in + len(comm.outs):]
        comm.start(ins, outs, send, recv)
        comm.finish(ins, outs, send, recv)

    return pl.pallas_call(
        body, name=name, in_specs=[ANY] * n_in, out_specs=[ANY] * len(comm.outs), out_shape=list(comm.outs),
        input_output_aliases=dict(comm.aliases),
        scratch_shapes=[pltpu.SemaphoreType.DMA((comm.nsem,)), pltpu.SemaphoreType.DMA((comm.nsem,))],
    )(*comm.ins)


def _like(arrs):
    return [jax.ShapeDtypeStruct(a.shape, a.dtype) for a in arrs]


def _gather_ici_comm(fulls, axes):
    n = len(fulls)

    def copies(outs, send, recv):
        x, y, c = _me()
        own_j = 2 * x + y
        res = []
        for t in range(n):
            for k, chip in enumerate(_other_chips(x, y)):
                own = _piece(outs[t], axes[t], own_j, c)
                landed = _piece(outs[t], axes[t], 2 * chip[0] + chip[1], c)
                res.append((_rcopy(own, own, send.at[3 * t + k], recv.at[3 * t + k], (*chip, c)),
                            _rcopy(landed, landed, send.at[3 * t + k], recv.at[3 * t + k], (*chip, c))))
        return res

    def start(ins, outs, send, recv):
        for out, _ in copies(outs, send, recv):
            out.start()

    def finish(ins, outs, send, recv):
        for out, arrival in copies(outs, send, recv):
            out.wait_send()
            arrival.wait_recv()

    return _Comm(fulls, _like(fulls), {t: t for t in range(n)}, 3 * n, start, finish)


def _gather_d2d_comm(fulls, axes):
    n = len(fulls)

    def copies(outs, send, recv):
        x, y, c = _me()
        res = []
        for t in range(n):
            for k, chip in enumerate(_other_chips(x, y)):
                landed = _piece(outs[t], axes[t], 2 * chip[0] + chip[1], c)
                theirs = _piece(outs[t], axes[t], 2 * chip[0] + chip[1], 1 - c)
                res.append((_rcopy(landed, landed, send.at[3 * t + k], recv.at[3 * t + k], (x, y, 1 - c)),
                            _rcopy(theirs, theirs, send.at[3 * t + k], recv.at[3 * t + k], (x, y, 1 - c))))
        return res

    def start(ins, outs, send, recv):
        for out, _ in copies(outs, send, recv):
            out.start()

    def finish(ins, outs, send, recv):
        for out, arrival in copies(outs, send, recv):
            out.wait_send()
            arrival.wait_recv()

    return _Comm(fulls, _like(fulls), {t: t for t in range(n)}, 3 * n, start, finish)


def _swap_comm(grads, axes):
    n = len(grads)
    half_shapes = [(g.shape[0] // 2, g.shape[1]) if a == 1 else (g.shape[0], g.shape[1] // 2) for g, a in zip(grads, axes)]

    def copies(ins, outs, send, recv):
        x, y, c = _me()
        return [_rcopy(_half(ins[t], axes[t], 1 - c), outs[t], send.at[t], recv.at[t], (x, y, 1 - c)) for t in range(n)]

    def start(ins, outs, send, recv):
        for cp in copies(ins, outs, send, recv):
            cp.start()

    def finish(ins, outs, send, recv):
        for cp in copies(ins, outs, send, recv):
            cp.wait()

    return _Comm(grads, [jax.ShapeDtypeStruct(s, F32) for s in half_shapes], {}, n, start, finish)


def _scatter_comm(parts, axes):
    n = len(parts)
    q_shapes = [(p.shape[0], p.shape[1] // 4) if a == 1 else (p.shape[0] // 4, p.shape[1]) for p, a in zip(parts, axes)]

    def copies(ins, outs, send, recv):
        x, y, c = _me()
        res = []
        for t in range(n):
            for k, chip in enumerate(_other_chips(x, y)):
                res.append(_rcopy(_chip_part(ins[t], axes[t], 2 * chip[0] + chip[1]), outs[3 * t + k], send.at[3 * t + k],
                                  recv.at[3 * t + k], (*chip, c)))
        return res

    def start(ins, outs, send, recv):
        for cp in copies(ins, outs, send, recv):
            cp.start()

    def finish(ins, outs, send, recv):
        for cp in copies(ins, outs, send, recv):
            cp.wait()

    return _Comm(parts, [jax.ShapeDtypeStruct(s, p.dtype) for s, p in zip(q_shapes, parts) for _ in range(3)], {}, 3 * n,
                 start, finish)


def _rs_share(shards, axes):
    n = len(shards)

    def body(*refs):
        outs = refs[n:2 * n]
        send, recv = refs[2 * n:]
        x, y, c = _me()
        sib = (x, y, 1 - c)
        out = []
        for t in range(n):
            mine = _half(outs[t], axes[t], c)
            out.append(_rcopy(mine, mine, send.at[t], recv.at[t], sib))
            out[-1].start()
        for t in range(n):
            out[t].wait_send()
            theirs = _half(outs[t], axes[t], 1 - c)
            _rcopy(theirs, theirs, send.at[t], recv.at[t], sib).wait_recv()

    return pl.pallas_call(
        body, name="rs_share",
        in_specs=[ANY] * n, out_specs=[ANY] * n,
        out_shape=[jax.ShapeDtypeStruct(s.shape, s.dtype) for s in shards],
        input_output_aliases={t: t for t in range(n)},
        scratch_shapes=[pltpu.SemaphoreType.DMA((n,)), pltpu.SemaphoreType.DMA((n,))],
    )(*shards)


def _cast_into_full(w, l, axis, sc):
    _, k, n = w.shape
    tr = _pick(k, 256, 16)
    if axis == 1:
        full, out_spec = (k, 4 * n), pl.BlockSpec((tr, n), lambda i, s: (i, s[1]))
    else:
        full, out_spec = (4 * k, n), pl.BlockSpec((tr, n), lambda i, s: (s[1] * (k // tr) + i, 0))

    def body(s_ref, w_ref, o_ref):
        o_ref[...] = w_ref[...].astype(BF16)

    return pl.pallas_call(
        body, name="cast_into_full",
        grid_spec=pltpu.PrefetchScalarGridSpec(
            num_scalar_prefetch=1, grid=(k // tr,),
            in_specs=[pl.BlockSpec((None, tr, n), lambda i, s: (l, i, 0))], out_specs=out_spec),
        out_shape=jax.ShapeDtypeStruct(full, BF16), compiler_params=_cparams(("arbitrary",)),
    )(sc, w)


def _rs_add2(g, got, axis, sc):
    k, n = g.shape
    hk, hn = (k // 2, n) if axis == 1 else (k, n // 2)
    tr = _pick(hk, max(16, (1 << 20) // hn), 16)
    if axis == 1:
        g_spec = pl.BlockSpec((tr, hn), lambda i, s: (s[0] * (hk // tr) + i, 0))
    else:
        g_spec = pl.BlockSpec((tr, hn), lambda i, s: (i, s[0]))
    blk = pl.BlockSpec((tr, hn), lambda i, s: (i, 0))

    def body(s_ref, a_ref, b_ref, o_ref):
        o_ref[...] = (a_ref[...] + b_ref[...]).astype(o_ref.dtype)

    return pl.pallas_call(
        body, name="rs_add2",
        grid_spec=pltpu.PrefetchScalarGridSpec(num_scalar_prefetch=1, grid=(hk // tr,), in_specs=[g_spec, blk], out_specs=blk),
        out_shape=jax.ShapeDtypeStruct((hk, hn), BF16), compiler_params=_cparams(("arbitrary",)),
    )(sc, g, got)


def _rs_add4(part, got3, axis, sc):
    k, n = part.shape
    qk, qn = (k, n // 4) if axis == 1 else (k // 4, n)
    tr = _pick(qk, max(16, (1 << 20) // qn), 16)
    if axis == 1:
        p_spec = pl.BlockSpec((tr, qn), lambda i, s: (i, s[1]))
        shard, o_spec = (2 * qk, qn), pl.BlockSpec((tr, qn), lambda i, s: (s[0] * (qk // tr) + i, 0))
    else:
        p_spec = pl.BlockSpec((tr, qn), lambda i, s: (s[1] * (qk // tr) + i, 0))
        shard, o_spec = (qk, 2 * qn), pl.BlockSpec((tr, qn), lambda i, s: (i, s[0]))
    blk = pl.BlockSpec((tr, qn), lambda i, s: (i, 0))

    def body(s_ref, p_ref, a_ref, b_ref, c_ref, o_ref):
        o_ref[...] = ((p_ref[...].astype(F32) + a_ref[...].astype(F32)) + b_ref[...].astype(F32)) + c_ref[...].astype(F32)

    return pl.pallas_call(
        body, name="rs_add4",
        grid_spec=pltpu.PrefetchScalarGridSpec(num_scalar_prefetch=1, grid=(qk // tr,), in_specs=[p_spec, blk, blk, blk],
                                               out_specs=o_spec),
        out_shape=jax.ShapeDtypeStruct(shard, F32), compiler_params=_cparams(("arbitrary",)),
    )(sc, part, *got3)


def _gather_small(v):
    def body(v_ref, out_ref, send, recv, loc):
        x, y, c = _me()
        sib = (x, y, 1 - c)
        chips = _other_chips(x, y)
        slot = lambda px, py, pc: out_ref.at[4 * px + 2 * py + pc]
        mine = pltpu.make_async_copy(v_ref, slot(x, y, c), loc)
        mine.start()
        first = [_rcopy(v_ref, slot(x, y, c), send.at[0], recv.at[0], sib)]
        first += [_rcopy(v_ref, slot(x, y, c), send.at[1 + k], recv.at[1 + k], (*chip, c)) for k, chip in enumerate(chips)]
        for cp in first:
            cp.start()
        passed = []
        for k, chip in enumerate(chips):
            landed = slot(*chip, c)
            _rcopy(landed, landed, send.at[1 + k], recv.at[1 + k], sib).wait_recv()
            passed.append(_rcopy(landed, landed, send.at[4 + k], recv.at[4 + k], sib))
            passed[-1].start()
        theirs = slot(x, y, 1 - c)
        _rcopy(theirs, theirs, send.at[0], recv.at[0], sib).wait_recv()
        for k, chip in enumerate(chips):
            theirs = slot(*chip, 1 - c)
            _rcopy(theirs, theirs, send.at[4 + k], recv.at[4 + k], sib).wait_recv()
        for cp in first + passed:
            cp.wait_send()
        mine.wait()

    return pl.pallas_call(
        body, name="gather_small",
        in_specs=[ANY], out_specs=ANY,
        out_shape=jax.ShapeDtypeStruct((8,) + v.shape, v.dtype),
        scratch_shapes=[pltpu.SemaphoreType.DMA((7,)), pltpu.SemaphoreType.DMA((7,)), pltpu.SemaphoreType.DMA],
    )(v)


class _WeightGather:
    def __init__(self, params, l, sc):
        self.bufs = {k: _cast_into_full(params[k], l, SHARD_AXIS[k], sc) for k in BIG}

    def run(self):
        self.update(BIG, _gather_weights([self.bufs[k] for k in BIG], [SHARD_AXIS[k] for k in BIG]))

    def ici(self, names):
        return _gather_ici_comm([self.bufs[k] for k in names], [SHARD_AXIS[k] for k in names])

    def d2d(self, names):
        return _gather_d2d_comm([self.bufs[k] for k in names], [SHARD_AXIS[k] for k in names])

    def update(self, names, outs):
        self.bufs.update(zip(names, outs))

    def weights(self):
        return self.bufs


class _GradReduce:
    def __init__(self, names, grads, sc):
        self.names, self.grads, self.sc, self.parts, self.theirs = names, grads, sc, None, None
        self.axes = [SHARD_AXIS[k] for k in names]

    def swap(self):
        return _swap_comm([self.grads[k] for k in self.names], self.axes)

    def swapped(self, got):
        self.parts = [_rs_add2(self.grads[k], s, a, self.sc) for k, s, a in zip(self.names, got, self.axes)]

    def scatter(self):
        return _scatter_comm(self.parts, self.axes)

    def scattered(self, outs):
        self.theirs = [outs[3 * q:3 * q + 3] for q in range(len(self.names))]

    def finish(self):
        halves = [_rs_add4(p, th, a, self.sc) for p, th, a in zip(self.parts, self.theirs, self.axes)]
        return dict(zip(self.names, _rs_share(halves, self.axes)))

    def run(self):
        self.swapped(_run_comm("rs_swap", self.swap()))
        self.scattered(_run_comm("rs_scatter", self.scatter()))
        return self.finish()


def _adam_math(w, g, m, v):
    m = ADAM_B1 * m + (1.0 - ADAM_B1) * g
    v = ADAM_B2 * v + (1.0 - ADAM_B2) * (g * g)
    m_hat = m / (1.0 - ADAM_B1 ** ADAM_STEP)
    v_hat = v / (1.0 - ADAM_B2 ** ADAM_STEP)
    delta = -ADAM_LR * (m_hat / (jnp.sqrt(v_hat) + ADAM_EPS) + ADAM_WD * w)
    return delta, m, v


def _adam_layer(w, m, v, l, g, prev):
    L, k, n = w.shape
    tr = _pick(k, 128, 8)
    blk = pl.BlockSpec((None, tr, n), lambda i: (l, i, 0))

    def body(*refs):
        w_ref, m_ref, v_ref, g_ref = refs[:4]
        go, do, mo, vo = refs[-4:]
        gv = g_ref[...]
        d, m2, v2 = _adam_math(w_ref[...], gv, m_ref[...], v_ref[...])
        go[...] = gv
        do[...] = d
        mo[...] = m2
        vo[...] = v2

    args = [w, m, v, g]
    in_specs = [blk, blk, blk, pl.BlockSpec((tr, n), lambda i: (i, 0))]
    aliases = {}
    if prev is not None:
        for q, p in enumerate(prev):
            aliases[len(args)] = q
            in_specs.append(ANY)
            args.append(p)
    return pl.pallas_call(
        body, name="adam_layer", grid=(k // tr,),
        in_specs=in_specs, out_specs=[blk] * 4, out_shape=[jax.ShapeDtypeStruct((L, k, n), F32)] * 4,
        input_output_aliases=aliases, compiler_params=_cparams(("arbitrary",)),
    )(*args)


def _adam_flat(w, g, m, v):
    r = w.shape[0]
    tr = _pick(r, 512, 8)
    fn = lambda i, j, w, g, m, v: _adam_math(w, g, m, v)
    return _rowcall("adam_flat", fn, nrows=r, tr=tr, ins=[(a, "rows", 128, 0) for a in (w, g, m, v)],
                    outs=[(F32, 128, 128, 0, None)] * 3)


def _sum8(gathered):
    _, r, _ = gathered.shape
    tr = _pick(r, 512, 8)

    def body(g_ref, o_ref):
        acc = g_ref[0]
        for d in range(1, 8):
            acc = acc + g_ref[d]
        o_ref[...] = acc

    return pl.pallas_call(
        body, name="sum8", grid=(r // tr,),
        in_specs=[pl.BlockSpec((8, tr, 128), lambda i: (0, i, 0))], out_specs=pl.BlockSpec((tr, 128), lambda i: (i, 0)),
        out_shape=jax.ShapeDtypeStruct((r, 128), F32), compiler_params=_cparams(("arbitrary",)),
    )(gathered)


def _hdot(a, b, form="NN"):
    return _dot(a.astype(BF16), b.astype(BF16), form)


def _ada_fwd(s16, w_ada, l):
    _, d, ns = w_ada.shape
    tc = _pick(ns, 512)

    def body(s_ref, w_ref, o_ref):
        o_ref[...] = _hdot(s_ref[...], w_ref[...])

    return pl.pallas_call(
        body, name="ada_fwd", grid=(ns // tc,),
        in_specs=[pl.BlockSpec((16, d), lambda j: (0, 0)), pl.BlockSpec((None, d, tc), lambda j: (l, 0, j))],
        out_specs=pl.BlockSpec((16, tc), lambda j: (0, j)),
        out_shape=jax.ShapeDtypeStruct((16, ns), F32), compiler_params=_cparams(("arbitrary",)),
    )(s16, w_ada)


def _ada_bwd(s16t, dm, w_ada, l):
    _, d, ns = w_ada.shape
    tc = _pick(ns, 512)

    def body(st_ref, dm_ref, w_ref, dw_ref, ds_ref):
        j = pl.program_id(0)
        dw_ref[...] = _hdot(st_ref[...], dm_ref[...])
        part = _hdot(dm_ref[...], w_ref[...], "NT")

        @pl.when(j == 0)
        def _():
            ds_ref[...] = part

        @pl.when(j > 0)
        def _():
            ds_ref[...] += part

    return pl.pallas_call(
        body, name="ada_bwd", grid=(ns // tc,),
        in_specs=[pl.BlockSpec((d, 16), lambda j: (0, 0)), pl.BlockSpec((16, tc), lambda j: (0, j)),
                  pl.BlockSpec((None, d, tc), lambda j: (l, 0, j))],
        out_specs=[pl.BlockSpec((d, tc), lambda j: (0, j)), pl.BlockSpec((16, d), lambda j: (0, 0))],
        out_shape=[jax.ShapeDtypeStruct((d, ns), F32), jax.ShapeDtypeStruct((16, d), F32)],
        compiler_params=_cparams(("arbitrary",)),
    )(s16t, dm, w_ada)


def _pack(arrs):
    flat = jnp.concatenate([a.reshape(-1).astype(F32) for a in arrs])
    pad = (-flat.shape[0]) % 1024
    return jnp.pad(flat, (0, pad)).reshape(-1, 128)


def _unpack(flat2d, shapes):
    flat = flat2d.reshape(-1)
    out, pos = [], 0
    for s in shapes:
        size = int(np.prod(s))
        out.append(flat[pos:pos + size].reshape(s))
        pos += size
    return out


SMALL = ("norm1_g", "norm2_g", "ret_decay_logit", "ret_norm_g", "pool_w", "pool_scale", "sg_norm_g", "sg_w", "sg_b")


def kernel(x, c, ctx, c_ctx, w_ada, b_ada, norm1_g, w_in, ret_decay_logit, ret_norm_g, pool_w, pool_scale, sg_norm_g, sg_w, sg_b, w_br, w_bp, w_bs, w_out, norm2_g, w1, w2, final_norm_g, loss_target, m_c_ctx, m_w_ada, m_b_ada, m_norm1_g, m_w_in, m_ret_decay_logit, m_ret_norm_g, m_pool_w, m_pool_scale, m_sg_norm_g, m_sg_w, m_sg_b, m_w_br, m_w_bp, m_w_bs, m_w_out, m_norm2_g, m_w1, m_w2, m_final_norm_g, v_c_ctx, v_w_ada, v_b_ada, v_norm1_g, v_w_in, v_ret_decay_logit, v_ret_norm_g, v_pool_w, v_pool_scale, v_sg_norm_g, v_sg_w, v_sg_b, v_w_br, v_w_bp, v_w_bs, v_w_out, v_norm2_g, v_w1, v_w2, v_final_norm_g):
    P = dict(c_ctx=c_ctx, w_ada=w_ada, b_ada=b_ada, norm1_g=norm1_g, w_in=w_in, ret_decay_logit=ret_decay_logit,
             ret_norm_g=ret_norm_g, pool_w=pool_w, pool_scale=pool_scale, sg_norm_g=sg_norm_g, sg_w=sg_w, sg_b=sg_b, w_br=w_br,
             w_bp=w_bp, w_bs=w_bs, w_out=w_out, norm2_g=norm2_g, w1=w1, w2=w2, final_norm_g=final_norm_g)
    Mo = dict(c_ctx=m_c_ctx, w_ada=m_w_ada, b_ada=m_b_ada, norm1_g=m_norm1_g, w_in=m_w_in, ret_decay_logit=m_ret_decay_logit,
              ret_norm_g=m_ret_norm_g, pool_w=m_pool_w, pool_scale=m_pool_scale, sg_norm_g=m_sg_norm_g, sg_w=m_sg_w, sg_b=m_sg_b,
              w_br=m_w_br, w_bp=m_w_bp, w_bs=m_w_bs, w_out=m_w_out, norm2_g=m_norm2_g, w1=m_w1, w2=m_w2,
              final_norm_g=m_final_norm_g)
    Vo = dict(c_ctx=v_c_ctx, w_ada=v_w_ada, b_ada=v_b_ada, norm1_g=v_norm1_g, w_in=v_w_in, ret_decay_logit=v_ret_decay_logit,
              ret_norm_g=v_ret_norm_g, pool_w=v_pool_w, pool_scale=v_pool_scale, sg_norm_g=v_sg_norm_g, sg_w=v_sg_w, sg_b=v_sg_b,
              w_br=v_w_br, w_bp=v_w_bp, w_bs=v_w_bs, w_out=v_w_out, norm2_g=v_norm2_g, w1=v_w1, w2=v_w2,
              final_norm_g=v_final_norm_g)
    names = ("c_ctx", "w_ada", "b_ada", "norm1_g", "w_in", "ret_decay_logit", "ret_norm_g", "pool_w", "pool_scale", "sg_norm_g",
             "sg_w", "sg_b", "w_br", "w_bp", "w_bs", "w_out", "norm2_g", "w1", "w2", "final_norm_g")
    L, D = w_in.shape[0], x.shape[-1]
    T, CTX = x.shape[1], ctx.shape[1]
    cfg = _Cfg(D, T, CTX, 4 * w1.shape[2])
    ns_ada = w_ada.shape[2]
    mx, my, mc = _me()
    dev = 4 * mx + 2 * my + mc
    chip = 2 * mx + my

    silu_cc = jax.nn.silu(c_ctx)
    silu_all = _gather_small(jax.nn.silu(c).reshape(-1, 128)).reshape(8, D)
    s16 = jnp.concatenate([silu_cc[None], silu_all, jnp.zeros((7, D), F32)], axis=0)
    proj = jnp.stack([_ada_fwd(s16, w_ada, l) for l in range(L)])
    proj_all = _gather_small(proj.reshape(-1, 128)).reshape(8, L, 16, ns_ada)
    mods_full = jnp.concatenate([proj_all[2 * j] for j in range(4)], axis=-1) + b_ada[:, None, :]
    mods = [jnp.concatenate([mods_full[l, 0:1], lax.dynamic_slice_in_dim(mods_full[l], 1 + dev, 1, axis=0)], axis=0)
            for l in range(L)]

    sc = jnp.stack([mc, chip]).astype(jnp.int32)
    gathers = [_WeightGather(P, l, sc) for l in range(L)]
    X0 = jnp.concatenate([ctx[0], x[0]], axis=0)
    small_w = {k: P[k] for k in SMALL}
    loss_part, dX, shard_g, small, dmods, dfinal = _local_fwd_bwd(
        cfg, X0, loss_target[0], mods, gathers, lambda names, grads: _GradReduce(names, grads, sc), small_w, final_norm_g)
    loss = lax.psum(loss_part, ("x", "y", "c"))
    grad_x = dX[CTX:][None]

    outs = {k: None for k in BIG}
    for l in range(L):
        for k in BIG:
            outs[k] = _adam_layer(P[k], Mo[k], Vo[k], l, shard_g[l][k], outs[k])

    per_layer = [[small[l][k] for k in SMALL] + [dmods[l][1], dmods[l][0]] for l in range(L)]
    payload = _pack([a for lay in per_layer for a in lay] + [dfinal])
    gathered = _gather_small(payload)
    total = _sum8(gathered)
    shapes = [P[k].shape[1:] for k in SMALL] + [(6 * D,), (6 * D,)]
    tot = _unpack(total, shapes * L + [(D,)])
    per = len(shapes)
    g_small = {k: jnp.stack([tot[l * per + q] for l in range(L)]) for q, k in enumerate(SMALL)}
    dmx_sum = jnp.stack([tot[l * per + per - 2] for l in range(L)])
    dmc_sum = jnp.stack([tot[l * per + per - 1] for l in range(L)])
    g_small["b_ada"] = dmx_sum + dmc_sum
    g_small["final_norm_g"] = tot[-1]
    offs = np.cumsum([0] + [int(np.prod(s)) for s in shapes])
    lay_size = int(offs[-1])
    gflat = gathered.reshape(8, -1)
    s16t = s16.T
    ada_out, ds_part = None, jnp.zeros((16, D), F32)
    for l in range(L):
        dmx_all = gflat[:, l * lay_size + int(offs[per - 2]):l * lay_size + int(offs[per - 1])]
        dm_full = jnp.concatenate([dmc_sum[l][None], dmx_all, jnp.zeros((7, 6 * D), F32)], axis=0)
        dm = lax.dynamic_slice_in_dim(dm_full, chip * ns_ada, ns_ada, axis=1)
        dw, ds = _ada_bwd(s16t, dm, w_ada, l)
        ds_part = ds_part + ds
        ada_out = _adam_layer(w_ada, Mo["w_ada"], Vo["w_ada"], l, dw, ada_out)
    outs["w_ada"] = ada_out
    ds_all = _gather_small(ds_part[0].reshape(-1, 128)).reshape(8, D)
    d_silu_cc = ds_all[0] + ds_all[2] + ds_all[4] + ds_all[6]
    g_small["c_ctx"] = jax.vjp(jax.nn.silu, c_ctx)[1](d_silu_cc)[0]

    small_names = [k for k in names if k not in BIG and k != "w_ada"]
    sm_shapes = [P[k].shape for k in small_names]
    res = _adam_flat(_pack([P[k] for k in small_names]), _pack([g_small[k] for k in small_names]),
                     _pack([Mo[k] for k in small_names]), _pack([Vo[k] for k in small_names]))
    d_s, m_s, v_s = [_unpack(r, sm_shapes) for r in res]
    for q, k in enumerate(small_names):
        outs[k] = (g_small[k].reshape(P[k].shape), d_s[q], m_s[q], v_s[q])

    return (loss, grad_x, *[outs[k][0] for k in names], *[outs[k][1] for k in names], *[outs[k][2] for k in names],
            *[outs[k][3] for k in names])
```

```python
import functools

import numpy as np
import jax
import jax.numpy as jnp
from jax import lax
from jax.experimental import pallas as pl
from jax.experimental.pallas import tpu as pltpu

F32 = jnp.float32
BF16 = jnp.bfloat16
EPS = 1e-6
CH = 128
GRID_W = 64
ROPE_THETA = 10000.0
POOL_WINDOWS = (2, 4, 8, 16)
PT = 256
VMEM_LIMIT = 56 * 1024 * 1024
MESH = pl.DeviceIdType.MESH
ANY = pl.BlockSpec(memory_space=pl.ANY)

ADAM_LR = 0.001
ADAM_B1 = 0.9
ADAM_B2 = 0.999
ADAM_EPS = 1e-08
ADAM_WD = 0.01
ADAM_STEP = 10

BIG = ("w_in", "w_br", "w_bp", "w_bs", "w_out", "w1", "w2")
SHARD_AXIS = {"w_in": 1, "w_br": 1, "w_bp": 1, "w_bs": 1, "w_out": 0, "w1": 1, "w2": 0}


def _pick(dim, pref, mult=128):
    best = None
    for t in range(mult, min(dim, pref) + 1, mult):
        if dim % t == 0:
            best = t
    return dim if best is None else best


def _cparams(sem=None):
    return pltpu.CompilerParams(dimension_semantics=sem, vmem_limit_bytes=VMEM_LIMIT)


def _rows(i, tr):
    return i * tr + lax.broadcasted_iota(jnp.int32, (tr, 1), 0)


def _sel(i, tr, n_ctx, v2):
    return jnp.where(_rows(i, tr) < n_ctx, v2[0:1, :], v2[1:2, :])


def _seg_sums(i, tr, n_ctx, d):
    is_ctx = _rows(i, tr) < n_ctx
    s_c = jnp.sum(jnp.where(is_ctx, d, 0.0), axis=0, keepdims=True)
    s_x = jnp.sum(jnp.where(is_ctx, 0.0, d), axis=0, keepdims=True)
    two = lax.broadcasted_iota(jnp.int32, (2, d.shape[1]), 0)
    return jnp.where(two == 0, s_c, s_x)


def _dot(a, b, form="NN"):
    dims = {"NN": (((1,), (0,)), ((), ())), "NT": (((1,), (1,)), ((), ())), "TN": (((0,), (0,)), ((), ()))}[form]
    return lax.dot_general(a, b, dims, preferred_element_type=F32)


def _gelu(x):
    return 0.5 * x * (1.0 + jnp.tanh(0.7978845608028654 * (x + 0.044715 * x * x * x)))


def _matmul(name, form, pairs, *, R, C, tr, tc, tk, nk, out_dtypes, epi, extras=(), a_pro=None, comm=None):
    npair, nex, nout = len(pairs), len(extras), len(out_dtypes)
    in_specs, args = [], []
    for a, b in pairs:
        if nk == 1:
            ka = a.shape[0] if form == "TN" else a.shape[1]
        else:
            ka = tk
        if form == "NN":
            in_specs += [pl.BlockSpec((tr, ka), lambda i, j, k: (i, k)), pl.BlockSpec((ka, tc), lambda i, j, k: (k, j))]
        elif form == "NT":
            in_specs += [pl.BlockSpec((tr, ka), lambda i, j, k: (i, k)), pl.BlockSpec((tc, ka), lambda i, j, k: (j, k))]
        else:
            in_specs += [pl.BlockSpec((ka, tr), lambda i, j, k: (k, i)), pl.BlockSpec((ka, tc), lambda i, j, k: (k, j))]
        args += [a, b]
    for kind, arr, off in extras:
        if kind == "tile":
            in_specs.append(pl.BlockSpec((tr, tc), lambda i, j, k, off=off: (i, j + off)))
        else:
            in_specs.append(pl.BlockSpec((2, tc), lambda i, j, k, off=off: (0, j + off)))
        args.append(arr)

    direct = epi is None
    n_acc = 0 if (nk == 1 or direct) else npair
    n_main = len(args)
    ni, nj = R // tr, C // tc
    aliases = {}
    out_specs = [pl.BlockSpec((tr, tc), lambda i, j, k: (i, j)) for _ in out_dtypes]
    out_shape = [jax.ShapeDtypeStruct((R, C), dt) for dt in out_dtypes]
    scratch = [pltpu.VMEM((tr, tc), F32) for _ in range(n_acc)]
    n_cin = n_cout = 0
    if comm is not None:
        n_cin, n_cout = len(comm.ins), len(comm.outs)
        in_specs = in_specs + [ANY] * n_cin
        args = args + list(comm.ins)
        out_specs = out_specs + [ANY] * n_cout
        out_shape = out_shape + list(comm.outs)
        aliases = {n_main + a: nout + b for a, b in comm.aliases.items()}
        scratch = scratch + [pltpu.SemaphoreType.DMA((comm.nsem,)), pltpu.SemaphoreType.DMA((comm.nsem,))]

    def body(*refs):
        ab = refs[:2 * npair]
        ex = refs[2 * npair:n_main]
        cin = refs[n_main:n_main + n_cin]
        outs = refs[n_main + n_cin:n_main + n_cin + nout]
        cout = refs[n_main + n_cin + nout:n_main + n_cin + nout + n_cout]
        accs = refs[n_main + n_cin + nout + n_cout:n_main + n_cin + nout + n_cout + n_acc]
        sems = refs[n_main + n_cin + nout + n_cout + n_acc:]
        i, j, k = pl.program_id(0), pl.program_id(1), pl.program_id(2)

        if comm is not None:
            @pl.when(jnp.logical_and(jnp.logical_and(i == 0, j == 0), k == 0))
            def _():
                comm.start(cin, cout, *sems)

        def products():
            res = []
            for p in range(npair):
                a = ab[2 * p][...]
                if a_pro is not None:
                    a = a_pro(a)
                res.append(_dot(a, ab[2 * p + 1][...], form))
            return res

        def finish(vals):
            res = epi(vals, [e[...] for e in ex], i)
            for o, v in zip(outs, res):
                o[...] = v.astype(o.dtype)

        if direct:
            prod = products()[0]
            if nk == 1:
                outs[0][...] = prod
            else:
                @pl.when(k == 0)
                def _():
                    outs[0][...] = prod

                @pl.when(k > 0)
                def _():
                    outs[0][...] += prod
        elif nk == 1:
            finish(products())
        else:
            prods = products()

            @pl.when(k == 0)
            def _():
                for acc, v in zip(accs, prods):
                    acc[...] = v

            @pl.when(k > 0)
            def _():
                for acc, v in zip(accs, prods):
                    acc[...] += v

            @pl.when(k == nk - 1)
            def _():
                finish([acc[...] for acc in accs])

        if comm is not None:
            @pl.when(jnp.logical_and(jnp.logical_and(i == ni - 1, j == nj - 1), k == nk - 1))
            def _():
                comm.finish(cin, cout, *sems)

    res = pl.pallas_call(
        body, name=name, grid=(ni, nj, nk),
        in_specs=in_specs, out_specs=out_specs, out_shape=out_shape, scratch_shapes=scratch,
        input_output_aliases=aliases,
        compiler_params=_cparams(("arbitrary", "arbitrary", "arbitrary")),
    )(*args)
    return res if comm is None else (res[:nout], res[nout:])


def _rowcall(name, fn, *, nrows, tr, ins, outs, accs=(), ncol=1):
    n_in, n_out, n_acc = len(ins), len(outs), len(accs)
    in_specs, args = [], []
    for arr, kind, w, off in ins:
        if kind == "rows":
            in_specs.append(pl.BlockSpec((tr, w), lambda i, j, off=off: (i, off + j)))
        elif kind == "full":
            in_specs.append(pl.BlockSpec(arr.shape, lambda i, j, nd=arr.ndim: (0,) * nd))
        else:
            in_specs.append(pl.BlockSpec((None,) + arr.shape[1:], lambda i, j, f=off, nd=arr.ndim: (f(i),) + (0,) * (nd - 1)))
        args.append(arr)
    aliases = {}
    out_specs, out_shape = [], []
    for o_idx, (dt, total, w, off, alias) in enumerate(outs):
        out_specs.append(pl.BlockSpec((tr, w), lambda i, j, off=off: (i, off + j)))
        out_shape.append(jax.ShapeDtypeStruct((nrows, total), dt))
        if alias is not None:
            aliases[len(args)] = o_idx
            in_specs.append(ANY)
            args.append(alias)
    n_alias = len(aliases)
    for shp in accs:
        out_specs.append(pl.BlockSpec(shp, lambda i, j, nd=len(shp): (0,) * nd))
        out_shape.append(jax.ShapeDtypeStruct(shp, F32))

    def body(*refs):
        in_refs = refs[:n_in]
        out_refs = refs[n_in + n_alias:n_in + n_alias + n_out]
        acc_refs = refs[n_in + n_alias + n_out:]
        i, j = pl.program_id(0), pl.program_id(1)
        res = fn(i, j, *[r[...] for r in in_refs])
        for o, v in zip(out_refs, res[:n_out]):
            o[...] = v.astype(o.dtype)
        first = jnp.logical_and(i == 0, j == 0)
        for acc, v in zip(acc_refs, res[n_out:]):
            @pl.when(first)
            def _(acc=acc, v=v):
                acc[...] = v

            @pl.when(jnp.logical_not(first))
            def _(acc=acc, v=v):
                acc[...] += v

    res = pl.pallas_call(
        body, name=name, grid=(nrows // tr, ncol),
        in_specs=in_specs, out_specs=out_specs, out_shape=out_shape,
        input_output_aliases=aliases,
        compiler_params=_cparams(("arbitrary", "arbitrary")),
    )(*args)
    return res


def _f_normmod(x, g, shift, scale):
    r = lax.rsqrt(jnp.mean(x * x, axis=-1, keepdims=True) + EPS)
    return (x * r * g) * (1.0 + scale) + shift


def _f_headnorm_gate(o, zg, ng):
    r = lax.rsqrt(jnp.mean(o * o, axis=-1, keepdims=True) + EPS)
    return (o * r * ng) * (zg * jax.nn.sigmoid(zg))


def _f_sgv(sv, g):
    v = _gelu(sv)
    r = lax.rsqrt(jnp.mean(v * v, axis=-1, keepdims=True) + EPS)
    return v * r * g


def _rope(t, cos, sin):
    w = t.shape[1]
    lane = lax.broadcasted_iota(jnp.int32, t.shape, 1)
    swapped = jnp.where(jnp.bitwise_and(lane, 63) < 32,pltpu.roll(t, w - 32, 1), pltpu.roll(t, 32, 1))
    return t * cos + swapped * sin


def _rope_t(d, cos, sin):
    w = d.shape[1]
    lane = lax.broadcasted_iota(jnp.int32, d.shape, 1)
    ds = d * sin
    swapped = jnp.where(jnp.bitwise_and(lane, 63) < 32,pltpu.roll(ds, w - 32, 1), pltpu.roll(ds, 32, 1))
    return d * cos + swapped


class _Cfg:
    def __init__(self, D, T, CTX, DFF):
        self.D, self.T, self.CTX, self.DFF = D, T, CTX, DFF
        self.TA = T + CTX
        self.RW = D // 2
        self.H = self.RW // CH
        self.PW = D // 4
        self.SW = D // 4
        self.NIN = 4 * self.RW + self.PW + 2 * self.SW + 3 * D
        self.NC = self.TA // CH
        self.NCC = CTX // CH
        self.k_scale = float(CH) ** -0.5
        self.tr = _pick(self.TA, 1408)
        self.tr_small = _pick(self.TA, 768)
        self.o_g = 3 * self.RW
        self.o_p = 4 * self.RW
        self.o_u = self.o_p + self.PW
        self.o_sv = self.o_u + self.SW
        self.o_gate = self.o_sv + self.SW


def _normmod_fwd(cfg, X, g, mods, i_shift, i_scale, name):
    D = cfg.D
    tr = _pick(cfg.TA, 384)

    def fn(i, j, x, g, m):
        sh = _sel(i, tr, cfg.CTX, m[:, i_shift * D:(i_shift + 1) * D])
        sc = _sel(i, tr, cfg.CTX, m[:, i_scale * D:(i_scale + 1) * D])
        return (_f_normmod(x, g, sh, sc),)

    return _rowcall(name, fn, nrows=cfg.TA, tr=tr, ins=[(X, "rows", D, 0), (g, "full", 0, 0), (mods, "full", 0, 0)],
                    outs=[(BF16, D, D, 0, None)])[0]


def _normmod_bwd(cfg, X, dH, dres, g, mods, i_shift, i_scale, name):
    D = cfg.D
    tr = _pick(cfg.TA, 384)

    def fn(i, j, x, dh, dr, g, m):
        sh = _sel(i, tr, cfg.CTX, m[:, i_shift * D:(i_shift + 1) * D])
        sc = _sel(i, tr, cfg.CTX, m[:, i_scale * D:(i_scale + 1) * D])
        _, vjp = jax.vjp(_f_normmod, x, g, sh, sc)
        dx, dg, dsh, dsc = vjp(dh)
        return dr + dx, dg, _seg_sums(i, tr, cfg.CTX, dsh), _seg_sums(i, tr, cfg.CTX, dsc)

    return _rowcall(name, fn, nrows=cfg.TA, tr=tr,
                    ins=[(X, "rows", D, 0), (dH, "rows", D, 0), (dres, "rows", D, 0), (g, "full", 0, 0), (mods, "full", 0, 0)],
                    outs=[(F32, D, D, 0, None)], accs=[(1, D), (2, D), (2, D)])


def _resgate_bwd(cfg, dX, M, mods, i_gate, name):
    D = cfg.D
    tr = _pick(cfg.TA, 384)

    def fn(i, j, dx, m, mm):
        gate = _sel(i, tr, cfg.CTX, mm[:, i_gate * D:(i_gate + 1) * D])
        return dx * gate, _seg_sums(i, tr, cfg.CTX, dx * m.astype(F32))

    return _rowcall(name, fn, nrows=cfg.TA, tr=tr, ins=[(dX, "rows", D, 0), (M, "rows", D, 0), (mods, "full", 0, 0)],
                    outs=[(BF16, D, D, 0, None)], accs=[(2, D)])


def _chunk_of(cfg, d, t):
    fwd = t
    bwd = jnp.where(t < cfg.NCC, cfg.NCC - 1 - t, cfg.NC - 1 - t + cfg.NCC)
    return jnp.where(d == 0, fwd, bwd)


def _ret_specs(cfg, cm):
    RW, H = cfg.RW, cfg.H
    return [
        pl.BlockSpec((CH, RW), lambda d, t: (cm(d, t), 0)),
        pl.BlockSpec((CH, RW), lambda d, t: (cm(d, t), 1)),
        pl.BlockSpec((CH, RW), lambda d, t: (cm(d, t), 2)),
        pl.BlockSpec((CH, CH), lambda d, t: (cm(d, t), 0)),
        pl.BlockSpec((CH, CH), lambda d, t: (cm(d, t), 0)),
        pl.BlockSpec((None, H, CH, CH), lambda d, t: (d, 0, 0, 0)),
        pl.BlockSpec((None, CH, RW), lambda d, t: (d, 0, 0)),
        pl.BlockSpec((None, CH, RW), lambda d, t: (d, 0, 0)),
        pl.BlockSpec((None, 1, RW), lambda d, t: (d, 0, 0)),
    ]


def _ret_prep(cfg, q_ref, k_ref, cos, sin, qd_ref, kd_ref, sl):
    qr = _rope(q_ref[:, sl].astype(F32), cos, sin)
    kr = _rope(k_ref[:, sl].astype(F32) * cfg.k_scale, cos, sin)
    return qr, kr, (qr * qd_ref[:, sl]).astype(BF16), (kr * kd_ref[:, sl]).astype(BF16)


def _ret_fwd(cfg, Z, tabs, decs):
    RW, H, TA, NC = cfg.RW, cfg.H, cfg.TA, cfg.NC
    cm = functools.partial(_chunk_of, cfg)

    def body(q_ref, k_ref, v_ref, ct_ref, st_ref, dm_ref, qd_ref, kd_ref, cd_ref, o_ref, so_ref, S):
        @pl.when(pl.program_id(1) == 0)
        def _():
            S[...] = jnp.zeros_like(S)

        cos, sin = ct_ref[...], st_ref[...]
        for h in range(H):
            sl = slice(h * CH, (h + 1) * CH)
            qr, kr, qd, kd = _ret_prep(cfg, q_ref, k_ref, cos, sin, qd_ref, kd_ref, sl)
            v = v_ref[:, sl].astype(BF16)
            p = (_dot(qr.astype(BF16), kr.astype(BF16), "NT") * dm_ref[h]).astype(BF16)
            s_h = S[h]
            so_ref[h] = s_h
            o_ref[:, sl] = _dot(p, v) + _dot(qd, s_h.astype(BF16))
            S[h] = s_h * cd_ref[:, sl] + _dot(kd, v, "TN")

    return pl.pallas_call(
        body, name="ret_fwd", grid=(2, NC),
        in_specs=_ret_specs(cfg, cm),
        out_specs=[pl.BlockSpec((None, CH, RW), lambda d, t: (d, cm(d, t), 0)),
                   pl.BlockSpec((None, None, H, CH, CH), lambda d, t: (d, cm(d, t), 0, 0, 0))],
        out_shape=[jax.ShapeDtypeStruct((2, TA, RW), F32), jax.ShapeDtypeStruct((2, NC, H, CH, CH), F32)],
        scratch_shapes=[pltpu.VMEM((H, CH, CH), F32)],
        compiler_params=_cparams(("arbitrary", "arbitrary")),
    )(Z, Z, Z, tabs["cos"], tabs["sin"], decs["dmat"], decs["qdec"], decs["kdec"], decs["cdec"])


def _ret_bwd(cfg, Z, tabs, decs, states, dO):
    RW, H, TA, NC = cfg.RW, cfg.H, cfg.TA, cfg.NC

    def cm(d, t):
        return _chunk_of(cfg, d, NC - 1 - t)

    def body(q_ref, k_ref, v_ref, ct_ref, st_ref, dm_ref, qd_ref, kd_ref, cd_ref, s_ref, do_ref, w_ref,
             dqkv_ref, dl_ref, dS):
        t = pl.program_id(1)

        @pl.when(t == 0)
        def _():
            dS[...] = jnp.zeros_like(dS)
            dl_ref[...] = jnp.zeros_like(dl_ref)

        cos, sin = ct_ref[...], st_ref[...]
        for h in range(H):
            sl = slice(h * CH, (h + 1) * CH)
            qh, kh, qd, kd = _ret_prep(cfg, q_ref, k_ref, cos, sin, qd_ref, kd_ref, sl)
            qb, kb = qh.astype(BF16), kh.astype(BF16)
            v = v_ref[:, sl].astype(BF16)
            dob = do_ref[:, sl].astype(BF16)
            dm = dm_ref[h]
            p = (_dot(qb, kb, "NT") * dm).astype(BF16)
            dp = (_dot(dob, v, "NT") * dm).astype(BF16)
            s_h = s_ref[h]
            ds_h = dS[h]
            sb, dsb = s_h.astype(BF16), ds_h.astype(BF16)
            dq_i = _dot(dp, kb)
            dk_i = _dot(dp, qb, "TN")
            dq_c = _dot(dob, sb, "NT") * qd_ref[:, sl]
            dk_s = _dot(v, dsb, "NT") * kd_ref[:, sl]
            dqkv_ref[:, sl] = dq_i + dq_c
            dqkv_ref[:, RW + h * CH:RW + (h + 1) * CH] = dk_i + dk_s
            dqkv_ref[:, 2 * RW + h * CH:2 * RW + (h + 1) * CH] = _dot(p, dob, "TN") + _dot(kd, dsb)
            lam = w_ref[0] * (qh * dq_i) + w_ref[1] * (qh * dq_c) + w_ref[2] * (kh * dk_i) + w_ref[3] * (kh * dk_s)
            lam_s = float(CH) * cd_ref[:, sl] * jnp.sum(ds_h * s_h, axis=0, keepdims=True)
            dl_ref[:, sl] += jnp.sum(lam, axis=0, keepdims=True) + lam_s
            dS[h] = ds_h * cd_ref[:, sl] + _dot(qd, dob, "TN")

    in_specs = _ret_specs(cfg, cm) + [
        pl.BlockSpec((None, None, H, CH, CH), lambda d, t: (d, cm(d, t), 0, 0, 0)),
        pl.BlockSpec((CH, RW), lambda d, t: (cm(d, t), 0)),
        pl.BlockSpec((None, 4, CH, CH), lambda d, t: (d, 0, 0, 0)),
    ]
    return pl.pallas_call(
        body, name="ret_bwd", grid=(2, NC),
        in_specs=in_specs,
        out_specs=[pl.BlockSpec((None, CH, 3 * RW), lambda d, t: (d, cm(d, t), 0)),
                   pl.BlockSpec((None, 1, RW), lambda d, t: (d, 0, 0))],
        out_shape=[jax.ShapeDtypeStruct((2, TA, 3 * RW), F32), jax.ShapeDtypeStruct((2, 1, RW), F32)],
        scratch_shapes=[pltpu.VMEM((H, CH, CH), F32)],
        compiler_params=_cparams(("arbitrary", "arbitrary")),
    )(Z, Z, Z, tabs["cos"], tabs["sin"], decs["dmat"], decs["qdec"], decs["kdec"], decs["cdec"], states, dO,
      tabs["lamw"])


def _rope_bwd(cfg, dqkv2, tabs, dz):
    RW, H = cfg.RW, cfg.H
    tr = PT

    def body(d0, d1, ct, st, dz_in, o):
        cos, sin = jnp.tile(ct[...], (1, H)), jnp.tile(st[...], (1, H))
        d = d0[...] + d1[...]
        o[:, :RW] = _rope_t(d[:, :RW], cos, sin).astype(o.dtype)
        o[:, RW:2 * RW] = (_rope_t(d[:, RW:2 * RW], cos, sin) * cfg.k_scale).astype(o.dtype)
        o[:, 2 * RW:] = d[:, 2 * RW:].astype(o.dtype)

    return pl.pallas_call(
        body, name="rope_bwd", grid=(cfg.TA // tr,),
        in_specs=[pl.BlockSpec((None, tr, 3 * RW), lambda i: (0, i, 0)), pl.BlockSpec((None, tr, 3 * RW), lambda i: (1, i, 0)),
                  pl.BlockSpec((tr, CH), lambda i: (i, 0)), pl.BlockSpec((tr, CH), lambda i: (i, 0)), ANY],
        out_specs=pl.BlockSpec((tr, 3 * RW), lambda i: (i, 0)),
        out_shape=jax.ShapeDtypeStruct((cfg.TA, cfg.NIN), BF16),
        input_output_aliases={4: 0},
        compiler_params=_cparams(("arbitrary",)),
    )(dqkv2, dqkv2, tabs["cos"], tabs["sin"], dz)


def _retout_fwd(cfg, o2, Z, ng):
    RW, H = cfg.RW, cfg.H
    tr = PT

    def body(o0, o1, zg, ng, out):
        o = o0[...] + o1[...]
        z = zg[...].astype(F32)
        for h in range(H):
            sl = slice(h * CH, (h + 1) * CH)
            out[:, sl] = _f_headnorm_gate(o[:, sl], z[:, sl], ng[:, sl]).astype(out.dtype)

    return pl.pallas_call(
        body, name="retout_fwd", grid=(cfg.TA // tr,),
        in_specs=[pl.BlockSpec((None, tr, RW), lambda i: (0, i, 0)), pl.BlockSpec((None, tr, RW), lambda i: (1, i, 0)),
                  pl.BlockSpec((tr, RW), lambda i: (i, 3)), pl.BlockSpec((1, RW), lambda i: (0, 0))],
        out_specs=pl.BlockSpec((tr, RW), lambda i: (i, 0)),
        out_shape=jax.ShapeDtypeStruct((cfg.TA, RW), BF16),
        compiler_params=_cparams(("arbitrary",)),
    )(o2, o2, Z, ng)


def _retout_bwd(cfg, o2, Z, ng, dret, dz):
    RW, H = cfg.RW, cfg.H
    tr = PT

    def body(o0, o1, zg, ng, dr, dz_in, do_out, dz_out, dng):
        i = pl.program_id(0)
        o = o0[...] + o1[...]
        z = zg[...].astype(F32)
        d = dr[...]
        acc = []
        for h in range(H):
            sl = slice(h * CH, (h + 1) * CH)
            _, vjp = jax.vjp(_f_headnorm_gate, o[:, sl], z[:, sl], ng[:, sl])
            do_h, dz_h, dg_h = vjp(d[:, sl])
            do_out[:, sl] = do_h
            dz_out[:, sl] = dz_h.astype(dz_out.dtype)
            acc.append(dg_h)

        @pl.when(i == 0)
        def _():
            for h in range(H):
                dng[:, h * CH:(h + 1) * CH] = acc[h]

        @pl.when(i > 0)
        def _():
            for h in range(H):
                dng[:, h * CH:(h + 1) * CH] += acc[h]

    return pl.pallas_call(
        body, name="retout_bwd", grid=(cfg.TA // tr,),
        in_specs=[pl.BlockSpec((None, tr, RW), lambda i: (0, i, 0)), pl.BlockSpec((None, tr, RW), lambda i: (1, i, 0)),
                  pl.BlockSpec((tr, RW), lambda i: (i, 3)), pl.BlockSpec((1, RW), lambda i: (0, 0)),
                  pl.BlockSpec((tr, RW), lambda i: (i, 0)), ANY],
        out_specs=[pl.BlockSpec((tr, RW), lambda i: (i, 0)), pl.BlockSpec((tr, RW), lambda i: (i, 3)),
                   pl.BlockSpec((1, RW), lambda i: (0, 0))],
        out_shape=[jax.ShapeDtypeStruct((cfg.TA, RW), F32), jax.ShapeDtypeStruct((cfg.TA, cfg.NIN), BF16),
                   jax.ShapeDtypeStruct((1, RW), F32)],
        input_output_aliases={5: 1},
        compiler_params=_cparams(("arbitrary",)),
    )(o2, o2, Z, ng, dret, dz)


def _pool_consts(ctx_len):
    assert ctx_len == PT
    bm = np.zeros((2, len(POOL_WINDOWS), PT, PT), np.float32)
    ic = np.zeros((2, len(POOL_WINDOWS), PT, CH), np.float32)
    for ty, seg in enumerate((ctx_len, GRID_W)):
        for gi, w in enumerate(POOL_WINDOWS):
            for r in range(PT):
                s0, pos = (r // seg) * seg, r % seg
                lo, hi = max(pos - w // 2, 0), min(pos + w // 2 - 1, seg - 1)
                bm[ty, gi, r, s0 + lo:s0 + hi + 1] = 1.0
                ic[ty, gi, r, :] = 1.0 / (hi - lo + 1)
    return jnp.asarray(bm, BF16), jnp.asarray(ic, F32)


def _pool_tile(p, bm, ic, pw, g):
    sl = slice(g * CH, (g + 1) * CH)
    pg = p[:, sl].astype(F32)
    hi = pg.astype(BF16)
    lo = (pg - hi.astype(F32)).astype(BF16)
    y = (_dot(bm[g], hi) + _dot(bm[g], lo)) * ic[g] - pg
    return y, _dot(y.astype(BF16), pw[g].astype(BF16))


def _pool_fwd(cfg, Z, consts, pool_w, pool_scale):
    PW = cfg.PW
    G = PW // CH
    nct = cfg.CTX // PT
    ty = lambda i: jnp.where(i < nct, 0, 1)

    def fn(i, j, p, bm, ic, pw, ps):
        outs = [_pool_tile(p, bm, ic, pw, g)[1] for g in range(G)]
        return (jnp.concatenate(outs, axis=1) * ps,)

    return _rowcall("pool_fwd", fn, nrows=cfg.TA, tr=PT,
                    ins=[(Z, "rows", PW, cfg.o_p // PW), (consts[0], "sel", 0, ty), (consts[1], "sel", 0, ty),
                         (pool_w, "full", 0, 0), (pool_scale, "full", 0, 0)],
                    outs=[(BF16, PW, PW, 0, None)])[0]


def _pool_bwd(cfg, Z, consts, pool_w, pool_scale, dpool, dz):
    PW = cfg.PW
    G = PW // CH
    nct = cfg.CTX // PT
    ty = lambda i: jnp.where(i < nct, 0, 1)

    def fn(i, j, p, bm, ic, pw, ps, dout):
        dps, dps_acc, dpw = [], [], []
        for g in range(G):
            sl = slice(g * CH, (g + 1) * CH)
            y, lin = _pool_tile(p, bm, ic, pw, g)
            dlin = (dout[:, sl] * ps[:, sl]).astype(BF16)
            dps_acc.append(jnp.sum(dout[:, sl] * lin, axis=0, keepdims=True))
            dy = _dot(dlin, pw[g].astype(BF16), "NT")
            dpw.append(_dot(y.astype(BF16), dlin, "TN"))
            t = dy * ic[g]
            hi = t.astype(BF16)
            lo = (t - hi.astype(F32)).astype(BF16)
            dps.append(_dot(bm[g], hi, "TN") + _dot(bm[g], lo, "TN") - dy)
        return (jnp.concatenate(dps, axis=1), jnp.stack(dpw), jnp.concatenate(dps_acc, axis=1))

    return _rowcall("pool_bwd", fn, nrows=cfg.TA, tr=PT,
                    ins=[(Z, "rows", PW, cfg.o_p // PW), (consts[0], "sel", 0, ty), (consts[1], "sel", 0, ty),
                         (pool_w, "full", 0, 0), (pool_scale, "full", 0, 0), (dpool, "rows", PW, 0)],
                    outs=[(BF16, cfg.NIN, PW, cfg.o_p // PW, dz)], accs=[(G, CH, CH), (1, PW)])


def _sg_mixed(vn, sw, sbb, g, c):
    rows = slice(c * CH, (c + 1) * CH)
    sl = slice(g * CH, (g + 1) * CH)
    return _dot(sw[g].astype(BF16), vn[rows, sl].astype(BF16)) + sbb[g]


def _sg_fwd(cfg, Z, sng, sw, sbb):
    SW = cfg.SW
    G = SW // CH

    def fn(i, j, u, sv, sng, sw, sbb):
        ug = _gelu(u.astype(F32))
        vn = _f_sgv(sv.astype(F32), sng)
        rows = []
        for c in range(PT // CH):
            mixed = jnp.concatenate([_sg_mixed(vn, sw, sbb, g, c) for g in range(G)], axis=1)
            rows.append(ug[c * CH:(c + 1) * CH, :] * mixed)
        return (jnp.concatenate(rows, axis=0),)

    return _rowcall("sg_fwd", fn, nrows=cfg.TA, tr=PT,
                    ins=[(Z, "rows", SW, cfg.o_u // SW), (Z, "rows", SW, cfg.o_sv // SW), (sng, "full", 0, 0),
                         (sw, "full", 0, 0), (sbb, "full", 0, 0)],
                    outs=[(BF16, SW, SW, 0, None)])[0]


def _sg_bwd(cfg, Z, sng, sw, sbb, dsg, dz):
    SW = cfg.SW
    G = SW // CH

    def fn(i, j, u, sv, sng, sw, sbb, dout):
        uf, svf = u.astype(F32), sv.astype(F32)
        ug, vjp_u = jax.vjp(_gelu, uf)
        vn, vjp_v = jax.vjp(_f_sgv, svf, sng)
        dug_rows, dvn_rows = [], []
        dsw = [jnp.zeros((CH, CH), F32) for _ in range(G)]
        dsb = [jnp.zeros((CH, CH), F32) for _ in range(G)]
        for c in range(PT // CH):
            rows = slice(c * CH, (c + 1) * CH)
            dug_g, dvn_g = [], []
            for g in range(G):
                sl = slice(g * CH, (g + 1) * CH)
                mixed = _sg_mixed(vn, sw, sbb, g, c)
                dmixed = dout[rows, sl] * ug[rows, sl]
                dug_g.append(dout[rows, sl] * mixed)
                dmb = dmixed.astype(BF16)
                dvn_g.append(_dot(sw[g].astype(BF16), dmb, "TN"))
                dsw[g] = dsw[g] + _dot(dmb, vn[rows, sl].astype(BF16), "NT")
                dsb[g] = dsb[g] + jnp.broadcast_to(jnp.sum(dmixed, axis=1, keepdims=True), (CH, CH))
            dug_rows.append(jnp.concatenate(dug_g, axis=1))
            dvn_rows.append(jnp.concatenate(dvn_g, axis=1))
        (du,) = vjp_u(jnp.concatenate(dug_rows, axis=0))
        dsv, dsng = vjp_v(jnp.concatenate(dvn_rows, axis=0))
        return du, dsv, jnp.stack(dsw), jnp.stack(dsb), dsng

    return _rowcall("sg_bwd", fn, nrows=cfg.TA, tr=PT,
                    ins=[(Z, "rows", SW, cfg.o_u // SW), (Z, "rows", SW, cfg.o_sv // SW), (sng, "full", 0, 0),
                         (sw, "full", 0, 0), (sbb, "full", 0, 0), (dsg, "rows", SW, 0)],
                    outs=[(BF16, cfg.NIN, SW, cfg.o_u // SW, dz), (BF16, SW, SW, 0, None)],
                    accs=[(G, CH, CH), (G, CH, CH), (1, SW)])


def _gate_bwd(cfg, b, dY, L, Z, dz):
    D = cfg.D
    tc = 512
    nj = D // tc
    off = cfg.o_gate // tc + b * nj

    def fn(i, j, dy, l, z):
        s = jax.nn.sigmoid(z.astype(F32))
        return dy * s, dy * l.astype(F32) * s * (1.0 - s)

    return _rowcall("gate_bwd", fn, nrows=cfg.TA, tr=cfg.tr,
                    ins=[(dY, "rows", tc, 0), (L, "rows", tc, 0), (Z, "rows", tc, off)],
                    outs=[(BF16, D, tc, 0, None), (BF16, cfg.NIN, tc, off, dz)], ncol=nj)


def _copy_cols(cfg, src, dz, col0):
    w = src.shape[1]
    return _rowcall("copy_cols", lambda i, j, s: (s,), nrows=cfg.TA, tr=cfg.tr, ins=[(src, "rows", w, 0)],
                    outs=[(dz.dtype, dz.shape[1], w, col0 // w, dz)])[0]


def _final(cfg, X, g, target):
    D = cfg.D
    tr = PT
    nct = cfg.CTX // tr

    def body(x_ref, g_ref, t_ref, dx_ref, loss_ref, dg_ref):
        i = pl.program_id(0)

        def f(x, g):
            r = lax.rsqrt(jnp.mean(x * x, axis=-1, keepdims=True) + EPS)
            return x * r * g

        y, vjp = jax.vjp(f, x_ref[...], g_ref[...])
        err = y - t_ref[...]
        dx, dg = vjp(err * (1.0 / D))
        part = 0.5 * jnp.sum(jnp.mean(err * err, axis=-1, keepdims=True), axis=0, keepdims=True)

        @pl.when(i == 0)
        def _():
            loss_ref[...] = jnp.zeros_like(loss_ref)
            dg_ref[...] = jnp.zeros_like(dg_ref)

        @pl.when(i < nct)
        def _():
            dx_ref[...] = jnp.zeros_like(dx_ref)

        @pl.when(i >= nct)
        def _():
            dx_ref[...] = dx
            loss_ref[...] += jnp.broadcast_to(part, loss_ref.shape)
            dg_ref[...] += dg

    return pl.pallas_call(
        body, name="final", grid=(cfg.TA // tr,),
        in_specs=[pl.BlockSpec((tr, D), lambda i: (i, 0)), pl.BlockSpec((1, D), lambda i: (0, 0)),
                  pl.BlockSpec((tr, D), lambda i: (jnp.maximum(i - nct, 0), 0))],
        out_specs=[pl.BlockSpec((tr, D), lambda i: (i, 0)), pl.BlockSpec((8, CH), lambda i: (0, 0)),
                   pl.BlockSpec((1, D), lambda i: (0, 0))],
        out_shape=[jax.ShapeDtypeStruct((cfg.TA, D), F32), jax.ShapeDtypeStruct((8, CH), F32),
                   jax.ShapeDtypeStruct((1, D), F32)],
        compiler_params=_cparams(("arbitrary",)),
    )(X, g, target)


GATHER_A = ("w_in", "w1")
GATHER_B = ("w2", "w_out", "w_br", "w_bp", "w_bs")
REDUCE_MLP = ("w2", "w1")
REDUCE_MIX = ("w_out", "w_br", "w_bp", "w_bs", "w_in")


def _hosted(res, comm, sink):
    if comm is None:
        return res
    main, outs = res
    sink(outs)
    return main


def _layer_fwd(cfg, X, mods, W, sm, tabs, decs, consts, nxt=None):
    D, TA, tr, RW, PW, SW, DFF, NIN = cfg.D, cfg.TA, cfg.tr, cfg.RW, cfg.PW, cfg.SW, cfg.DFF, cfg.NIN
    ident = lambda accs, ex, i: (accs[0],)
    H1 = _normmod_fwd(cfg, X, sm["norm1_g"], mods, 0, 1, "normmod1_fwd")
    comm = nxt.ici(GATHER_A) if nxt else None
    (Z,) = _hosted(_matmul("z_mm", "NN", [(H1, W["w_in"])], R=TA, C=NIN, tr=tr, tc=512, tk=D, nk=1, out_dtypes=[F32],
                           epi=ident, comm=comm), comm, lambda o: nxt.update(GATHER_A, o))
    o2, states = _ret_fwd(cfg, Z, tabs, decs)
    ret = _retout_fwd(cfg, o2, Z, sm["ret_norm_g"])
    pool = _pool_fwd(cfg, Z, consts, sm["pool_w"], sm["pool_scale"])
    sg = _sg_fwd(cfg, Z, sm["sg_norm_g"], sm["sg_w"], sm["sg_bb"])

    tc = 512
    goff = cfg.o_gate // tc

    def epi_branch(accs, ex, i):
        y = sum(jax.nn.sigmoid(z.astype(F32)) * a for a, z in zip(accs, ex))
        return (y, accs[0], accs[1], accs[2])

    Y, Lr, Lp, Ls = _matmul("branch_mm", "NN", [(ret, W["w_br"]), (pool, W["w_bp"]), (sg, W["w_bs"])], R=TA, C=D, tr=cfg.tr_small,
                            tc=tc, tk=0, nk=1, out_dtypes=[BF16] * 4, epi=epi_branch,
                            extras=[("tile", Z, goff + b * (D // tc)) for b in range(3)])

    def epi_res(rows):
        return lambda accs, ex, i: (ex[0] + _sel(i, rows, cfg.CTX, ex[1]) * accs[0], accs[0])

    X2, O = _matmul("out_mm", "NN", [(Y, W["w_out"])], R=TA, C=D, tr=tr, tc=tc, tk=D, nk=1, out_dtypes=[F32, BF16],
                    epi=epi_res(tr), extras=[("tile", X, 0), ("rows2", mods, 2 * (D // tc))])
    H2 = _normmod_fwd(cfg, X2, sm["norm2_g"], mods, 3, 4, "normmod2_fwd")
    tcf = _pick(DFF, 1024)
    def epi_relu(accs, ex, i):
        r = jnp.maximum(accs[0], 0.0)
        return (r * r, r)

    comm = nxt.ici(GATHER_B) if nxt else None
    A2, Rr = _hosted(_matmul("w1_mm", "NN", [(H2, W["w1"])], R=TA, C=DFF, tr=tr, tc=tcf, tk=D, nk=1, out_dtypes=[BF16, BF16],
                             epi=epi_relu, comm=comm), comm, lambda o: nxt.update(GATHER_B, o))
    comm = nxt.d2d(BIG) if nxt else None
    tr2 = _pick(TA, 704)
    X3, M = _hosted(_matmul("w2_mm", "NN", [(A2, W["w2"])], R=TA, C=D, tr=tr2, tc=tc, tk=DFF, nk=1,
                            out_dtypes=[F32, BF16], epi=epi_res(tr2), extras=[("tile", X2, 0), ("rows2", mods, 5 * (D // tc))],
                            comm=comm), comm, lambda o: nxt.update(BIG, o))
    saved = dict(X=X, H1=H1, Z=Z, o2=o2, states=states, ret=ret, pool=pool, sg=sg, Y=Y, L=(Lr, Lp, Ls), O=O, X2=X2,
                 H2=H2, R=Rr, A2=A2, M=M)
    return X3, saved


def _layer_bwd(cfg, dX3, sv, mods, W, sm, tabs, decs, consts, rs, make_reduce):
    D, TA, tr, RW, PW, SW, DFF, NIN = cfg.D, cfg.TA, cfg.tr, cfg.RW, cfg.PW, cfg.SW, cfg.DFF, cfg.NIN
    ident = lambda accs, ex, i: (accs[0],)
    tw = 512
    g = {}

    dM, dgate2 = _resgate_bwd(cfg, dX3, sv["M"], mods, 5, "resgate2_bwd")
    tcf = _pick(DFF, 1024)
    comm = rs.swap() if rs else None
    (dPre,) = _hosted(_matmul("dpre_mm", "NT", [(dM, W["w2"])], R=TA, C=DFF, tr=tr, tc=tcf, tk=D, nk=1, out_dtypes=[BF16],
                              epi=lambda accs, ex, i: (accs[0] * (2.0 * ex[0].astype(F32)),), extras=[("tile", sv["R"], 0)],
                              comm=comm), comm, lambda o: rs.swapped(o))
    comm = rs.scatter() if rs else None
    (g["w2"],) = _hosted(_matmul("dw2_mm", "TN", [(sv["A2"], dM)], R=DFF, C=D, tr=tw, tc=tw, tk=TA, nk=1, out_dtypes=[F32],
                                 epi=None, comm=comm), comm, lambda o: rs.scattered(o))
    comm = rs.share() if rs else None
    (g["w1"],) = _hosted(_matmul("dw1_mm", "TN", [(sv["H2"], dPre)], R=D, C=DFF, tr=tw, tc=tw, tk=TA, nk=1, out_dtypes=[F32],
                                 epi=None, comm=comm), comm, lambda o: rs.shared(o))
    mlp = make_reduce(REDUCE_MLP, {k: g.pop(k) for k in REDUCE_MLP})
    tkf = _pick(DFF, 2048)
    comm = mlp.swap()
    (dH2,) = _hosted(_matmul("dh2_mm", "NT", [(dPre, W["w1"])], R=TA, C=D, tr=tr, tc=1024, tk=tkf, nk=DFF // tkf,
                             out_dtypes=[F32], epi=ident, comm=comm), comm, lambda o: mlp.swapped(o))
    dX2, dn2, dsh2, dsc2 = _normmod_bwd(cfg, sv["X2"], dH2, dX3, sm["norm2_g"], mods, 3, 4, "normmod2_bwd")

    dO, dgate1 = _resgate_bwd(cfg, dX2, sv["O"], mods, 2, "resgate1_bwd")
    (dY,) = _matmul("dy_mm", "NT", [(dO, W["w_out"])], R=TA, C=D, tr=tr, tc=1024, tk=D, nk=1, out_dtypes=[F32], epi=ident)
    (g["w_out"],) = _matmul("dwout_mm", "TN", [(sv["Y"], dO)], R=D, C=D, tr=tw, tc=tw, tk=TA, nk=1,
                            out_dtypes=[F32], epi=None)
    dLr, dz = _gate_bwd(cfg, 0, dY, sv["L"][0], sv["Z"], None)
    dLp, dz = _gate_bwd(cfg, 1, dY, sv["L"][1], sv["Z"], dz)
    dLs, dz = _gate_bwd(cfg, 2, dY, sv["L"][2], sv["Z"], dz)

    (dret,) = _matmul("dret_mm", "NT", [(dLr, W["w_br"])], R=TA, C=RW, tr=tr, tc=RW, tk=D, nk=1, out_dtypes=[F32], epi=ident)
    (dpool,) = _matmul("dpool_mm", "NT", [(dLp, W["w_bp"])], R=TA, C=PW, tr=tr, tc=PW, tk=D, nk=1, out_dtypes=[F32], epi=ident)
    (dsg,) = _matmul("dsg_mm", "NT", [(dLs, W["w_bs"])], R=TA, C=SW, tr=tr, tc=SW, tk=D, nk=1, out_dtypes=[F32], epi=ident)
    (g["w_br"],) = _matmul("dwbr_mm", "TN", [(sv["ret"], dLr)], R=RW, C=D, tr=tw, tc=tw, tk=TA, nk=1, out_dtypes=[F32],
                           epi=None)
    (g["w_bp"],) = _matmul("dwbp_mm", "TN", [(sv["pool"], dLp)], R=PW, C=D, tr=tw, tc=tw, tk=TA, nk=1, out_dtypes=[F32],
                           epi=None)
    (g["w_bs"],) = _matmul("dwbs_mm", "TN", [(sv["sg"], dLs)], R=SW, C=D, tr=tw, tc=tw, tk=TA, nk=1, out_dtypes=[F32],
                           epi=None)
    dOr, dz, dretng = _retout_bwd(cfg, sv["o2"], sv["Z"], sm["ret_norm_g"], dret, dz)
    dqkv2, dlam = _ret_bwd(cfg, sv["Z"], tabs, decs, sv["states"], dOr)
    dz = _rope_bwd(cfg, dqkv2, tabs, dz)
    dz, dpw, dps = _pool_bwd(cfg, sv["Z"], consts, sm["pool_w"], sm["pool_scale"], dpool, dz)
    dz, dz_sv, dsw, dsb, dsng = _sg_bwd(cfg, sv["Z"], sm["sg_norm_g"], sm["sg_w"], sm["sg_bb"], dsg, dz)
    dz = _copy_cols(cfg, dz_sv, dz, cfg.o_sv)

    tkz = _pick(NIN, 2944)
    comm = mlp.scatter()
    (dH1,) = _hosted(_matmul("dh1_mm", "NT", [(dz, W["w_in"])], R=TA, C=D, tr=tr, tc=1024, tk=tkz, nk=NIN // tkz,
                             out_dtypes=[F32], epi=ident, comm=comm), comm, lambda o: mlp.scattered(o))
    comm = mlp.share()
    (g["w_in"],) = _hosted(_matmul("dwin_mm", "TN", [(sv["H1"], dz)], R=D, C=NIN, tr=tw, tc=tw, tk=TA, nk=1,
                                   out_dtypes=[F32], epi=None, comm=comm), comm, lambda o: mlp.shared(o))
    dX, dn1, dsh1, dsc1 = _normmod_bwd(cfg, sv["X"], dH1, dX2, sm["norm1_g"], mods, 0, 1, "normmod1_bwd")
    dmods = jnp.concatenate([dsh1, dsc1, dgate1, dsh2, dsc2, dgate2], axis=1)
    small = dict(norm1_g=dn1, norm2_g=dn2, ret_norm_g=dretng, pool_w=dpw, pool_scale=dps, sg_norm_g=dsng, sg_w=dsw,
                 sg_b=dsb[:, :, 0], dlam=dlam)
    return dX, g, mlp.result, small, dmods


def _tables(cfg):
    nf = CH // 4
    inv = ROPE_THETA ** (-jnp.arange(nf, dtype=F32) / nf)
    tok = jnp.arange(cfg.T)
    ar = (tok // GRID_W).astype(F32)[:, None] * inv[None]
    ac = (tok % GRID_W).astype(F32)[:, None] * inv[None]
    cos = jnp.concatenate([jnp.cos(ar), jnp.cos(ar), jnp.cos(ac), jnp.cos(ac)], axis=1)
    sin = jnp.concatenate([-jnp.sin(ar), jnp.sin(ar), -jnp.sin(ac), jnp.sin(ac)], axis=1)
    cos = jnp.concatenate([jnp.ones((cfg.CTX, CH), F32), cos], axis=0)
    sin = jnp.concatenate([jnp.zeros((cfg.CTX, CH), F32), sin], axis=0)
    idx = np.broadcast_to(np.arange(CH, dtype=np.float32)[:, None], (CH, CH))
    lamw = np.stack([np.stack([idx, idx + 1.0, -idx, CH - 1.0 - idx]), np.stack([-idx, CH - idx, idx, idx])])
    return dict(cos=cos, sin=sin, lamw=jnp.asarray(lamw, F32))


def _decays(cfg, logit):
    H, RW = cfg.H, cfg.RW
    lam = jax.nn.log_sigmoid(logit.astype(F32))
    idx = jnp.arange(CH, dtype=F32)
    dist = idx[:, None] - idx[None, :]
    d0 = jnp.where(dist >= 0, jnp.exp(lam[0][:, None, None] * jnp.maximum(dist, 0.0)), 0.0)
    d1 = jnp.where(dist <= 0, jnp.exp(lam[1][:, None, None] * jnp.maximum(-dist, 0.0)), 0.0)
    lanes = lambda a: jnp.repeat(a.T, CH, axis=1)
    qdec = jnp.stack([lanes(jnp.exp(lam[0][:, None] * (idx + 1.0)[None])), lanes(jnp.exp(lam[1][:, None] * (CH - idx)[None]))])
    kdec = jnp.stack([lanes(jnp.exp(lam[0][:, None] * (CH - 1.0 - idx)[None])), lanes(jnp.exp(lam[1][:, None] * idx[None]))])
    cdec = jnp.repeat(jnp.exp(lam * CH), CH, axis=1)[:, None, :]
    return dict(dmat=jnp.stack([d0, d1]), qdec=qdec, kdec=kdec, cdec=cdec)


def _small_of_layer(small_w, l):
    sm = {k: v[l] for k, v in small_w.items()}
    sm["norm1_g"] = sm["norm1_g"][None]
    sm["norm2_g"] = sm["norm2_g"][None]
    sm["ret_norm_g"] = sm["ret_norm_g"][None]
    sm["pool_scale"] = sm["pool_scale"][None]
    sm["sg_norm_g"] = sm["sg_norm_g"][None]
    sm["sg_bb"] = jnp.broadcast_to(sm["sg_b"][:, :, None], sm["sg_b"].shape + (CH,))
    return sm


def _local_fwd_bwd(cfg, X0, target, mods, gathers, make_reduce, small_w, final_g):
    depth = len(mods)
    tabs = _tables(cfg)
    consts = _pool_consts(cfg.CTX)
    X, saved, Ws, sms, decs = X0, [], [], [], []
    gathers[0].run()
    for l in range(depth):
        Ws.append(gathers[l].weights())
        sms.append(_small_of_layer(small_w, l))
        decs.append(_decays(cfg, small_w["ret_decay_logit"][l]))
        X, sv = _layer_fwd(cfg, X, mods[l], Ws[l], sms[l], tabs, decs[l], consts, gathers[l + 1] if l + 1 < depth else None)
        saved.append(sv)
    dX, loss_acc, dfinal = _final(cfg, X, final_g[None], target)
    shard_g, small, dmods = [None] * depth, [None] * depth, [None] * depth
    pending = None
    for l in reversed(range(depth)):
        dX, mix, shard_g[l], small[l], dmods[l] = _layer_bwd(cfg, dX, saved[l], mods[l], Ws[l], sms[l], tabs, decs[l], consts,
                                                             pending, make_reduce)
        if pending is not None:
            shard_g[l + 1].update(pending.result)
        pending = make_reduce(REDUCE_MIX, mix)
        lam_grad = jnp.sum(small[l].pop("dlam").reshape(2, cfg.H, CH), axis=-1)
        small[l]["ret_decay_logit"] = lam_grad * jax.nn.sigmoid(-small_w["ret_decay_logit"][l].astype(F32))
    shard_g[0].update(pending.run())
    return loss_acc[0, 0], dX, shard_g, small, dmods, dfinal


def _me():
    return lax.axis_index("x"), lax.axis_index("y"), lax.axis_index("c")


def _other_chips(x, y):
    return [(1 - x, y), (x, 1 - y), (1 - x, 1 - y)]


def _rcopy(src, dst, send_sem, recv_sem, dev):
    return pltpu.make_async_remote_copy(src_ref=src, dst_ref=dst, send_sem=send_sem, recv_sem=recv_sem,
                                        device_id=dev, device_id_type=MESH)


def _half(ref, axis, c):
    k, n = ref.shape
    if axis == 1:
        return ref.at[pl.ds(c * (k // 2), k // 2), :]
    return ref.at[:, pl.ds(c * (n // 2), n // 2)]


def _chip_part(ref, axis, j):
    k, n = ref.shape
    if axis == 1:
        return ref.at[:, pl.ds(j * (n // 4), n // 4)]
    return ref.at[pl.ds(j * (k // 4), k // 4), :]


def _piece(ref, axis, j, c):
    k, n = ref.shape
    if axis == 1:
        return ref.at[pl.ds(c * (k // 2), k // 2), pl.ds(j * (n // 4), n // 4)]
    return ref.at[pl.ds(j * (k // 4), k // 4), pl.ds(c * (n // 2), n // 2)]


def _gather_weights(fulls, axes):
    n = len(fulls)

    def body(*refs):
        outs = refs[n:2 * n]
        send, recv = refs[2 * n:]
        x, y, c = _me()
        j = 2 * x + y
        sib = (x, y, 1 - c)
        chips = _other_chips(x, y)
        first = []
        for t in range(n):
            for k, chip in enumerate(chips):
                own = _piece(outs[t], axes[t], j, c)
                first.append(_rcopy(own, own, send.at[6 * t + k], recv.at[6 * t + k], (*chip, c)))
                first[-1].start()
        passed = []
        for t in range(n):
            for k, chip in enumerate(chips):
                landed = _piece(outs[t], axes[t], 2 * chip[0] + chip[1], c)
                _rcopy(landed, landed, send.at[6 * t + k], recv.at[6 * t + k], sib).wait_recv()
                passed.append(_rcopy(landed, landed, send.at[6 * t + 3 + k], recv.at[6 * t + 3 + k], sib))
                passed[-1].start()
        for t in range(n):
            for k, chip in enumerate(chips):
                theirs = _piece(outs[t], axes[t], 2 * chip[0] + chip[1], 1 - c)
                _rcopy(theirs, theirs, send.at[6 * t + 3 + k], recv.at[6 * t + 3 + k], sib).wait_recv()
        for cp in first + passed:
            cp.wait_send()

    return pl.pallas_call(
        body, name="gather_weights",
        in_specs=[ANY] * n, out_specs=[ANY] * n,
        out_shape=[jax.ShapeDtypeStruct(f.shape, f.dtype) for f in fulls],
        input_output_aliases={t: t for t in range(n)},
        scratch_shapes=[pltpu.SemaphoreType.DMA((6 * n,)), pltpu.SemaphoreType.DMA((6 * n,))],
    )(*fulls)


class _Comm:
    def __init__(self, ins, outs, aliases, nsem, start, finish):
        self.ins, self.outs, self.aliases, self.nsem, self.start, self.finish = ins, outs, aliases, nsem, start, finish


def _run_comm(name, comm):
    n_in = len(comm.ins)

    def body(*refs):
        ins, outs = refs[:n_in], refs[n_in:n_in + len(comm.outs)]
        send, recv = refs[n_in + len(comm.outs):]
        comm.start(ins, outs, send, recv)
        comm.finish(ins, outs, send, recv)

    return pl.pallas_call(
        body, name=name, in_specs=[ANY] * n_in, out_specs=[ANY] * len(comm.outs), out_shape=list(comm.outs),
        input_output_aliases=dict(comm.aliases),
        scratch_shapes=[pltpu.SemaphoreType.DMA((comm.nsem,)), pltpu.SemaphoreType.DMA((comm.nsem,))],
    )(*comm.ins)


def _like(arrs):
    return [jax.ShapeDtypeStruct(a.shape, a.dtype) for a in arrs]


def _gather_ici_comm(fulls, axes):
    n = len(fulls)

    def copies(outs, send, recv):
        x, y, c = _me()
        own_j = 2 * x + y
        res = []
        for t in range(n):
            for k, chip in enumerate(_other_chips(x, y)):
                own = _piece(outs[t], axes[t], own_j, c)
                landed = _piece(outs[t], axes[t], 2 * chip[0] + chip[1], c)
                res.append((_rcopy(own, own, send.at[3 * t + k], recv.at[3 * t + k], (*chip, c)),
                            _rcopy(landed, landed, send.at[3 * t + k], recv.at[3 * t + k], (*chip, c))))
        return res

    def start(ins, outs, send, recv):
        for out, _ in copies(outs, send, recv):
            out.start()

    def finish(ins, outs, send, recv):
        for out, arrival in copies(outs, send, recv):
            out.wait_send()
            arrival.wait_recv()

    return _Comm(fulls, _like(fulls), {t: t for t in range(n)}, 3 * n, start, finish)


def _gather_d2d_comm(fulls, axes):
    n = len(fulls)

    def copies(outs, send, recv):
        x, y, c = _me()
        res = []
        for t in range(n):
            for k, chip in enumerate(_other_chips(x, y)):
                landed = _piece(outs[t], axes[t], 2 * chip[0] + chip[1], c)
                theirs = _piece(outs[t], axes[t], 2 * chip[0] + chip[1], 1 - c)
                res.append((_rcopy(landed, landed, send.at[3 * t + k], recv.at[3 * t + k], (x, y, 1 - c)),
                            _rcopy(theirs, theirs, send.at[3 * t + k], recv.at[3 * t + k], (x, y, 1 - c))))
        return res

    def start(ins, outs, send, recv):
        for out, _ in copies(outs, send, recv):
            out.start()

    def finish(ins, outs, send, recv):
        for out, arrival in copies(outs, send, recv):
            out.wait_send()
            arrival.wait_recv()

    return _Comm(fulls, _like(fulls), {t: t for t in range(n)}, 3 * n, start, finish)


def _swap_comm(grads, axes):
    n = len(grads)
    half_shapes = [(g.shape[0] // 2, g.shape[1]) if a == 1 else (g.shape[0], g.shape[1] // 2) for g, a in zip(grads, axes)]

    def copies(ins, outs, send, recv):
        x, y, c = _me()
        return [_rcopy(_half(ins[t], axes[t], 1 - c), outs[t], send.at[t], recv.at[t], (x, y, 1 - c)) for t in range(n)]

    def start(ins, outs, send, recv):
        for cp in copies(ins, outs, send, recv):
            cp.start()

    def finish(ins, outs, send, recv):
        for cp in copies(ins, outs, send, recv):
            cp.wait()

    return _Comm(grads, [jax.ShapeDtypeStruct(s, F32) for s in half_shapes], {}, n, start, finish)


def _scatter_comm(parts, axes):
    n = len(parts)
    q_shapes = [(p.shape[0], p.shape[1] // 4) if a == 1 else (p.shape[0] // 4, p.shape[1]) for p, a in zip(parts, axes)]

    def copies(ins, outs, send, recv):
        x, y, c = _me()
        res = []
        for t in range(n):
            for k, chip in enumerate(_other_chips(x, y)):
                res.append(_rcopy(_chip_part(ins[t], axes[t], 2 * chip[0] + chip[1]), outs[3 * t + k], send.at[3 * t + k],
                                  recv.at[3 * t + k], (*chip, c)))
        return res

    def start(ins, outs, send, recv):
        for cp in copies(ins, outs, send, recv):
            cp.start()

    def finish(ins, outs, send, recv):
        for cp in copies(ins, outs, send, recv):
            cp.wait()

    return _Comm(parts, [jax.ShapeDtypeStruct(s, p.dtype) for s, p in zip(q_shapes, parts) for _ in range(3)], {}, 3 * n,
                 start, finish)


def _share_comm(shards, axes):
    n = len(shards)

    def copies(outs, send, recv):
        x, y, c = _me()
        res = []
        for t in range(n):
            mine, theirs = _half(outs[t], axes[t], c), _half(outs[t], axes[t], 1 - c)
            res.append((_rcopy(mine, mine, send.at[t], recv.at[t], (x, y, 1 - c)),
                        _rcopy(theirs, theirs, send.at[t], recv.at[t], (x, y, 1 - c))))
        return res

    def start(ins, outs, send, recv):
        for out, _ in copies(outs, send, recv):
            out.start()

    def finish(ins, outs, send, recv):
        for out, arrival in copies(outs, send, recv):
            out.wait_send()
            arrival.wait_recv()

    return _Comm(shards, _like(shards), {t: t for t in range(n)}, n, start, finish)


def _cast_into_full(w, l, axis, sc):
    _, k, n = w.shape
    tr = _pick(k, 256, 16)
    if axis == 1:
        full, out_spec = (k, 4 * n), pl.BlockSpec((tr, n), lambda i, s: (i, s[1]))
    else:
        full, out_spec = (4 * k, n), pl.BlockSpec((tr, n), lambda i, s: (s[1] * (k // tr) + i, 0))

    def body(s_ref, w_ref, o_ref):
        o_ref[...] = w_ref[...].astype(BF16)

    return pl.pallas_call(
        body, name="cast_into_full",
        grid_spec=pltpu.PrefetchScalarGridSpec(
            num_scalar_prefetch=1, grid=(k // tr,),
            in_specs=[pl.BlockSpec((None, tr, n), lambda i, s: (l, i, 0))], out_specs=out_spec),
        out_shape=jax.ShapeDtypeStruct(full, BF16), compiler_params=_cparams(("arbitrary",)),
    )(sc, w)


def _rs_add2(g, got, axis, sc):
    k, n = g.shape
    hk, hn = (k // 2, n) if axis == 1 else (k, n // 2)
    tr = _pick(hk, max(16, (1 << 20) // hn), 16)
    if axis == 1:
        g_spec = pl.BlockSpec((tr, hn), lambda i, s: (s[0] * (hk // tr) + i, 0))
    else:
        g_spec = pl.BlockSpec((tr, hn), lambda i, s: (i, s[0]))
    blk = pl.BlockSpec((tr, hn), lambda i, s: (i, 0))

    def body(s_ref, a_ref, b_ref, o_ref):
        o_ref[...] = (a_ref[...] + b_ref[...]).astype(o_ref.dtype)

    return pl.pallas_call(
        body, name="rs_add2",
        grid_spec=pltpu.PrefetchScalarGridSpec(num_scalar_prefetch=1, grid=(hk // tr,), in_specs=[g_spec, blk], out_specs=blk),
        out_shape=jax.ShapeDtypeStruct((hk, hn), BF16), compiler_params=_cparams(("arbitrary",)),
    )(sc, g, got)


def _rs_add4(part, got3, axis, sc):
    k, n = part.shape
    qk, qn = (k, n // 4) if axis == 1 else (k // 4, n)
    tr = _pick(qk, max(16, (1 << 20) // qn), 16)
    if axis == 1:
        p_spec = pl.BlockSpec((tr, qn), lambda i, s: (i, s[1]))
        shard, o_spec = (2 * qk, qn), pl.BlockSpec((tr, qn), lambda i, s: (s[0] * (qk // tr) + i, 0))
    else:
        p_spec = pl.BlockSpec((tr, qn), lambda i, s: (s[1] * (qk // tr) + i, 0))
        shard, o_spec = (qk, 2 * qn), pl.BlockSpec((tr, qn), lambda i, s: (i, s[0]))
    blk = pl.BlockSpec((tr, qn), lambda i, s: (i, 0))

    def body(s_ref, p_ref, a_ref, b_ref, c_ref, o_ref):
        o_ref[...] = ((p_ref[...].astype(F32) + a_ref[...].astype(F32)) + b_ref[...].astype(F32)) + c_ref[...].astype(F32)

    return pl.pallas_call(
        body, name="rs_add4",
        grid_spec=pltpu.PrefetchScalarGridSpec(num_scalar_prefetch=1, grid=(qk // tr,), in_specs=[p_spec, blk, blk, blk],
                                               out_specs=o_spec),
        out_shape=jax.ShapeDtypeStruct(shard, F32), compiler_params=_cparams(("arbitrary",)),
    )(sc, part, *got3)


def _gather_small(v):
    def body(v_ref, out_ref, send, recv, loc):
        x, y, c = _me()
        sib = (x, y, 1 - c)
        chips = _other_chips(x, y)
        slot = lambda px, py, pc: out_ref.at[4 * px + 2 * py + pc]
        mine = pltpu.make_async_copy(v_ref, slot(x, y, c), loc)
        mine.start()
        first = [_rcopy(v_ref, slot(x, y, c), send.at[0], recv.at[0], sib)]
        first += [_rcopy(v_ref, slot(x, y, c), send.at[1 + k], recv.at[1 + k], (*chip, c)) for k, chip in enumerate(chips)]
        for cp in first:
            cp.start()
        passed = []
        for k, chip in enumerate(chips):
            landed = slot(*chip, c)
            _rcopy(landed, landed, send.at[1 + k], recv.at[1 + k], sib).wait_recv()
            passed.append(_rcopy(landed, landed, send.at[4 + k], recv.at[4 + k], sib))
            passed[-1].start()
        theirs = slot(x, y, 1 - c)
        _rcopy(theirs, theirs, send.at[0], recv.at[0], sib).wait_recv()
        for k, chip in enumerate(chips):
            theirs = slot(*chip, 1 - c)
            _rcopy(theirs, theirs, send.at[4 + k], recv.at[4 + k], sib).wait_recv()
        for cp in first + passed:
            cp.wait_send()
        mine.wait()

    return pl.pallas_call(
        body, name="gather_small",
        in_specs=[ANY], out_specs=ANY,
        out_shape=jax.ShapeDtypeStruct((8,) + v.shape, v.dtype),
        scratch_shapes=[pltpu.SemaphoreType.DMA((7,)), pltpu.SemaphoreType.DMA((7,)), pltpu.SemaphoreType.DMA],
    )(v)


class _WeightGather:
    def __init__(self, params, l, sc):
        self.bufs = {k: _cast_into_full(params[k], l, SHARD_AXIS[k], sc) for k in BIG}

    def run(self):
        self.update(BIG, _gather_weights([self.bufs[k] for k in BIG], [SHARD_AXIS[k] for k in BIG]))

    def ici(self, names):
        return _gather_ici_comm([self.bufs[k] for k in names], [SHARD_AXIS[k] for k in names])

    def d2d(self, names):
        return _gather_d2d_comm([self.bufs[k] for k in names], [SHARD_AXIS[k] for k in names])

    def update(self, names, outs):
        self.bufs.update(zip(names, outs))

    def weights(self):
        return self.bufs


class _GradReduce:
    def __init__(self, names, grads, sc):
        self.names, self.grads, self.sc, self.parts, self.theirs = names, grads, sc, None, None
        self.axes = [SHARD_AXIS[k] for k in names]

    def swap(self):
        return _swap_comm([self.grads[k] for k in self.names], self.axes)

    def swapped(self, got):
        self.parts = [_rs_add2(self.grads[k], s, a, self.sc) for k, s, a in zip(self.names, got, self.axes)]

    def scatter(self):
        return _scatter_comm(self.parts, self.axes)

    def scattered(self, outs):
        self.theirs = [outs[3 * q:3 * q + 3] for q in range(len(self.names))]

    def share(self):
        halves = [_rs_add4(p, th, a, self.sc) for p, th, a in zip(self.parts, self.theirs, self.axes)]
        return _share_comm(halves, self.axes)

    def shared(self, outs):
        self.result = dict(zip(self.names, outs))

    def run(self):
        self.swapped(_run_comm("rs_swap", self.swap()))
        self.scattered(_run_comm("rs_scatter", self.scatter()))
        self.shared(_run_comm("rs_share", self.share()))
        return self.result


def _adam_math(w, g, m, v):
    m = ADAM_B1 * m + (1.0 - ADAM_B1) * g
    v = ADAM_B2 * v + (1.0 - ADAM_B2) * (g * g)
    m_hat = m / (1.0 - ADAM_B1 ** ADAM_STEP)
    v_hat = v / (1.0 - ADAM_B2 ** ADAM_STEP)
    delta = -ADAM_LR * (m_hat / (jnp.sqrt(v_hat) + ADAM_EPS) + ADAM_WD * w)
    return delta, m, v


def _adam_layer(w, m, v, l, g, prev):
    L, k, n = w.shape
    tr = _pick(k, 128, 8)
    blk = pl.BlockSpec((None, tr, n), lambda i: (l, i, 0))

    def body(*refs):
        w_ref, m_ref, v_ref, g_ref = refs[:4]
        go, do, mo, vo = refs[-4:]
        gv = g_ref[...]
        d, m2, v2 = _adam_math(w_ref[...], gv, m_ref[...], v_ref[...])
        go[...] = gv
        do[...] = d
        mo[...] = m2
        vo[...] = v2

    args = [w, m, v, g]
    in_specs = [blk, blk, blk, pl.BlockSpec((tr, n), lambda i: (i, 0))]
    aliases = {}
    if prev is not None:
        for q, p in enumerate(prev):
            aliases[len(args)] = q
            in_specs.append(ANY)
            args.append(p)
    return pl.pallas_call(
        body, name="adam_layer", grid=(k // tr,),
        in_specs=in_specs, out_specs=[blk] * 4, out_shape=[jax.ShapeDtypeStruct((L, k, n), F32)] * 4,
        input_output_aliases=aliases, compiler_params=_cparams(("arbitrary",)),
    )(*args)


def _adam_flat(w, g, m, v):
    r = w.shape[0]
    tr = _pick(r, 512, 8)
    fn = lambda i, j, w, g, m, v: _adam_math(w, g, m, v)
    return _rowcall("adam_flat", fn, nrows=r, tr=tr, ins=[(a, "rows", 128, 0) for a in (w, g, m, v)],
                    outs=[(F32, 128, 128, 0, None)] * 3)


def _sum8(gathered):
    _, r, _ = gathered.shape
    tr = _pick(r, 512, 8)

    def body(g_ref, o_ref):
        acc = g_ref[0]
        for d in range(1, 8):
            acc = acc + g_ref[d]
        o_ref[...] = acc

    return pl.pallas_call(
        body, name="sum8", grid=(r // tr,),
        in_specs=[pl.BlockSpec((8, tr, 128), lambda i: (0, i, 0))], out_specs=pl.BlockSpec((tr, 128), lambda i: (i, 0)),
        out_shape=jax.ShapeDtypeStruct((r, 128), F32), compiler_params=_cparams(("arbitrary",)),
    )(gathered)


def _hdot(a, b, form="NN"):
    return _dot(a.astype(BF16), b.astype(BF16), form)


def _ada_fwd(s16, w_ada, l):
    _, d, ns = w_ada.shape
    tc = _pick(ns, 512)

    def body(s_ref, w_ref, o_ref):
        o_ref[...] = _hdot(s_ref[...], w_ref[...])

    return pl.pallas_call(
        body, name="ada_fwd", grid=(ns // tc,),
        in_specs=[pl.BlockSpec((16, d), lambda j: (0, 0)), pl.BlockSpec((None, d, tc), lambda j: (l, 0, j))],
        out_specs=pl.BlockSpec((16, tc), lambda j: (0, j)),
        out_shape=jax.ShapeDtypeStruct((16, ns), F32), compiler_params=_cparams(("arbitrary",)),
    )(s16, w_ada)


def _ada_bwd(s16t, dm, w_ada, l):
    _, d, ns = w_ada.shape
    tc = _pick(ns, 512)

    def body(st_ref, dm_ref, w_ref, dw_ref, ds_ref):
        j = pl.program_id(0)
        dw_ref[...] = _hdot(st_ref[...], dm_ref[...])
        part = _hdot(dm_ref[...], w_ref[...], "NT")

        @pl.when(j == 0)
        def _():
            ds_ref[...] = part

        @pl.when(j > 0)
        def _():
            ds_ref[...] += part

    return pl.pallas_call(
        body, name="ada_bwd", grid=(ns // tc,),
        in_specs=[pl.BlockSpec((d, 16), lambda j: (0, 0)), pl.BlockSpec((16, tc), lambda j: (0, j)),
                  pl.BlockSpec((None, d, tc), lambda j: (l, 0, j))],
        out_specs=[pl.BlockSpec((d, tc), lambda j: (0, j)), pl.BlockSpec((16, d), lambda j: (0, 0))],
        out_shape=[jax.ShapeDtypeStruct((d, ns), F32), jax.ShapeDtypeStruct((16, d), F32)],
        compiler_params=_cparams(("arbitrary",)),
    )(s16t, dm, w_ada)


def _pack(arrs):
    flat = jnp.concatenate([a.reshape(-1).astype(F32) for a in arrs])
    pad = (-flat.shape[0]) % 1024
    return jnp.pad(flat, (0, pad)).reshape(-1, 128)


def _unpack(flat2d, shapes):
    flat = flat2d.reshape(-1)
    out, pos = [], 0
    for s in shapes:
        size = int(np.prod(s))
        out.append(flat[pos:pos + size].reshape(s))
        pos += size
    return out


SMALL = ("norm1_g", "norm2_g", "ret_decay_logit", "ret_norm_g", "pool_w", "pool_scale", "sg_norm_g", "sg_w", "sg_b")


def kernel(x, c, ctx, c_ctx, w_ada, b_ada, norm1_g, w_in, ret_decay_logit, ret_norm_g, pool_w, pool_scale, sg_norm_g, sg_w, sg_b, w_br, w_bp, w_bs, w_out, norm2_g, w1, w2, final_norm_g, loss_target, m_c_ctx, m_w_ada, m_b_ada, m_norm1_g, m_w_in, m_ret_decay_logit, m_ret_norm_g, m_pool_w, m_pool_scale, m_sg_norm_g, m_sg_w, m_sg_b, m_w_br, m_w_bp, m_w_bs, m_w_out, m_norm2_g, m_w1, m_w2, m_final_norm_g, v_c_ctx, v_w_ada, v_b_ada, v_norm1_g, v_w_in, v_ret_decay_logit, v_ret_norm_g, v_pool_w, v_pool_scale, v_sg_norm_g, v_sg_w, v_sg_b, v_w_br, v_w_bp, v_w_bs, v_w_out, v_norm2_g, v_w1, v_w2, v_final_norm_g):
    P = dict(c_ctx=c_ctx, w_ada=w_ada, b_ada=b_ada, norm1_g=norm1_g, w_in=w_in, ret_decay_logit=ret_decay_logit,
             ret_norm_g=ret_norm_g, pool_w=pool_w, pool_scale=pool_scale, sg_norm_g=sg_norm_g, sg_w=sg_w, sg_b=sg_b, w_br=w_br,
             w_bp=w_bp, w_bs=w_bs, w_out=w_out, norm2_g=norm2_g, w1=w1, w2=w2, final_norm_g=final_norm_g)
    Mo = dict(c_ctx=m_c_ctx, w_ada=m_w_ada, b_ada=m_b_ada, norm1_g=m_norm1_g, w_in=m_w_in, ret_decay_logit=m_ret_decay_logit,
              ret_norm_g=m_ret_norm_g, pool_w=m_pool_w, pool_scale=m_pool_scale, sg_norm_g=m_sg_norm_g, sg_w=m_sg_w, sg_b=m_sg_b,
              w_br=m_w_br, w_bp=m_w_bp, w_bs=m_w_bs, w_out=m_w_out, norm2_g=m_norm2_g, w1=m_w1, w2=m_w2,
              final_norm_g=m_final_norm_g)
    Vo = dict(c_ctx=v_c_ctx, w_ada=v_w_ada, b_ada=v_b_ada, norm1_g=v_norm1_g, w_in=v_w_in, ret_decay_logit=v_ret_decay_logit,
              ret_norm_g=v_ret_norm_g, pool_w=v_pool_w, pool_scale=v_pool_scale, sg_norm_g=v_sg_norm_g, sg_w=v_sg_w, sg_b=v_sg_b,
              w_br=v_w_br, w_bp=v_w_bp, w_bs=v_w_bs, w_out=v_w_out, norm2_g=v_norm2_g, w1=v_w1, w2=v_w2,
              final_norm_g=v_final_norm_g)
    names = ("c_ctx", "w_ada", "b_ada", "norm1_g", "w_in", "ret_decay_logit", "ret_norm_g", "pool_w", "pool_scale", "sg_norm_g",
             "sg_w", "sg_b", "w_br", "w_bp", "w_bs", "w_out", "norm2_g", "w1", "w2", "final_norm_g")
    L, D = w_in.shape[0], x.shape[-1]
    T, CTX = x.shape[1], ctx.shape[1]
    cfg = _Cfg(D, T, CTX, 4 * w1.shape[2])
    ns_ada = w_ada.shape[2]
    mx, my, mc = _me()
    dev = 4 * mx + 2 * my + mc
    chip = 2 * mx + my

    silu_cc = jax.nn.silu(c_ctx)
    silu_all = _gather_small(jax.nn.silu(c).reshape(-1, 128)).reshape(8, D)
    s16 = jnp.concatenate([silu_cc[None], silu_all, jnp.zeros((7, D), F32)], axis=0)
    proj = jnp.stack([_ada_fwd(s16, w_ada, l) for l in range(L)])
    proj_all = _gather_small(proj.reshape(-1, 128)).reshape(8, L, 16, ns_ada)
    mods_full = jnp.concatenate([proj_all[2 * j] for j in range(4)], axis=-1) + b_ada[:, None, :]
    mods = [jnp.concatenate([mods_full[l, 0:1], lax.dynamic_slice_in_dim(mods_full[l], 1 + dev, 1, axis=0)], axis=0)
            for l in range(L)]

    sc = jnp.stack([mc, chip]).astype(jnp.int32)
    gathers = [_WeightGather(P, l, sc) for l in range(L)]
    X0 = jnp.concatenate([ctx[0], x[0]], axis=0)
    small_w = {k: P[k] for k in SMALL}
    loss_part, dX, shard_g, small, dmods, dfinal = _local_fwd_bwd(
        cfg, X0, loss_target[0], mods, gathers, lambda names, grads: _GradReduce(names, grads, sc), small_w, final_norm_g)
    loss = lax.psum(loss_part, ("x", "y", "c"))
    grad_x = dX[CTX:][None]

    outs = {k: None for k in BIG}
    for l in range(L):
        for k in BIG:
            outs[k] = _adam_layer(P[k], Mo[k], Vo[k], l, shard_g[l][k], outs[k])

    per_layer = [[small[l][k] for k in SMALL] + [dmods[l][1], dmods[l][0]] for l in range(L)]
    payload = _pack([a for lay in per_layer for a in lay] + [dfinal])
    gathered = _gather_small(payload)
    total = _sum8(gathered)
    shapes = [P[k].shape[1:] for k in SMALL] + [(6 * D,), (6 * D,)]
    tot = _unpack(total, shapes * L + [(D,)])
    per = len(shapes)
    g_small = {k: jnp.stack([tot[l * per + q] for l in range(L)]) for q, k in enumerate(SMALL)}
    dmx_sum = jnp.stack([tot[l * per + per - 2] for l in range(L)])
    dmc_sum = jnp.stack([tot[l * per + per - 1] for l in range(L)])
    g_small["b_ada"] = dmx_sum + dmc_sum
    g_small["final_norm_g"] = tot[-1]
    offs = np.cumsum([0] + [int(np.prod(s)) for s in shapes])
    lay_size = int(offs[-1])
    gflat = gathered.reshape(8, -1)
    s16t = s16.T
    ada_out, ds_part = None, jnp.zeros((16, D), F32)
    for l in range(L):
        dmx_all = gflat[:, l * lay_size + int(offs[per - 2]):l * lay_size + int(offs[per - 1])]
        dm_full = jnp.concatenate([dmc_sum[l][None], dmx_all, jnp.zeros((7, 6 * D), F32)], axis=0)
        dm = lax.dynamic_slice_in_dim(dm_full, chip * ns_ada, ns_ada, axis=1)
        dw, ds = _ada_bwd(s16t, dm, w_ada, l)
        ds_part = ds_part + ds
        ada_out = _adam_layer(w_ada, Mo["w_ada"], Vo["w_ada"], l, dw, ada_out)
    outs["w_ada"] = ada_out
    ds_all = _gather_small(ds_part[0].reshape(-1, 128)).reshape(8, D)
    d_silu_cc = ds_all[0] + ds_all[2] + ds_all[4] + ds_all[6]
    g_small["c_ctx"] = jax.vjp(jax.nn.silu, c_ctx)[1](d_silu_cc)[0]

    small_names = [k for k in names if k not in BIG and k != "w_ada"]
    sm_shapes = [P[k].shape for k in small_names]
    res = _adam_flat(_pack([P[k] for k in small_names]), _pack([g_small[k] for k in small_names]),
                     _pack([Mo[k] for k in small_names]), _pack([Vo[k] for k in small_names]))
    d_s, m_s, v_s = [_unpack(r, sm_shapes) for r in res]
    for q, k in enumerate(small_names):
        outs[k] = (g_small[k].reshape(P[k].shape), d_s[q], m_s[q], v_s[q])

    return (loss, grad_x, *[outs[k][0] for k in names], *[outs[k][1] for k in names], *[outs[k][2] for k in names],
            *[outs[k][3] for k in names])
```

```python
import functools

import numpy as np
import jax
import jax.numpy as jnp
from jax import lax
from jax.experimental import pallas as pl
from jax.experimental.pallas import tpu as pltpu

F32 = jnp.float32
BF16 = jnp.bfloat16
EPS = 1e-6
CH = 128
GRID_W = 64
ROPE_THETA = 10000.0
POOL_WINDOWS = (2, 4, 8, 16)
PT = 256
VMEM_LIMIT = 56 * 1024 * 1024
MESH = pl.DeviceIdType.MESH
ANY = pl.BlockSpec(memory_space=pl.ANY)

ADAM_LR = 0.001
ADAM_B1 = 0.9
ADAM_B2 = 0.999
ADAM_EPS = 1e-08
ADAM_WD = 0.01
ADAM_STEP = 10

BIG = ("w_in", "w_br", "w_bp", "w_bs", "w_out", "w1", "w2")
SHARD_AXIS = {"w_in": 1, "w_br": 1, "w_bp": 1, "w_bs": 1, "w_out": 0, "w1": 1, "w2": 0}


def _pick(dim, pref, mult=128):
    best = None
    for t in range(mult, min(dim, pref) + 1, mult):
        if dim % t == 0:
            best = t
    return dim if best is None else best


def _cparams(sem=None):
    return pltpu.CompilerParams(dimension_semantics=sem, vmem_limit_bytes=VMEM_LIMIT)


def _rows(i, tr):
    return i * tr + lax.broadcasted_iota(jnp.int32, (tr, 1), 0)


def _sel(i, tr, n_ctx, v2):
    return jnp.where(_rows(i, tr) < n_ctx, v2[0:1, :], v2[1:2, :])


def _seg_sums(i, tr, n_ctx, d):
    is_ctx = _rows(i, tr) < n_ctx
    s_c = jnp.sum(jnp.where(is_ctx, d, 0.0), axis=0, keepdims=True)
    s_x = jnp.sum(jnp.where(is_ctx, 0.0, d), axis=0, keepdims=True)
    two = lax.broadcasted_iota(jnp.int32, (2, d.shape[1]), 0)
    return jnp.where(two == 0, s_c, s_x)


def _dot(a, b, form="NN"):
    dims = {"NN": (((1,), (0,)), ((), ())), "NT": (((1,), (1,)), ((), ())), "TN": (((0,), (0,)), ((), ()))}[form]
    return lax.dot_general(a, b, dims, preferred_element_type=F32)


def _gelu(x):
    return 0.5 * x * (1.0 + jnp.tanh(0.7978845608028654 * (x + 0.044715 * x * x * x)))


def _matmul(name, form, pairs, *, R, C, tr, tc, tk, nk, out_dtypes, epi, extras=(), a_pro=None, comm=None):
    npair, nex, nout = len(pairs), len(extras), len(out_dtypes)
    in_specs, args = [], []
    for a, b in pairs:
        if nk == 1:
            ka = a.shape[0] if form == "TN" else a.shape[1]
        else:
            ka = tk
        if form == "NN":
            in_specs += [pl.BlockSpec((tr, ka), lambda i, j, k: (i, k)), pl.BlockSpec((ka, tc), lambda i, j, k: (k, j))]
        elif form == "NT":
            in_specs += [pl.BlockSpec((tr, ka), lambda i, j, k: (i, k)), pl.BlockSpec((tc, ka), lambda i, j, k: (j, k))]
        else:
            in_specs += [pl.BlockSpec((ka, tr), lambda i, j, k: (k, i)), pl.BlockSpec((ka, tc), lambda i, j, k: (k, j))]
        args += [a, b]
    for kind, arr, off in extras:
        if kind == "tile":
            in_specs.append(pl.BlockSpec((tr, tc), lambda i, j, k, off=off: (i, j + off)))
        else:
            in_specs.append(pl.BlockSpec((2, tc), lambda i, j, k, off=off: (0, j + off)))
        args.append(arr)

    direct = epi is None
    n_acc = 0 if (nk == 1 or direct) else npair
    n_main = len(args)
    ni, nj = R // tr, C // tc
    aliases = {}
    out_specs = [pl.BlockSpec((tr, tc), lambda i, j, k: (i, j)) for _ in out_dtypes]
    out_shape = [jax.ShapeDtypeStruct((R, C), dt) for dt in out_dtypes]
    scratch = [pltpu.VMEM((tr, tc), F32) for _ in range(n_acc)]
    n_cin = n_cout = 0
    if comm is not None:
        n_cin, n_cout = len(comm.ins), len(comm.outs)
        in_specs = in_specs + [ANY] * n_cin
        args = args + list(comm.ins)
        out_specs = out_specs + [ANY] * n_cout
        out_shape = out_shape + list(comm.outs)
        aliases = {n_main + a: nout + b for a, b in comm.aliases.items()}
        scratch = scratch + [pltpu.SemaphoreType.DMA((comm.nsem,)), pltpu.SemaphoreType.DMA((comm.nsem,))]

    def body(*refs):
        ab = refs[:2 * npair]
        ex = refs[2 * npair:n_main]
        cin = refs[n_main:n_main + n_cin]
        outs = refs[n_main + n_cin:n_main + n_cin + nout]
        cout = refs[n_main + n_cin + nout:n_main + n_cin + nout + n_cout]
        accs = refs[n_main + n_cin + nout + n_cout:n_main + n_cin + nout + n_cout + n_acc]
        sems = refs[n_main + n_cin + nout + n_cout + n_acc:]
        i, j, k = pl.program_id(0), pl.program_id(1), pl.program_id(2)

        if comm is not None:
            @pl.when(jnp.logical_and(jnp.logical_and(i == 0, j == 0), k == 0))
            def _():
                comm.start(cin, cout, *sems)

        def products():
            res = []
            for p in range(npair):
                a = ab[2 * p][...]
                if a_pro is not None:
                    a = a_pro(a)
                res.append(_dot(a, ab[2 * p + 1][...], form))
            return res

        def finish(vals):
            res = epi(vals, [e[...] for e in ex], i)
            for o, v in zip(outs, res):
                o[...] = v.astype(o.dtype)

        if direct:
            prod = products()[0]
            if nk == 1:
                outs[0][...] = prod
            else:
                @pl.when(k == 0)
                def _():
                    outs[0][...] = prod

                @pl.when(k > 0)
                def _():
                    outs[0][...] += prod
        elif nk == 1:
            finish(products())
        else:
            prods = products()

            @pl.when(k == 0)
            def _():
                for acc, v in zip(accs, prods):
                    acc[...] = v

            @pl.when(k > 0)
            def _():
                for acc, v in zip(accs, prods):
                    acc[...] += v

            @pl.when(k == nk - 1)
            def _():
                finish([acc[...] for acc in accs])

        if comm is not None:
            @pl.when(jnp.logical_and(jnp.logical_and(i == ni - 1, j == nj - 1), k == nk - 1))
            def _():
                comm.finish(cin, cout, *sems)

    res = pl.pallas_call(
        body, name=name, grid=(ni, nj, nk),
        in_specs=in_specs, out_specs=out_specs, out_shape=out_shape, scratch_shapes=scratch,
        input_output_aliases=aliases,
        compiler_params=_cparams(("arbitrary", "arbitrary", "arbitrary")),
    )(*args)
    return res if comm is None else (res[:nout], res[nout:])


def _rowcall(name, fn, *, nrows, tr, ins, outs, accs=(), ncol=1):
    n_in, n_out, n_acc = len(ins), len(outs), len(accs)
    in_specs, args = [], []
    for arr, kind, w, off in ins:
        if kind == "rows":
            in_specs.append(pl.BlockSpec((tr, w), lambda i, j, off=off: (i, off + j)))
        elif kind == "full":
            in_specs.append(pl.BlockSpec(arr.shape, lambda i, j, nd=arr.ndim: (0,) * nd))
        else:
            in_specs.append(pl.BlockSpec((None,) + arr.shape[1:], lambda i, j, f=off, nd=arr.ndim: (f(i),) + (0,) * (nd - 1)))
        args.append(arr)
    aliases = {}
    out_specs, out_shape = [], []
    for o_idx, (dt, total, w, off, alias) in enumerate(outs):
        out_specs.append(pl.BlockSpec((tr, w), lambda i, j, off=off: (i, off + j)))
        out_shape.append(jax.ShapeDtypeStruct((nrows, total), dt))
        if alias is not None:
            aliases[len(args)] = o_idx
            in_specs.append(ANY)
            args.append(alias)
    n_alias = len(aliases)
    for shp in accs:
        out_specs.append(pl.BlockSpec(shp, lambda i, j, nd=len(shp): (0,) * nd))
        out_shape.append(jax.ShapeDtypeStruct(shp, F32))

    def body(*refs):
        in_refs = refs[:n_in]
        out_refs = refs[n_in + n_alias:n_in + n_alias + n_out]
        acc_refs = refs[n_in + n_alias + n_out:]
        i, j = pl.program_id(0), pl.program_id(1)
        res = fn(i, j, *[r[...] for r in in_refs])
        for o, v in zip(out_refs, res[:n_out]):
            o[...] = v.astype(o.dtype)
        first = jnp.logical_and(i == 0, j == 0)
        for acc, v in zip(acc_refs, res[n_out:]):
            @pl.when(first)
            def _(acc=acc, v=v):
                acc[...] = v

            @pl.when(jnp.logical_not(first))
            def _(acc=acc, v=v):
                acc[...] += v

    res = pl.pallas_call(
        body, name=name, grid=(nrows // tr, ncol),
        in_specs=in_specs, out_specs=out_specs, out_shape=out_shape,
        input_output_aliases=aliases,
        compiler_params=_cparams(("arbitrary", "arbitrary")),
    )(*args)
    return res


def _f_normmod(x, g, shift, scale):
    r = lax.rsqrt(jnp.mean(x * x, axis=-1, keepdims=True) + EPS)
    return (x * r * g) * (1.0 + scale) + shift


def _f_headnorm_gate(o, zg, ng):
    r = lax.rsqrt(jnp.mean(o * o, axis=-1, keepdims=True) + EPS)
    return (o * r * ng) * (zg * jax.nn.sigmoid(zg))


def _f_sgv(sv, g):
    v = _gelu(sv)
    r = lax.rsqrt(jnp.mean(v * v, axis=-1, keepdims=True) + EPS)
    return v * r * g


def _rope(t, cos, sin):
    w = t.shape[1]
    lane = lax.broadcasted_iota(jnp.int32, t.shape, 1)
    swapped = jnp.where(jnp.bitwise_and(lane, 63) < 32,pltpu.roll(t, w - 32, 1), pltpu.roll(t, 32, 1))
    return t * cos + swapped * sin


def _rope_t(d, cos, sin):
    w = d.shape[1]
    lane = lax.broadcasted_iota(jnp.int32, d.shape, 1)
    ds = d * sin
    swapped = jnp.where(jnp.bitwise_and(lane, 63) < 32,pltpu.roll(ds, w - 32, 1), pltpu.roll(ds, 32, 1))
    return d * cos + swapped


class _Cfg:
    def __init__(self, D, T, CTX, DFF):
        self.D, self.T, self.CTX, self.DFF = D, T, CTX, DFF
        self.TA = T + CTX
        self.RW = D // 2
        self.H = self.RW // CH
        self.PW = D // 4
        self.SW = D // 4
        self.NIN = 4 * self.RW + self.PW + 2 * self.SW + 3 * D
        self.NC = self.TA // CH
        self.NCC = CTX // CH
        self.k_scale = float(CH) ** -0.5
        self.tr = _pick(self.TA, 1408)
        self.tr_small = _pick(self.TA, 768)
        self.o_g = 3 * self.RW
        self.o_p = 4 * self.RW
        self.o_u = self.o_p + self.PW
        self.o_sv = self.o_u + self.SW
        self.o_gate = self.o_sv + self.SW


def _normmod_fwd(cfg, X, g, mods, i_shift, i_scale, name):
    D = cfg.D
    tr = _pick(cfg.TA, 384)

    def fn(i, j, x, g, m):
        sh = _sel(i, tr, cfg.CTX, m[:, i_shift * D:(i_shift + 1) * D])
        sc = _sel(i, tr, cfg.CTX, m[:, i_scale * D:(i_scale + 1) * D])
        return (_f_normmod(x, g, sh, sc),)

    return _rowcall(name, fn, nrows=cfg.TA, tr=tr, ins=[(X, "rows", D, 0), (g, "full", 0, 0), (mods, "full", 0, 0)],
                    outs=[(BF16, D, D, 0, None)])[0]


def _normmod_bwd(cfg, X, dH, dres, g, mods, i_shift, i_scale, name):
    D = cfg.D
    tr = _pick(cfg.TA, 384)

    def fn(i, j, x, dh, dr, g, m):
        sh = _sel(i, tr, cfg.CTX, m[:, i_shift * D:(i_shift + 1) * D])
        sc = _sel(i, tr, cfg.CTX, m[:, i_scale * D:(i_scale + 1) * D])
        _, vjp = jax.vjp(_f_normmod, x, g, sh, sc)
        dx, dg, dsh, dsc = vjp(dh)
        return dr + dx, dg, _seg_sums(i, tr, cfg.CTX, dsh), _seg_sums(i, tr, cfg.CTX, dsc)

    return _rowcall(name, fn, nrows=cfg.TA, tr=tr,
                    ins=[(X, "rows", D, 0), (dH, "rows", D, 0), (dres, "rows", D, 0), (g, "full", 0, 0), (mods, "full", 0, 0)],
                    outs=[(F32, D, D, 0, None)], accs=[(1, D), (2, D), (2, D)])


def _resgate_bwd(cfg, dX, M, mods, i_gate, name):
    D = cfg.D
    tr = _pick(cfg.TA, 384)

    def fn(i, j, dx, m, mm):
        gate = _sel(i, tr, cfg.CTX, mm[:, i_gate * D:(i_gate + 1) * D])
        return dx * gate, _seg_sums(i, tr, cfg.CTX, dx * m.astype(F32))

    return _rowcall(name, fn, nrows=cfg.TA, tr=tr, ins=[(dX, "rows", D, 0), (M, "rows", D, 0), (mods, "full", 0, 0)],
                    outs=[(BF16, D, D, 0, None)], accs=[(2, D)])


def _chunk_of(cfg, d, t):
    fwd = t
    bwd = jnp.where(t < cfg.NCC, cfg.NCC - 1 - t, cfg.NC - 1 - t + cfg.NCC)
    return jnp.where(d == 0, fwd, bwd)


def _ret_specs(cfg, cm):
    RW, H = cfg.RW, cfg.H
    return [
        pl.BlockSpec((CH, RW), lambda d, t: (cm(d, t), 0)),
        pl.BlockSpec((CH, RW), lambda d, t: (cm(d, t), 1)),
        pl.BlockSpec((CH, RW), lambda d, t: (cm(d, t), 2)),
        pl.BlockSpec((CH, CH), lambda d, t: (cm(d, t), 0)),
        pl.BlockSpec((CH, CH), lambda d, t: (cm(d, t), 0)),
        pl.BlockSpec((None, H, CH, CH), lambda d, t: (d, 0, 0, 0)),
        pl.BlockSpec((None, CH, RW), lambda d, t: (d, 0, 0)),
        pl.BlockSpec((None, CH, RW), lambda d, t: (d, 0, 0)),
        pl.BlockSpec((None, 1, RW), lambda d, t: (d, 0, 0)),
    ]


def _ret_prep(cfg, q_ref, k_ref, cos, sin, qd_ref, kd_ref, sl):
    qr = _rope(q_ref[:, sl].astype(F32), cos, sin)
    kr = _rope(k_ref[:, sl].astype(F32) * cfg.k_scale, cos, sin)
    return qr, kr, (qr * qd_ref[:, sl]).astype(BF16), (kr * kd_ref[:, sl]).astype(BF16)


def _ret_fwd(cfg, Z, tabs, decs):
    RW, H, TA, NC = cfg.RW, cfg.H, cfg.TA, cfg.NC
    cm = functools.partial(_chunk_of, cfg)

    def body(q_ref, k_ref, v_ref, ct_ref, st_ref, dm_ref, qd_ref, kd_ref, cd_ref, o_ref, so_ref, S):
        @pl.when(pl.program_id(1) == 0)
        def _():
            S[...] = jnp.zeros_like(S)

        cos, sin = ct_ref[...], st_ref[...]
        for h in range(H):
            sl = slice(h * CH, (h + 1) * CH)
            qr, kr, qd, kd = _ret_prep(cfg, q_ref, k_ref, cos, sin, qd_ref, kd_ref, sl)
            v = v_ref[:, sl].astype(BF16)
            p = (_dot(qr.astype(BF16), kr.astype(BF16), "NT") * dm_ref[h]).astype(BF16)
            s_h = S[h]
            so_ref[h] = s_h
            o_ref[:, sl] = _dot(p, v) + _dot(qd, s_h.astype(BF16))
            S[h] = s_h * cd_ref[:, sl] + _dot(kd, v, "TN")

    return pl.pallas_call(
        body, name="ret_fwd", grid=(2, NC),
        in_specs=_ret_specs(cfg, cm),
        out_specs=[pl.BlockSpec((None, CH, RW), lambda d, t: (d, cm(d, t), 0)),
                   pl.BlockSpec((None, None, H, CH, CH), lambda d, t: (d, cm(d, t), 0, 0, 0))],
        out_shape=[jax.ShapeDtypeStruct((2, TA, RW), F32), jax.ShapeDtypeStruct((2, NC, H, CH, CH), F32)],
        scratch_shapes=[pltpu.VMEM((H, CH, CH), F32)],
        compiler_params=_cparams(("arbitrary", "arbitrary")),
    )(Z, Z, Z, tabs["cos"], tabs["sin"], decs["dmat"], decs["qdec"], decs["kdec"], decs["cdec"])


def _ret_bwd(cfg, Z, tabs, decs, states, dO):
    RW, H, TA, NC = cfg.RW, cfg.H, cfg.TA, cfg.NC

    def cm(d, t):
        return _chunk_of(cfg, d, NC - 1 - t)

    def body(q_ref, k_ref, v_ref, ct_ref, st_ref, dm_ref, qd_ref, kd_ref, cd_ref, s_ref, do_ref, w_ref,
             dqkv_ref, dl_ref, dS):
        t = pl.program_id(1)

        @pl.when(t == 0)
        def _():
            dS[...] = jnp.zeros_like(dS)
            dl_ref[...] = jnp.zeros_like(dl_ref)

        cos, sin = ct_ref[...], st_ref[...]
        for h in range(H):
            sl = slice(h * CH, (h + 1) * CH)
            qh, kh, qd, kd = _ret_prep(cfg, q_ref, k_ref, cos, sin, qd_ref, kd_ref, sl)
            qb, kb = qh.astype(BF16), kh.astype(BF16)
            v = v_ref[:, sl].astype(BF16)
            dob = do_ref[:, sl].astype(BF16)
            dm = dm_ref[h]
            p = (_dot(qb, kb, "NT") * dm).astype(BF16)
            dp = (_dot(dob, v, "NT") * dm).astype(BF16)
            s_h = s_ref[h]
            ds_h = dS[h]
            sb, dsb = s_h.astype(BF16), ds_h.astype(BF16)
            dq_i = _dot(dp, kb)
            dk_i = _dot(dp, qb, "TN")
            dq_c = _dot(dob, sb, "NT") * qd_ref[:, sl]
            dk_s = _dot(v, dsb, "NT") * kd_ref[:, sl]
            dqkv_ref[:, sl] = dq_i + dq_c
            dqkv_ref[:, RW + h * CH:RW + (h + 1) * CH] = dk_i + dk_s
            dqkv_ref[:, 2 * RW + h * CH:2 * RW + (h + 1) * CH] = _dot(p, dob, "TN") + _dot(kd, dsb)
            lam = w_ref[0] * (qh * dq_i) + w_ref[1] * (qh * dq_c) + w_ref[2] * (kh * dk_i) + w_ref[3] * (kh * dk_s)
            lam_s = float(CH) * cd_ref[:, sl] * jnp.sum(ds_h * s_h, axis=0, keepdims=True)
            dl_ref[:, sl] += jnp.sum(lam, axis=0, keepdims=True) + lam_s
            dS[h] = ds_h * cd_ref[:, sl] + _dot(qd, dob, "TN")

    in_specs = _ret_specs(cfg, cm) + [
        pl.BlockSpec((None, None, H, CH, CH), lambda d, t: (d, cm(d, t), 0, 0, 0)),
        pl.BlockSpec((CH, RW), lambda d, t: (cm(d, t), 0)),
        pl.BlockSpec((None, 4, CH, CH), lambda d, t: (d, 0, 0, 0)),
    ]
    return pl.pallas_call(
        body, name="ret_bwd", grid=(2, NC),
        in_specs=in_specs,
        out_specs=[pl.BlockSpec((None, CH, 3 * RW), lambda d, t: (d, cm(d, t), 0)),
                   pl.BlockSpec((None, 1, RW), lambda d, t: (d, 0, 0))],
        out_shape=[jax.ShapeDtypeStruct((2, TA, 3 * RW), F32), jax.ShapeDtypeStruct((2, 1, RW), F32)],
        scratch_shapes=[pltpu.VMEM((H, CH, CH), F32)],
        compiler_params=_cparams(("arbitrary", "arbitrary")),
    )(Z, Z, Z, tabs["cos"], tabs["sin"], decs["dmat"], decs["qdec"], decs["kdec"], decs["cdec"], states, dO,
      tabs["lamw"])


def _rope_bwd(cfg, dqkv2, tabs, dz):
    RW, H = cfg.RW, cfg.H
    tr = PT

    def body(d0, d1, ct, st, dz_in, o):
        cos, sin = jnp.tile(ct[...], (1, H)), jnp.tile(st[...], (1, H))
        d = d0[...] + d1[...]
        o[:, :RW] = _rope_t(d[:, :RW], cos, sin).astype(o.dtype)
        o[:, RW:2 * RW] = (_rope_t(d[:, RW:2 * RW], cos, sin) * cfg.k_scale).astype(o.dtype)
        o[:, 2 * RW:] = d[:, 2 * RW:].astype(o.dtype)

    return pl.pallas_call(
        body, name="rope_bwd", grid=(cfg.TA // tr,),
        in_specs=[pl.BlockSpec((None, tr, 3 * RW), lambda i: (0, i, 0)), pl.BlockSpec((None, tr, 3 * RW), lambda i: (1, i, 0)),
                  pl.BlockSpec((tr, CH), lambda i: (i, 0)), pl.BlockSpec((tr, CH), lambda i: (i, 0)), ANY],
        out_specs=pl.BlockSpec((tr, 3 * RW), lambda i: (i, 0)),
        out_shape=jax.ShapeDtypeStruct((cfg.TA, cfg.NIN), BF16),
        input_output_aliases={4: 0},
        compiler_params=_cparams(("arbitrary",)),
    )(dqkv2, dqkv2, tabs["cos"], tabs["sin"], dz)


def _retout_fwd(cfg, o2, Z, ng):
    RW, H = cfg.RW, cfg.H
    tr = PT

    def body(o0, o1, zg, ng, out):
        o = o0[...] + o1[...]
        z = zg[...].astype(F32)
        for h in range(H):
            sl = slice(h * CH, (h + 1) * CH)
            out[:, sl] = _f_headnorm_gate(o[:, sl], z[:, sl], ng[:, sl]).astype(out.dtype)

    return pl.pallas_call(
        body, name="retout_fwd", grid=(cfg.TA // tr,),
        in_specs=[pl.BlockSpec((None, tr, RW), lambda i: (0, i, 0)), pl.BlockSpec((None, tr, RW), lambda i: (1, i, 0)),
                  pl.BlockSpec((tr, RW), lambda i: (i, 3)), pl.BlockSpec((1, RW), lambda i: (0, 0))],
        out_specs=pl.BlockSpec((tr, RW), lambda i: (i, 0)),
        out_shape=jax.ShapeDtypeStruct((cfg.TA, RW), BF16),
        compiler_params=_cparams(("arbitrary",)),
    )(o2, o2, Z, ng)


def _retout_bwd(cfg, o2, Z, ng, dret, dz):
    RW, H = cfg.RW, cfg.H
    tr = PT

    def body(o0, o1, zg, ng, dr, dz_in, do_out, dz_out, dng):
        i = pl.program_id(0)
        o = o0[...] + o1[...]
        z = zg[...].astype(F32)
        d = dr[...]
        acc = []
        for h in range(H):
            sl = slice(h * CH, (h + 1) * CH)
            _, vjp = jax.vjp(_f_headnorm_gate, o[:, sl], z[:, sl], ng[:, sl])
            do_h, dz_h, dg_h = vjp(d[:, sl])
            do_out[:, sl] = do_h
            dz_out[:, sl] = dz_h.astype(dz_out.dtype)
            acc.append(dg_h)

        @pl.when(i == 0)
        def _():
            for h in range(H):
                dng[:, h * CH:(h + 1) * CH] = acc[h]

        @pl.when(i > 0)
        def _():
            for h in range(H):
                dng[:, h * CH:(h + 1) * CH] += acc[h]

    return pl.pallas_call(
        body, name="retout_bwd", grid=(cfg.TA // tr,),
        in_specs=[pl.BlockSpec((None, tr, RW), lambda i: (0, i, 0)), pl.BlockSpec((None, tr, RW), lambda i: (1, i, 0)),
                  pl.BlockSpec((tr, RW), lambda i: (i, 3)), pl.BlockSpec((1, RW), lambda i: (0, 0)),
                  pl.BlockSpec((tr, RW), lambda i: (i, 0)), ANY],
        out_specs=[pl.BlockSpec((tr, RW), lambda i: (i, 0)), pl.BlockSpec((tr, RW), lambda i: (i, 3)),
                   pl.BlockSpec((1, RW), lambda i: (0, 0))],
        out_shape=[jax.ShapeDtypeStruct((cfg.TA, RW), F32), jax.ShapeDtypeStruct((cfg.TA, cfg.NIN), BF16),
                   jax.ShapeDtypeStruct((1, RW), F32)],
        input_output_aliases={5: 1},
        compiler_params=_cparams(("arbitrary",)),
    )(o2, o2, Z, ng, dret, dz)


def _pool_consts(ctx_len):
    assert ctx_len == PT
    bm = np.zeros((2, len(POOL_WINDOWS), PT, PT), np.float32)
    ic = np.zeros((2, len(POOL_WINDOWS), PT, CH), np.float32)
    for ty, seg in enumerate((ctx_len, GRID_W)):
        for gi, w in enumerate(POOL_WINDOWS):
            for r in range(PT):
                s0, pos = (r // seg) * seg, r % seg
                lo, hi = max(pos - w // 2, 0), min(pos + w // 2 - 1, seg - 1)
                bm[ty, gi, r, s0 + lo:s0 + hi + 1] = 1.0
                ic[ty, gi, r, :] = 1.0 / (hi - lo + 1)
    return jnp.asarray(bm, BF16), jnp.asarray(ic, F32)


def _pool_tile(p, bm, ic, pw, g):
    sl = slice(g * CH, (g + 1) * CH)
    pg = p[:, sl].astype(F32)
    hi = pg.astype(BF16)
    lo = (pg - hi.astype(F32)).astype(BF16)
    y = (_dot(bm[g], hi) + _dot(bm[g], lo)) * ic[g] - pg
    return y, _dot(y.astype(BF16), pw[g].astype(BF16))


def _pool_fwd(cfg, Z, consts, pool_w, pool_scale):
    PW = cfg.PW
    G = PW // CH
    nct = cfg.CTX // PT
    ty = lambda i: jnp.where(i < nct, 0, 1)

    def fn(i, j, p, bm, ic, pw, ps):
        outs = [_pool_tile(p, bm, ic, pw, g)[1] for g in range(G)]
        return (jnp.concatenate(outs, axis=1) * ps,)

    return _rowcall("pool_fwd", fn, nrows=cfg.TA, tr=PT,
                    ins=[(Z, "rows", PW, cfg.o_p // PW), (consts[0], "sel", 0, ty), (consts[1], "sel", 0, ty),
                         (pool_w, "full", 0, 0), (pool_scale, "full", 0, 0)],
                    outs=[(BF16, PW, PW, 0, None)])[0]


def _pool_bwd(cfg, Z, consts, pool_w, pool_scale, dpool, dz):
    PW = cfg.PW
    G = PW // CH
    nct = cfg.CTX // PT
    ty = lambda i: jnp.where(i < nct, 0, 1)

    def fn(i, j, p, bm, ic, pw, ps, dout):
        dps, dps_acc, dpw = [], [], []
        for g in range(G):
            sl = slice(g * CH, (g + 1) * CH)
            y, lin = _pool_tile(p, bm, ic, pw, g)
            dlin = (dout[:, sl] * ps[:, sl]).astype(BF16)
            dps_acc.append(jnp.sum(dout[:, sl] * lin, axis=0, keepdims=True))
            dy = _dot(dlin, pw[g].astype(BF16), "NT")
            dpw.append(_dot(y.astype(BF16), dlin, "TN"))
            t = dy * ic[g]
            hi = t.astype(BF16)
            lo = (t - hi.astype(F32)).astype(BF16)
            dps.append(_dot(bm[g], hi, "TN") + _dot(bm[g], lo, "TN") - dy)
        return (jnp.concatenate(dps, axis=1), jnp.stack(dpw), jnp.concatenate(dps_acc, axis=1))

    return _rowcall("pool_bwd", fn, nrows=cfg.TA, tr=PT,
                    ins=[(Z, "rows", PW, cfg.o_p // PW), (consts[0], "sel", 0, ty), (consts[1], "sel", 0, ty),
                         (pool_w, "full", 0, 0), (pool_scale, "full", 0, 0), (dpool, "rows", PW, 0)],
                    outs=[(BF16, cfg.NIN, PW, cfg.o_p // PW, dz)], accs=[(G, CH, CH), (1, PW)])


def _sg_mixed(vn, sw, sbb, g, c):
    rows = slice(c * CH, (c + 1) * CH)
    sl = slice(g * CH, (g + 1) * CH)
    return _dot(sw[g].astype(BF16), vn[rows, sl].astype(BF16)) + sbb[g]


def _sg_fwd(cfg, Z, sng, sw, sbb):
    SW = cfg.SW
    G = SW // CH

    def fn(i, j, u, sv, sng, sw, sbb):
        ug = _gelu(u.astype(F32))
        vn = _f_sgv(sv.astype(F32), sng)
        rows = []
        for c in range(PT // CH):
            mixed = jnp.concatenate([_sg_mixed(vn, sw, sbb, g, c) for g in range(G)], axis=1)
            rows.append(ug[c * CH:(c + 1) * CH, :] * mixed)
        return (jnp.concatenate(rows, axis=0),)

    return _rowcall("sg_fwd", fn, nrows=cfg.TA, tr=PT,
                    ins=[(Z, "rows", SW, cfg.o_u // SW), (Z, "rows", SW, cfg.o_sv // SW), (sng, "full", 0, 0),
                         (sw, "full", 0, 0), (sbb, "full", 0, 0)],
                    outs=[(BF16, SW, SW, 0, None)])[0]


def _sg_bwd(cfg, Z, sng, sw, sbb, dsg, dz):
    SW = cfg.SW
    G = SW // CH

    def fn(i, j, u, sv, sng, sw, sbb, dout):
        uf, svf = u.astype(F32), sv.astype(F32)
        ug, vjp_u = jax.vjp(_gelu, uf)
        vn, vjp_v = jax.vjp(_f_sgv, svf, sng)
        dug_rows, dvn_rows = [], []
        dsw = [jnp.zeros((CH, CH), F32) for _ in range(G)]
        dsb = [jnp.zeros((CH, CH), F32) for _ in range(G)]
        for c in range(PT // CH):
            rows = slice(c * CH, (c + 1) * CH)
            dug_g, dvn_g = [], []
            for g in range(G):
                sl = slice(g * CH, (g + 1) * CH)
                mixed = _sg_mixed(vn, sw, sbb, g, c)
                dmixed = dout[rows, sl] * ug[rows, sl]
                dug_g.append(dout[rows, sl] * mixed)
                dmb = dmixed.astype(BF16)
                dvn_g.append(_dot(sw[g].astype(BF16), dmb, "TN"))
                dsw[g] = dsw[g] + _dot(dmb, vn[rows, sl].astype(BF16), "NT")
                dsb[g] = dsb[g] + jnp.broadcast_to(jnp.sum(dmixed, axis=1, keepdims=True), (CH, CH))
            dug_rows.append(jnp.concatenate(dug_g, axis=1))
            dvn_rows.append(jnp.concatenate(dvn_g, axis=1))
        (du,) = vjp_u(jnp.concatenate(dug_rows, axis=0))
        dsv, dsng = vjp_v(jnp.concatenate(dvn_rows, axis=0))
        return du, dsv, jnp.stack(dsw), jnp.stack(dsb), dsng

    return _rowcall("sg_bwd", fn, nrows=cfg.TA, tr=PT,
                    ins=[(Z, "rows", SW, cfg.o_u // SW), (Z, "rows", SW, cfg.o_sv // SW), (sng, "full", 0, 0),
                         (sw, "full", 0, 0), (sbb, "full", 0, 0), (dsg, "rows", SW, 0)],
                    outs=[(BF16, cfg.NIN, SW, cfg.o_u // SW, dz), (BF16, SW, SW, 0, None)],
                    accs=[(G, CH, CH), (G, CH, CH), (1, SW)])


def _gate_bwd(cfg, b, dY, L, Z, dz):
    D = cfg.D
    tc = 512
    nj = D // tc
    off = cfg.o_gate // tc + b * nj

    def fn(i, j, dy, l, z):
        s = jax.nn.sigmoid(z.astype(F32))
        return dy * s, dy * l.astype(F32) * s * (1.0 - s)

    return _rowcall("gate_bwd", fn, nrows=cfg.TA, tr=cfg.tr,
                    ins=[(dY, "rows", tc, 0), (L, "rows", tc, 0), (Z, "rows", tc, off)],
                    outs=[(BF16, D, tc, 0, None), (BF16, cfg.NIN, tc, off, dz)], ncol=nj)


def _copy_cols(cfg, src, dz, col0):
    w = src.shape[1]
    return _rowcall("copy_cols", lambda i, j, s: (s,), nrows=cfg.TA, tr=cfg.tr, ins=[(src, "rows", w, 0)],
                    outs=[(dz.dtype, dz.shape[1], w, col0 // w, dz)])[0]


def _final(cfg, X, g, target):
    D = cfg.D
    tr = PT
    nct = cfg.CTX // tr

    def body(x_ref, g_ref, t_ref, dx_ref, loss_ref, dg_ref):
        i = pl.program_id(0)

        def f(x, g):
            r = lax.rsqrt(jnp.mean(x * x, axis=-1, keepdims=True) + EPS)
            return x * r * g

        y, vjp = jax.vjp(f, x_ref[...], g_ref[...])
        err = y - t_ref[...]
        dx, dg = vjp(err * (1.0 / D))
        part = 0.5 * jnp.sum(jnp.mean(err * err, axis=-1, keepdims=True), axis=0, keepdims=True)

        @pl.when(i == 0)
        def _():
            loss_ref[...] = jnp.zeros_like(loss_ref)
            dg_ref[...] = jnp.zeros_like(dg_ref)

        @pl.when(i < nct)
        def _():
            dx_ref[...] = jnp.zeros_like(dx_ref)

        @pl.when(i >= nct)
        def _():
            dx_ref[...] = dx
            loss_ref[...] += jnp.broadcast_to(part, loss_ref.shape)
            dg_ref[...] += dg

    return pl.pallas_call(
        body, name="final", grid=(cfg.TA // tr,),
        in_specs=[pl.BlockSpec((tr, D), lambda i: (i, 0)), pl.BlockSpec((1, D), lambda i: (0, 0)),
                  pl.BlockSpec((tr, D), lambda i: (jnp.maximum(i - nct, 0), 0))],
        out_specs=[pl.BlockSpec((tr, D), lambda i: (i, 0)), pl.BlockSpec((8, CH), lambda i: (0, 0)),
                   pl.BlockSpec((1, D), lambda i: (0, 0))],
        out_shape=[jax.ShapeDtypeStruct((cfg.TA, D), F32), jax.ShapeDtypeStruct((8, CH), F32),
                   jax.ShapeDtypeStruct((1, D), F32)],
        compiler_params=_cparams(("arbitrary",)),
    )(X, g, target)


GATHER_A = ("w_in", "w1")
GATHER_B = ("w2", "w_out", "w_br", "w_bp", "w_bs")
REDUCE_MLP = ("w2", "w1")
REDUCE_MIX = ("w_out", "w_br", "w_bp", "w_bs", "w_in")


def _hosted(res, comm, sink):
    if comm is None:
        return res
    main, outs = res
    sink(outs)
    return main


def _layer_fwd(cfg, X, mods, W, sm, tabs, decs, consts, nxt=None, own=None):
    D, TA, tr, RW, PW, SW, DFF, NIN = cfg.D, cfg.TA, cfg.tr, cfg.RW, cfg.PW, cfg.SW, cfg.DFF, cfg.NIN
    ident = lambda accs, ex, i: (accs[0],)
    rest = tuple(k for k in BIG if k != "w_in")
    if own is not None:
        plan = {"z_mm": (own, "ici", rest), "w1_mm": (nxt, "ici", ("w_in", "w_out", "w_br", "w_bp", "w_bs")),
                "w2_mm": (nxt, "ici", ("w1", "w2"))}
    else:
        plan = {"z_mm": (nxt, "ici", GATHER_A), "w1_mm": (nxt, "ici", GATHER_B), "w2_mm": (nxt, "d2d", BIG)}

    def carry(name):
        gather, hop, names = plan[name]
        if gather is None:
            return None, None
        comm = gather.ici(names) if hop == "ici" else gather.d2d(names)
        return comm, lambda o: gather.update(names, o)

    H1 = _normmod_fwd(cfg, X, sm["norm1_g"], mods, 0, 1, "normmod1_fwd")
    comm, sink = carry("z_mm")
    (Z,) = _hosted(_matmul("z_mm", "NN", [(H1, W["w_in"])], R=TA, C=NIN, tr=tr, tc=512, tk=D, nk=1, out_dtypes=[F32],
                           epi=ident, comm=comm), comm, sink)
    if own is not None:
        own.update(rest, _run_comm("gather_d2d", own.d2d(rest)))
    o2, states = _ret_fwd(cfg, Z, tabs, decs)
    ret = _retout_fwd(cfg, o2, Z, sm["ret_norm_g"])
    pool = _pool_fwd(cfg, Z, consts, sm["pool_w"], sm["pool_scale"])
    sg = _sg_fwd(cfg, Z, sm["sg_norm_g"], sm["sg_w"], sm["sg_bb"])

    tc = 512
    goff = cfg.o_gate // tc

    def epi_branch(accs, ex, i):
        y = sum(jax.nn.sigmoid(z.astype(F32)) * a for a, z in zip(accs, ex))
        return (y, accs[0], accs[1], accs[2])

    Y, Lr, Lp, Ls = _matmul("branch_mm", "NN", [(ret, W["w_br"]), (pool, W["w_bp"]), (sg, W["w_bs"])], R=TA, C=D, tr=cfg.tr_small,
                            tc=tc, tk=0, nk=1, out_dtypes=[BF16] * 4, epi=epi_branch,
                            extras=[("tile", Z, goff + b * (D // tc)) for b in range(3)])

    def epi_res(rows):
        return lambda accs, ex, i: (ex[0] + _sel(i, rows, cfg.CTX, ex[1]) * accs[0], accs[0])

    X2, O = _matmul("out_mm", "NN", [(Y, W["w_out"])], R=TA, C=D, tr=tr, tc=tc, tk=D, nk=1, out_dtypes=[F32, BF16],
                    epi=epi_res(tr), extras=[("tile", X, 0), ("rows2", mods, 2 * (D // tc))])
    H2 = _normmod_fwd(cfg, X2, sm["norm2_g"], mods, 3, 4, "normmod2_fwd")
    tcf = _pick(DFF, 1024)
    def epi_relu(accs, ex, i):
        r = jnp.maximum(accs[0], 0.0)
        return (r * r, r)

    comm, sink = carry("w1_mm")
    A2, Rr = _hosted(_matmul("w1_mm", "NN", [(H2, W["w1"])], R=TA, C=DFF, tr=tr, tc=tcf, tk=D, nk=1, out_dtypes=[BF16, BF16],
                             epi=epi_relu, comm=comm), comm, sink)
    comm, sink = carry("w2_mm")
    tr2 = _pick(TA, 704)
    X3, M = _hosted(_matmul("w2_mm", "NN", [(A2, W["w2"])], R=TA, C=D, tr=tr2, tc=tc, tk=DFF, nk=1,
                            out_dtypes=[F32, BF16], epi=epi_res(tr2), extras=[("tile", X2, 0), ("rows2", mods, 5 * (D // tc))],
                            comm=comm), comm, sink)
    if own is not None and nxt is not None:
        nxt.update(BIG, _run_comm("gather_d2d", nxt.d2d(BIG)))
    saved = dict(X=X, H1=H1, Z=Z, o2=o2, states=states, ret=ret, pool=pool, sg=sg, Y=Y, L=(Lr, Lp, Ls), O=O, X2=X2,
                 H2=H2, R=Rr, A2=A2, M=M)
    return X3, saved


def _layer_bwd(cfg, dX3, sv, mods, W, sm, tabs, decs, consts, rs, make_reduce):
    D, TA, tr, RW, PW, SW, DFF, NIN = cfg.D, cfg.TA, cfg.tr, cfg.RW, cfg.PW, cfg.SW, cfg.DFF, cfg.NIN
    ident = lambda accs, ex, i: (accs[0],)
    tw = 512
    g = {}

    dM, dgate2 = _resgate_bwd(cfg, dX3, sv["M"], mods, 5, "resgate2_bwd")
    tcf = _pick(DFF, 1024)
    comm = rs.swap() if rs else None
    (dPre,) = _hosted(_matmul("dpre_mm", "NT", [(dM, W["w2"])], R=TA, C=DFF, tr=tr, tc=tcf, tk=D, nk=1, out_dtypes=[BF16],
                              epi=lambda accs, ex, i: (accs[0] * (2.0 * ex[0].astype(F32)),), extras=[("tile", sv["R"], 0)],
                              comm=comm), comm, lambda o: rs.swapped(o))
    comm = rs.scatter() if rs else None
    (g["w2"],) = _hosted(_matmul("dw2_mm", "TN", [(sv["A2"], dM)], R=DFF, C=D, tr=tw, tc=tw, tk=TA, nk=1, out_dtypes=[F32],
                                 epi=None, comm=comm), comm, lambda o: rs.scattered(o))
    comm = rs.share() if rs else None
    (g["w1"],) = _hosted(_matmul("dw1_mm", "TN", [(sv["H2"], dPre)], R=D, C=DFF, tr=tw, tc=tw, tk=TA, nk=1, out_dtypes=[F32],
                                 epi=None, comm=comm), comm, lambda o: rs.shared(o))
    mlp = make_reduce(REDUCE_MLP, {k: g.pop(k) for k in REDUCE_MLP})
    tkf = _pick(DFF, 2048)
    comm = mlp.swap()
    (dH2,) = _hosted(_matmul("dh2_mm", "NT", [(dPre, W["w1"])], R=TA, C=D, tr=tr, tc=1024, tk=tkf, nk=DFF // tkf,
                             out_dtypes=[F32], epi=ident, comm=comm), comm, lambda o: mlp.swapped(o))
    dX2, dn2, dsh2, dsc2 = _normmod_bwd(cfg, sv["X2"], dH2, dX3, sm["norm2_g"], mods, 3, 4, "normmod2_bwd")

    dO, dgate1 = _resgate_bwd(cfg, dX2, sv["O"], mods, 2, "resgate1_bwd")
    (dY,) = _matmul("dy_mm", "NT", [(dO, W["w_out"])], R=TA, C=D, tr=tr, tc=1024, tk=D, nk=1, out_dtypes=[F32], epi=ident)
    (g["w_out"],) = _matmul("dwout_mm", "TN", [(sv["Y"], dO)], R=D, C=D, tr=tw, tc=tw, tk=TA, nk=1,
                            out_dtypes=[F32], epi=None)
    dLr, dz = _gate_bwd(cfg, 0, dY, sv["L"][0], sv["Z"], None)
    dLp, dz = _gate_bwd(cfg, 1, dY, sv["L"][1], sv["Z"], dz)
    dLs, dz = _gate_bwd(cfg, 2, dY, sv["L"][2], sv["Z"], dz)

    (dret,) = _matmul("dret_mm", "NT", [(dLr, W["w_br"])], R=TA, C=RW, tr=tr, tc=RW, tk=D, nk=1, out_dtypes=[F32], epi=ident)
    (dpool,) = _matmul("dpool_mm", "NT", [(dLp, W["w_bp"])], R=TA, C=PW, tr=tr, tc=PW, tk=D, nk=1, out_dtypes=[F32], epi=ident)
    (dsg,) = _matmul("dsg_mm", "NT", [(dLs, W["w_bs"])], R=TA, C=SW, tr=tr, tc=SW, tk=D, nk=1, out_dtypes=[F32], epi=ident)
    (g["w_br"],) = _matmul("dwbr_mm", "TN", [(sv["ret"], dLr)], R=RW, C=D, tr=tw, tc=tw, tk=TA, nk=1, out_dtypes=[F32],
                           epi=None)
    (g["w_bp"],) = _matmul("dwbp_mm", "TN", [(sv["pool"], dLp)], R=PW, C=D, tr=tw, tc=tw, tk=TA, nk=1, out_dtypes=[F32],
                           epi=None)
    (g["w_bs"],) = _matmul("dwbs_mm", "TN", [(sv["sg"], dLs)], R=SW, C=D, tr=tw, tc=tw, tk=TA, nk=1, out_dtypes=[F32],
                           epi=None)
    dOr, dz, dretng = _retout_bwd(cfg, sv["o2"], sv["Z"], sm["ret_norm_g"], dret, dz)
    dqkv2, dlam = _ret_bwd(cfg, sv["Z"], tabs, decs, sv["states"], dOr)
    dz = _rope_bwd(cfg, dqkv2, tabs, dz)
    dz, dpw, dps = _pool_bwd(cfg, sv["Z"], consts, sm["pool_w"], sm["pool_scale"], dpool, dz)
    dz, dz_sv, dsw, dsb, dsng = _sg_bwd(cfg, sv["Z"], sm["sg_norm_g"], sm["sg_w"], sm["sg_bb"], dsg, dz)
    dz = _copy_cols(cfg, dz_sv, dz, cfg.o_sv)

    tkz = _pick(NIN, 2944)
    comm = mlp.scatter()
    (dH1,) = _hosted(_matmul("dh1_mm", "NT", [(dz, W["w_in"])], R=TA, C=D, tr=tr, tc=1024, tk=tkz, nk=NIN // tkz,
                             out_dtypes=[F32], epi=ident, comm=comm), comm, lambda o: mlp.scattered(o))
    comm = mlp.share()
    (g["w_in"],) = _hosted(_matmul("dwin_mm", "TN", [(sv["H1"], dz)], R=D, C=NIN, tr=tw, tc=tw, tk=TA, nk=1,
                                   out_dtypes=[F32], epi=None, comm=comm), comm, lambda o: mlp.shared(o))
    dX, dn1, dsh1, dsc1 = _normmod_bwd(cfg, sv["X"], dH1, dX2, sm["norm1_g"], mods, 0, 1, "normmod1_bwd")
    dmods = jnp.concatenate([dsh1, dsc1, dgate1, dsh2, dsc2, dgate2], axis=1)
    small = dict(norm1_g=dn1, norm2_g=dn2, ret_norm_g=dretng, pool_w=dpw, pool_scale=dps, sg_norm_g=dsng, sg_w=dsw,
                 sg_b=dsb[:, :, 0], dlam=dlam)
    return dX, g, mlp.result, small, dmods


def _tables(cfg):
    nf = CH // 4
    inv = ROPE_THETA ** (-jnp.arange(nf, dtype=F32) / nf)
    tok = jnp.arange(cfg.T)
    ar = (tok // GRID_W).astype(F32)[:, None] * inv[None]
    ac = (tok % GRID_W).astype(F32)[:, None] * inv[None]
    cos = jnp.concatenate([jnp.cos(ar), jnp.cos(ar), jnp.cos(ac), jnp.cos(ac)], axis=1)
    sin = jnp.concatenate([-jnp.sin(ar), jnp.sin(ar), -jnp.sin(ac), jnp.sin(ac)], axis=1)
    cos = jnp.concatenate([jnp.ones((cfg.CTX, CH), F32), cos], axis=0)
    sin = jnp.concatenate([jnp.zeros((cfg.CTX, CH), F32), sin], axis=0)
    idx = np.broadcast_to(np.arange(CH, dtype=np.float32)[:, None], (CH, CH))
    lamw = np.stack([np.stack([idx, idx + 1.0, -idx, CH - 1.0 - idx]), np.stack([-idx, CH - idx, idx, idx])])
    return dict(cos=cos, sin=sin, lamw=jnp.asarray(lamw, F32))


def _decays(cfg, logit):
    H, RW = cfg.H, cfg.RW
    lam = jax.nn.log_sigmoid(logit.astype(F32))
    idx = jnp.arange(CH, dtype=F32)
    dist = idx[:, None] - idx[None, :]
    d0 = jnp.where(dist >= 0, jnp.exp(lam[0][:, None, None] * jnp.maximum(dist, 0.0)), 0.0)
    d1 = jnp.where(dist <= 0, jnp.exp(lam[1][:, None, None] * jnp.maximum(-dist, 0.0)), 0.0)
    lanes = lambda a: jnp.repeat(a.T, CH, axis=1)
    qdec = jnp.stack([lanes(jnp.exp(lam[0][:, None] * (idx + 1.0)[None])), lanes(jnp.exp(lam[1][:, None] * (CH - idx)[None]))])
    kdec = jnp.stack([lanes(jnp.exp(lam[0][:, None] * (CH - 1.0 - idx)[None])), lanes(jnp.exp(lam[1][:, None] * idx[None]))])
    cdec = jnp.repeat(jnp.exp(lam * CH), CH, axis=1)[:, None, :]
    return dict(dmat=jnp.stack([d0, d1]), qdec=qdec, kdec=kdec, cdec=cdec)


def _small_of_layer(small_w, l):
    sm = {k: v[l] for k, v in small_w.items()}
    sm["norm1_g"] = sm["norm1_g"][None]
    sm["norm2_g"] = sm["norm2_g"][None]
    sm["ret_norm_g"] = sm["ret_norm_g"][None]
    sm["pool_scale"] = sm["pool_scale"][None]
    sm["sg_norm_g"] = sm["sg_norm_g"][None]
    sm["sg_bb"] = jnp.broadcast_to(sm["sg_b"][:, :, None], sm["sg_b"].shape + (CH,))
    return sm


def _local_fwd_bwd(cfg, X0, target, mods, gathers, make_reduce, small_w, final_g):
    depth = len(mods)
    tabs = _tables(cfg)
    consts = _pool_consts(cfg.CTX)
    X, saved, Ws, sms, decs = X0, [], [], [], []
    gathers[0].run(("w_in",))
    for l in range(depth):
        Ws.append(gathers[l].weights())
        sms.append(_small_of_layer(small_w, l))
        decs.append(_decays(cfg, small_w["ret_decay_logit"][l]))
        X, sv = _layer_fwd(cfg, X, mods[l], Ws[l], sms[l], tabs, decs[l], consts, gathers[l + 1] if l + 1 < depth else None,
                           gathers[0] if l == 0 else None)
        saved.append(sv)
    dX, loss_acc, dfinal = _final(cfg, X, final_g[None], target)
    shard_g, small, dmods = [None] * depth, [None] * depth, [None] * depth
    pending = None
    for l in reversed(range(depth)):
        dX, mix, shard_g[l], small[l], dmods[l] = _layer_bwd(cfg, dX, saved[l], mods[l], Ws[l], sms[l], tabs, decs[l], consts,
                                                             pending, make_reduce)
        if pending is not None:
            shard_g[l + 1].update(pending.result)
        pending = make_reduce(REDUCE_MIX, mix)
        lam_grad = jnp.sum(small[l].pop("dlam").reshape(2, cfg.H, CH), axis=-1)
        small[l]["ret_decay_logit"] = lam_grad * jax.nn.sigmoid(-small_w["ret_decay_logit"][l].astype(F32))
    shard_g[0].update(pending.run())
    return loss_acc[0, 0], dX, shard_g, small, dmods, dfinal


def _me():
    return lax.axis_index("x"), lax.axis_index("y"), lax.axis_index("c")


def _other_chips(x, y):
    return [(1 - x, y), (x, 1 - y), (1 - x, 1 - y)]


def _rcopy(src, dst, send_sem, recv_sem, dev):
    return pltpu.make_async_remote_copy(src_ref=src, dst_ref=dst, send_sem=send_sem, recv_sem=recv_sem,
                                        device_id=dev, device_id_type=MESH)


def _half(ref, axis, c):
    k, n = ref.shape
    if axis == 1:
        return ref.at[pl.ds(c * (k // 2), k // 2), :]
    return ref.at[:, pl.ds(c * (n // 2), n // 2)]


def _chip_part(ref, axis, j):
    k, n = ref.shape
    if axis == 1:
        return ref.at[:, pl.ds(j * (n // 4), n // 4)]
    return ref.at[pl.ds(j * (k // 4), k // 4), :]


def _piece(ref, axis, j, c):
    k, n = ref.shape
    if axis == 1:
        return ref.at[pl.ds(c * (k // 2), k // 2), pl.ds(j * (n // 4), n // 4)]
    return ref.at[pl.ds(j * (k // 4), k // 4), pl.ds(c * (n // 2), n // 2)]


def _gather_weights(fulls, axes):
    n = len(fulls)

    def body(*refs):
        outs = refs[n:2 * n]
        send, recv = refs[2 * n:]
        x, y, c = _me()
        j = 2 * x + y
        sib = (x, y, 1 - c)
        chips = _other_chips(x, y)
        first = []
        for t in range(n):
            for k, chip in enumerate(chips):
                own = _piece(outs[t], axes[t], j, c)
                first.append(_rcopy(own, own, send.at[6 * t + k], recv.at[6 * t + k], (*chip, c)))
                first[-1].start()
        passed = []
        for t in range(n):
            for k, chip in enumerate(chips):
                landed = _piece(outs[t], axes[t], 2 * chip[0] + chip[1], c)
                _rcopy(landed, landed, send.at[6 * t + k], recv.at[6 * t + k], sib).wait_recv()
                passed.append(_rcopy(landed, landed, send.at[6 * t + 3 + k], recv.at[6 * t + 3 + k], sib))
                passed[-1].start()
        for t in range(n):
            for k, chip in enumerate(chips):
                theirs = _piece(outs[t], axes[t], 2 * chip[0] + chip[1], 1 - c)
                _rcopy(theirs, theirs, send.at[6 * t + 3 + k], recv.at[6 * t + 3 + k], sib).wait_recv()
        for cp in first + passed:
            cp.wait_send()

    return pl.pallas_call(
        body, name="gather_weights",
        in_specs=[ANY] * n, out_specs=[ANY] * n,
        out_shape=[jax.ShapeDtypeStruct(f.shape, f.dtype) for f in fulls],
        input_output_aliases={t: t for t in range(n)},
        scratch_shapes=[pltpu.SemaphoreType.DMA((6 * n,)), pltpu.SemaphoreType.DMA((6 * n,))],
    )(*fulls)


class _Comm:
    def __init__(self, ins, outs, aliases, nsem, start, finish):
        self.ins, self.outs, self.aliases, self.nsem, self.start, self.finish = ins, outs, aliases, nsem, start, finish


def _run_comm(name, comm):
    n_in = len(comm.ins)

    def body(*refs):
        ins, outs = refs[:n_in], refs[n_in:n_in + len(comm.outs)]
        send, recv = refs[n_in + len(comm.outs):]
        comm.start(ins, outs, send, recv)
        comm.finish(ins, outs, send, recv)

    return pl.pallas_call(
        body, name=name, in_specs=[ANY] * n_in, out_specs=[ANY] * len(comm.outs), out_shape=list(comm.outs),
        input_output_aliases=dict(comm.aliases),
        scratch_shapes=[pltpu.SemaphoreType.DMA((comm.nsem,)), pltpu.SemaphoreType.DMA((comm.nsem,))],
    )(*comm.ins)


def _like(arrs):
    return [jax.ShapeDtypeStruct(a.shape, a.dtype) for a in arrs]


def _gather_ici_comm(fulls, axes):
    n = len(fulls)

    def copies(outs, send, recv):
        x, y, c = _me()
        own_j = 2 * x + y
        res = []
        for t in range(n):
            for k, chip in enumerate(_other_chips(x, y)):
                own = _piece(outs[t], axes[t], own_j, c)
                landed = _piece(outs[t], axes[t], 2 * chip[0] + chip[1], c)
                res.append((_rcopy(own, own, send.at[3 * t + k], recv.at[3 * t + k], (*chip, c)),
                            _rcopy(landed, landed, send.at[3 * t + k], recv.at[3 * t + k], (*chip, c))))
        return res

    def start(ins, outs, send, recv):
        for out, _ in copies(outs, send, recv):
            out.start()

    def finish(ins, outs, send, recv):
        for out, arrival in copies(outs, send, recv):
            out.wait_send()
            arrival.wait_recv()

    return _Comm(fulls, _like(fulls), {t: t for t in range(n)}, 3 * n, start, finish)


def _gather_d2d_comm(fulls, axes):
    n = len(fulls)

    def copies(outs, send, recv):
        x, y, c = _me()
        res = []
        for t in range(n):
            for k, chip in enumerate(_other_chips(x, y)):
                landed = _piece(outs[t], axes[t], 2 * chip[0] + chip[1], c)
                theirs = _piece(outs[t], axes[t], 2 * chip[0] + chip[1], 1 - c)
                res.append((_rcopy(landed, landed, send.at[3 * t + k], recv.at[3 * t + k], (x, y, 1 - c)),
                            _rcopy(theirs, theirs, send.at[3 * t + k], recv.at[3 * t + k], (x, y, 1 - c))))
        return res

    def start(ins, outs, send, recv):
        for out, _ in copies(outs, send, recv):
            out.start()

    def finish(ins, outs, send, recv):
        for out, arrival in copies(outs, send, recv):
            out.wait_send()
            arrival.wait_recv()

    return _Comm(fulls, _like(fulls), {t: t for t in range(n)}, 3 * n, start, finish)


def _swap_comm(grads, axes):
    n = len(grads)
    half_shapes = [(g.shape[0] // 2, g.shape[1]) if a == 1 else (g.shape[0], g.shape[1] // 2) for g, a in zip(grads, axes)]

    def copies(ins, outs, send, recv):
        x, y, c = _me()
        return [_rcopy(_half(ins[t], axes[t], 1 - c), outs[t], send.at[t], recv.at[t], (x, y, 1 - c)) for t in range(n)]

    def start(ins, outs, send, recv):
        for cp in copies(ins, outs, send, recv):
            cp.start()

    def finish(ins, outs, send, recv):
        for cp in copies(ins, outs, send, recv):
            cp.wait()

    return _Comm(grads, [jax.ShapeDtypeStruct(s, F32) for s in half_shapes], {}, n, start, finish)


def _scatter_comm(parts, axes):
    n = len(parts)
    q_shapes = [(p.shape[0], p.shape[1] // 4) if a == 1 else (p.shape[0] // 4, p.shape[1]) for p, a in zip(parts, axes)]

    def copies(ins, outs, send, recv):
        x, y, c = _me()
        res = []
        for t in range(n):
            for k, chip in enumerate(_other_chips(x, y)):
                res.append(_rcopy(_chip_part(ins[t], axes[t], 2 * chip[0] + chip[1]), outs[3 * t + k], send.at[3 * t + k],
                                  recv.at[3 * t + k], (*chip, c)))
        return res

    def start(ins, outs, send, recv):
        for cp in copies(ins, outs, send, recv):
            cp.start()

    def finish(ins, outs, send, recv):
        for cp in copies(ins, outs, send, recv):
            cp.wait()

    return _Comm(parts, [jax.ShapeDtypeStruct(s, p.dtype) for s, p in zip(q_shapes, parts) for _ in range(3)], {}, 3 * n,
                 start, finish)


def _share_comm(shards, axes):
    n = len(shards)

    def copies(outs, send, recv):
        x, y, c = _me()
        res = []
        for t in range(n):
            mine, theirs = _half(outs[t], axes[t], c), _half(outs[t], axes[t], 1 - c)
            res.append((_rcopy(mine, mine, send.at[t], recv.at[t], (x, y, 1 - c)),
                        _rcopy(theirs, theirs, send.at[t], recv.at[t], (x, y, 1 - c))))
        return res

    def start(ins, outs, send, recv):
        for out, _ in copies(outs, send, recv):
            out.start()

    def finish(ins, outs, send, recv):
        for out, arrival in copies(outs, send, recv):
            out.wait_send()
            arrival.wait_recv()

    return _Comm(shards, _like(shards), {t: t for t in range(n)}, n, start, finish)


def _cast_into_full(w, l, axis, sc):
    _, k, n = w.shape
    tr = _pick(k, 256, 16)
    if axis == 1:
        full, out_spec = (k, 4 * n), pl.BlockSpec((tr, n), lambda i, s: (i, s[1]))
    else:
        full, out_spec = (4 * k, n), pl.BlockSpec((tr, n), lambda i, s: (s[1] * (k // tr) + i, 0))

    def body(s_ref, w_ref, o_ref):
        o_ref[...] = w_ref[...].astype(BF16)

    return pl.pallas_call(
        body, name="cast_into_full",
        grid_spec=pltpu.PrefetchScalarGridSpec(
            num_scalar_prefetch=1, grid=(k // tr,),
            in_specs=[pl.BlockSpec((None, tr, n), lambda i, s: (l, i, 0))], out_specs=out_spec),
        out_shape=jax.ShapeDtypeStruct(full, BF16), compiler_params=_cparams(("arbitrary",)),
    )(sc, w)


def _rs_add2(g, got, axis, sc):
    k, n = g.shape
    hk, hn = (k // 2, n) if axis == 1 else (k, n // 2)
    tr = _pick(hk, max(16, (1 << 20) // hn), 16)
    if axis == 1:
        g_spec = pl.BlockSpec((tr, hn), lambda i, s: (s[0] * (hk // tr) + i, 0))
    else:
        g_spec = pl.BlockSpec((tr, hn), lambda i, s: (i, s[0]))
    blk = pl.BlockSpec((tr, hn), lambda i, s: (i, 0))

    def body(s_ref, a_ref, b_ref, o_ref):
        o_ref[...] = (a_ref[...] + b_ref[...]).astype(o_ref.dtype)

    return pl.pallas_call(
        body, name="rs_add2",
        grid_spec=pltpu.PrefetchScalarGridSpec(num_scalar_prefetch=1, grid=(hk // tr,), in_specs=[g_spec, blk], out_specs=blk),
        out_shape=jax.ShapeDtypeStruct((hk, hn), BF16), compiler_params=_cparams(("arbitrary",)),
    )(sc, g, got)


def _rs_add4(part, got3, axis, sc):
    k, n = part.shape
    qk, qn = (k, n // 4) if axis == 1 else (k // 4, n)
    tr = _pick(qk, max(16, (1 << 20) // qn), 16)
    if axis == 1:
        p_spec = pl.BlockSpec((tr, qn), lambda i, s: (i, s[1]))
        shard, o_spec = (2 * qk, qn), pl.BlockSpec((tr, qn), lambda i, s: (s[0] * (qk // tr) + i, 0))
    else:
        p_spec = pl.BlockSpec((tr, qn), lambda i, s: (s[1] * (qk // tr) + i, 0))
        shard, o_spec = (qk, 2 * qn), pl.BlockSpec((tr, qn), lambda i, s: (i, s[0]))
    blk = pl.BlockSpec((tr, qn), lambda i, s: (i, 0))

    def body(s_ref, p_ref, a_ref, b_ref, c_ref, o_ref):
        o_ref[...] = ((p_ref[...].astype(F32) + a_ref[...].astype(F32)) + b_ref[...].astype(F32)) + c_ref[...].astype(F32)

    return pl.pallas_call(
        body, name="rs_add4",
        grid_spec=pltpu.PrefetchScalarGridSpec(num_scalar_prefetch=1, grid=(qk // tr,), in_specs=[p_spec, blk, blk, blk],
                                               out_specs=o_spec),
        out_shape=jax.ShapeDtypeStruct(shard, F32), compiler_params=_cparams(("arbitrary",)),
    )(sc, part, *got3)


def _gather_small(v):
    def body(v_ref, out_ref, send, recv, loc):
        x, y, c = _me()
        sib = (x, y, 1 - c)
        chips = _other_chips(x, y)
        slot = lambda px, py, pc: out_ref.at[4 * px + 2 * py + pc]
        mine = pltpu.make_async_copy(v_ref, slot(x, y, c), loc)
        mine.start()
        first = [_rcopy(v_ref, slot(x, y, c), send.at[0], recv.at[0], sib)]
        first += [_rcopy(v_ref, slot(x, y, c), send.at[1 + k], recv.at[1 + k], (*chip, c)) for k, chip in enumerate(chips)]
        for cp in first:
            cp.start()
        passed = []
        for k, chip in enumerate(chips):
            landed = slot(*chip, c)
            _rcopy(landed, landed, send.at[1 + k], recv.at[1 + k], sib).wait_recv()
            passed.append(_rcopy(landed, landed, send.at[4 + k], recv.at[4 + k], sib))
            passed[-1].start()
        theirs = slot(x, y, 1 - c)
        _rcopy(theirs, theirs, send.at[0], recv.at[0], sib).wait_recv()
        for k, chip in enumerate(chips):
            theirs = slot(*chip, 1 - c)
            _rcopy(theirs, theirs, send.at[4 + k], recv.at[4 + k], sib).wait_recv()
        for cp in first + passed:
            cp.wait_send()
        mine.wait()

    return pl.pallas_call(
        body, name="gather_small",
        in_specs=[ANY], out_specs=ANY,
        out_shape=jax.ShapeDtypeStruct((8,) + v.shape, v.dtype),
        scratch_shapes=[pltpu.SemaphoreType.DMA((7,)), pltpu.SemaphoreType.DMA((7,)), pltpu.SemaphoreType.DMA],
    )(v)


class _WeightGather:
    def __init__(self, params, l, sc):
        self.bufs = {k: _cast_into_full(params[k], l, SHARD_AXIS[k], sc) for k in BIG}

    def run(self, names):
        self.update(names, _gather_weights([self.bufs[k] for k in names], [SHARD_AXIS[k] for k in names]))

    def ici(self, names):
        return _gather_ici_comm([self.bufs[k] for k in names], [SHARD_AXIS[k] for k in names])

    def d2d(self, names):
        return _gather_d2d_comm([self.bufs[k] for k in names], [SHARD_AXIS[k] for k in names])

    def update(self, names, outs):
        self.bufs.update(zip(names, outs))

    def weights(self):
        return self.bufs


class _GradReduce:
    def __init__(self, names, grads, sc):
        self.names, self.grads, self.sc, self.parts, self.theirs = names, grads, sc, None, None
        self.axes = [SHARD_AXIS[k] for k in names]

    def swap(self):
        return _swap_comm([self.grads[k] for k in self.names], self.axes)

    def swapped(self, got):
        self.parts = [_rs_add2(self.grads[k], s, a, self.sc) for k, s, a in zip(self.names, got, self.axes)]

    def scatter(self):
        return _scatter_comm(self.parts, self.axes)

    def scattered(self, outs):
        self.theirs = [outs[3 * q:3 * q + 3] for q in range(len(self.names))]

    def share(self):
        halves = [_rs_add4(p, th, a, self.sc) for p, th, a in zip(self.parts, self.theirs, self.axes)]
        return _share_comm(halves, self.axes)

    def shared(self, outs):
        self.result = dict(zip(self.names, outs))

    def run(self):
        self.swapped(_run_comm("rs_swap", self.swap()))
        self.scattered(_run_comm("rs_scatter", self.scatter()))
        self.shared(_run_comm("rs_share", self.share()))
        return self.result


def _adam_math(w, g, m, v):
    m = ADAM_B1 * m + (1.0 - ADAM_B1) * g
    v = ADAM_B2 * v + (1.0 - ADAM_B2) * (g * g)
    m_hat = m / (1.0 - ADAM_B1 ** ADAM_STEP)
    v_hat = v / (1.0 - ADAM_B2 ** ADAM_STEP)
    delta = -ADAM_LR * (m_hat / (jnp.sqrt(v_hat) + ADAM_EPS) + ADAM_WD * w)
    return delta, m, v


def _adam_layer(w, m, v, l, g, prev):
    L, k, n = w.shape
    tr = _pick(k, 128, 8)
    blk = pl.BlockSpec((None, tr, n), lambda i: (l, i, 0))

    def body(*refs):
        w_ref, m_ref, v_ref, g_ref = refs[:4]
        go, do, mo, vo = refs[-4:]
        gv = g_ref[...]
        d, m2, v2 = _adam_math(w_ref[...], gv, m_ref[...], v_ref[...])
        go[...] = gv
        do[...] = d
        mo[...] = m2
        vo[...] = v2

    args = [w, m, v, g]
    in_specs = [blk, blk, blk, pl.BlockSpec((tr, n), lambda i: (i, 0))]
    aliases = {}
    if prev is not None:
        for q, p in enumerate(prev):
            aliases[len(args)] = q
            in_specs.append(ANY)
            args.append(p)
    return pl.pallas_call(
        body, name="adam_layer", grid=(k // tr,),
        in_specs=in_specs, out_specs=[blk] * 4, out_shape=[jax.ShapeDtypeStruct((L, k, n), F32)] * 4,
        input_output_aliases=aliases, compiler_params=_cparams(("arbitrary",)),
    )(*args)


def _adam_flat(w, g, m, v):
    r = w.shape[0]
    tr = _pick(r, 512, 8)
    fn = lambda i, j, w, g, m, v: _adam_math(w, g, m, v)
    return _rowcall("adam_flat", fn, nrows=r, tr=tr, ins=[(a, "rows", 128, 0) for a in (w, g, m, v)],
                    outs=[(F32, 128, 128, 0, None)] * 3)


def _sum8(gathered):
    _, r, _ = gathered.shape
    tr = _pick(r, 512, 8)

    def body(g_ref, o_ref):
        acc = g_ref[0]
        for d in range(1, 8):
            acc = acc + g_ref[d]
        o_ref[...] = acc

    return pl.pallas_call(
        body, name="sum8", grid=(r // tr,),
        in_specs=[pl.BlockSpec((8, tr, 128), lambda i: (0, i, 0))], out_specs=pl.BlockSpec((tr, 128), lambda i: (i, 0)),
        out_shape=jax.ShapeDtypeStruct((r, 128), F32), compiler_params=_cparams(("arbitrary",)),
    )(gathered)


def _hdot(a, b, form="NN"):
    return _dot(a.astype(BF16), b.astype(BF16), form)


def _ada_fwd(s16, w_ada, l):
    _, d, ns = w_ada.shape
    tc = _pick(ns, 512)

    def body(s_ref, w_ref, o_ref):
        o_ref[...] = _hdot(s_ref[...], w_ref[...])

    return pl.pallas_call(
        body, name="ada_fwd", grid=(ns // tc,),
        in_specs=[pl.BlockSpec((16, d), lambda j: (0, 0)), pl.BlockSpec((None, d, tc), lambda j: (l, 0, j))],
        out_specs=pl.BlockSpec((16, tc), lambda j: (0, j)),
        out_shape=jax.ShapeDtypeStruct((16, ns), F32), compiler_params=_cparams(("arbitrary",)),
    )(s16, w_ada)


def _ada_bwd(s16t, dm, w_ada, l):
    _, d, ns = w_ada.shape
    tc = _pick(ns, 512)

    def body(st_ref, dm_ref, w_ref, dw_ref, ds_ref):
        j = pl.program_id(0)
        dw_ref[...] = _hdot(st_ref[...], dm_ref[...])
        part = _hdot(dm_ref[...], w_ref[...], "NT")

        @pl.when(j == 0)
        def _():
            ds_ref[...] = part

        @pl.when(j > 0)
        def _():
            ds_ref[...] += part

    return pl.pallas_call(
        body, name="ada_bwd", grid=(ns // tc,),
        in_specs=[pl.BlockSpec((d, 16), lambda j: (0, 0)), pl.BlockSpec((16, tc), lambda j: (0, j)),
                  pl.BlockSpec((None, d, tc), lambda j: (l, 0, j))],
        out_specs=[pl.BlockSpec((d, tc), lambda j: (0, j)), pl.BlockSpec((16, d), lambda j: (0, 0))],
        out_shape=[jax.ShapeDtypeStruct((d, ns), F32), jax.ShapeDtypeStruct((16, d), F32)],
        compiler_params=_cparams(("arbitrary",)),
    )(s16t, dm, w_ada)


def _tile_rows(shape):
    return -(-int(np.prod(shape)) // 1024) * 8


def _pack(arrs):
    parts = []
    for a in arrs:
        flat = a.reshape(-1).astype(F32)
        parts.append(jnp.pad(flat, (0, _tile_rows(a.shape) * 128 - flat.shape[0])).reshape(-1, 128))
    return jnp.concatenate(parts, axis=0)


def _unpack(packed, shapes):
    out, row = [], 0
    for s in shapes:
        nr = _tile_rows(s)
        out.append(packed[row:row + nr].reshape(-1)[:int(np.prod(s))].reshape(s))
        row += nr
    return out


SMALL = ("norm1_g", "norm2_g", "ret_decay_logit", "ret_norm_g", "pool_w", "pool_scale", "sg_norm_g", "sg_w", "sg_b")


def kernel(x, c, ctx, c_ctx, w_ada, b_ada, norm1_g, w_in, ret_decay_logit, ret_norm_g, pool_w, pool_scale, sg_norm_g, sg_w, sg_b, w_br, w_bp, w_bs, w_out, norm2_g, w1, w2, final_norm_g, loss_target, m_c_ctx, m_w_ada, m_b_ada, m_norm1_g, m_w_in, m_ret_decay_logit, m_ret_norm_g, m_pool_w, m_pool_scale, m_sg_norm_g, m_sg_w, m_sg_b, m_w_br, m_w_bp, m_w_bs, m_w_out, m_norm2_g, m_w1, m_w2, m_final_norm_g, v_c_ctx, v_w_ada, v_b_ada, v_norm1_g, v_w_in, v_ret_decay_logit, v_ret_norm_g, v_pool_w, v_pool_scale, v_sg_norm_g, v_sg_w, v_sg_b, v_w_br, v_w_bp, v_w_bs, v_w_out, v_norm2_g, v_w1, v_w2, v_final_norm_g):
    P = dict(c_ctx=c_ctx, w_ada=w_ada, b_ada=b_ada, norm1_g=norm1_g, w_in=w_in, ret_decay_logit=ret_decay_logit,
             ret_norm_g=ret_norm_g, pool_w=pool_w, pool_scale=pool_scale, sg_norm_g=sg_norm_g, sg_w=sg_w, sg_b=sg_b, w_br=w_br,
             w_bp=w_bp, w_bs=w_bs, w_out=w_out, norm2_g=norm2_g, w1=w1, w2=w2, final_norm_g=final_norm_g)
    Mo = dict(c_ctx=m_c_ctx, w_ada=m_w_ada, b_ada=m_b_ada, norm1_g=m_norm1_g, w_in=m_w_in, ret_decay_logit=m_ret_decay_logit,
              ret_norm_g=m_ret_norm_g, pool_w=m_pool_w, pool_scale=m_pool_scale, sg_norm_g=m_sg_norm_g, sg_w=m_sg_w, sg_b=m_sg_b,
              w_br=m_w_br, w_bp=m_w_bp, w_bs=m_w_bs, w_out=m_w_out, norm2_g=m_norm2_g, w1=m_w1, w2=m_w2,
              final_norm_g=m_final_norm_g)
    Vo = dict(c_ctx=v_c_ctx, w_ada=v_w_ada, b_ada=v_b_ada, norm1_g=v_norm1_g, w_in=v_w_in, ret_decay_logit=v_ret_decay_logit,
              ret_norm_g=v_ret_norm_g, pool_w=v_pool_w, pool_scale=v_pool_scale, sg_norm_g=v_sg_norm_g, sg_w=v_sg_w, sg_b=v_sg_b,
              w_br=v_w_br, w_bp=v_w_bp, w_bs=v_w_bs, w_out=v_w_out, norm2_g=v_norm2_g, w1=v_w1, w2=v_w2,
              final_norm_g=v_final_norm_g)
    names = ("c_ctx", "w_ada", "b_ada", "norm1_g", "w_in", "ret_decay_logit", "ret_norm_g", "pool_w", "pool_scale", "sg_norm_g",
             "sg_w", "sg_b", "w_br", "w_bp", "w_bs", "w_out", "norm2_g", "w1", "w2", "final_norm_g")
    L, D = w_in.shape[0], x.shape[-1]
    T, CTX = x.shape[1], ctx.shape[1]
    cfg = _Cfg(D, T, CTX, 4 * w1.shape[2])
    ns_ada = w_ada.shape[2]
    mx, my, mc = _me()
    dev = 4 * mx + 2 * my + mc
    chip = 2 * mx + my

    silu_cc = jax.nn.silu(c_ctx)
    silu_all = _gather_small(jax.nn.silu(c).reshape(-1, 128)).reshape(8, D)
    s16 = jnp.concatenate([silu_cc[None], silu_all, jnp.zeros((7, D), F32)], axis=0)
    proj = jnp.stack([_ada_fwd(s16, w_ada, l) for l in range(L)])
    proj_all = _gather_small(proj.reshape(-1, 128)).reshape(8, L, 16, ns_ada)
    mods_full = jnp.concatenate([proj_all[2 * j] for j in range(4)], axis=-1) + b_ada[:, None, :]
    mods = [jnp.concatenate([mods_full[l, 0:1], lax.dynamic_slice_in_dim(mods_full[l], 1 + dev, 1, axis=0)], axis=0)
            for l in range(L)]

    sc = jnp.stack([mc, chip]).astype(jnp.int32)
    gathers = [_WeightGather(P, l, sc) for l in range(L)]
    X0 = jnp.concatenate([ctx[0], x[0]], axis=0)
    small_w = {k: P[k] for k in SMALL}
    loss_part, dX, shard_g, small, dmods, dfinal = _local_fwd_bwd(
        cfg, X0, loss_target[0], mods, gathers, lambda names, grads: _GradReduce(names, grads, sc), small_w, final_norm_g)
    loss = lax.psum(loss_part, ("x", "y", "c"))
    grad_x = dX[CTX:][None]

    outs = {k: None for k in BIG}
    for l in range(L):
        for k in BIG:
            outs[k] = _adam_layer(P[k], Mo[k], Vo[k], l, shard_g[l][k], outs[k])

    per_layer = [[small[l][k] for k in SMALL] + [dmods[l][1], dmods[l][0]] for l in range(L)]
    payload = _pack([a for lay in per_layer for a in lay] + [dfinal])
    gathered = _gather_small(payload)
    total = _sum8(gathered)
    shapes = [P[k].shape[1:] for k in SMALL] + [(6 * D,), (6 * D,)]
    tot = _unpack(total, shapes * L + [(D,)])
    per = len(shapes)
    g_small = {k: jnp.stack([tot[l * per + q] for l in range(L)]) for q, k in enumerate(SMALL)}
    dmx_sum = jnp.stack([tot[l * per + per - 2] for l in range(L)])
    dmc_sum = jnp.stack([tot[l * per + per - 1] for l in range(L)])
    g_small["b_ada"] = dmx_sum + dmc_sum
    g_small["final_norm_g"] = tot[-1]
    rows = [_tile_rows(s) for s in shapes]
    s16t = s16.T
    ada_out, ds_part = None, jnp.zeros((16, D), F32)
    for l in range(L):
        r0 = l * sum(rows) + sum(rows[:per - 2])
        dmx_all = gathered[:, r0:r0 + rows[per - 2], :].reshape(8, -1)[:, :6 * D]
        dm_full = jnp.concatenate([dmc_sum[l][None], dmx_all, jnp.zeros((7, 6 * D), F32)], axis=0)
        dm = lax.dynamic_slice_in_dim(dm_full, chip * ns_ada, ns_ada, axis=1)
        dw, ds = _ada_bwd(s16t, dm, w_ada, l)
        ds_part = ds_part + ds
        ada_out = _adam_layer(w_ada, Mo["w_ada"], Vo["w_ada"], l, dw, ada_out)
    outs["w_ada"] = ada_out
    ds_all = _gather_small(ds_part[0].reshape(-1, 128)).reshape(8, D)
    d_silu_cc = ds_all[0] + ds_all[2] + ds_all[4] + ds_all[6]
    g_small["c_ctx"] = jax.vjp(jax.nn.silu, c_ctx)[1](d_silu_cc)[0]

    small_names = [k for k in names if k not in BIG and k != "w_ada"]
    sm_shapes = [P[k].shape for k in small_names]
    res = _adam_flat(_pack([P[k] for k in small_names]), _pack([g_small[k] for k in small_names]),
                     _pack([Mo[k] for k in small_names]), _pack([Vo[k] for k in small_names]))
    d_s, m_s, v_s = [_unpack(r, sm_shapes) for r in res]
    for q, k in enumerate(small_names):
        outs[k] = (g_small[k].reshape(P[k].shape), d_s[q], m_s[q], v_s[q])

    return (loss, grad_x, *[outs[k][0] for k in names], *[outs[k][1] for k in names], *[outs[k][2] for k in names],
            *[outs[k][3] for k in names])
```

```python
import functools

import numpy as np
import jax
import jax.numpy as jnp
from jax import lax
from jax.experimental import pallas as pl
from jax.experimental.pallas import tpu as pltpu

F32 = jnp.float32
BF16 = jnp.bfloat16
EPS = 1e-6
CH = 128
GRID_W = 64
ROPE_THETA = 10000.0
POOL_WINDOWS = (2, 4, 8, 16)
PT = 256
VMEM_LIMIT = 56 * 1024 * 1024
MESH = pl.DeviceIdType.MESH
ANY = pl.BlockSpec(memory_space=pl.ANY)

ADAM_LR = 0.001
ADAM_B1 = 0.9
ADAM_B2 = 0.999
ADAM_EPS = 1e-08
ADAM_WD = 0.01
ADAM_STEP = 10

BIG = ("w_in", "w_br", "w_bp", "w_bs", "w_out", "w1", "w2")
SHARD_AXIS = {"w_in": 1, "w_br": 1, "w_bp": 1, "w_bs": 1, "w_out": 0, "w1": 1, "w2": 0}


def _pick(dim, pref, mult=128):
    best = None
    for t in range(mult, min(dim, pref) + 1, mult):
        if dim % t == 0:
            best = t
    return dim if best is None else best


def _cparams(sem=None):
    return pltpu.CompilerParams(dimension_semantics=sem, vmem_limit_bytes=VMEM_LIMIT)


def _rows(i, tr):
    return i * tr + lax.broadcasted_iota(jnp.int32, (tr, 1), 0)


def _sel(i, tr, n_ctx, v2):
    return jnp.where(_rows(i, tr) < n_ctx, v2[0:1, :], v2[1:2, :])


def _seg_sums(i, tr, n_ctx, d):
    is_ctx = _rows(i, tr) < n_ctx
    s_c = jnp.sum(jnp.where(is_ctx, d, 0.0), axis=0, keepdims=True)
    s_x = jnp.sum(jnp.where(is_ctx, 0.0, d), axis=0, keepdims=True)
    two = lax.broadcasted_iota(jnp.int32, (2, d.shape[1]), 0)
    return jnp.where(two == 0, s_c, s_x)


def _dot(a, b, form="NN"):
    dims = {"NN": (((1,), (0,)), ((), ())), "NT": (((1,), (1,)), ((), ())), "TN": (((0,), (0,)), ((), ()))}[form]
    return lax.dot_general(a, b, dims, preferred_element_type=F32)


def _gelu(x):
    return 0.5 * x * (1.0 + jnp.tanh(0.7978845608028654 * (x + 0.044715 * x * x * x)))


def _matmul(name, form, pairs, *, R, C, tr, tc, tk, nk, out_dtypes, epi, extras=(), a_pro=None, comm=None):
    npair, nex, nout = len(pairs), len(extras), len(out_dtypes)
    in_specs, args = [], []
    for a, b in pairs:
        if nk == 1:
            ka = a.shape[0] if form == "TN" else a.shape[1]
        else:
            ka = tk
        if form == "NN":
            in_specs += [pl.BlockSpec((tr, ka), lambda i, j, k: (i, k)), pl.BlockSpec((ka, tc), lambda i, j, k: (k, j))]
        elif form == "NT":
            in_specs += [pl.BlockSpec((tr, ka), lambda i, j, k: (i, k)), pl.BlockSpec((tc, ka), lambda i, j, k: (j, k))]
        else:
            in_specs += [pl.BlockSpec((ka, tr), lambda i, j, k: (k, i)), pl.BlockSpec((ka, tc), lambda i, j, k: (k, j))]
        args += [a, b]
    for kind, arr, off in extras:
        if kind == "tile":
            in_specs.append(pl.BlockSpec((tr, tc), lambda i, j, k, off=off: (i, j + off)))
        else:
            in_specs.append(pl.BlockSpec((2, tc), lambda i, j, k, off=off: (0, j + off)))
        args.append(arr)

    direct = epi is None
    n_acc = 0 if (nk == 1 or direct) else npair
    n_main = len(args)
    ni, nj = R // tr, C // tc
    aliases = {}
    out_specs = [pl.BlockSpec((tr, tc), lambda i, j, k: (i, j)) for _ in out_dtypes]
    out_shape = [jax.ShapeDtypeStruct((R, C), dt) for dt in out_dtypes]
    scratch = [pltpu.VMEM((tr, tc), F32) for _ in range(n_acc)]
    n_cin = n_cout = 0
    if comm is not None:
        n_cin, n_cout = len(comm.ins), len(comm.outs)
        in_specs = in_specs + [ANY] * n_cin
        args = args + list(comm.ins)
        out_specs = out_specs + [ANY] * n_cout
        out_shape = out_shape + list(comm.outs)
        aliases = {n_main + a: nout + b for a, b in comm.aliases.items()}
        scratch = scratch + [pltpu.SemaphoreType.DMA((comm.nsem,)), pltpu.SemaphoreType.DMA((comm.nsem,))]

    def body(*refs):
        ab = refs[:2 * npair]
        ex = refs[2 * npair:n_main]
        cin = refs[n_main:n_main + n_cin]
        outs = refs[n_main + n_cin:n_main + n_cin + nout]
        cout = refs[n_main + n_cin + nout:n_main + n_cin + nout + n_cout]
        accs = refs[n_main + n_cin + nout + n_cout:n_main + n_cin + nout + n_cout + n_acc]
        sems = refs[n_main + n_cin + nout + n_cout + n_acc:]
        i, j, k = pl.program_id(0), pl.program_id(1), pl.program_id(2)

        if comm is not None:
            @pl.when(jnp.logical_and(jnp.logical_and(i == 0, j == 0), k == 0))
            def _():
                comm.start(cin, cout, *sems)

        def products():
            res = []
            for p in range(npair):
                a = ab[2 * p][...]
                if a_pro is not None:
                    a = a_pro(a)
                res.append(_dot(a, ab[2 * p + 1][...], form))
            return res

        def finish(vals):
            res = epi(vals, [e[...] for e in ex], i)
            for o, v in zip(outs, res):
                o[...] = v.astype(o.dtype)

        if direct:
            prod = products()[0]
            if nk == 1:
                outs[0][...] = prod
            else:
                @pl.when(k == 0)
                def _():
                    outs[0][...] = prod

                @pl.when(k > 0)
                def _():
                    outs[0][...] += prod
        elif nk == 1:
            finish(products())
        else:
            prods = products()

            @pl.when(k == 0)
            def _():
                for acc, v in zip(accs, prods):
                    acc[...] = v

            @pl.when(k > 0)
            def _():
                for acc, v in zip(accs, prods):
                    acc[...] += v

            @pl.when(k == nk - 1)
            def _():
                finish([acc[...] for acc in accs])

        if comm is not None:
            @pl.when(jnp.logical_and(jnp.logical_and(i == ni - 1, j == nj - 1), k == nk - 1))
            def _():
                comm.finish(cin, cout, *sems)

    res = pl.pallas_call(
        body, name=name, grid=(ni, nj, nk),
        in_specs=in_specs, out_specs=out_specs, out_shape=out_shape, scratch_shapes=scratch,
        input_output_aliases=aliases,
        compiler_params=_cparams(("arbitrary", "arbitrary", "arbitrary")),
    )(*args)
    return res if comm is None else (res[:nout], res[nout:])


def _rowcall(name, fn, *, nrows, tr, ins, outs, accs=(), ncol=1):
    n_in, n_out, n_acc = len(ins), len(outs), len(accs)
    in_specs, args = [], []
    for arr, kind, w, off in ins:
        if kind == "rows":
            in_specs.append(pl.BlockSpec((tr, w), lambda i, j, off=off: (i, off + j)))
        elif kind == "full":
            in_specs.append(pl.BlockSpec(arr.shape, lambda i, j, nd=arr.ndim: (0,) * nd))
        else:
            in_specs.append(pl.BlockSpec((None,) + arr.shape[1:], lambda i, j, f=off, nd=arr.ndim: (f(i),) + (0,) * (nd - 1)))
        args.append(arr)
    aliases = {}
    out_specs, out_shape = [], []
    for o_idx, (dt, total, w, off, alias) in enumerate(outs):
        out_specs.append(pl.BlockSpec((tr, w), lambda i, j, off=off: (i, off + j)))
        out_shape.append(jax.ShapeDtypeStruct((nrows, total), dt))
        if alias is not None:
            aliases[len(args)] = o_idx
            in_specs.append(ANY)
            args.append(alias)
    n_alias = len(aliases)
    for shp in accs:
        out_specs.append(pl.BlockSpec(shp, lambda i, j, nd=len(shp): (0,) * nd))
        out_shape.append(jax.ShapeDtypeStruct(shp, F32))

    def body(*refs):
        in_refs = refs[:n_in]
        out_refs = refs[n_in + n_alias:n_in + n_alias + n_out]
        acc_refs = refs[n_in + n_alias + n_out:]
        i, j = pl.program_id(0), pl.program_id(1)
        res = fn(i, j, *[r[...] for r in in_refs])
        for o, v in zip(out_refs, res[:n_out]):
            o[...] = v.astype(o.dtype)
        first = jnp.logical_and(i == 0, j == 0)
        for acc, v in zip(acc_refs, res[n_out:]):
            @pl.when(first)
            def _(acc=acc, v=v):
                acc[...] = v

            @pl.when(jnp.logical_not(first))
            def _(acc=acc, v=v):
                acc[...] += v

    res = pl.pallas_call(
        body, name=name, grid=(nrows // tr, ncol),
        in_specs=in_specs, out_specs=out_specs, out_shape=out_shape,
        input_output_aliases=aliases,
        compiler_params=_cparams(("arbitrary", "arbitrary")),
    )(*args)
    return res


def _f_normmod(x, g, shift, scale):
    r = lax.rsqrt(jnp.mean(x * x, axis=-1, keepdims=True) + EPS)
    return (x * r * g) * (1.0 + scale) + shift


def _f_headnorm_gate(o, zg, ng):
    r = lax.rsqrt(jnp.mean(o * o, axis=-1, keepdims=True) + EPS)
    return (o * r * ng) * (zg * jax.nn.sigmoid(zg))


def _f_sgv(sv, g):
    v = _gelu(sv)
    r = lax.rsqrt(jnp.mean(v * v, axis=-1, keepdims=True) + EPS)
    return v * r * g


def _rope(t, cos, sin):
    w = t.shape[1]
    lane = lax.broadcasted_iota(jnp.int32, t.shape, 1)
    swapped = jnp.where(jnp.bitwise_and(lane, 63) < 32,pltpu.roll(t, w - 32, 1), pltpu.roll(t, 32, 1))
    return t * cos + swapped * sin


def _rope_t(d, cos, sin):
    w = d.shape[1]
    lane = lax.broadcasted_iota(jnp.int32, d.shape, 1)
    ds = d * sin
    swapped = jnp.where(jnp.bitwise_and(lane, 63) < 32,pltpu.roll(ds, w - 32, 1), pltpu.roll(ds, 32, 1))
    return d * cos + swapped


class _Cfg:
    def __init__(self, D, T, CTX, DFF):
        self.D, self.T, self.CTX, self.DFF = D, T, CTX, DFF
        self.TA = T + CTX
        self.RW = D // 2
        self.H = self.RW // CH
        self.PW = D // 4
        self.SW = D // 4
        self.NIN = 4 * self.RW + self.PW + 2 * self.SW + 3 * D
        self.NC = self.TA // CH
        self.NCC = CTX // CH
        self.k_scale = float(CH) ** -0.5
        self.tr = _pick(self.TA, 1408)
        self.tr_small = _pick(self.TA, 768)
        self.o_g = 3 * self.RW
        self.o_p = 4 * self.RW
        self.o_u = self.o_p + self.PW
        self.o_sv = self.o_u + self.SW
        self.o_gate = self.o_sv + self.SW


def _normmod_fwd(cfg, X, g, mods, i_shift, i_scale, name):
    D = cfg.D
    tr = _pick(cfg.TA, 384)

    def fn(i, j, x, g, m):
        sh = _sel(i, tr, cfg.CTX, m[:, i_shift * D:(i_shift + 1) * D])
        sc = _sel(i, tr, cfg.CTX, m[:, i_scale * D:(i_scale + 1) * D])
        return (_f_normmod(x, g, sh, sc),)

    return _rowcall(name, fn, nrows=cfg.TA, tr=tr, ins=[(X, "rows", D, 0), (g, "full", 0, 0), (mods, "full", 0, 0)],
                    outs=[(BF16, D, D, 0, None)])[0]


def _normmod_bwd(cfg, X, dH, dres, g, mods, i_shift, i_scale, name):
    D = cfg.D
    tr = _pick(cfg.TA, 384)

    def fn(i, j, x, dh, dr, g, m):
        sh = _sel(i, tr, cfg.CTX, m[:, i_shift * D:(i_shift + 1) * D])
        sc = _sel(i, tr, cfg.CTX, m[:, i_scale * D:(i_scale + 1) * D])
        _, vjp = jax.vjp(_f_normmod, x, g, sh, sc)
        dx, dg, dsh, dsc = vjp(dh)
        return dr + dx, dg, _seg_sums(i, tr, cfg.CTX, dsh), _seg_sums(i, tr, cfg.CTX, dsc)

    return _rowcall(name, fn, nrows=cfg.TA, tr=tr,
                    ins=[(X, "rows", D, 0), (dH, "rows", D, 0), (dres, "rows", D, 0), (g, "full", 0, 0), (mods, "full", 0, 0)],
                    outs=[(F32, D, D, 0, None)], accs=[(1, D), (2, D), (2, D)])


def _resgate_bwd(cfg, dX, M, mods, i_gate, name):
    D = cfg.D
    tr = _pick(cfg.TA, 384)

    def fn(i, j, dx, m, mm):
        gate = _sel(i, tr, cfg.CTX, mm[:, i_gate * D:(i_gate + 1) * D])
        return dx * gate, _seg_sums(i, tr, cfg.CTX, dx * m.astype(F32))

    return _rowcall(name, fn, nrows=cfg.TA, tr=tr, ins=[(dX, "rows", D, 0), (M, "rows", D, 0), (mods, "full", 0, 0)],
                    outs=[(BF16, D, D, 0, None)], accs=[(2, D)])


def _chunk_of(cfg, d, t):
    fwd = t
    bwd = jnp.where(t < cfg.NCC, cfg.NCC - 1 - t, cfg.NC - 1 - t + cfg.NCC)
    return jnp.where(d == 0, fwd, bwd)


def _ret_specs(cfg, cm):
    RW, H = cfg.RW, cfg.H
    return [
        pl.BlockSpec((CH, RW), lambda d, t: (cm(d, t), 0)),
        pl.BlockSpec((CH, RW), lambda d, t: (cm(d, t), 1)),
        pl.BlockSpec((CH, RW), lambda d, t: (cm(d, t), 2)),
        pl.BlockSpec((CH, CH), lambda d, t: (cm(d, t), 0)),
        pl.BlockSpec((CH, CH), lambda d, t: (cm(d, t), 0)),
        pl.BlockSpec((None, H, CH, CH), lambda d, t: (d, 0, 0, 0)),
        pl.BlockSpec((None, CH, RW), lambda d, t: (d, 0, 0)),
        pl.BlockSpec((None, CH, RW), lambda d, t: (d, 0, 0)),
        pl.BlockSpec((None, 1, RW), lambda d, t: (d, 0, 0)),
    ]


def _ret_prep(cfg, q_ref, k_ref, cos, sin, qd_ref, kd_ref, sl):
    qr = _rope(q_ref[:, sl].astype(F32), cos, sin)
    kr = _rope(k_ref[:, sl].astype(F32) * cfg.k_scale, cos, sin)
    return qr, kr, (qr * qd_ref[:, sl]).astype(BF16), (kr * kd_ref[:, sl]).astype(BF16)


def _ret_fwd(cfg, Z, tabs, decs):
    RW, H, TA, NC = cfg.RW, cfg.H, cfg.TA, cfg.NC
    cm = functools.partial(_chunk_of, cfg)

    def body(q_ref, k_ref, v_ref, ct_ref, st_ref, dm_ref, qd_ref, kd_ref, cd_ref, o_ref, so_ref, S):
        @pl.when(pl.program_id(1) == 0)
        def _():
            S[...] = jnp.zeros_like(S)

        cos, sin = ct_ref[...], st_ref[...]
        for h in range(H):
            sl = slice(h * CH, (h + 1) * CH)
            qr, kr, qd, kd = _ret_prep(cfg, q_ref, k_ref, cos, sin, qd_ref, kd_ref, sl)
            v = v_ref[:, sl].astype(BF16)
            p = (_dot(qr.astype(BF16), kr.astype(BF16), "NT") * dm_ref[h]).astype(BF16)
            s_h = S[h]
            so_ref[h] = s_h
            o_ref[:, sl] = _dot(p, v) + _dot(qd, s_h.astype(BF16))
            S[h] = s_h * cd_ref[:, sl] + _dot(kd, v, "TN")

    return pl.pallas_call(
        body, name="ret_fwd", grid=(2, NC),
        in_specs=_ret_specs(cfg, cm),
        out_specs=[pl.BlockSpec((None, CH, RW), lambda d, t: (d, cm(d, t), 0)),
                   pl.BlockSpec((None, None, H, CH, CH), lambda d, t: (d, cm(d, t), 0, 0, 0))],
        out_shape=[jax.ShapeDtypeStruct((2, TA, RW), F32), jax.ShapeDtypeStruct((2, NC, H, CH, CH), F32)],
        scratch_shapes=[pltpu.VMEM((H, CH, CH), F32)],
        compiler_params=_cparams(("arbitrary", "arbitrary")),
    )(Z, Z, Z, tabs["cos"], tabs["sin"], decs["dmat"], decs["qdec"], decs["kdec"], decs["cdec"])


def _ret_bwd(cfg, Z, tabs, decs, states, dO):
    RW, H, TA, NC = cfg.RW, cfg.H, cfg.TA, cfg.NC

    def cm(d, t):
        return _chunk_of(cfg, d, NC - 1 - t)

    def body(q_ref, k_ref, v_ref, ct_ref, st_ref, dm_ref, qd_ref, kd_ref, cd_ref, s_ref, do_ref, w_ref,
             dqkv_ref, dl_ref, dS):
        t = pl.program_id(1)

        @pl.when(t == 0)
        def _():
            dS[...] = jnp.zeros_like(dS)
            dl_ref[...] = jnp.zeros_like(dl_ref)

        cos, sin = ct_ref[...], st_ref[...]
        for h in range(H):
            sl = slice(h * CH, (h + 1) * CH)
            qh, kh, qd, kd = _ret_prep(cfg, q_ref, k_ref, cos, sin, qd_ref, kd_ref, sl)
            qb, kb = qh.astype(BF16), kh.astype(BF16)
            v = v_ref[:, sl].astype(BF16)
            dob = do_ref[:, sl].astype(BF16)
            dm = dm_ref[h]
            p = (_dot(qb, kb, "NT") * dm).astype(BF16)
            dp = (_dot(dob, v, "NT") * dm).astype(BF16)
            s_h = s_ref[h]
            ds_h = dS[h]
            sb, dsb = s_h.astype(BF16), ds_h.astype(BF16)
            dq_i = _dot(dp, kb)
            dk_i = _dot(dp, qb, "TN")
            dq_c = _dot(dob, sb, "NT") * qd_ref[:, sl]
            dk_s = _dot(v, dsb, "NT") * kd_ref[:, sl]
            dqkv_ref[:, sl] = dq_i + dq_c
            dqkv_ref[:, RW + h * CH:RW + (h + 1) * CH] = dk_i + dk_s
            dqkv_ref[:, 2 * RW + h * CH:2 * RW + (h + 1) * CH] = _dot(p, dob, "TN") + _dot(kd, dsb)
            lam = w_ref[0] * (qh * dq_i) + w_ref[1] * (qh * dq_c) + w_ref[2] * (kh * dk_i) + w_ref[3] * (kh * dk_s)
            lam_s = float(CH) * cd_ref[:, sl] * jnp.sum(ds_h * s_h, axis=0, keepdims=True)
            dl_ref[:, sl] += jnp.sum(lam, axis=0, keepdims=True) + lam_s
            dS[h] = ds_h * cd_ref[:, sl] + _dot(qd, dob, "TN")

    in_specs = _ret_specs(cfg, cm) + [
        pl.BlockSpec((None, None, H, CH, CH), lambda d, t: (d, cm(d, t), 0, 0, 0)),
        pl.BlockSpec((CH, RW), lambda d, t: (cm(d, t), 0)),
        pl.BlockSpec((None, 4, CH, CH), lambda d, t: (d, 0, 0, 0)),
    ]
    return pl.pallas_call(
        body, name="ret_bwd", grid=(2, NC),
        in_specs=in_specs,
        out_specs=[pl.BlockSpec((None, CH, 3 * RW), lambda d, t: (d, cm(d, t), 0)),
                   pl.BlockSpec((None, 1, RW), lambda d, t: (d, 0, 0))],
        out_shape=[jax.ShapeDtypeStruct((2, TA, 3 * RW), F32), jax.ShapeDtypeStruct((2, 1, RW), F32)],
        scratch_shapes=[pltpu.VMEM((H, CH, CH), F32)],
        compiler_params=_cparams(("arbitrary", "arbitrary")),
    )(Z, Z, Z, tabs["cos"], tabs["sin"], decs["dmat"], decs["qdec"], decs["kdec"], decs["cdec"], states, dO,
      tabs["lamw"])


def _rope_bwd(cfg, dqkv2, tabs, dz):
    RW, H = cfg.RW, cfg.H
    tr = PT

    def body(d0, d1, ct, st, dz_in, o):
        cos, sin = jnp.tile(ct[...], (1, H)), jnp.tile(st[...], (1, H))
        d = d0[...] + d1[...]
        o[:, :RW] = _rope_t(d[:, :RW], cos, sin).astype(o.dtype)
        o[:, RW:2 * RW] = (_rope_t(d[:, RW:2 * RW], cos, sin) * cfg.k_scale).astype(o.dtype)
        o[:, 2 * RW:] = d[:, 2 * RW:].astype(o.dtype)

    return pl.pallas_call(
        body, name="rope_bwd", grid=(cfg.TA // tr,),
        in_specs=[pl.BlockSpec((None, tr, 3 * RW), lambda i: (0, i, 0)), pl.BlockSpec((None, tr, 3 * RW), lambda i: (1, i, 0)),
                  pl.BlockSpec((tr, CH), lambda i: (i, 0)), pl.BlockSpec((tr, CH), lambda i: (i, 0)), ANY],
        out_specs=pl.BlockSpec((tr, 3 * RW), lambda i: (i, 0)),
        out_shape=jax.ShapeDtypeStruct((cfg.TA, cfg.NIN), BF16),
        input_output_aliases={4: 0},
        compiler_params=_cparams(("arbitrary",)),
    )(dqkv2, dqkv2, tabs["cos"], tabs["sin"], dz)


def _retout_fwd(cfg, o2, Z, ng):
    RW, H = cfg.RW, cfg.H
    tr = PT

    def body(o0, o1, zg, ng, out):
        o = o0[...] + o1[...]
        z = zg[...].astype(F32)
        for h in range(H):
            sl = slice(h * CH, (h + 1) * CH)
            out[:, sl] = _f_headnorm_gate(o[:, sl], z[:, sl], ng[:, sl]).astype(out.dtype)

    return pl.pallas_call(
        body, name="retout_fwd", grid=(cfg.TA // tr,),
        in_specs=[pl.BlockSpec((None, tr, RW), lambda i: (0, i, 0)), pl.BlockSpec((None, tr, RW), lambda i: (1, i, 0)),
                  pl.BlockSpec((tr, RW), lambda i: (i, 3)), pl.BlockSpec((1, RW), lambda i: (0, 0))],
        out_specs=pl.BlockSpec((tr, RW), lambda i: (i, 0)),
        out_shape=jax.ShapeDtypeStruct((cfg.TA, RW), BF16),
        compiler_params=_cparams(("arbitrary",)),
    )(o2, o2, Z, ng)


def _retout_bwd(cfg, o2, Z, ng, dret, dz):
    RW, H = cfg.RW, cfg.H
    tr = PT

    def body(o0, o1, zg, ng, dr, dz_in, do_out, dz_out, dng):
        i = pl.program_id(0)
        o = o0[...] + o1[...]
        z = zg[...].astype(F32)
        d = dr[...]
        acc = []
        for h in range(H):
            sl = slice(h * CH, (h + 1) * CH)
            _, vjp = jax.vjp(_f_headnorm_gate, o[:, sl], z[:, sl], ng[:, sl])
            do_h, dz_h, dg_h = vjp(d[:, sl])
            do_out[:, sl] = do_h
            dz_out[:, sl] = dz_h.astype(dz_out.dtype)
            acc.append(dg_h)

        @pl.when(i == 0)
        def _():
            for h in range(H):
                dng[:, h * CH:(h + 1) * CH] = acc[h]

        @pl.when(i > 0)
        def _():
            for h in range(H):
                dng[:, h * CH:(h + 1) * CH] += acc[h]

    return pl.pallas_call(
        body, name="retout_bwd", grid=(cfg.TA // tr,),
        in_specs=[pl.BlockSpec((None, tr, RW), lambda i: (0, i, 0)), pl.BlockSpec((None, tr, RW), lambda i: (1, i, 0)),
                  pl.BlockSpec((tr, RW), lambda i: (i, 3)), pl.BlockSpec((1, RW), lambda i: (0, 0)),
                  pl.BlockSpec((tr, RW), lambda i: (i, 0)), ANY],
        out_specs=[pl.BlockSpec((tr, RW), lambda i: (i, 0)), pl.BlockSpec((tr, RW), lambda i: (i, 3)),
                   pl.BlockSpec((1, RW), lambda i: (0, 0))],
        out_shape=[jax.ShapeDtypeStruct((cfg.TA, RW), F32), jax.ShapeDtypeStruct((cfg.TA, cfg.NIN), BF16),
                   jax.ShapeDtypeStruct((1, RW), F32)],
        input_output_aliases={5: 1},
        compiler_params=_cparams(("arbitrary",)),
    )(o2, o2, Z, ng, dret, dz)


def _pool_consts(ctx_len):
    assert ctx_len == PT
    bm = np.zeros((2, len(POOL_WINDOWS), PT, PT), np.float32)
    ic = np.zeros((2, len(POOL_WINDOWS), PT, CH), np.float32)
    for ty, seg in enumerate((ctx_len, GRID_W)):
        for gi, w in enumerate(POOL_WINDOWS):
            for r in range(PT):
                s0, pos = (r // seg) * seg, r % seg
                lo, hi = max(pos - w // 2, 0), min(pos + w // 2 - 1, seg - 1)
                bm[ty, gi, r, s0 + lo:s0 + hi + 1] = 1.0
                ic[ty, gi, r, :] = 1.0 / (hi - lo + 1)
    return jnp.asarray(bm, BF16), jnp.asarray(ic, F32)


def _pool_tile(p, bm, ic, pw, g):
    sl = slice(g * CH, (g + 1) * CH)
    pg = p[:, sl].astype(F32)
    hi = pg.astype(BF16)
    lo = (pg - hi.astype(F32)).astype(BF16)
    y = (_dot(bm[g], hi) + _dot(bm[g], lo)) * ic[g] - pg
    return y, _dot(y.astype(BF16), pw[g].astype(BF16))


def _pool_fwd(cfg, Z, consts, pool_w, pool_scale):
    PW = cfg.PW
    G = PW // CH
    nct = cfg.CTX // PT
    ty = lambda i: jnp.where(i < nct, 0, 1)

    def fn(i, j, p, bm, ic, pw, ps):
        outs = [_pool_tile(p, bm, ic, pw, g)[1] for g in range(G)]
        return (jnp.concatenate(outs, axis=1) * ps,)

    return _rowcall("pool_fwd", fn, nrows=cfg.TA, tr=PT,
                    ins=[(Z, "rows", PW, cfg.o_p // PW), (consts[0], "sel", 0, ty), (consts[1], "sel", 0, ty),
                         (pool_w, "full", 0, 0), (pool_scale, "full", 0, 0)],
                    outs=[(BF16, PW, PW, 0, None)])[0]


def _pool_bwd(cfg, Z, consts, pool_w, pool_scale, dpool, dz):
    PW = cfg.PW
    G = PW // CH
    nct = cfg.CTX // PT
    ty = lambda i: jnp.where(i < nct, 0, 1)

    def fn(i, j, p, bm, ic, pw, ps, dout):
        dps, dps_acc, dpw = [], [], []
        for g in range(G):
            sl = slice(g * CH, (g + 1) * CH)
            y, lin = _pool_tile(p, bm, ic, pw, g)
            dlin = (dout[:, sl] * ps[:, sl]).astype(BF16)
            dps_acc.append(jnp.sum(dout[:, sl] * lin, axis=0, keepdims=True))
            dy = _dot(dlin, pw[g].astype(BF16), "NT")
            dpw.append(_dot(y.astype(BF16), dlin, "TN"))
            t = dy * ic[g]
            hi = t.astype(BF16)
            lo = (t - hi.astype(F32)).astype(BF16)
            dps.append(_dot(bm[g], hi, "TN") + _dot(bm[g], lo, "TN") - dy)
        return (jnp.concatenate(dps, axis=1), jnp.stack(dpw), jnp.concatenate(dps_acc, axis=1))

    return _rowcall("pool_bwd", fn, nrows=cfg.TA, tr=PT,
                    ins=[(Z, "rows", PW, cfg.o_p // PW), (consts[0], "sel", 0, ty), (consts[1], "sel", 0, ty),
                         (pool_w, "full", 0, 0), (pool_scale, "full", 0, 0), (dpool, "rows", PW, 0)],
                    outs=[(BF16, cfg.NIN, PW, cfg.o_p // PW, dz)], accs=[(G, CH, CH), (1, PW)])


def _sg_mixed(vn, sw, sbb, g, c):
    rows = slice(c * CH, (c + 1) * CH)
    sl = slice(g * CH, (g + 1) * CH)
    return _dot(sw[g].astype(BF16), vn[rows, sl].astype(BF16)) + sbb[g]


def _sg_fwd(cfg, Z, sng, sw, sbb):
    SW = cfg.SW
    G = SW // CH

    def fn(i, j, u, sv, sng, sw, sbb):
        ug = _gelu(u.astype(F32))
        vn = _f_sgv(sv.astype(F32), sng)
        rows = []
        for c in range(PT // CH):
            mixed = jnp.concatenate([_sg_mixed(vn, sw, sbb, g, c) for g in range(G)], axis=1)
            rows.append(ug[c * CH:(c + 1) * CH, :] * mixed)
        return (jnp.concatenate(rows, axis=0),)

    return _rowcall("sg_fwd", fn, nrows=cfg.TA, tr=PT,
                    ins=[(Z, "rows", SW, cfg.o_u // SW), (Z, "rows", SW, cfg.o_sv // SW), (sng, "full", 0, 0),
                         (sw, "full", 0, 0), (sbb, "full", 0, 0)],
                    outs=[(BF16, SW, SW, 0, None)])[0]


def _sg_bwd(cfg, Z, sng, sw, sbb, dsg, dz):
    SW = cfg.SW
    G = SW // CH

    def fn(i, j, u, sv, sng, sw, sbb, dout):
        uf, svf = u.astype(F32), sv.astype(F32)
        ug, vjp_u = jax.vjp(_gelu, uf)
        vn, vjp_v = jax.vjp(_f_sgv, svf, sng)
        dug_rows, dvn_rows = [], []
        dsw = [jnp.zeros((CH, CH), F32) for _ in range(G)]
        dsb = [jnp.zeros((CH, CH), F32) for _ in range(G)]
        for c in range(PT // CH):
            rows = slice(c * CH, (c + 1) * CH)
            dug_g, dvn_g = [], []
            for g in range(G):
                sl = slice(g * CH, (g + 1) * CH)
                mixed = _sg_mixed(vn, sw, sbb, g, c)
                dmixed = dout[rows, sl] * ug[rows, sl]
                dug_g.append(dout[rows, sl] * mixed)
                dmb = dmixed.astype(BF16)
                dvn_g.append(_dot(sw[g].astype(BF16), dmb, "TN"))
                dsw[g] = dsw[g] + _dot(dmb, vn[rows, sl].astype(BF16), "NT")
                dsb[g] = dsb[g] + jnp.broadcast_to(jnp.sum(dmixed, axis=1, keepdims=True), (CH, CH))
            dug_rows.append(jnp.concatenate(dug_g, axis=1))
            dvn_rows.append(jnp.concatenate(dvn_g, axis=1))
        (du,) = vjp_u(jnp.concatenate(dug_rows, axis=0))
        dsv, dsng = vjp_v(jnp.concatenate(dvn_rows, axis=0))
        return du, dsv, jnp.stack(dsw), jnp.stack(dsb), dsng

    return _rowcall("sg_bwd", fn, nrows=cfg.TA, tr=PT,
                    ins=[(Z, "rows", SW, cfg.o_u // SW), (Z, "rows", SW, cfg.o_sv // SW), (sng, "full", 0, 0),
                         (sw, "full", 0, 0), (sbb, "full", 0, 0), (dsg, "rows", SW, 0)],
                    outs=[(BF16, cfg.NIN, SW, cfg.o_u // SW, dz), (BF16, SW, SW, 0, None)],
                    accs=[(G, CH, CH), (G, CH, CH), (1, SW)])


def _gate_bwd(cfg, b, dY, L, Z, dz):
    D = cfg.D
    tc = 512
    nj = D // tc
    off = cfg.o_gate // tc + b * nj

    def fn(i, j, dy, l, z):
        s = jax.nn.sigmoid(z.astype(F32))
        return dy * s, dy * l.astype(F32) * s * (1.0 - s)

    return _rowcall("gate_bwd", fn, nrows=cfg.TA, tr=cfg.tr,
                    ins=[(dY, "rows", tc, 0), (L, "rows", tc, 0), (Z, "rows", tc, off)],
                    outs=[(BF16, D, tc, 0, None), (BF16, cfg.NIN, tc, off, dz)], ncol=nj)


def _copy_cols(cfg, src, dz, col0):
    w = src.shape[1]
    return _rowcall("copy_cols", lambda i, j, s: (s,), nrows=cfg.TA, tr=cfg.tr, ins=[(src, "rows", w, 0)],
                    outs=[(dz.dtype, dz.shape[1], w, col0 // w, dz)])[0]


def _final(cfg, X, g, target):
    D = cfg.D
    tr = PT
    nct = cfg.CTX // tr

    def body(x_ref, g_ref, t_ref, dx_ref, loss_ref, dg_ref):
        i = pl.program_id(0)

        def f(x, g):
            r = lax.rsqrt(jnp.mean(x * x, axis=-1, keepdims=True) + EPS)
            return x * r * g

        y, vjp = jax.vjp(f, x_ref[...], g_ref[...])
        err = y - t_ref[...]
        dx, dg = vjp(err * (1.0 / D))
        part = 0.5 * jnp.sum(jnp.mean(err * err, axis=-1, keepdims=True), axis=0, keepdims=True)

        @pl.when(i == 0)
        def _():
            loss_ref[...] = jnp.zeros_like(loss_ref)
            dg_ref[...] = jnp.zeros_like(dg_ref)

        @pl.when(i < nct)
        def _():
            dx_ref[...] = jnp.zeros_like(dx_ref)

        @pl.when(i >= nct)
        def _():
            dx_ref[...] = dx
            loss_ref[...] += jnp.broadcast_to(part, loss_ref.shape)
            dg_ref[...] += dg

    return pl.pallas_call(
        body, name="final", grid=(cfg.TA // tr,),
        in_specs=[pl.BlockSpec((tr, D), lambda i: (i, 0)), pl.BlockSpec((1, D), lambda i: (0, 0)),
                  pl.BlockSpec((tr, D), lambda i: (jnp.maximum(i - nct, 0), 0))],
        out_specs=[pl.BlockSpec((tr, D), lambda i: (i, 0)), pl.BlockSpec((8, CH), lambda i: (0, 0)),
                   pl.BlockSpec((1, D), lambda i: (0, 0))],
        out_shape=[jax.ShapeDtypeStruct((cfg.TA, D), F32), jax.ShapeDtypeStruct((8, CH), F32),
                   jax.ShapeDtypeStruct((1, D), F32)],
        compiler_params=_cparams(("arbitrary",)),
    )(X, g, target)


GATHER_A = ("w_in", "w1")
GATHER_B = ("w2", "w_out", "w_br", "w_bp", "w_bs")
REDUCE_MLP = ("w2", "w1")
REDUCE_MIX = ("w_out", "w_br", "w_bp", "w_bs", "w_in")


def _hosted(res, comm, sink):
    if comm is None:
        return res
    main, outs = res
    sink(outs)
    return main


def _layer_fwd(cfg, X, mods, W, sm, tabs, decs, consts, nxt=None, own=None):
    D, TA, tr, RW, PW, SW, DFF, NIN = cfg.D, cfg.TA, cfg.tr, cfg.RW, cfg.PW, cfg.SW, cfg.DFF, cfg.NIN
    ident = lambda accs, ex, i: (accs[0],)
    rest = tuple(k for k in BIG if k != "w_in")
    if own is not None:
        plan = {"z_mm": (own, "ici", rest), "w1_mm": (nxt, "ici", ("w_in", "w_out", "w_br", "w_bp", "w_bs")),
                "w2_mm": (nxt, "ici", ("w1", "w2"))}
    else:
        plan = {"z_mm": (nxt, "ici", GATHER_A), "w1_mm": (nxt, "ici", GATHER_B), "w2_mm": (nxt, "d2d", BIG)}

    def carry(name):
        gather, hop, names = plan[name]
        if gather is None:
            return None, None
        comm = gather.ici(names) if hop == "ici" else gather.d2d(names)
        return comm, lambda o: gather.update(names, o)

    H1 = _normmod_fwd(cfg, X, sm["norm1_g"], mods, 0, 1, "normmod1_fwd")
    comm, sink = carry("z_mm")
    (Z,) = _hosted(_matmul("z_mm", "NN", [(H1, W["w_in"])], R=TA, C=NIN, tr=_pick(TA, 2816), tc=512, tk=D, nk=1, out_dtypes=[F32],
                           epi=ident, comm=comm), comm, sink)
    if own is not None:
        own.update(rest, _run_comm("gather_d2d", own.d2d(rest)))
    o2, states = _ret_fwd(cfg, Z, tabs, decs)
    ret = _retout_fwd(cfg, o2, Z, sm["ret_norm_g"])
    pool = _pool_fwd(cfg, Z, consts, sm["pool_w"], sm["pool_scale"])
    sg = _sg_fwd(cfg, Z, sm["sg_norm_g"], sm["sg_w"], sm["sg_bb"])

    tc = 512
    goff = cfg.o_gate // tc

    def epi_branch(accs, ex, i):
        y = sum(jax.nn.sigmoid(z.astype(F32)) * a for a, z in zip(accs, ex))
        return (y, accs[0], accs[1], accs[2])

    Y, Lr, Lp, Ls = _matmul("branch_mm", "NN", [(ret, W["w_br"]), (pool, W["w_bp"]), (sg, W["w_bs"])], R=TA, C=D, tr=cfg.tr_small,
                            tc=tc, tk=0, nk=1, out_dtypes=[BF16] * 4, epi=epi_branch,
                            extras=[("tile", Z, goff + b * (D // tc)) for b in range(3)])

    def epi_res(rows):
        return lambda accs, ex, i: (ex[0] + _sel(i, rows, cfg.CTX, ex[1]) * accs[0], accs[0])

    X2, O = _matmul("out_mm", "NN", [(Y, W["w_out"])], R=TA, C=D, tr=tr, tc=tc, tk=D, nk=1, out_dtypes=[F32, BF16],
                    epi=epi_res(tr), extras=[("tile", X, 0), ("rows2", mods, 2 * (D // tc))])
    H2 = _normmod_fwd(cfg, X2, sm["norm2_g"], mods, 3, 4, "normmod2_fwd")
    tcf = _pick(DFF, 1024)
    def epi_relu(accs, ex, i):
        r = jnp.maximum(accs[0], 0.0)
        return (r * r, r)

    comm, sink = carry("w1_mm")
    A2, Rr = _hosted(_matmul("w1_mm", "NN", [(H2, W["w1"])], R=TA, C=DFF, tr=tr, tc=tcf, tk=D, nk=1, out_dtypes=[BF16, BF16],
                             epi=epi_relu, comm=comm), comm, sink)
    comm, sink = carry("w2_mm")
    tr2 = _pick(TA, 704)
    X3, M = _hosted(_matmul("w2_mm", "NN", [(A2, W["w2"])], R=TA, C=D, tr=tr2, tc=tc, tk=DFF, nk=1,
                            out_dtypes=[F32, BF16], epi=epi_res(tr2), extras=[("tile", X2, 0), ("rows2", mods, 5 * (D // tc))],
                            comm=comm), comm, sink)
    if own is not None and nxt is not None:
        nxt.update(BIG, _run_comm("gather_d2d", nxt.d2d(BIG)))
    saved = dict(X=X, H1=H1, Z=Z, o2=o2, states=states, ret=ret, pool=pool, sg=sg, Y=Y, L=(Lr, Lp, Ls), O=O, X2=X2,
                 H2=H2, R=Rr, A2=A2, M=M)
    return X3, saved


def _layer_bwd(cfg, dX3, sv, mods, W, sm, tabs, decs, consts, rs, make_reduce):
    D, TA, tr, RW, PW, SW, DFF, NIN = cfg.D, cfg.TA, cfg.tr, cfg.RW, cfg.PW, cfg.SW, cfg.DFF, cfg.NIN
    ident = lambda accs, ex, i: (accs[0],)
    tw = 512
    g = {}

    dM, dgate2 = _resgate_bwd(cfg, dX3, sv["M"], mods, 5, "resgate2_bwd")
    tcf = _pick(DFF, 1024)
    comm = rs.swap() if rs else None
    (dPre,) = _hosted(_matmul("dpre_mm", "NT", [(dM, W["w2"])], R=TA, C=DFF, tr=tr, tc=tcf, tk=D, nk=1, out_dtypes=[BF16],
                              epi=lambda accs, ex, i: (accs[0] * (2.0 * ex[0].astype(F32)),), extras=[("tile", sv["R"], 0)],
                              comm=comm), comm, lambda o: rs.swapped(o))
    comm = rs.scatter() if rs else None
    (g["w2"],) = _hosted(_matmul("dw2_mm", "TN", [(sv["A2"], dM)], R=DFF, C=D, tr=tw, tc=tw, tk=TA, nk=1, out_dtypes=[F32],
                                 epi=None, comm=comm), comm, lambda o: rs.scattered(o))
    comm = rs.share() if rs else None
    (g["w1"],) = _hosted(_matmul("dw1_mm", "TN", [(sv["H2"], dPre)], R=D, C=DFF, tr=tw, tc=tw, tk=TA, nk=1, out_dtypes=[F32],
                                 epi=None, comm=comm), comm, lambda o: rs.shared(o))
    mlp = make_reduce(REDUCE_MLP, {k: g.pop(k) for k in REDUCE_MLP})
    tkf = _pick(DFF, 2048)
    comm = mlp.swap()
    (dH2,) = _hosted(_matmul("dh2_mm", "NT", [(dPre, W["w1"])], R=TA, C=D, tr=tr, tc=1024, tk=tkf, nk=DFF // tkf,
                             out_dtypes=[F32], epi=ident, comm=comm), comm, lambda o: mlp.swapped(o))
    dX2, dn2, dsh2, dsc2 = _normmod_bwd(cfg, sv["X2"], dH2, dX3, sm["norm2_g"], mods, 3, 4, "normmod2_bwd")

    dO, dgate1 = _resgate_bwd(cfg, dX2, sv["O"], mods, 2, "resgate1_bwd")
    (dY,) = _matmul("dy_mm", "NT", [(dO, W["w_out"])], R=TA, C=D, tr=tr, tc=1024, tk=D, nk=1, out_dtypes=[F32], epi=ident)
    (g["w_out"],) = _matmul("dwout_mm", "TN", [(sv["Y"], dO)], R=D, C=D, tr=tw, tc=tw, tk=TA, nk=1,
                            out_dtypes=[F32], epi=None)
    dLr, dz = _gate_bwd(cfg, 0, dY, sv["L"][0], sv["Z"], None)
    dLp, dz = _gate_bwd(cfg, 1, dY, sv["L"][1], sv["Z"], dz)
    dLs, dz = _gate_bwd(cfg, 2, dY, sv["L"][2], sv["Z"], dz)

    (dret,) = _matmul("dret_mm", "NT", [(dLr, W["w_br"])], R=TA, C=RW, tr=tr, tc=RW, tk=D, nk=1, out_dtypes=[F32], epi=ident)
    (dpool,) = _matmul("dpool_mm", "NT", [(dLp, W["w_bp"])], R=TA, C=PW, tr=tr, tc=PW, tk=D, nk=1, out_dtypes=[F32], epi=ident)
    (dsg,) = _matmul("dsg_mm", "NT", [(dLs, W["w_bs"])], R=TA, C=SW, tr=tr, tc=SW, tk=D, nk=1, out_dtypes=[F32], epi=ident)
    (g["w_br"],) = _matmul("dwbr_mm", "TN", [(sv["ret"], dLr)], R=RW, C=D, tr=tw, tc=tw, tk=TA, nk=1, out_dtypes=[F32],
                           epi=None)
    (g["w_bp"],) = _matmul("dwbp_mm", "TN", [(sv["pool"], dLp)], R=PW, C=D, tr=tw, tc=tw, tk=TA, nk=1, out_dtypes=[F32],
                           epi=None)
    (g["w_bs"],) = _matmul("dwbs_mm", "TN", [(sv["sg"], dLs)], R=SW, C=D, tr=tw, tc=tw, tk=TA, nk=1, out_dtypes=[F32],
                           epi=None)
    dOr, dz, dretng = _retout_bwd(cfg, sv["o2"], sv["Z"], sm["ret_norm_g"], dret, dz)
    dqkv2, dlam = _ret_bwd(cfg, sv["Z"], tabs, decs, sv["states"], dOr)
    dz = _rope_bwd(cfg, dqkv2, tabs, dz)
    dz, dpw, dps = _pool_bwd(cfg, sv["Z"], consts, sm["pool_w"], sm["pool_scale"], dpool, dz)
    dz, dz_sv, dsw, dsb, dsng = _sg_bwd(cfg, sv["Z"], sm["sg_norm_g"], sm["sg_w"], sm["sg_bb"], dsg, dz)
    dz = _copy_cols(cfg, dz_sv, dz, cfg.o_sv)

    tkz = _pick(NIN, 2944)
    comm = mlp.scatter()
    (dH1,) = _hosted(_matmul("dh1_mm", "NT", [(dz, W["w_in"])], R=TA, C=D, tr=tr, tc=1024, tk=tkz, nk=NIN // tkz,
                             out_dtypes=[F32], epi=ident, comm=comm), comm, lambda o: mlp.scattered(o))
    comm = mlp.share()
    (g["w_in"],) = _hosted(_matmul("dwin_mm", "TN", [(sv["H1"], dz)], R=D, C=NIN, tr=tw, tc=tw, tk=TA, nk=1,
                                   out_dtypes=[F32], epi=None, comm=comm), comm, lambda o: mlp.shared(o))
    dX, dn1, dsh1, dsc1 = _normmod_bwd(cfg, sv["X"], dH1, dX2, sm["norm1_g"], mods, 0, 1, "normmod1_bwd")
    dmods = jnp.concatenate([dsh1, dsc1, dgate1, dsh2, dsc2, dgate2], axis=1)
    small = dict(norm1_g=dn1, norm2_g=dn2, ret_norm_g=dretng, pool_w=dpw, pool_scale=dps, sg_norm_g=dsng, sg_w=dsw,
                 sg_b=dsb[:, :, 0], dlam=dlam)
    return dX, g, mlp.result, small, dmods


def _tables(cfg):
    nf = CH // 4
    inv = ROPE_THETA ** (-jnp.arange(nf, dtype=F32) / nf)
    tok = jnp.arange(cfg.T)
    ar = (tok // GRID_W).astype(F32)[:, None] * inv[None]
    ac = (tok % GRID_W).astype(F32)[:, None] * inv[None]
    cos = jnp.concatenate([jnp.cos(ar), jnp.cos(ar), jnp.cos(ac), jnp.cos(ac)], axis=1)
    sin = jnp.concatenate([-jnp.sin(ar), jnp.sin(ar), -jnp.sin(ac), jnp.sin(ac)], axis=1)
    cos = jnp.concatenate([jnp.ones((cfg.CTX, CH), F32), cos], axis=0)
    sin = jnp.concatenate([jnp.zeros((cfg.CTX, CH), F32), sin], axis=0)
    idx = np.broadcast_to(np.arange(CH, dtype=np.float32)[:, None], (CH, CH))
    lamw = np.stack([np.stack([idx, idx + 1.0, -idx, CH - 1.0 - idx]), np.stack([-idx, CH - idx, idx, idx])])
    return dict(cos=cos, sin=sin, lamw=jnp.asarray(lamw, F32))


def _decays(cfg, logit):
    H, RW = cfg.H, cfg.RW
    lam = jax.nn.log_sigmoid(logit.astype(F32))
    idx = jnp.arange(CH, dtype=F32)
    dist = idx[:, None] - idx[None, :]
    d0 = jnp.where(dist >= 0, jnp.exp(lam[0][:, None, None] * jnp.maximum(dist, 0.0)), 0.0)
    d1 = jnp.where(dist <= 0, jnp.exp(lam[1][:, None, None] * jnp.maximum(-dist, 0.0)), 0.0)
    lanes = lambda a: jnp.repeat(a.T, CH, axis=1)
    qdec = jnp.stack([lanes(jnp.exp(lam[0][:, None] * (idx + 1.0)[None])), lanes(jnp.exp(lam[1][:, None] * (CH - idx)[None]))])
    kdec = jnp.stack([lanes(jnp.exp(lam[0][:, None] * (CH - 1.0 - idx)[None])), lanes(jnp.exp(lam[1][:, None] * idx[None]))])
    cdec = jnp.repeat(jnp.exp(lam * CH), CH, axis=1)[:, None, :]
    return dict(dmat=jnp.stack([d0, d1]), qdec=qdec, kdec=kdec, cdec=cdec)


def _small_of_layer(small_w, l):
    sm = {k: v[l] for k, v in small_w.items()}
    sm["norm1_g"] = sm["norm1_g"][None]
    sm["norm2_g"] = sm["norm2_g"][None]
    sm["ret_norm_g"] = sm["ret_norm_g"][None]
    sm["pool_scale"] = sm["pool_scale"][None]
    sm["sg_norm_g"] = sm["sg_norm_g"][None]
    sm["sg_bb"] = jnp.broadcast_to(sm["sg_b"][:, :, None], sm["sg_b"].shape + (CH,))
    return sm


def _local_fwd_bwd(cfg, X0, target, mods, gathers, make_reduce, small_w, final_g):
    depth = len(mods)
    tabs = _tables(cfg)
    consts = _pool_consts(cfg.CTX)
    X, saved, Ws, sms, decs = X0, [], [], [], []
    gathers[0].run(("w_in",))
    for l in range(depth):
        Ws.append(gathers[l].weights())
        sms.append(_small_of_layer(small_w, l))
        decs.append(_decays(cfg, small_w["ret_decay_logit"][l]))
        X, sv = _layer_fwd(cfg, X, mods[l], Ws[l], sms[l], tabs, decs[l], consts, gathers[l + 1] if l + 1 < depth else None,
                           gathers[0] if l == 0 else None)
        saved.append(sv)
    dX, loss_acc, dfinal = _final(cfg, X, final_g[None], target)
    shard_g, small, dmods = [None] * depth, [None] * depth, [None] * depth
    pending = None
    for l in reversed(range(depth)):
        dX, mix, shard_g[l], small[l], dmods[l] = _layer_bwd(cfg, dX, saved[l], mods[l], Ws[l], sms[l], tabs, decs[l], consts,
                                                             pending, make_reduce)
        if pending is not None:
            shard_g[l + 1].update(pending.result)
        pending = make_reduce(REDUCE_MIX, mix)
        lam_grad = jnp.sum(small[l].pop("dlam").reshape(2, cfg.H, CH), axis=-1)
        small[l]["ret_decay_logit"] = lam_grad * jax.nn.sigmoid(-small_w["ret_decay_logit"][l].astype(F32))
    shard_g[0].update(pending.run())
    return loss_acc[0, 0], dX, shard_g, small, dmods, dfinal


def _me():
    return lax.axis_index("x"), lax.axis_index("y"), lax.axis_index("c")


def _other_chips(x, y):
    return [(1 - x, y), (x, 1 - y), (1 - x, 1 - y)]


def _rcopy(src, dst, send_sem, recv_sem, dev):
    return pltpu.make_async_remote_copy(src_ref=src, dst_ref=dst, send_sem=send_sem, recv_sem=recv_sem,
                                        device_id=dev, device_id_type=MESH)


def _half(ref, axis, c):
    k, n = ref.shape
    if axis == 1:
        return ref.at[pl.ds(c * (k // 2), k // 2), :]
    return ref.at[:, pl.ds(c * (n // 2), n // 2)]


def _chip_part(ref, axis, j):
    k, n = ref.shape
    if axis == 1:
        return ref.at[:, pl.ds(j * (n // 4), n // 4)]
    return ref.at[pl.ds(j * (k // 4), k // 4), :]


def _piece(ref, axis, j, c):
    k, n = ref.shape
    if axis == 1:
        return ref.at[pl.ds(c * (k // 2), k // 2), pl.ds(j * (n // 4), n // 4)]
    return ref.at[pl.ds(j * (k // 4), k // 4), pl.ds(c * (n // 2), n // 2)]


def _gather_weights(fulls, axes):
    n = len(fulls)

    def body(*refs):
        outs = refs[n:2 * n]
        send, recv = refs[2 * n:]
        x, y, c = _me()
        j = 2 * x + y
        sib = (x, y, 1 - c)
        chips = _other_chips(x, y)
        first = []
        for t in range(n):
            for k, chip in enumerate(chips):
                own = _piece(outs[t], axes[t], j, c)
                first.append(_rcopy(own, own, send.at[6 * t + k], recv.at[6 * t + k], (*chip, c)))
                first[-1].start()
        passed = []
        for t in range(n):
            for k, chip in enumerate(chips):
                landed = _piece(outs[t], axes[t], 2 * chip[0] + chip[1], c)
                _rcopy(landed, landed, send.at[6 * t + k], recv.at[6 * t + k], sib).wait_recv()
                passed.append(_rcopy(landed, landed, send.at[6 * t + 3 + k], recv.at[6 * t + 3 + k], sib))
                passed[-1].start()
        for t in range(n):
            for k, chip in enumerate(chips):
                theirs = _piece(outs[t], axes[t], 2 * chip[0] + chip[1], 1 - c)
                _rcopy(theirs, theirs, send.at[6 * t + 3 + k], recv.at[6 * t + 3 + k], sib).wait_recv()
        for cp in first + passed:
            cp.wait_send()

    return pl.pallas_call(
        body, name="gather_weights",
        in_specs=[ANY] * n, out_specs=[ANY] * n,
        out_shape=[jax.ShapeDtypeStruct(f.shape, f.dtype) for f in fulls],
        input_output_aliases={t: t for t in range(n)},
        scratch_shapes=[pltpu.SemaphoreType.DMA((6 * n,)), pltpu.SemaphoreType.DMA((6 * n,))],
    )(*fulls)


class _Comm:
    def __init__(self, ins, outs, aliases, nsem, start, finish):
        self.ins, self.outs, self.aliases, self.nsem, self.start, self.finish = ins, outs, aliases, nsem, start, finish


def _run_comm(name, comm):
    n_in = len(comm.ins)

    def body(*refs):
        ins, outs = refs[:n_in], refs[n_in:n_in + len(comm.outs)]
        send, recv = refs[n_in + len(comm.outs):]
        comm.start(ins, outs, send, recv)
        comm.finish(ins, outs, send, recv)

    return pl.pallas_call(
        body, name=name, in_specs=[ANY] * n_in, out_specs=[ANY] * len(comm.outs), out_shape=list(comm.outs),
        input_output_aliases=dict(comm.aliases),
        scratch_shapes=[pltpu.SemaphoreType.DMA((comm.nsem,)), pltpu.SemaphoreType.DMA((comm.nsem,))],
    )(*comm.ins)


def _like(arrs):
    return [jax.ShapeDtypeStruct(a.shape, a.dtype) for a in arrs]


def _gather_ici_comm(fulls, axes):
    n = len(fulls)

    def copies(outs, send, recv):
        x, y, c = _me()
        own_j = 2 * x + y
        res = []
        for t in range(n):
            for k, chip in enumerate(_other_chips(x, y)):
                own = _piece(outs[t], axes[t], own_j, c)
                landed = _piece(outs[t], axes[t], 2 * chip[0] + chip[1], c)
                res.append((_rcopy(own, own, send.at[3 * t + k], recv.at[3 * t + k], (*chip, c)),
                            _rcopy(landed, landed, send.at[3 * t + k], recv.at[3 * t + k], (*chip, c))))
        return res

    def start(ins, outs, send, recv):
        for out, _ in copies(outs, send, recv):
            out.start()

    def finish(ins, outs, send, recv):
        for out, arrival in copies(outs, send, recv):
            out.wait_send()
            arrival.wait_recv()

    return _Comm(fulls, _like(fulls), {t: t for t in range(n)}, 3 * n, start, finish)


def _gather_d2d_comm(fulls, axes):
    n = len(fulls)

    def copies(outs, send, recv):
        x, y, c = _me()
        res = []
        for t in range(n):
            for k, chip in enumerate(_other_chips(x, y)):
                landed = _piece(outs[t], axes[t], 2 * chip[0] + chip[1], c)
                theirs = _piece(outs[t], axes[t], 2 * chip[0] + chip[1], 1 - c)
                res.append((_rcopy(landed, landed, send.at[3 * t + k], recv.at[3 * t + k], (x, y, 1 - c)),
                            _rcopy(theirs, theirs, send.at[3 * t + k], recv.at[3 * t + k], (x, y, 1 - c))))
        return res

    def start(ins, outs, send, recv):
        for out, _ in copies(outs, send, recv):
            out.start()

    def finish(ins, outs, send, recv):
        for out, arrival in copies(outs, send, recv):
            out.wait_send()
            arrival.wait_recv()

    return _Comm(fulls, _like(fulls), {t: t for t in range(n)}, 3 * n, start, finish)


def _swap_comm(grads, axes):
    n = len(grads)
    half_shapes = [(g.shape[0] // 2, g.shape[1]) if a == 1 else (g.shape[0], g.shape[1] // 2) for g, a in zip(grads, axes)]

    def copies(ins, outs, send, recv):
        x, y, c = _me()
        return [_rcopy(_half(ins[t], axes[t], 1 - c), outs[t], send.at[t], recv.at[t], (x, y, 1 - c)) for t in range(n)]

    def start(ins, outs, send, recv):
        for cp in copies(ins, outs, send, recv):
            cp.start()

    def finish(ins, outs, send, recv):
        for cp in copies(ins, outs, send, recv):
            cp.wait()

    return _Comm(grads, [jax.ShapeDtypeStruct(s, F32) for s in half_shapes], {}, n, start, finish)


def _scatter_comm(parts, axes):
    n = len(parts)
    q_shapes = [(p.shape[0], p.shape[1] // 4) if a == 1 else (p.shape[0] // 4, p.shape[1]) for p, a in zip(parts, axes)]

    def copies(ins, outs, send, recv):
        x, y, c = _me()
        res = []
        for t in range(n):
            for k, chip in enumerate(_other_chips(x, y)):
                res.append(_rcopy(_chip_part(ins[t], axes[t], 2 * chip[0] + chip[1]), outs[3 * t + k], send.at[3 * t + k],
                                  recv.at[3 * t + k], (*chip, c)))
        return res

    def start(ins, outs, send, recv):
        for cp in copies(ins, outs, send, recv):
            cp.start()

    def finish(ins, outs, send, recv):
        for cp in copies(ins, outs, send, recv):
            cp.wait()

    return _Comm(parts, [jax.ShapeDtypeStruct(s, p.dtype) for s, p in zip(q_shapes, parts) for _ in range(3)], {}, 3 * n,
                 start, finish)


def _share_comm(shards, axes):
    n = len(shards)

    def copies(outs, send, recv):
        x, y, c = _me()
        res = []
        for t in range(n):
            mine, theirs = _half(outs[t], axes[t], c), _half(outs[t], axes[t], 1 - c)
            res.append((_rcopy(mine, mine, send.at[t], recv.at[t], (x, y, 1 - c)),
                        _rcopy(theirs, theirs, send.at[t], recv.at[t], (x, y, 1 - c))))
        return res

    def start(ins, outs, send, recv):
        for out, _ in copies(outs, send, recv):
            out.start()

    def finish(ins, outs, send, recv):
        for out, arrival in copies(outs, send, recv):
            out.wait_send()
            arrival.wait_recv()

    return _Comm(shards, _like(shards), {t: t for t in range(n)}, n, start, finish)


def _cast_into_full(w, l, axis, sc):
    _, k, n = w.shape
    tr = _pick(k, 256, 16)
    if axis == 1:
        full, out_spec = (k, 4 * n), pl.BlockSpec((tr, n), lambda i, s: (i, s[1]))
    else:
        full, out_spec = (4 * k, n), pl.BlockSpec((tr, n), lambda i, s: (s[1] * (k // tr) + i, 0))

    def body(s_ref, w_ref, o_ref):
        o_ref[...] = w_ref[...].astype(BF16)

    return pl.pallas_call(
        body, name="cast_into_full",
        grid_spec=pltpu.PrefetchScalarGridSpec(
            num_scalar_prefetch=1, grid=(k // tr,),
            in_specs=[pl.BlockSpec((None, tr, n), lambda i, s: (l, i, 0))], out_specs=out_spec),
        out_shape=jax.ShapeDtypeStruct(full, BF16), compiler_params=_cparams(("arbitrary",)),
    )(sc, w)


def _rs_add2(g, got, axis, sc):
    k, n = g.shape
    hk, hn = (k // 2, n) if axis == 1 else (k, n // 2)
    tr = _pick(hk, max(16, (1 << 20) // hn), 16)
    if axis == 1:
        g_spec = pl.BlockSpec((tr, hn), lambda i, s: (s[0] * (hk // tr) + i, 0))
    else:
        g_spec = pl.BlockSpec((tr, hn), lambda i, s: (i, s[0]))
    blk = pl.BlockSpec((tr, hn), lambda i, s: (i, 0))

    def body(s_ref, a_ref, b_ref, o_ref):
        o_ref[...] = (a_ref[...] + b_ref[...]).astype(o_ref.dtype)

    return pl.pallas_call(
        body, name="rs_add2",
        grid_spec=pltpu.PrefetchScalarGridSpec(num_scalar_prefetch=1, grid=(hk // tr,), in_specs=[g_spec, blk], out_specs=blk),
        out_shape=jax.ShapeDtypeStruct((hk, hn), BF16), compiler_params=_cparams(("arbitrary",)),
    )(sc, g, got)


def _rs_add4(part, got3, axis, sc):
    k, n = part.shape
    qk, qn = (k, n // 4) if axis == 1 else (k // 4, n)
    tr = _pick(qk, max(16, (1 << 20) // qn), 16)
    if axis == 1:
        p_spec = pl.BlockSpec((tr, qn), lambda i, s: (i, s[1]))
        shard, o_spec = (2 * qk, qn), pl.BlockSpec((tr, qn), lambda i, s: (s[0] * (qk // tr) + i, 0))
    else:
        p_spec = pl.BlockSpec((tr, qn), lambda i, s: (s[1] * (qk // tr) + i, 0))
        shard, o_spec = (qk, 2 * qn), pl.BlockSpec((tr, qn), lambda i, s: (i, s[0]))
    blk = pl.BlockSpec((tr, qn), lambda i, s: (i, 0))

    def body(s_ref, p_ref, a_ref, b_ref, c_ref, o_ref):
        o_ref[...] = ((p_ref[...].astype(F32) + a_ref[...].astype(F32)) + b_ref[...].astype(F32)) + c_ref[...].astype(F32)

    return pl.pallas_call(
        body, name="rs_add4",
        grid_spec=pltpu.PrefetchScalarGridSpec(num_scalar_prefetch=1, grid=(qk // tr,), in_specs=[p_spec, blk, blk, blk],
                                               out_specs=o_spec),
        out_shape=jax.ShapeDtypeStruct(shard, F32), compiler_params=_cparams(("arbitrary",)),
    )(sc, part, *got3)


def _gather_small(v):
    def body(v_ref, out_ref, send, recv, loc):
        x, y, c = _me()
        sib = (x, y, 1 - c)
        chips = _other_chips(x, y)
        slot = lambda px, py, pc: out_ref.at[4 * px + 2 * py + pc]
        mine = pltpu.make_async_copy(v_ref, slot(x, y, c), loc)
        mine.start()
        first = [_rcopy(v_ref, slot(x, y, c), send.at[0], recv.at[0], sib)]
        first += [_rcopy(v_ref, slot(x, y, c), send.at[1 + k], recv.at[1 + k], (*chip, c)) for k, chip in enumerate(chips)]
        for cp in first:
            cp.start()
        passed = []
        for k, chip in enumerate(chips):
            landed = slot(*chip, c)
            _rcopy(landed, landed, send.at[1 + k], recv.at[1 + k], sib).wait_recv()
            passed.append(_rcopy(landed, landed, send.at[4 + k], recv.at[4 + k], sib))
            passed[-1].start()
        theirs = slot(x, y, 1 - c)
        _rcopy(theirs, theirs, send.at[0], recv.at[0], sib).wait_recv()
        for k, chip in enumerate(chips):
            theirs = slot(*chip, 1 - c)
            _rcopy(theirs, theirs, send.at[4 + k], recv.at[4 + k], sib).wait_recv()
        for cp in first + passed:
            cp.wait_send()
        mine.wait()

    return pl.pallas_call(
        body, name="gather_small",
        in_specs=[ANY], out_specs=ANY,
        out_shape=jax.ShapeDtypeStruct((8,) + v.shape, v.dtype),
        scratch_shapes=[pltpu.SemaphoreType.DMA((7,)), pltpu.SemaphoreType.DMA((7,)), pltpu.SemaphoreType.DMA],
    )(v)


class _WeightGather:
    def __init__(self, params, l, sc):
        self.bufs = {k: _cast_into_full(params[k], l, SHARD_AXIS[k], sc) for k in BIG}

    def run(self, names):
        self.update(names, _gather_weights([self.bufs[k] for k in names], [SHARD_AXIS[k] for k in names]))

    def ici(self, names):
        return _gather_ici_comm([self.bufs[k] for k in names], [SHARD_AXIS[k] for k in names])

    def d2d(self, names):
        return _gather_d2d_comm([self.bufs[k] for k in names], [SHARD_AXIS[k] for k in names])

    def update(self, names, outs):
        self.bufs.update(zip(names, outs))

    def weights(self):
        return self.bufs


class _GradReduce:
    def __init__(self, names, grads, sc):
        self.names, self.grads, self.sc, self.parts, self.theirs = names, grads, sc, None, None
        self.axes = [SHARD_AXIS[k] for k in names]

    def swap(self):
        return _swap_comm([self.grads[k] for k in self.names], self.axes)

    def swapped(self, got):
        self.parts = [_rs_add2(self.grads[k], s, a, self.sc) for k, s, a in zip(self.names, got, self.axes)]

    def scatter(self):
        return _scatter_comm(self.parts, self.axes)

    def scattered(self, outs):
        self.theirs = [outs[3 * q:3 * q + 3] for q in range(len(self.names))]

    def share(self):
        halves = [_rs_add4(p, th, a, self.sc) for p, th, a in zip(self.parts, self.theirs, self.axes)]
        return _share_comm(halves, self.axes)

    def shared(self, outs):
        self.result = dict(zip(self.names, outs))

    def run(self):
        self.swapped(_run_comm("rs_swap", self.swap()))
        self.scattered(_run_comm("rs_scatter", self.scatter()))
        self.shared(_run_comm("rs_share", self.share()))
        return self.result


def _adam_math(w, g, m, v):
    m = ADAM_B1 * m + (1.0 - ADAM_B1) * g
    v = ADAM_B2 * v + (1.0 - ADAM_B2) * (g * g)
    m_hat = m / (1.0 - ADAM_B1 ** ADAM_STEP)
    v_hat = v / (1.0 - ADAM_B2 ** ADAM_STEP)
    delta = -ADAM_LR * (m_hat / (jnp.sqrt(v_hat) + ADAM_EPS) + ADAM_WD * w)
    return delta, m, v


def _adam_layer(w, m, v, l, g, prev):
    L, k, n = w.shape
    tr = _pick(k, 128, 8)
    blk = pl.BlockSpec((None, tr, n), lambda i: (l, i, 0))

    def body(*refs):
        w_ref, m_ref, v_ref, g_ref = refs[:4]
        go, do, mo, vo = refs[-4:]
        gv = g_ref[...]
        d, m2, v2 = _adam_math(w_ref[...], gv, m_ref[...], v_ref[...])
        go[...] = gv
        do[...] = d
        mo[...] = m2
        vo[...] = v2

    args = [w, m, v, g]
    in_specs = [blk, blk, blk, pl.BlockSpec((tr, n), lambda i: (i, 0))]
    aliases = {}
    if prev is not None:
        for q, p in enumerate(prev):
            aliases[len(args)] = q
            in_specs.append(ANY)
            args.append(p)
    return pl.pallas_call(
        body, name="adam_layer", grid=(k // tr,),
        in_specs=in_specs, out_specs=[blk] * 4, out_shape=[jax.ShapeDtypeStruct((L, k, n), F32)] * 4,
        input_output_aliases=aliases, compiler_params=_cparams(("arbitrary",)),
    )(*args)


def _adam_flat(w, g, m, v):
    r = w.shape[0]
    tr = _pick(r, 512, 8)
    fn = lambda i, j, w, g, m, v: _adam_math(w, g, m, v)
    return _rowcall("adam_flat", fn, nrows=r, tr=tr, ins=[(a, "rows", 128, 0) for a in (w, g, m, v)],
                    outs=[(F32, 128, 128, 0, None)] * 3)


def _sum8(gathered):
    _, r, _ = gathered.shape
    tr = _pick(r, 512, 8)

    def body(g_ref, o_ref):
        acc = g_ref[0]
        for d in range(1, 8):
            acc = acc + g_ref[d]
        o_ref[...] = acc

    return pl.pallas_call(
        body, name="sum8", grid=(r // tr,),
        in_specs=[pl.BlockSpec((8, tr, 128), lambda i: (0, i, 0))], out_specs=pl.BlockSpec((tr, 128), lambda i: (i, 0)),
        out_shape=jax.ShapeDtypeStruct((r, 128), F32), compiler_params=_cparams(("arbitrary",)),
    )(gathered)


def _hdot(a, b, form="NN"):
    return _dot(a.astype(BF16), b.astype(BF16), form)


def _ada_fwd(s16, w_ada, l):
    _, d, ns = w_ada.shape
    tc = _pick(ns, 512)

    def body(s_ref, w_ref, o_ref):
        o_ref[...] = _hdot(s_ref[...], w_ref[...])

    return pl.pallas_call(
        body, name="ada_fwd", grid=(ns // tc,),
        in_specs=[pl.BlockSpec((16, d), lambda j: (0, 0)), pl.BlockSpec((None, d, tc), lambda j: (l, 0, j))],
        out_specs=pl.BlockSpec((16, tc), lambda j: (0, j)),
        out_shape=jax.ShapeDtypeStruct((16, ns), F32), compiler_params=_cparams(("arbitrary",)),
    )(s16, w_ada)


def _ada_bwd(s16t, dm, w_ada, l):
    _, d, ns = w_ada.shape
    tc = _pick(ns, 512)

    def body(st_ref, dm_ref, w_ref, dw_ref, ds_ref):
        j = pl.program_id(0)
        dw_ref[...] = _hdot(st_ref[...], dm_ref[...])
        part = _hdot(dm_ref[...], w_ref[...], "NT")

        @pl.when(j == 0)
        def _():
            ds_ref[...] = part

        @pl.when(j > 0)
        def _():
            ds_ref[...] += part

    return pl.pallas_call(
        body, name="ada_bwd", grid=(ns // tc,),
        in_specs=[pl.BlockSpec((d, 16), lambda j: (0, 0)), pl.BlockSpec((16, tc), lambda j: (0, j)),
                  pl.BlockSpec((None, d, tc), lambda j: (l, 0, j))],
        out_specs=[pl.BlockSpec((d, tc), lambda j: (0, j)), pl.BlockSpec((16, d), lambda j: (0, 0))],
        out_shape=[jax.ShapeDtypeStruct((d, ns), F32), jax.ShapeDtypeStruct((16, d), F32)],
        compiler_params=_cparams(("arbitrary",)),
    )(s16t, dm, w_ada)


def _tile_rows(shape):
    return -(-int(np.prod(shape)) // 1024) * 8


def _pack(arrs):
    parts = []
    for a in arrs:
        flat = a.reshape(-1).astype(F32)
        parts.append(jnp.pad(flat, (0, _tile_rows(a.shape) * 128 - flat.shape[0])).reshape(-1, 128))
    rows = sum(p.shape[0] for p in parts)
    parts.append(jnp.zeros(((-rows) % 512, 128), F32))
    return jnp.concatenate(parts, axis=0)


def _unpack(packed, shapes):
    out, row = [], 0
    for s in shapes:
        nr = _tile_rows(s)
        out.append(packed[row:row + nr].reshape(-1)[:int(np.prod(s))].reshape(s))
        row += nr
    return out


SMALL = ("norm1_g", "norm2_g", "ret_decay_logit", "ret_norm_g", "pool_w", "pool_scale", "sg_norm_g", "sg_w", "sg_b")


def kernel(x, c, ctx, c_ctx, w_ada, b_ada, norm1_g, w_in, ret_decay_logit, ret_norm_g, pool_w, pool_scale, sg_norm_g, sg_w, sg_b, w_br, w_bp, w_bs, w_out, norm2_g, w1, w2, final_norm_g, loss_target, m_c_ctx, m_w_ada, m_b_ada, m_norm1_g, m_w_in, m_ret_decay_logit, m_ret_norm_g, m_pool_w, m_pool_scale, m_sg_norm_g, m_sg_w, m_sg_b, m_w_br, m_w_bp, m_w_bs, m_w_out, m_norm2_g, m_w1, m_w2, m_final_norm_g, v_c_ctx, v_w_ada, v_b_ada, v_norm1_g, v_w_in, v_ret_decay_logit, v_ret_norm_g, v_pool_w, v_pool_scale, v_sg_norm_g, v_sg_w, v_sg_b, v_w_br, v_w_bp, v_w_bs, v_w_out, v_norm2_g, v_w1, v_w2, v_final_norm_g):
    P = dict(c_ctx=c_ctx, w_ada=w_ada, b_ada=b_ada, norm1_g=norm1_g, w_in=w_in, ret_decay_logit=ret_decay_logit,
             ret_norm_g=ret_norm_g, pool_w=pool_w, pool_scale=pool_scale, sg_norm_g=sg_norm_g, sg_w=sg_w, sg_b=sg_b, w_br=w_br,
             w_bp=w_bp, w_bs=w_bs, w_out=w_out, norm2_g=norm2_g, w1=w1, w2=w2, final_norm_g=final_norm_g)
    Mo = dict(c_ctx=m_c_ctx, w_ada=m_w_ada, b_ada=m_b_ada, norm1_g=m_norm1_g, w_in=m_w_in, ret_decay_logit=m_ret_decay_logit,
              ret_norm_g=m_ret_norm_g, pool_w=m_pool_w, pool_scale=m_pool_scale, sg_norm_g=m_sg_norm_g, sg_w=m_sg_w, sg_b=m_sg_b,
              w_br=m_w_br, w_bp=m_w_bp, w_bs=m_w_bs, w_out=m_w_out, norm2_g=m_norm2_g, w1=m_w1, w2=m_w2,
              final_norm_g=m_final_norm_g)
    Vo = dict(c_ctx=v_c_ctx, w_ada=v_w_ada, b_ada=v_b_ada, norm1_g=v_norm1_g, w_in=v_w_in, ret_decay_logit=v_ret_decay_logit,
              ret_norm_g=v_ret_norm_g, pool_w=v_pool_w, pool_scale=v_pool_scale, sg_norm_g=v_sg_norm_g, sg_w=v_sg_w, sg_b=v_sg_b,
              w_br=v_w_br, w_bp=v_w_bp, w_bs=v_w_bs, w_out=v_w_out, norm2_g=v_norm2_g, w1=v_w1, w2=v_w2,
              final_norm_g=v_final_norm_g)
    names = ("c_ctx", "w_ada", "b_ada", "norm1_g", "w_in", "ret_decay_logit", "ret_norm_g", "pool_w", "pool_scale", "sg_norm_g",
             "sg_w", "sg_b", "w_br", "w_bp", "w_bs", "w_out", "norm2_g", "w1", "w2", "final_norm_g")
    L, D = w_in.shape[0], x.shape[-1]
    T, CTX = x.shape[1], ctx.shape[1]
    cfg = _Cfg(D, T, CTX, 4 * w1.shape[2])
    ns_ada = w_ada.shape[2]
    mx, my, mc = _me()
    dev = 4 * mx + 2 * my + mc
    chip = 2 * mx + my

    silu_cc = jax.nn.silu(c_ctx)
    silu_all = _gather_small(jax.nn.silu(c).reshape(-1, 128)).reshape(8, D)
    s16 = jnp.concatenate([silu_cc[None], silu_all, jnp.zeros((7, D), F32)], axis=0)
    proj = jnp.stack([_ada_fwd(s16, w_ada, l) for l in range(L)])
    proj_all = _gather_small(proj.reshape(-1, 128)).reshape(8, L, 16, ns_ada)
    mods_full = jnp.concatenate([proj_all[2 * j] for j in range(4)], axis=-1) + b_ada[:, None, :]
    mods = [jnp.concatenate([mods_full[l, 0:1], lax.dynamic_slice_in_dim(mods_full[l], 1 + dev, 1, axis=0)], axis=0)
            for l in range(L)]

    sc = jnp.stack([mc, chip]).astype(jnp.int32)
    gathers = [_WeightGather(P, l, sc) for l in range(L)]
    X0 = jnp.concatenate([ctx[0], x[0]], axis=0)
    small_w = {k: P[k] for k in SMALL}
    loss_part, dX, shard_g, small, dmods, dfinal = _local_fwd_bwd(
        cfg, X0, loss_target[0], mods, gathers, lambda names, grads: _GradReduce(names, grads, sc), small_w, final_norm_g)
    loss = lax.psum(loss_part, ("x", "y", "c"))
    grad_x = dX[CTX:][None]

    outs = {k: None for k in BIG}
    for l in range(L):
        for k in BIG:
            outs[k] = _adam_layer(P[k], Mo[k], Vo[k], l, shard_g[l][k], outs[k])

    per_layer = [[small[l][k] for k in SMALL] + [dmods[l][1], dmods[l][0]] for l in range(L)]
    payload = _pack([a for lay in per_layer for a in lay] + [dfinal])
    gathered = _gather_small(payload)
    total = _sum8(gathered)
    shapes = [P[k].shape[1:] for k in SMALL] + [(6 * D,), (6 * D,)]
    tot = _unpack(total, shapes * L + [(D,)])
    per = len(shapes)
    g_small = {k: jnp.stack([tot[l * per + q] for l in range(L)]) for q, k in enumerate(SMALL)}
    dmx_sum = jnp.stack([tot[l * per + per - 2] for l in range(L)])
    dmc_sum = jnp.stack([tot[l * per + per - 1] for l in range(L)])
    g_small["b_ada"] = dmx_sum + dmc_sum
    g_small["final_norm_g"] = tot[-1]
    rows = [_tile_rows(s) for s in shapes]
    s16t = s16.T
    ada_out, ds_part = None, jnp.zeros((16, D), F32)
    for l in range(L):
        r0 = l * sum(rows) + sum(rows[:per - 2])
        dmx_all = gathered[:, r0:r0 + rows[per - 2], :].reshape(8, -1)[:, :6 * D]
        dm_full = jnp.concatenate([dmc_sum[l][None], dmx_all, jnp.zeros((7, 6 * D), F32)], axis=0)
        dm = lax.dynamic_slice_in_dim(dm_full, chip * ns_ada, ns_ada, axis=1)
        dw, ds = _ada_bwd(s16t, dm, w_ada, l)
        ds_part = ds_part + ds
        ada_out = _adam_layer(w_ada, Mo["w_ada"], Vo["w_ada"], l, dw, ada_out)
    outs["w_ada"] = ada_out
    ds_all = _gather_small(ds_part[0].reshape(-1, 128)).reshape(8, D)
    d_silu_cc = ds_all[0] + ds_all[2] + ds_all[4] + ds_all[6]
    g_small["c_ctx"] = jax.vjp(jax.nn.silu, c_ctx)[1](d_silu_cc)[0]

    small_names = [k for k in names if k not in BIG and k != "w_ada"]
    sm_shapes = [P[k].shape for k in small_names]
    res = _adam_flat(_pack([P[k] for k in small_names]), _pack([g_small[k] for k in small_names]),
                     _pack([Mo[k] for k in small_names]), _pack([Vo[k] for k in small_names]))
    d_s, m_s, v_s = [_unpack(r, sm_shapes) for r in res]
    for q, k in enumerate(small_names):
        outs[k] = (g_small[k].reshape(P[k].shape), d_s[q], m_s[q], v_s[q])

    return (loss, grad_x, *[outs[k][0] for k in names], *[outs[k][1] for k in names], *[outs[k][2] for k in names],
            *[outs[k][3] for k in names])
```

```python
import functools

import numpy as np
import jax
import jax.numpy as jnp
from jax import lax
from jax.experimental import pallas as pl
from jax.experimental.pallas import tpu as pltpu

F32 = jnp.float32
BF16 = jnp.bfloat16
EPS = 1e-6
CH = 128
GRID_W = 64
ROPE_THETA = 10000.0
POOL_WINDOWS = (2, 4, 8, 16)
PT = 256
VMEM_LIMIT = 56 * 1024 * 1024
MESH = pl.DeviceIdType.MESH
ANY = pl.BlockSpec(memory_space=pl.ANY)

ADAM_LR = 0.001
ADAM_B1 = 0.9
ADAM_B2 = 0.999
ADAM_EPS = 1e-08
ADAM_WD = 0.01
ADAM_STEP = 10

BIG = ("w_in", "w_br", "w_bp", "w_bs", "w_out", "w1", "w2")
SHARD_AXIS = {"w_in": 1, "w_br": 1, "w_bp": 1, "w_bs": 1, "w_out": 0, "w1": 1, "w2": 0}


def _pick(dim, pref, mult=128):
    best = None
    for t in range(mult, min(dim, pref) + 1, mult):
        if dim % t == 0:
            best = t
    return dim if best is None else best


def _cparams(sem=None):
    return pltpu.CompilerParams(dimension_semantics=sem, vmem_limit_bytes=VMEM_LIMIT)


def _rows(i, tr):
    return i * tr + lax.broadcasted_iota(jnp.int32, (tr, 1), 0)


def _sel(i, tr, n_ctx, v2):
    return jnp.where(_rows(i, tr) < n_ctx, v2[0:1, :], v2[1:2, :])


def _seg_sums(i, tr, n_ctx, d):
    is_ctx = _rows(i, tr) < n_ctx
    s_c = jnp.sum(jnp.where(is_ctx, d, 0.0), axis=0, keepdims=True)
    s_x = jnp.sum(jnp.where(is_ctx, 0.0, d), axis=0, keepdims=True)
    two = lax.broadcasted_iota(jnp.int32, (2, d.shape[1]), 0)
    return jnp.where(two == 0, s_c, s_x)


def _dot(a, b, form="NN"):
    dims = {"NN": (((1,), (0,)), ((), ())), "NT": (((1,), (1,)), ((), ())), "TN": (((0,), (0,)), ((), ()))}[form]
    return lax.dot_general(a, b, dims, preferred_element_type=F32)


def _gelu(x):
    return 0.5 * x * (1.0 + jnp.tanh(0.7978845608028654 * (x + 0.044715 * x * x * x)))


def _matmul(name, form, pairs, *, R, C, tr, tc, tk, nk, out_dtypes, epi, extras=(), a_pro=None, comm=None):
    npair, nex, nout = len(pairs), len(extras), len(out_dtypes)
    in_specs, args = [], []
    for a, b in pairs:
        if nk == 1:
            ka = a.shape[0] if form == "TN" else a.shape[1]
        else:
            ka = tk
        if form == "NN":
            in_specs += [pl.BlockSpec((tr, ka), lambda i, j, k: (i, k)), pl.BlockSpec((ka, tc), lambda i, j, k: (k, j))]
        elif form == "NT":
            in_specs += [pl.BlockSpec((tr, ka), lambda i, j, k: (i, k)), pl.BlockSpec((tc, ka), lambda i, j, k: (j, k))]
        else:
            in_specs += [pl.BlockSpec((ka, tr), lambda i, j, k: (k, i)), pl.BlockSpec((ka, tc), lambda i, j, k: (k, j))]
        args += [a, b]
    for kind, arr, off in extras:
        if kind == "tile":
            in_specs.append(pl.BlockSpec((tr, tc), lambda i, j, k, off=off: (i, j + off)))
        else:
            in_specs.append(pl.BlockSpec((2, tc), lambda i, j, k, off=off: (0, j + off)))
        args.append(arr)

    direct = epi is None
    n_acc = 0 if (nk == 1 or direct) else npair
    n_main = len(args)
    ni, nj = R // tr, C // tc
    aliases = {}
    out_specs = [pl.BlockSpec((tr, tc), lambda i, j, k: (i, j)) for _ in out_dtypes]
    out_shape = [jax.ShapeDtypeStruct((R, C), dt) for dt in out_dtypes]
    scratch = [pltpu.VMEM((tr, tc), F32) for _ in range(n_acc)]
    n_cin = n_cout = 0
    if comm is not None:
        n_cin, n_cout = len(comm.ins), len(comm.outs)
        in_specs = in_specs + [ANY] * n_cin
        args = args + list(comm.ins)
        out_specs = out_specs + [ANY] * n_cout
        out_shape = out_shape + list(comm.outs)
        aliases = {n_main + a: nout + b for a, b in comm.aliases.items()}
        scratch = scratch + [pltpu.SemaphoreType.DMA((comm.nsem,)), pltpu.SemaphoreType.DMA((comm.nsem,))]

    def body(*refs):
        ab = refs[:2 * npair]
        ex = refs[2 * npair:n_main]
        cin = refs[n_main:n_main + n_cin]
        outs = refs[n_main + n_cin:n_main + n_cin + nout]
        cout = refs[n_main + n_cin + nout:n_main + n_cin + nout + n_cout]
        accs = refs[n_main + n_cin + nout + n_cout:n_main + n_cin + nout + n_cout + n_acc]
        sems = refs[n_main + n_cin + nout + n_cout + n_acc:]
        i, j, k = pl.program_id(0), pl.program_id(1), pl.program_id(2)

        if comm is not None:
            @pl.when(jnp.logical_and(jnp.logical_and(i == 0, j == 0), k == 0))
            def _():
                comm.start(cin, cout, *sems)

        def products():
            res = []
            for p in range(npair):
                a = ab[2 * p][...]
                if a_pro is not None:
                    a = a_pro(a)
                res.append(_dot(a, ab[2 * p + 1][...], form))
            return res

        def finish(vals):
            res = epi(vals, [e[...] for e in ex], i)
            for o, v in zip(outs, res):
                o[...] = v.astype(o.dtype)

        if direct:
            prod = products()[0]
            if nk == 1:
                outs[0][...] = prod
            else:
                @pl.when(k == 0)
                def _():
                    outs[0][...] = prod

                @pl.when(k > 0)
                def _():
                    outs[0][...] += prod
        elif nk == 1:
            finish(products())
        else:
            prods = products()

            @pl.when(k == 0)
            def _():
                for acc, v in zip(accs, prods):
                    acc[...] = v

            @pl.when(k > 0)
            def _():
                for acc, v in zip(accs, prods):
                    acc[...] += v

            @pl.when(k == nk - 1)
            def _():
                finish([acc[...] for acc in accs])

        if comm is not None:
            @pl.when(jnp.logical_and(jnp.logical_and(i == ni - 1, j == nj - 1), k == nk - 1))
            def _():
                comm.finish(cin, cout, *sems)

    res = pl.pallas_call(
        body, name=name, grid=(ni, nj, nk),
        in_specs=in_specs, out_specs=out_specs, out_shape=out_shape, scratch_shapes=scratch,
        input_output_aliases=aliases,
        compiler_params=_cparams(("arbitrary", "arbitrary", "arbitrary")),
    )(*args)
    return res if comm is None else (res[:nout], res[nout:])


def _rowcall(name, fn, *, nrows, tr, ins, outs, accs=(), ncol=1):
    n_in, n_out, n_acc = len(ins), len(outs), len(accs)
    in_specs, args = [], []
    for arr, kind, w, off in ins:
        if kind == "rows":
            in_specs.append(pl.BlockSpec((tr, w), lambda i, j, off=off: (i, off + j)))
        elif kind == "full":
            in_specs.append(pl.BlockSpec(arr.shape, lambda i, j, nd=arr.ndim: (0,) * nd))
        else:
            in_specs.append(pl.BlockSpec((None,) + arr.shape[1:], lambda i, j, f=off, nd=arr.ndim: (f(i),) + (0,) * (nd - 1)))
        args.append(arr)
    aliases = {}
    out_specs, out_shape = [], []
    for o_idx, (dt, total, w, off, alias) in enumerate(outs):
        out_specs.append(pl.BlockSpec((tr, w), lambda i, j, off=off: (i, off + j)))
        out_shape.append(jax.ShapeDtypeStruct((nrows, total), dt))
        if alias is not None:
            aliases[len(args)] = o_idx
            in_specs.append(ANY)
            args.append(alias)
    n_alias = len(aliases)
    for shp in accs:
        out_specs.append(pl.BlockSpec(shp, lambda i, j, nd=len(shp): (0,) * nd))
        out_shape.append(jax.ShapeDtypeStruct(shp, F32))

    def body(*refs):
        in_refs = refs[:n_in]
        out_refs = refs[n_in + n_alias:n_in + n_alias + n_out]
        acc_refs = refs[n_in + n_alias + n_out:]
        i, j = pl.program_id(0), pl.program_id(1)
        res = fn(i, j, *[r[...] for r in in_refs])
        for o, v in zip(out_refs, res[:n_out]):
            o[...] = v.astype(o.dtype)
        first = jnp.logical_and(i == 0, j == 0)
        for acc, v in zip(acc_refs, res[n_out:]):
            @pl.when(first)
            def _(acc=acc, v=v):
                acc[...] = v

            @pl.when(jnp.logical_not(first))
            def _(acc=acc, v=v):
                acc[...] += v

    res = pl.pallas_call(
        body, name=name, grid=(nrows // tr, ncol),
        in_specs=in_specs, out_specs=out_specs, out_shape=out_shape,
        input_output_aliases=aliases,
        compiler_params=_cparams(("arbitrary", "arbitrary")),
    )(*args)
    return res


def _f_normmod(x, g, shift, scale):
    r = lax.rsqrt(jnp.mean(x * x, axis=-1, keepdims=True) + EPS)
    return (x * r * g) * (1.0 + scale) + shift


def _f_headnorm_gate(o, zg, ng):
    r = lax.rsqrt(jnp.mean(o * o, axis=-1, keepdims=True) + EPS)
    return (o * r * ng) * (zg * jax.nn.sigmoid(zg))


def _f_sgv(sv, g):
    v = _gelu(sv)
    r = lax.rsqrt(jnp.mean(v * v, axis=-1, keepdims=True) + EPS)
    return v * r * g


def _rope(t, cos, sin):
    w = t.shape[1]
    lane = lax.broadcasted_iota(jnp.int32, t.shape, 1)
    swapped = jnp.where(jnp.bitwise_and(lane, 63) < 32,pltpu.roll(t, w - 32, 1), pltpu.roll(t, 32, 1))
    return t * cos + swapped * sin


def _rope_t(d, cos, sin):
    w = d.shape[1]
    lane = lax.broadcasted_iota(jnp.int32, d.shape, 1)
    ds = d * sin
    swapped = jnp.where(jnp.bitwise_and(lane, 63) < 32,pltpu.roll(ds, w - 32, 1), pltpu.roll(ds, 32, 1))
    return d * cos + swapped


class _Cfg:
    def __init__(self, D, T, CTX, DFF):
        self.D, self.T, self.CTX, self.DFF = D, T, CTX, DFF
        self.TA = T + CTX
        self.RW = D // 2
        self.H = self.RW // CH
        self.PW = D // 4
        self.SW = D // 4
        self.NIN = 4 * self.RW + self.PW + 2 * self.SW + 3 * D
        self.NC = self.TA // CH
        self.NCC = CTX // CH
        self.k_scale = float(CH) ** -0.5
        self.tr = _pick(self.TA, 1408)
        self.tr_small = _pick(self.TA, 1408)
        self.o_g = 3 * self.RW
        self.o_p = 4 * self.RW
        self.o_u = self.o_p + self.PW
        self.o_sv = self.o_u + self.SW
        self.o_gate = self.o_sv + self.SW


def _normmod_fwd(cfg, X, g, mods, i_shift, i_scale, name):
    D = cfg.D
    tr = _pick(cfg.TA, 384)

    def fn(i, j, x, g, m):
        sh = _sel(i, tr, cfg.CTX, m[:, i_shift * D:(i_shift + 1) * D])
        sc = _sel(i, tr, cfg.CTX, m[:, i_scale * D:(i_scale + 1) * D])
        return (_f_normmod(x, g, sh, sc),)

    return _rowcall(name, fn, nrows=cfg.TA, tr=tr, ins=[(X, "rows", D, 0), (g, "full", 0, 0), (mods, "full", 0, 0)],
                    outs=[(BF16, D, D, 0, None)])[0]


def _normmod_bwd(cfg, X, dH, dres, g, mods, i_shift, i_scale, name):
    D = cfg.D
    tr = _pick(cfg.TA, 384)

    def fn(i, j, x, dh, dr, g, m):
        sh = _sel(i, tr, cfg.CTX, m[:, i_shift * D:(i_shift + 1) * D])
        sc = _sel(i, tr, cfg.CTX, m[:, i_scale * D:(i_scale + 1) * D])
        _, vjp = jax.vjp(_f_normmod, x, g, sh, sc)
        dx, dg, dsh, dsc = vjp(dh)
        return dr + dx, dg, _seg_sums(i, tr, cfg.CTX, dsh), _seg_sums(i, tr, cfg.CTX, dsc)

    return _rowcall(name, fn, nrows=cfg.TA, tr=tr,
                    ins=[(X, "rows", D, 0), (dH, "rows", D, 0), (dres, "rows", D, 0), (g, "full", 0, 0), (mods, "full", 0, 0)],
                    outs=[(F32, D, D, 0, None)], accs=[(1, D), (2, D), (2, D)])


def _resgate_bwd(cfg, dX, M, mods, i_gate, name):
    D = cfg.D
    tr = _pick(cfg.TA, 384)

    def fn(i, j, dx, m, mm):
        gate = _sel(i, tr, cfg.CTX, mm[:, i_gate * D:(i_gate + 1) * D])
        return dx * gate, _seg_sums(i, tr, cfg.CTX, dx * m.astype(F32))

    return _rowcall(name, fn, nrows=cfg.TA, tr=tr, ins=[(dX, "rows", D, 0), (M, "rows", D, 0), (mods, "full", 0, 0)],
                    outs=[(BF16, D, D, 0, None)], accs=[(2, D)])


def _chunk_of(cfg, d, t):
    fwd = t
    bwd = jnp.where(t < cfg.NCC, cfg.NCC - 1 - t, cfg.NC - 1 - t + cfg.NCC)
    return jnp.where(d == 0, fwd, bwd)


def _ret_specs(cfg, cm):
    RW, H = cfg.RW, cfg.H
    return [
        pl.BlockSpec((CH, RW), lambda d, t: (cm(d, t), 0)),
        pl.BlockSpec((CH, RW), lambda d, t: (cm(d, t), 1)),
        pl.BlockSpec((CH, RW), lambda d, t: (cm(d, t), 2)),
        pl.BlockSpec((CH, CH), lambda d, t: (cm(d, t), 0)),
        pl.BlockSpec((CH, CH), lambda d, t: (cm(d, t), 0)),
        pl.BlockSpec((None, H, CH, CH), lambda d, t: (d, 0, 0, 0)),
        pl.BlockSpec((None, CH, RW), lambda d, t: (d, 0, 0)),
        pl.BlockSpec((None, CH, RW), lambda d, t: (d, 0, 0)),
        pl.BlockSpec((None, 1, RW), lambda d, t: (d, 0, 0)),
    ]


def _ret_prep(cfg, q_ref, k_ref, cos, sin, qd_ref, kd_ref, sl):
    qr = _rope(q_ref[:, sl].astype(F32), cos, sin)
    kr = _rope(k_ref[:, sl].astype(F32) * cfg.k_scale, cos, sin)
    return qr, kr, (qr * qd_ref[:, sl]).astype(BF16), (kr * kd_ref[:, sl]).astype(BF16)


def _ret_fwd(cfg, Z, tabs, decs):
    RW, H, TA, NC = cfg.RW, cfg.H, cfg.TA, cfg.NC
    cm = functools.partial(_chunk_of, cfg)

    def body(q_ref, k_ref, v_ref, ct_ref, st_ref, dm_ref, qd_ref, kd_ref, cd_ref, o_ref, so_ref, S):
        @pl.when(pl.program_id(1) == 0)
        def _():
            S[...] = jnp.zeros_like(S)

        cos, sin = ct_ref[...], st_ref[...]
        for h in range(H):
            sl = slice(h * CH, (h + 1) * CH)
            qr, kr, qd, kd = _ret_prep(cfg, q_ref, k_ref, cos, sin, qd_ref, kd_ref, sl)
            v = v_ref[:, sl].astype(BF16)
            p = (_dot(qr.astype(BF16), kr.astype(BF16), "NT") * dm_ref[h]).astype(BF16)
            s_h = S[h]
            so_ref[h] = s_h
            o_ref[:, sl] = _dot(p, v) + _dot(qd, s_h.astype(BF16))
            S[h] = s_h * cd_ref[:, sl] + _dot(kd, v, "TN")

    return pl.pallas_call(
        body, name="ret_fwd", grid=(2, NC),
        in_specs=_ret_specs(cfg, cm),
        out_specs=[pl.BlockSpec((None, CH, RW), lambda d, t: (d, cm(d, t), 0)),
                   pl.BlockSpec((None, None, H, CH, CH), lambda d, t: (d, cm(d, t), 0, 0, 0))],
        out_shape=[jax.ShapeDtypeStruct((2, TA, RW), F32), jax.ShapeDtypeStruct((2, NC, H, CH, CH), F32)],
        scratch_shapes=[pltpu.VMEM((H, CH, CH), F32)],
        compiler_params=_cparams(("arbitrary", "arbitrary")),
    )(Z, Z, Z, tabs["cos"], tabs["sin"], decs["dmat"], decs["qdec"], decs["kdec"], decs["cdec"])


def _ret_bwd(cfg, Z, tabs, decs, states, dO):
    RW, H, TA, NC = cfg.RW, cfg.H, cfg.TA, cfg.NC

    def cm(d, t):
        return _chunk_of(cfg, d, NC - 1 - t)

    def body(q_ref, k_ref, v_ref, ct_ref, st_ref, dm_ref, qd_ref, kd_ref, cd_ref, s_ref, do_ref, w_ref,
             dqkv_ref, dl_ref, dS):
        t = pl.program_id(1)

        @pl.when(t == 0)
        def _():
            dS[...] = jnp.zeros_like(dS)
            dl_ref[...] = jnp.zeros_like(dl_ref)

        cos, sin = ct_ref[...], st_ref[...]
        for h in range(H):
            sl = slice(h * CH, (h + 1) * CH)
            qh, kh, qd, kd = _ret_prep(cfg, q_ref, k_ref, cos, sin, qd_ref, kd_ref, sl)
            qb, kb = qh.astype(BF16), kh.astype(BF16)
            v = v_ref[:, sl].astype(BF16)
            dob = do_ref[:, sl].astype(BF16)
            dm = dm_ref[h]
            p = (_dot(qb, kb, "NT") * dm).astype(BF16)
            dp = (_dot(dob, v, "NT") * dm).astype(BF16)
            s_h = s_ref[h]
            ds_h = dS[h]
            sb, dsb = s_h.astype(BF16), ds_h.astype(BF16)
            dq_i = _dot(dp, kb)
            dk_i = _dot(dp, qb, "TN")
            dq_c = _dot(dob, sb, "NT") * qd_ref[:, sl]
            dk_s = _dot(v, dsb, "NT") * kd_ref[:, sl]
            dqkv_ref[:, sl] = dq_i + dq_c
            dqkv_ref[:, RW + h * CH:RW + (h + 1) * CH] = dk_i + dk_s
            dqkv_ref[:, 2 * RW + h * CH:2 * RW + (h + 1) * CH] = _dot(p, dob, "TN") + _dot(kd, dsb)
            lam = w_ref[0] * (qh * dq_i) + w_ref[1] * (qh * dq_c) + w_ref[2] * (kh * dk_i) + w_ref[3] * (kh * dk_s)
            lam_s = float(CH) * cd_ref[:, sl] * jnp.sum(ds_h * s_h, axis=0, keepdims=True)
            dl_ref[:, sl] += jnp.sum(lam, axis=0, keepdims=True) + lam_s
            dS[h] = ds_h * cd_ref[:, sl] + _dot(qd, dob, "TN")

    in_specs = _ret_specs(cfg, cm) + [
        pl.BlockSpec((None, None, H, CH, CH), lambda d, t: (d, cm(d, t), 0, 0, 0)),
        pl.BlockSpec((CH, RW), lambda d, t: (cm(d, t), 0)),
        pl.BlockSpec((None, 4, CH, CH), lambda d, t: (d, 0, 0, 0)),
    ]
    return pl.pallas_call(
        body, name="ret_bwd", grid=(2, NC),
        in_specs=in_specs,
        out_specs=[pl.BlockSpec((None, CH, 3 * RW), lambda d, t: (d, cm(d, t), 0)),
                   pl.BlockSpec((None, 1, RW), lambda d, t: (d, 0, 0))],
        out_shape=[jax.ShapeDtypeStruct((2, TA, 3 * RW), F32), jax.ShapeDtypeStruct((2, 1, RW), F32)],
        scratch_shapes=[pltpu.VMEM((H, CH, CH), F32)],
        compiler_params=_cparams(("arbitrary", "arbitrary")),
    )(Z, Z, Z, tabs["cos"], tabs["sin"], decs["dmat"], decs["qdec"], decs["kdec"], decs["cdec"], states, dO,
      tabs["lamw"])


def _rope_bwd(cfg, dqkv2, tabs, dz):
    RW, H = cfg.RW, cfg.H
    tr = PT

    def body(d0, d1, ct, st, dz_in, o):
        cos, sin = jnp.tile(ct[...], (1, H)), jnp.tile(st[...], (1, H))
        d = d0[...] + d1[...]
        o[:, :RW] = _rope_t(d[:, :RW], cos, sin).astype(o.dtype)
        o[:, RW:2 * RW] = (_rope_t(d[:, RW:2 * RW], cos, sin) * cfg.k_scale).astype(o.dtype)
        o[:, 2 * RW:] = d[:, 2 * RW:].astype(o.dtype)

    return pl.pallas_call(
        body, name="rope_bwd", grid=(cfg.TA // tr,),
        in_specs=[pl.BlockSpec((None, tr, 3 * RW), lambda i: (0, i, 0)), pl.BlockSpec((None, tr, 3 * RW), lambda i: (1, i, 0)),
                  pl.BlockSpec((tr, CH), lambda i: (i, 0)), pl.BlockSpec((tr, CH), lambda i: (i, 0)), ANY],
        out_specs=pl.BlockSpec((tr, 3 * RW), lambda i: (i, 0)),
        out_shape=jax.ShapeDtypeStruct((cfg.TA, cfg.NIN), BF16),
        input_output_aliases={4: 0},
        compiler_params=_cparams(("arbitrary",)),
    )(dqkv2, dqkv2, tabs["cos"], tabs["sin"], dz)


def _retout_fwd(cfg, o2, Z, ng):
    RW, H = cfg.RW, cfg.H
    tr = PT

    def body(o0, o1, zg, ng, out):
        o = o0[...] + o1[...]
        z = zg[...].astype(F32)
        for h in range(H):
            sl = slice(h * CH, (h + 1) * CH)
            out[:, sl] = _f_headnorm_gate(o[:, sl], z[:, sl], ng[:, sl]).astype(out.dtype)

    return pl.pallas_call(
        body, name="retout_fwd", grid=(cfg.TA // tr,),
        in_specs=[pl.BlockSpec((None, tr, RW), lambda i: (0, i, 0)), pl.BlockSpec((None, tr, RW), lambda i: (1, i, 0)),
                  pl.BlockSpec((tr, RW), lambda i: (i, 3)), pl.BlockSpec((1, RW), lambda i: (0, 0))],
        out_specs=pl.BlockSpec((tr, RW), lambda i: (i, 0)),
        out_shape=jax.ShapeDtypeStruct((cfg.TA, RW), BF16),
        compiler_params=_cparams(("arbitrary",)),
    )(o2, o2, Z, ng)


def _retout_bwd(cfg, o2, Z, ng, dret, dz):
    RW, H = cfg.RW, cfg.H
    tr = PT

    def body(o0, o1, zg, ng, dr, dz_in, do_out, dz_out, dng):
        i = pl.program_id(0)
        o = o0[...] + o1[...]
        z = zg[...].astype(F32)
        d = dr[...]
        acc = []
        for h in range(H):
            sl = slice(h * CH, (h + 1) * CH)
            _, vjp = jax.vjp(_f_headnorm_gate, o[:, sl], z[:, sl], ng[:, sl])
            do_h, dz_h, dg_h = vjp(d[:, sl])
            do_out[:, sl] = do_h
            dz_out[:, sl] = dz_h.astype(dz_out.dtype)
            acc.append(dg_h)

        @pl.when(i == 0)
        def _():
            for h in range(H):
                dng[:, h * CH:(h + 1) * CH] = acc[h]

        @pl.when(i > 0)
        def _():
            for h in range(H):
                dng[:, h * CH:(h + 1) * CH] += acc[h]

    return pl.pallas_call(
        body, name="retout_bwd", grid=(cfg.TA // tr,),
        in_specs=[pl.BlockSpec((None, tr, RW), lambda i: (0, i, 0)), pl.BlockSpec((None, tr, RW), lambda i: (1, i, 0)),
                  pl.BlockSpec((tr, RW), lambda i: (i, 3)), pl.BlockSpec((1, RW), lambda i: (0, 0)),
                  pl.BlockSpec((tr, RW), lambda i: (i, 0)), ANY],
        out_specs=[pl.BlockSpec((tr, RW), lambda i: (i, 0)), pl.BlockSpec((tr, RW), lambda i: (i, 3)),
                   pl.BlockSpec((1, RW), lambda i: (0, 0))],
        out_shape=[jax.ShapeDtypeStruct((cfg.TA, RW), F32), jax.ShapeDtypeStruct((cfg.TA, cfg.NIN), BF16),
                   jax.ShapeDtypeStruct((1, RW), F32)],
        input_output_aliases={5: 1},
        compiler_params=_cparams(("arbitrary",)),
    )(o2, o2, Z, ng, dret, dz)


def _pool_consts(ctx_len):
    assert ctx_len == PT
    bm = np.zeros((2, len(POOL_WINDOWS), PT, PT), np.float32)
    ic = np.zeros((2, len(POOL_WINDOWS), PT, CH), np.float32)
    for ty, seg in enumerate((ctx_len, GRID_W)):
        for gi, w in enumerate(POOL_WINDOWS):
            for r in range(PT):
                s0, pos = (r // seg) * seg, r % seg
                lo, hi = max(pos - w // 2, 0), min(pos + w // 2 - 1, seg - 1)
                bm[ty, gi, r, s0 + lo:s0 + hi + 1] = 1.0
                ic[ty, gi, r, :] = 1.0 / (hi - lo + 1)
    return jnp.asarray(bm, BF16), jnp.asarray(ic, F32)


def _pool_tile(p, bm, ic, pw, g):
    sl = slice(g * CH, (g + 1) * CH)
    pg = p[:, sl].astype(F32)
    hi = pg.astype(BF16)
    lo = (pg - hi.astype(F32)).astype(BF16)
    y = (_dot(bm[g], hi) + _dot(bm[g], lo)) * ic[g] - pg
    return y, _dot(y.astype(BF16), pw[g].astype(BF16))


def _pool_fwd(cfg, Z, consts, pool_w, pool_scale):
    PW = cfg.PW
    G = PW // CH
    nct = cfg.CTX // PT
    ty = lambda i: jnp.where(i < nct, 0, 1)

    def fn(i, j, p, bm, ic, pw, ps):
        outs = [_pool_tile(p, bm, ic, pw, g)[1] for g in range(G)]
        return (jnp.concatenate(outs, axis=1) * ps,)

    return _rowcall("pool_fwd", fn, nrows=cfg.TA, tr=PT,
                    ins=[(Z, "rows", PW, cfg.o_p // PW), (consts[0], "sel", 0, ty), (consts[1], "sel", 0, ty),
                         (pool_w, "full", 0, 0), (pool_scale, "full", 0, 0)],
                    outs=[(BF16, PW, PW, 0, None)])[0]


def _pool_bwd(cfg, Z, consts, pool_w, pool_scale, dpool, dz):
    PW = cfg.PW
    G = PW // CH
    nct = cfg.CTX // PT
    ty = lambda i: jnp.where(i < nct, 0, 1)

    def fn(i, j, p, bm, ic, pw, ps, dout):
        dps, dps_acc, dpw = [], [], []
        for g in range(G):
            sl = slice(g * CH, (g + 1) * CH)
            y, lin = _pool_tile(p, bm, ic, pw, g)
            dlin = (dout[:, sl] * ps[:, sl]).astype(BF16)
            dps_acc.append(jnp.sum(dout[:, sl] * lin, axis=0, keepdims=True))
            dy = _dot(dlin, pw[g].astype(BF16), "NT")
            dpw.append(_dot(y.astype(BF16), dlin, "TN"))
            t = dy * ic[g]
            hi = t.astype(BF16)
            lo = (t - hi.astype(F32)).astype(BF16)
            dps.append(_dot(bm[g], hi, "TN") + _dot(bm[g], lo, "TN") - dy)
        return (jnp.concatenate(dps, axis=1), jnp.stack(dpw), jnp.concatenate(dps_acc, axis=1))

    return _rowcall("pool_bwd", fn, nrows=cfg.TA, tr=PT,
                    ins=[(Z, "rows", PW, cfg.o_p // PW), (consts[0], "sel", 0, ty), (consts[1], "sel", 0, ty),
                         (pool_w, "full", 0, 0), (pool_scale, "full", 0, 0), (dpool, "rows", PW, 0)],
                    outs=[(BF16, cfg.NIN, PW, cfg.o_p // PW, dz)], accs=[(G, CH, CH), (1, PW)])


def _sg_mixed(vn, sw, sbb, g, c):
    rows = slice(c * CH, (c + 1) * CH)
    sl = slice(g * CH, (g + 1) * CH)
    return _dot(sw[g].astype(BF16), vn[rows, sl].astype(BF16)) + sbb[g]


def _sg_fwd(cfg, Z, sng, sw, sbb):
    SW = cfg.SW
    G = SW // CH

    def fn(i, j, u, sv, sng, sw, sbb):
        ug = _gelu(u.astype(F32))
        vn = _f_sgv(sv.astype(F32), sng)
        rows = []
        for c in range(PT // CH):
            mixed = jnp.concatenate([_sg_mixed(vn, sw, sbb, g, c) for g in range(G)], axis=1)
            rows.append(ug[c * CH:(c + 1) * CH, :] * mixed)
        return (jnp.concatenate(rows, axis=0),)

    return _rowcall("sg_fwd", fn, nrows=cfg.TA, tr=PT,
                    ins=[(Z, "rows", SW, cfg.o_u // SW), (Z, "rows", SW, cfg.o_sv // SW), (sng, "full", 0, 0),
                         (sw, "full", 0, 0), (sbb, "full", 0, 0)],
                    outs=[(BF16, SW, SW, 0, None)])[0]


def _sg_bwd(cfg, Z, sng, sw, sbb, dsg, dz):
    SW = cfg.SW
    G = SW // CH

    def fn(i, j, u, sv, sng, sw, sbb, dout):
        uf, svf = u.astype(F32), sv.astype(F32)
        ug, vjp_u = jax.vjp(_gelu, uf)
        vn, vjp_v = jax.vjp(_f_sgv, svf, sng)
        dug_rows, dvn_rows = [], []
        dsw = [jnp.zeros((CH, CH), F32) for _ in range(G)]
        dsb = [jnp.zeros((CH, CH), F32) for _ in range(G)]
        for c in range(PT // CH):
            rows = slice(c * CH, (c + 1) * CH)
            dug_g, dvn_g = [], []
            for g in range(G):
                sl = slice(g * CH, (g + 1) * CH)
                mixed = _sg_mixed(vn, sw, sbb, g, c)
                dmixed = dout[rows, sl] * ug[rows, sl]
                dug_g.append(dout[rows, sl] * mixed)
                dmb = dmixed.astype(BF16)
                dvn_g.append(_dot(sw[g].astype(BF16), dmb, "TN"))
                dsw[g] = dsw[g] + _dot(dmb, vn[rows, sl].astype(BF16), "NT")
                dsb[g] = dsb[g] + jnp.broadcast_to(jnp.sum(dmixed, axis=1, keepdims=True), (CH, CH))
            dug_rows.append(jnp.concatenate(dug_g, axis=1))
            dvn_rows.append(jnp.concatenate(dvn_g, axis=1))
        (du,) = vjp_u(jnp.concatenate(dug_rows, axis=0))
        dsv, dsng = vjp_v(jnp.concatenate(dvn_rows, axis=0))
        return du, dsv, jnp.stack(dsw), jnp.stack(dsb), dsng

    return _rowcall("sg_bwd", fn, nrows=cfg.TA, tr=PT,
                    ins=[(Z, "rows", SW, cfg.o_u // SW), (Z, "rows", SW, cfg.o_sv // SW), (sng, "full", 0, 0),
                         (sw, "full", 0, 0), (sbb, "full", 0, 0), (dsg, "rows", SW, 0)],
                    outs=[(BF16, cfg.NIN, SW, cfg.o_u // SW, dz), (BF16, SW, SW, 0, None)],
                    accs=[(G, CH, CH), (G, CH, CH), (1, SW)])


def _gate_bwd(cfg, b, dY, L, Z, dz):
    D = cfg.D
    tc = 512
    nj = D // tc
    off = cfg.o_gate // tc + b * nj

    def fn(i, j, dy, l, z):
        s = jax.nn.sigmoid(z.astype(F32))
        return dy * s, dy * l.astype(F32) * s * (1.0 - s)

    return _rowcall("gate_bwd", fn, nrows=cfg.TA, tr=cfg.tr,
                    ins=[(dY, "rows", tc, 0), (L, "rows", tc, 0), (Z, "rows", tc, off)],
                    outs=[(BF16, D, tc, 0, None), (BF16, cfg.NIN, tc, off, dz)], ncol=nj)


def _copy_cols(cfg, src, dz, col0):
    w = src.shape[1]
    return _rowcall("copy_cols", lambda i, j, s: (s,), nrows=cfg.TA, tr=cfg.tr, ins=[(src, "rows", w, 0)],
                    outs=[(dz.dtype, dz.shape[1], w, col0 // w, dz)])[0]


def _final(cfg, X, g, target):
    D = cfg.D
    tr = PT
    nct = cfg.CTX // tr

    def body(x_ref, g_ref, t_ref, dx_ref, loss_ref, dg_ref):
        i = pl.program_id(0)

        def f(x, g):
            r = lax.rsqrt(jnp.mean(x * x, axis=-1, keepdims=True) + EPS)
            return x * r * g

        y, vjp = jax.vjp(f, x_ref[...], g_ref[...])
        err = y - t_ref[...]
        dx, dg = vjp(err * (1.0 / D))
        part = 0.5 * jnp.sum(jnp.mean(err * err, axis=-1, keepdims=True), axis=0, keepdims=True)

        @pl.when(i == 0)
        def _():
            loss_ref[...] = jnp.zeros_like(loss_ref)
            dg_ref[...] = jnp.zeros_like(dg_ref)

        @pl.when(i < nct)
        def _():
            dx_ref[...] = jnp.zeros_like(dx_ref)

        @pl.when(i >= nct)
        def _():
            dx_ref[...] = dx
            loss_ref[...] += jnp.broadcast_to(part, loss_ref.shape)
            dg_ref[...] += dg

    return pl.pallas_call(
        body, name="final", grid=(cfg.TA // tr,),
        in_specs=[pl.BlockSpec((tr, D), lambda i: (i, 0)), pl.BlockSpec((1, D), lambda i: (0, 0)),
                  pl.BlockSpec((tr, D), lambda i: (jnp.maximum(i - nct, 0), 0))],
        out_specs=[pl.BlockSpec((tr, D), lambda i: (i, 0)), pl.BlockSpec((8, CH), lambda i: (0, 0)),
                   pl.BlockSpec((1, D), lambda i: (0, 0))],
        out_shape=[jax.ShapeDtypeStruct((cfg.TA, D), F32), jax.ShapeDtypeStruct((8, CH), F32),
                   jax.ShapeDtypeStruct((1, D), F32)],
        compiler_params=_cparams(("arbitrary",)),
    )(X, g, target)


GATHER_A = ("w_in", "w1")
GATHER_B = ("w2", "w_out", "w_br", "w_bp", "w_bs")
REDUCE_MLP = ("w2", "w1")
REDUCE_MIX = ("w_out", "w_br", "w_bp", "w_bs", "w_in")


def _hosted(res, comm, sink):
    if comm is None:
        return res
    main, outs = res
    sink(outs)
    return main


def _layer_fwd(cfg, X, mods, W, sm, tabs, decs, consts, nxt=None, own=None):
    D, TA, tr, RW, PW, SW, DFF, NIN = cfg.D, cfg.TA, cfg.tr, cfg.RW, cfg.PW, cfg.SW, cfg.DFF, cfg.NIN
    ident = lambda accs, ex, i: (accs[0],)
    rest = tuple(k for k in BIG if k != "w_in")
    if own is not None:
        plan = {"z_mm": (own, "ici", rest), "w1_mm": (nxt, "ici", ("w_in", "w_out", "w_br", "w_bp", "w_bs")),
                "w2_mm": (nxt, "ici", ("w1", "w2"))}
    else:
        plan = {"z_mm": (nxt, "ici", GATHER_A), "w1_mm": (nxt, "ici", GATHER_B), "w2_mm": (nxt, "d2d", BIG)}

    def carry(name):
        gather, hop, names = plan[name]
        if gather is None:
            return None, None
        comm = gather.ici(names) if hop == "ici" else gather.d2d(names)
        return comm, lambda o: gather.update(names, o)

    H1 = _normmod_fwd(cfg, X, sm["norm1_g"], mods, 0, 1, "normmod1_fwd")
    comm, sink = carry("z_mm")
    (Z,) = _hosted(_matmul("z_mm", "NN", [(H1, W["w_in"])], R=TA, C=NIN, tr=_pick(TA, 2816), tc=512, tk=D, nk=1, out_dtypes=[F32],
                           epi=ident, comm=comm), comm, sink)
    if own is not None:
        own.update(rest, _run_comm("gather_d2d", own.d2d(rest)))
    o2, states = _ret_fwd(cfg, Z, tabs, decs)
    ret = _retout_fwd(cfg, o2, Z, sm["ret_norm_g"])
    pool = _pool_fwd(cfg, Z, consts, sm["pool_w"], sm["pool_scale"])
    sg = _sg_fwd(cfg, Z, sm["sg_norm_g"], sm["sg_w"], sm["sg_bb"])

    tc = 512
    goff = cfg.o_gate // tc

    def epi_branch(accs, ex, i):
        y = sum(jax.nn.sigmoid(z.astype(F32)) * a for a, z in zip(accs, ex))
        return (y, accs[0], accs[1], accs[2])

    Y, Lr, Lp, Ls = _matmul("branch_mm", "NN", [(ret, W["w_br"]), (pool, W["w_bp"]), (sg, W["w_bs"])], R=TA, C=D, tr=cfg.tr_small,
                            tc=tc, tk=0, nk=1, out_dtypes=[BF16] * 4, epi=epi_branch,
                            extras=[("tile", Z, goff + b * (D // tc)) for b in range(3)])

    def epi_res(rows):
        return lambda accs, ex, i: (ex[0] + _sel(i, rows, cfg.CTX, ex[1]) * accs[0], accs[0])

    X2, O = _matmul("out_mm", "NN", [(Y, W["w_out"])], R=TA, C=D, tr=tr, tc=tc, tk=D, nk=1, out_dtypes=[F32, BF16],
                    epi=epi_res(tr), extras=[("tile", X, 0), ("rows2", mods, 2 * (D // tc))])
    H2 = _normmod_fwd(cfg, X2, sm["norm2_g"], mods, 3, 4, "normmod2_fwd")
    tcf = _pick(DFF, 1024)
    def epi_relu(accs, ex, i):
        r = jnp.maximum(accs[0], 0.0)
        return (r * r, r)

    comm, sink = carry("w1_mm")
    A2, Rr = _hosted(_matmul("w1_mm", "NN", [(H2, W["w1"])], R=TA, C=DFF, tr=tr, tc=tcf, tk=D, nk=1, out_dtypes=[BF16, BF16],
                             epi=epi_relu, comm=comm), comm, sink)
    comm, sink = carry("w2_mm")
    tr2 = _pick(TA, 704)
    X3, M = _hosted(_matmul("w2_mm", "NN", [(A2, W["w2"])], R=TA, C=D, tr=tr2, tc=tc, tk=DFF, nk=1,
                            out_dtypes=[F32, BF16], epi=epi_res(tr2), extras=[("tile", X2, 0), ("rows2", mods, 5 * (D // tc))],
                            comm=comm), comm, sink)
    if own is not None and nxt is not None:
        nxt.update(BIG, _run_comm("gather_d2d", nxt.d2d(BIG)))
    saved = dict(X=X, H1=H1, Z=Z, o2=o2, states=states, ret=ret, pool=pool, sg=sg, Y=Y, L=(Lr, Lp, Ls), O=O, X2=X2,
                 H2=H2, R=Rr, A2=A2, M=M)
    return X3, saved


def _layer_bwd(cfg, dX3, sv, mods, W, sm, tabs, decs, consts, rs, make_reduce):
    D, TA, tr, RW, PW, SW, DFF, NIN = cfg.D, cfg.TA, cfg.tr, cfg.RW, cfg.PW, cfg.SW, cfg.DFF, cfg.NIN
    ident = lambda accs, ex, i: (accs[0],)
    tw = 512
    g = {}

    dM, dgate2 = _resgate_bwd(cfg, dX3, sv["M"], mods, 5, "resgate2_bwd")
    tcf = _pick(DFF, 1024)
    comm = rs.swap() if rs else None
    (dPre,) = _hosted(_matmul("dpre_mm", "NT", [(dM, W["w2"])], R=TA, C=DFF, tr=tr, tc=tcf, tk=D, nk=1, out_dtypes=[BF16],
                              epi=lambda accs, ex, i: (accs[0] * (2.0 * ex[0].astype(F32)),), extras=[("tile", sv["R"], 0)],
                              comm=comm), comm, lambda o: rs.swapped(o))
    comm = rs.scatter() if rs else None
    (g["w2"],) = _hosted(_matmul("dw2_mm", "TN", [(sv["A2"], dM)], R=DFF, C=D, tr=tw, tc=tw, tk=TA, nk=1, out_dtypes=[F32],
                                 epi=None, comm=comm), comm, lambda o: rs.scattered(o))
    comm = rs.share() if rs else None
    (g["w1"],) = _hosted(_matmul("dw1_mm", "TN", [(sv["H2"], dPre)], R=D, C=DFF, tr=tw, tc=tw, tk=TA, nk=1, out_dtypes=[F32],
                                 epi=None, comm=comm), comm, lambda o: rs.shared(o))
    mlp = make_reduce(REDUCE_MLP, {k: g.pop(k) for k in REDUCE_MLP})
    tkf = _pick(DFF, 2048)
    comm = mlp.swap()
    (dH2,) = _hosted(_matmul("dh2_mm", "NT", [(dPre, W["w1"])], R=TA, C=D, tr=tr, tc=1024, tk=tkf, nk=DFF // tkf,
                             out_dtypes=[F32], epi=ident, comm=comm), comm, lambda o: mlp.swapped(o))
    dX2, dn2, dsh2, dsc2 = _normmod_bwd(cfg, sv["X2"], dH2, dX3, sm["norm2_g"], mods, 3, 4, "normmod2_bwd")

    dO, dgate1 = _resgate_bwd(cfg, dX2, sv["O"], mods, 2, "resgate1_bwd")
    (dY,) = _matmul("dy_mm", "NT", [(dO, W["w_out"])], R=TA, C=D, tr=tr, tc=1024, tk=D, nk=1, out_dtypes=[F32], epi=ident)
    (g["w_out"],) = _matmul("dwout_mm", "TN", [(sv["Y"], dO)], R=D, C=D, tr=tw, tc=tw, tk=TA, nk=1,
                            out_dtypes=[F32], epi=None)
    dLr, dz = _gate_bwd(cfg, 0, dY, sv["L"][0], sv["Z"], None)
    dLp, dz = _gate_bwd(cfg, 1, dY, sv["L"][1], sv["Z"], dz)
    dLs, dz = _gate_bwd(cfg, 2, dY, sv["L"][2], sv["Z"], dz)

    (dret,) = _matmul("dret_mm", "NT", [(dLr, W["w_br"])], R=TA, C=RW, tr=tr, tc=RW, tk=D, nk=1, out_dtypes=[F32], epi=ident)
    (dpool,) = _matmul("dpool_mm", "NT", [(dLp, W["w_bp"])], R=TA, C=PW, tr=tr, tc=PW, tk=D, nk=1, out_dtypes=[F32], epi=ident)
    (dsg,) = _matmul("dsg_mm", "NT", [(dLs, W["w_bs"])], R=TA, C=SW, tr=tr, tc=SW, tk=D, nk=1, out_dtypes=[F32], epi=ident)
    (g["w_br"],) = _matmul("dwbr_mm", "TN", [(sv["ret"], dLr)], R=RW, C=D, tr=tw, tc=tw, tk=TA, nk=1, out_dtypes=[F32],
                           epi=None)
    (g["w_bp"],) = _matmul("dwbp_mm", "TN", [(sv["pool"], dLp)], R=PW, C=D, tr=tw, tc=tw, tk=TA, nk=1, out_dtypes=[F32],
                           epi=None)
    (g["w_bs"],) = _matmul("dwbs_mm", "TN", [(sv["sg"], dLs)], R=SW, C=D, tr=tw, tc=tw, tk=TA, nk=1, out_dtypes=[F32],
                           epi=None)
    dOr, dz, dretng = _retout_bwd(cfg, sv["o2"], sv["Z"], sm["ret_norm_g"], dret, dz)
    dqkv2, dlam = _ret_bwd(cfg, sv["Z"], tabs, decs, sv["states"], dOr)
    dz = _rope_bwd(cfg, dqkv2, tabs, dz)
    dz, dpw, dps = _pool_bwd(cfg, sv["Z"], consts, sm["pool_w"], sm["pool_scale"], dpool, dz)
    dz, dz_sv, dsw, dsb, dsng = _sg_bwd(cfg, sv["Z"], sm["sg_norm_g"], sm["sg_w"], sm["sg_bb"], dsg, dz)
    dz = _copy_cols(cfg, dz_sv, dz, cfg.o_sv)

    tkz = _pick(NIN, 2944)
    comm = mlp.scatter()
    (dH1,) = _hosted(_matmul("dh1_mm", "NT", [(dz, W["w_in"])], R=TA, C=D, tr=tr, tc=1024, tk=tkz, nk=NIN // tkz,
                             out_dtypes=[F32], epi=ident, comm=comm), comm, lambda o: mlp.scattered(o))
    comm = mlp.share()
    (g["w_in"],) = _hosted(_matmul("dwin_mm", "TN", [(sv["H1"], dz)], R=D, C=NIN, tr=tw, tc=tw, tk=TA, nk=1,
                                   out_dtypes=[F32], epi=None, comm=comm), comm, lambda o: mlp.shared(o))
    dX, dn1, dsh1, dsc1 = _normmod_bwd(cfg, sv["X"], dH1, dX2, sm["norm1_g"], mods, 0, 1, "normmod1_bwd")
    dmods = jnp.concatenate([dsh1, dsc1, dgate1, dsh2, dsc2, dgate2], axis=1)
    small = dict(norm1_g=dn1, norm2_g=dn2, ret_norm_g=dretng, pool_w=dpw, pool_scale=dps, sg_norm_g=dsng, sg_w=dsw,
                 sg_b=dsb[:, :, 0], dlam=dlam)
    return dX, g, mlp.result, small, dmods


def _tables(cfg):
    nf = CH // 4
    inv = ROPE_THETA ** (-jnp.arange(nf, dtype=F32) / nf)
    tok = jnp.arange(cfg.T)
    ar = (tok // GRID_W).astype(F32)[:, None] * inv[None]
    ac = (tok % GRID_W).astype(F32)[:, None] * inv[None]
    cos = jnp.concatenate([jnp.cos(ar), jnp.cos(ar), jnp.cos(ac), jnp.cos(ac)], axis=1)
    sin = jnp.concatenate([-jnp.sin(ar), jnp.sin(ar), -jnp.sin(ac), jnp.sin(ac)], axis=1)
    cos = jnp.concatenate([jnp.ones((cfg.CTX, CH), F32), cos], axis=0)
    sin = jnp.concatenate([jnp.zeros((cfg.CTX, CH), F32), sin], axis=0)
    idx = np.broadcast_to(np.arange(CH, dtype=np.float32)[:, None], (CH, CH))
    lamw = np.stack([np.stack([idx, idx + 1.0, -idx, CH - 1.0 - idx]), np.stack([-idx, CH - idx, idx, idx])])
    return dict(cos=cos, sin=sin, lamw=jnp.asarray(lamw, F32))


def _decays(cfg, logit):
    H, RW = cfg.H, cfg.RW
    lam = jax.nn.log_sigmoid(logit.astype(F32))
    idx = jnp.arange(CH, dtype=F32)
    dist = idx[:, None] - idx[None, :]
    d0 = jnp.where(dist >= 0, jnp.exp(lam[0][:, None, None] * jnp.maximum(dist, 0.0)), 0.0)
    d1 = jnp.where(dist <= 0, jnp.exp(lam[1][:, None, None] * jnp.maximum(-dist, 0.0)), 0.0)
    lanes = lambda a: jnp.repeat(a.T, CH, axis=1)
    qdec = jnp.stack([lanes(jnp.exp(lam[0][:, None] * (idx + 1.0)[None])), lanes(jnp.exp(lam[1][:, None] * (CH - idx)[None]))])
    kdec = jnp.stack([lanes(jnp.exp(lam[0][:, None] * (CH - 1.0 - idx)[None])), lanes(jnp.exp(lam[1][:, None] * idx[None]))])
    cdec = jnp.repeat(jnp.exp(lam * CH), CH, axis=1)[:, None, :]
    return dict(dmat=jnp.stack([d0, d1]), qdec=qdec, kdec=kdec, cdec=cdec)


def _small_of_layer(small_w, l):
    sm = {k: v[l] for k, v in small_w.items()}
    sm["norm1_g"] = sm["norm1_g"][None]
    sm["norm2_g"] = sm["norm2_g"][None]
    sm["ret_norm_g"] = sm["ret_norm_g"][None]
    sm["pool_scale"] = sm["pool_scale"][None]
    sm["sg_norm_g"] = sm["sg_norm_g"][None]
    sm["sg_bb"] = jnp.broadcast_to(sm["sg_b"][:, :, None], sm["sg_b"].shape + (CH,))
    return sm


def _local_fwd_bwd(cfg, X0, target, mods, gathers, make_reduce, small_w, final_g):
    depth = len(mods)
    tabs = _tables(cfg)
    consts = _pool_consts(cfg.CTX)
    X, saved, Ws, sms, decs = X0, [], [], [], []
    gathers[0].run(("w_in",))
    for l in range(depth):
        Ws.append(gathers[l].weights())
        sms.append(_small_of_layer(small_w, l))
        decs.append(_decays(cfg, small_w["ret_decay_logit"][l]))
        X, sv = _layer_fwd(cfg, X, mods[l], Ws[l], sms[l], tabs, decs[l], consts, gathers[l + 1] if l + 1 < depth else None,
                           gathers[0] if l == 0 else None)
        saved.append(sv)
    dX, loss_acc, dfinal = _final(cfg, X, final_g[None], target)
    shard_g, small, dmods = [None] * depth, [None] * depth, [None] * depth
    pending = None
    for l in reversed(range(depth)):
        dX, mix, shard_g[l], small[l], dmods[l] = _layer_bwd(cfg, dX, saved[l], mods[l], Ws[l], sms[l], tabs, decs[l], consts,
                                                             pending, make_reduce)
        if pending is not None:
            shard_g[l + 1].update(pending.result)
        pending = make_reduce(REDUCE_MIX, mix)
        lam_grad = jnp.sum(small[l].pop("dlam").reshape(2, cfg.H, CH), axis=-1)
        small[l]["ret_decay_logit"] = lam_grad * jax.nn.sigmoid(-small_w["ret_decay_logit"][l].astype(F32))
    shard_g[0].update(pending.run())
    return loss_acc[0, 0], dX, shard_g, small, dmods, dfinal


def _me():
    return lax.axis_index("x"), lax.axis_index("y"), lax.axis_index("c")


def _other_chips(x, y):
    return [(1 - x, y), (x, 1 - y), (1 - x, 1 - y)]


def _rcopy(src, dst, send_sem, recv_sem, dev):
    return pltpu.make_async_remote_copy(src_ref=src, dst_ref=dst, send_sem=send_sem, recv_sem=recv_sem,
                                        device_id=dev, device_id_type=MESH)


def _half(ref, axis, c):
    k, n = ref.shape
    if axis == 1:
        return ref.at[pl.ds(c * (k // 2), k // 2), :]
    return ref.at[:, pl.ds(c * (n // 2), n // 2)]


def _chip_part(ref, axis, j):
    k, n = ref.shape
    if axis == 1:
        return ref.at[:, pl.ds(j * (n // 4), n // 4)]
    return ref.at[pl.ds(j * (k // 4), k // 4), :]


def _piece(ref, axis, j, c):
    k, n = ref.shape
    if axis == 1:
        return ref.at[pl.ds(c * (k // 2), k // 2), pl.ds(j * (n // 4), n // 4)]
    return ref.at[pl.ds(j * (k // 4), k // 4), pl.ds(c * (n // 2), n // 2)]


def _gather_weights(fulls, axes):
    n = len(fulls)

    def body(*refs):
        outs = refs[n:2 * n]
        send, recv = refs[2 * n:]
        x, y, c = _me()
        j = 2 * x + y
        sib = (x, y, 1 - c)
        chips = _other_chips(x, y)
        first = []
        for t in range(n):
            for k, chip in enumerate(chips):
                own = _piece(outs[t], axes[t], j, c)
                first.append(_rcopy(own, own, send.at[6 * t + k], recv.at[6 * t + k], (*chip, c)))
                first[-1].start()
        passed = []
        for t in range(n):
            for k, chip in enumerate(chips):
                landed = _piece(outs[t], axes[t], 2 * chip[0] + chip[1], c)
                _rcopy(landed, landed, send.at[6 * t + k], recv.at[6 * t + k], sib).wait_recv()
                passed.append(_rcopy(landed, landed, send.at[6 * t + 3 + k], recv.at[6 * t + 3 + k], sib))
                passed[-1].start()
        for t in range(n):
            for k, chip in enumerate(chips):
                theirs = _piece(outs[t], axes[t], 2 * chip[0] + chip[1], 1 - c)
                _rcopy(theirs, theirs, send.at[6 * t + 3 + k], recv.at[6 * t + 3 + k], sib).wait_recv()
        for cp in first + passed:
            cp.wait_send()

    return pl.pallas_call(
        body, name="gather_weights",
        in_specs=[ANY] * n, out_specs=[ANY] * n,
        out_shape=[jax.ShapeDtypeStruct(f.shape, f.dtype) for f in fulls],
        input_output_aliases={t: t for t in range(n)},
        scratch_shapes=[pltpu.SemaphoreType.DMA((6 * n,)), pltpu.SemaphoreType.DMA((6 * n,))],
    )(*fulls)


class _Comm:
    def __init__(self, ins, outs, aliases, nsem, start, finish):
        self.ins, self.outs, self.aliases, self.nsem, self.start, self.finish = ins, outs, aliases, nsem, start, finish


def _run_comm(name, comm):
    n_in = len(comm.ins)

    def body(*refs):
        ins, outs = refs[:n_in], refs[n_in:n_in + len(comm.outs)]
        send, recv = refs[n_in + len(comm.outs):]
        comm.start(ins, outs, send, recv)
        comm.finish(ins, outs, send, recv)

    return pl.pallas_call(
        body, name=name, in_specs=[ANY] * n_in, out_specs=[ANY] * len(comm.outs), out_shape=list(comm.outs),
        input_output_aliases=dict(comm.aliases),
        scratch_shapes=[pltpu.SemaphoreType.DMA((comm.nsem,)), pltpu.SemaphoreType.DMA((comm.nsem,))],
    )(*comm.ins)


def _like(arrs):
    return [jax.ShapeDtypeStruct(a.shape, a.dtype) for a in arrs]


def _gather_ici_comm(fulls, axes):
    n = len(fulls)

    def copies(outs, send, recv):
        x, y, c = _me()
        own_j = 2 * x + y
        res = []
        for t in range(n):
            for k, chip in enumerate(_other_chips(x, y)):
                own = _piece(outs[t], axes[t], own_j, c)
                landed = _piece(outs[t], axes[t], 2 * chip[0] + chip[1], c)
                res.append((_rcopy(own, own, send.at[3 * t + k], recv.at[3 * t + k], (*chip, c)),
                            _rcopy(landed, landed, send.at[3 * t + k], recv.at[3 * t + k], (*chip, c))))
        return res

    def start(ins, outs, send, recv):
        for out, _ in copies(outs, send, recv):
            out.start()

    def finish(ins, outs, send, recv):
        for out, arrival in copies(outs, send, recv):
            out.wait_send()
            arrival.wait_recv()

    return _Comm(fulls, _like(fulls), {t: t for t in range(n)}, 3 * n, start, finish)


def _gather_d2d_comm(fulls, axes):
    n = len(fulls)

    def copies(outs, send, recv):
        x, y, c = _me()
        res = []
        for t in range(n):
            for k, chip in enumerate(_other_chips(x, y)):
                landed = _piece(outs[t], axes[t], 2 * chip[0] + chip[1], c)
                theirs = _piece(outs[t], axes[t], 2 * chip[0] + chip[1], 1 - c)
                res.append((_rcopy(landed, landed, send.at[3 * t + k], recv.at[3 * t + k], (x, y, 1 - c)),
                            _rcopy(theirs, theirs, send.at[3 * t + k], recv.at[3 * t + k], (x, y, 1 - c))))
        return res

    def start(ins, outs, send, recv):
        for out, _ in copies(outs, send, recv):
            out.start()

    def finish(ins, outs, send, recv):
        for out, arrival in copies(outs, send, recv):
            out.wait_send()
            arrival.wait_recv()

    return _Comm(fulls, _like(fulls), {t: t for t in range(n)}, 3 * n, start, finish)


def _swap_comm(grads, axes):
    n = len(grads)
    half_shapes = [(g.shape[0] // 2, g.shape[1]) if a == 1 else (g.shape[0], g.shape[1] // 2) for g, a in zip(grads, axes)]

    def copies(ins, outs, send, recv):
        x, y, c = _me()
        return [_rcopy(_half(ins[t], axes[t], 1 - c), outs[t], send.at[t], recv.at[t], (x, y, 1 - c)) for t in range(n)]

    def start(ins, outs, send, recv):
        for cp in copies(ins, outs, send, recv):
            cp.start()

    def finish(ins, outs, send, recv):
        for cp in copies(ins, outs, send, recv):
            cp.wait()

    return _Comm(grads, [jax.ShapeDtypeStruct(s, F32) for s in half_shapes], {}, n, start, finish)


def _scatter_comm(parts, axes):
    n = len(parts)
    q_shapes = [(p.shape[0], p.shape[1] // 4) if a == 1 else (p.shape[0] // 4, p.shape[1]) for p, a in zip(parts, axes)]

    def copies(ins, outs, send, recv):
        x, y, c = _me()
        res = []
        for t in range(n):
            for k, chip in enumerate(_other_chips(x, y)):
                res.append(_rcopy(_chip_part(ins[t], axes[t], 2 * chip[0] + chip[1]), outs[3 * t + k], send.at[3 * t + k],
                                  recv.at[3 * t + k], (*chip, c)))
        return res

    def start(ins, outs, send, recv):
        for cp in copies(ins, outs, send, recv):
            cp.start()

    def finish(ins, outs, send, recv):
        for cp in copies(ins, outs, send, recv):
            cp.wait()

    return _Comm(parts, [jax.ShapeDtypeStruct(s, p.dtype) for s, p in zip(q_shapes, parts) for _ in range(3)], {}, 3 * n,
                 start, finish)


def _share_comm(shards, axes):
    n = len(shards)

    def copies(outs, send, recv):
        x, y, c = _me()
        res = []
        for t in range(n):
            mine, theirs = _half(outs[t], axes[t], c), _half(outs[t], axes[t], 1 - c)
            res.append((_rcopy(mine, mine, send.at[t], recv.at[t], (x, y, 1 - c)),
                        _rcopy(theirs, theirs, send.at[t], recv.at[t], (x, y, 1 - c))))
        return res

    def start(ins, outs, send, recv):
        for out, _ in copies(outs, send, recv):
            out.start()

    def finish(ins, outs, send, recv):
        for out, arrival in copies(outs, send, recv):
            out.wait_send()
            arrival.wait_recv()

    return _Comm(shards, _like(shards), {t: t for t in range(n)}, n, start, finish)


def _cast_into_full(w, l, axis, sc):
    _, k, n = w.shape
    tr = _pick(k, 256, 16)
    if axis == 1:
        full, out_spec = (k, 4 * n), pl.BlockSpec((tr, n), lambda i, s: (i, s[1]))
    else:
        full, out_spec = (4 * k, n), pl.BlockSpec((tr, n), lambda i, s: (s[1] * (k // tr) + i, 0))

    def body(s_ref, w_ref, o_ref):
        o_ref[...] = w_ref[...].astype(BF16)

    return pl.pallas_call(
        body, name="cast_into_full",
        grid_spec=pltpu.PrefetchScalarGridSpec(
            num_scalar_prefetch=1, grid=(k // tr,),
            in_specs=[pl.BlockSpec((None, tr, n), lambda i, s: (l, i, 0))], out_specs=out_spec),
        out_shape=jax.ShapeDtypeStruct(full, BF16), compiler_params=_cparams(("arbitrary",)),
    )(sc, w)


def _rs_add2(g, got, axis, sc):
    k, n = g.shape
    hk, hn = (k // 2, n) if axis == 1 else (k, n // 2)
    tr = _pick(hk, max(16, (1 << 20) // hn), 16)
    if axis == 1:
        g_spec = pl.BlockSpec((tr, hn), lambda i, s: (s[0] * (hk // tr) + i, 0))
    else:
        g_spec = pl.BlockSpec((tr, hn), lambda i, s: (i, s[0]))
    blk = pl.BlockSpec((tr, hn), lambda i, s: (i, 0))

    def body(s_ref, a_ref, b_ref, o_ref):
        o_ref[...] = (a_ref[...] + b_ref[...]).astype(o_ref.dtype)

    return pl.pallas_call(
        body, name="rs_add2",
        grid_spec=pltpu.PrefetchScalarGridSpec(num_scalar_prefetch=1, grid=(hk // tr,), in_specs=[g_spec, blk], out_specs=blk),
        out_shape=jax.ShapeDtypeStruct((hk, hn), BF16), compiler_params=_cparams(("arbitrary",)),
    )(sc, g, got)


def _rs_add4(part, got3, axis, sc):
    k, n = part.shape
    qk, qn = (k, n // 4) if axis == 1 else (k // 4, n)
    tr = _pick(qk, max(16, (1 << 20) // qn), 16)
    if axis == 1:
        p_spec = pl.BlockSpec((tr, qn), lambda i, s: (i, s[1]))
        shard, o_spec = (2 * qk, qn), pl.BlockSpec((tr, qn), lambda i, s: (s[0] * (qk // tr) + i, 0))
    else:
        p_spec = pl.BlockSpec((tr, qn), lambda i, s: (s[1] * (qk // tr) + i, 0))
        shard, o_spec = (qk, 2 * qn), pl.BlockSpec((tr, qn), lambda i, s: (i, s[0]))
    blk = pl.BlockSpec((tr, qn), lambda i, s: (i, 0))

    def body(s_ref, p_ref, a_ref, b_ref, c_ref, o_ref):
        o_ref[...] = ((p_ref[...].astype(F32) + a_ref[...].astype(F32)) + b_ref[...].astype(F32)) + c_ref[...].astype(F32)

    return pl.pallas_call(
        body, name="rs_add4",
        grid_spec=pltpu.PrefetchScalarGridSpec(num_scalar_prefetch=1, grid=(qk // tr,), in_specs=[p_spec, blk, blk, blk],
                                               out_specs=o_spec),
        out_shape=jax.ShapeDtypeStruct(shard, F32), compiler_params=_cparams(("arbitrary",)),
    )(sc, part, *got3)


def _gather_small(v):
    def body(v_ref, out_ref, send, recv, loc):
        x, y, c = _me()
        sib = (x, y, 1 - c)
        chips = _other_chips(x, y)
        slot = lambda px, py, pc: out_ref.at[4 * px + 2 * py + pc]
        mine = pltpu.make_async_copy(v_ref, slot(x, y, c), loc)
        mine.start()
        first = [_rcopy(v_ref, slot(x, y, c), send.at[0], recv.at[0], sib)]
        first += [_rcopy(v_ref, slot(x, y, c), send.at[1 + k], recv.at[1 + k], (*chip, c)) for k, chip in enumerate(chips)]
        for cp in first:
            cp.start()
        passed = []
        for k, chip in enumerate(chips):
            landed = slot(*chip, c)
            _rcopy(landed, landed, send.at[1 + k], recv.at[1 + k], sib).wait_recv()
            passed.append(_rcopy(landed, landed, send.at[4 + k], recv.at[4 + k], sib))
            passed[-1].start()
        theirs = slot(x, y, 1 - c)
        _rcopy(theirs, theirs, send.at[0], recv.at[0], sib).wait_recv()
        for k, chip in enumerate(chips):
            theirs = slot(*chip, 1 - c)
            _rcopy(theirs, theirs, send.at[4 + k], recv.at[4 + k], sib).wait_recv()
        for cp in first + passed:
            cp.wait_send()
        mine.wait()

    return pl.pallas_call(
        body, name="gather_small",
        in_specs=[ANY], out_specs=ANY,
        out_shape=jax.ShapeDtypeStruct((8,) + v.shape, v.dtype),
        scratch_shapes=[pltpu.SemaphoreType.DMA((7,)), pltpu.SemaphoreType.DMA((7,)), pltpu.SemaphoreType.DMA],
    )(v)


class _WeightGather:
    def __init__(self, params, l, sc):
        self.bufs = {k: _cast_into_full(params[k], l, SHARD_AXIS[k], sc) for k in BIG}

    def run(self, names):
        self.update(names, _gather_weights([self.bufs[k] for k in names], [SHARD_AXIS[k] for k in names]))

    def ici(self, names):
        return _gather_ici_comm([self.bufs[k] for k in names], [SHARD_AXIS[k] for k in names])

    def d2d(self, names):
        return _gather_d2d_comm([self.bufs[k] for k in names], [SHARD_AXIS[k] for k in names])

    def update(self, names, outs):
        self.bufs.update(zip(names, outs))

    def weights(self):
        return self.bufs


class _GradReduce:
    def __init__(self, names, grads, sc):
        self.names, self.grads, self.sc, self.parts, self.theirs = names, grads, sc, None, None
        self.axes = [SHARD_AXIS[k] for k in names]

    def swap(self):
        return _swap_comm([self.grads[k] for k in self.names], self.axes)

    def swapped(self, got):
        self.parts = [_rs_add2(self.grads[k], s, a, self.sc) for k, s, a in zip(self.names, got, self.axes)]

    def scatter(self):
        return _scatter_comm(self.parts, self.axes)

    def scattered(self, outs):
        self.theirs = [outs[3 * q:3 * q + 3] for q in range(len(self.names))]

    def share(self):
        halves = [_rs_add4(p, th, a, self.sc) for p, th, a in zip(self.parts, self.theirs, self.axes)]
        return _share_comm(halves, self.axes)

    def shared(self, outs):
        self.result = dict(zip(self.names, outs))

    def run(self):
        self.swapped(_run_comm("rs_swap", self.swap()))
        self.scattered(_run_comm("rs_scatter", self.scatter()))
        self.shared(_run_comm("rs_share", self.share()))
        return self.result


def _adam_math(w, g, m, v):
    m = ADAM_B1 * m + (1.0 - ADAM_B1) * g
    v = ADAM_B2 * v + (1.0 - ADAM_B2) * (g * g)
    m_hat = m / (1.0 - ADAM_B1 ** ADAM_STEP)
    v_hat = v / (1.0 - ADAM_B2 ** ADAM_STEP)
    delta = -ADAM_LR * (m_hat / (jnp.sqrt(v_hat) + ADAM_EPS) + ADAM_WD * w)
    return delta, m, v


def _adam_layer(w, m, v, l, g, prev):
    L, k, n = w.shape
    tr = _pick(k, 128, 8)
    blk = pl.BlockSpec((None, tr, n), lambda i: (l, i, 0))

    def body(*refs):
        w_ref, m_ref, v_ref, g_ref = refs[:4]
        go, do, mo, vo = refs[-4:]
        gv = g_ref[...]
        d, m2, v2 = _adam_math(w_ref[...], gv, m_ref[...], v_ref[...])
        go[...] = gv
        do[...] = d
        mo[...] = m2
        vo[...] = v2

    args = [w, m, v, g]
    in_specs = [blk, blk, blk, pl.BlockSpec((tr, n), lambda i: (i, 0))]
    aliases = {}
    if prev is not None:
        for q, p in enumerate(prev):
            aliases[len(args)] = q
            in_specs.append(ANY)
            args.append(p)
    return pl.pallas_call(
        body, name="adam_layer", grid=(k // tr,),
        in_specs=in_specs, out_specs=[blk] * 4, out_shape=[jax.ShapeDtypeStruct((L, k, n), F32)] * 4,
        input_output_aliases=aliases, compiler_params=_cparams(("arbitrary",)),
    )(*args)


def _adam_flat(w, g, m, v):
    r = w.shape[0]
    tr = _pick(r, 512, 8)
    fn = lambda i, j, w, g, m, v: _adam_math(w, g, m, v)
    return _rowcall("adam_flat", fn, nrows=r, tr=tr, ins=[(a, "rows", 128, 0) for a in (w, g, m, v)],
                    outs=[(F32, 128, 128, 0, None)] * 3)


def _sum8(gathered):
    _, r, _ = gathered.shape
    tr = _pick(r, 512, 8)

    def body(g_ref, o_ref):
        acc = g_ref[0]
        for d in range(1, 8):
            acc = acc + g_ref[d]
        o_ref[...] = acc

    return pl.pallas_call(
        body, name="sum8", grid=(r // tr,),
        in_specs=[pl.BlockSpec((8, tr, 128), lambda i: (0, i, 0))], out_specs=pl.BlockSpec((tr, 128), lambda i: (i, 0)),
        out_shape=jax.ShapeDtypeStruct((r, 128), F32), compiler_params=_cparams(("arbitrary",)),
    )(gathered)


def _hdot(a, b, form="NN"):
    return _dot(a.astype(BF16), b.astype(BF16), form)


def _ada_fwd(s16, w_ada, l):
    _, d, ns = w_ada.shape
    tc = _pick(ns, 512)

    def body(s_ref, w_ref, o_ref):
        o_ref[...] = _hdot(s_ref[...], w_ref[...])

    return pl.pallas_call(
        body, name="ada_fwd", grid=(ns // tc,),
        in_specs=[pl.BlockSpec((16, d), lambda j: (0, 0)), pl.BlockSpec((None, d, tc), lambda j: (l, 0, j))],
        out_specs=pl.BlockSpec((16, tc), lambda j: (0, j)),
        out_shape=jax.ShapeDtypeStruct((16, ns), F32), compiler_params=_cparams(("arbitrary",)),
    )(s16, w_ada)


def _ada_bwd(s16t, dm, w_ada, l):
    _, d, ns = w_ada.shape
    tc = _pick(ns, 512)

    def body(st_ref, dm_ref, w_ref, dw_ref, ds_ref):
        j = pl.program_id(0)
        dw_ref[...] = _hdot(st_ref[...], dm_ref[...])
        part = _hdot(dm_ref[...], w_ref[...], "NT")

        @pl.when(j == 0)
        def _():
            ds_ref[...] = part

        @pl.when(j > 0)
        def _():
            ds_ref[...] += part

    return pl.pallas_call(
        body, name="ada_bwd", grid=(ns // tc,),
        in_specs=[pl.BlockSpec((d, 16), lambda j: (0, 0)), pl.BlockSpec((16, tc), lambda j: (0, j)),
                  pl.BlockSpec((None, d, tc), lambda j: (l, 0, j))],
        out_specs=[pl.BlockSpec((d, tc), lambda j: (0, j)), pl.BlockSpec((16, d), lambda j: (0, 0))],
        out_shape=[jax.ShapeDtypeStruct((d, ns), F32), jax.ShapeDtypeStruct((16, d), F32)],
        compiler_params=_cparams(("arbitrary",)),
    )(s16t, dm, w_ada)


def _tile_rows(shape):
    return -(-int(np.prod(shape)) // 1024) * 8


def _pack(arrs):
    parts = []
    for a in arrs:
        flat = a.reshape(-1).astype(F32)
        parts.append(jnp.pad(flat, (0, _tile_rows(a.shape) * 128 - flat.shape[0])).reshape(-1, 128))
    rows = sum(p.shape[0] for p in parts)
    parts.append(jnp.zeros(((-rows) % 512, 128), F32))
    return jnp.concatenate(parts, axis=0)


def _unpack(packed, shapes):
    out, row = [], 0
    for s in shapes:
        nr = _tile_rows(s)
        out.append(packed[row:row + nr].reshape(-1)[:int(np.prod(s))].reshape(s))
        row += nr
    return out


SMALL = ("norm1_g", "norm2_g", "ret_decay_logit", "ret_norm_g", "pool_w", "pool_scale", "sg_norm_g", "sg_w", "sg_b")


def kernel(x, c, ctx, c_ctx, w_ada, b_ada, norm1_g, w_in, ret_decay_logit, ret_norm_g, pool_w, pool_scale, sg_norm_g, sg_w, sg_b, w_br, w_bp, w_bs, w_out, norm2_g, w1, w2, final_norm_g, loss_target, m_c_ctx, m_w_ada, m_b_ada, m_norm1_g, m_w_in, m_ret_decay_logit, m_ret_norm_g, m_pool_w, m_pool_scale, m_sg_norm_g, m_sg_w, m_sg_b, m_w_br, m_w_bp, m_w_bs, m_w_out, m_norm2_g, m_w1, m_w2, m_final_norm_g, v_c_ctx, v_w_ada, v_b_ada, v_norm1_g, v_w_in, v_ret_decay_logit, v_ret_norm_g, v_pool_w, v_pool_scale, v_sg_norm_g, v_sg_w, v_sg_b, v_w_br, v_w_bp, v_w_bs, v_w_out, v_norm2_g, v_w1, v_w2, v_final_norm_g):
    P = dict(c_ctx=c_ctx, w_ada=w_ada, b_ada=b_ada, norm1_g=norm1_g, w_in=w_in, ret_decay_logit=ret_decay_logit,
             ret_norm_g=ret_norm_g, pool_w=pool_w, pool_scale=pool_scale, sg_norm_g=sg_norm_g, sg_w=sg_w, sg_b=sg_b, w_br=w_br,
             w_bp=w_bp, w_bs=w_bs, w_out=w_out, norm2_g=norm2_g, w1=w1, w2=w2, final_norm_g=final_norm_g)
    Mo = dict(c_ctx=m_c_ctx, w_ada=m_w_ada, b_ada=m_b_ada, norm1_g=m_norm1_g, w_in=m_w_in, ret_decay_logit=m_ret_decay_logit,
              ret_norm_g=m_ret_norm_g, pool_w=m_pool_w, pool_scale=m_pool_scale, sg_norm_g=m_sg_norm_g, sg_w=m_sg_w, sg_b=m_sg_b,
              w_br=m_w_br, w_bp=m_w_bp, w_bs=m_w_bs, w_out=m_w_out, norm2_g=m_norm2_g, w1=m_w1, w2=m_w2,
              final_norm_g=m_final_norm_g)
    Vo = dict(c_ctx=v_c_ctx, w_ada=v_w_ada, b_ada=v_b_ada, norm1_g=v_norm1_g, w_in=v_w_in, ret_decay_logit=v_ret_decay_logit,
              ret_norm_g=v_ret_norm_g, pool_w=v_pool_w, pool_scale=v_pool_scale, sg_norm_g=v_sg_norm_g, sg_w=v_sg_w, sg_b=v_sg_b,
              w_br=v_w_br, w_bp=v_w_bp, w_bs=v_w_bs, w_out=v_w_out, norm2_g=v_norm2_g, w1=v_w1, w2=v_w2,
              final_norm_g=v_final_norm_g)
    names = ("c_ctx", "w_ada", "b_ada", "norm1_g", "w_in", "ret_decay_logit", "ret_norm_g", "pool_w", "pool_scale", "sg_norm_g",
             "sg_w", "sg_b", "w_br", "w_bp", "w_bs", "w_out", "norm2_g", "w1", "w2", "final_norm_g")
    L, D = w_in.shape[0], x.shape[-1]
    T, CTX = x.shape[1], ctx.shape[1]
    cfg = _Cfg(D, T, CTX, 4 * w1.shape[2])
    ns_ada = w_ada.shape[2]
    mx, my, mc = _me()
    dev = 4 * mx + 2 * my + mc
    chip = 2 * mx + my

    silu_cc = jax.nn.silu(c_ctx)
    silu_all = _gather_small(jax.nn.silu(c).reshape(-1, 128)).reshape(8, D)
    s16 = jnp.concatenate([silu_cc[None], silu_all, jnp.zeros((7, D), F32)], axis=0)
    proj = jnp.stack([_ada_fwd(s16, w_ada, l) for l in range(L)])
    proj_all = _gather_small(proj.reshape(-1, 128)).reshape(8, L, 16, ns_ada)
    mods_full = jnp.concatenate([proj_all[2 * j] for j in range(4)], axis=-1) + b_ada[:, None, :]
    mods = [jnp.concatenate([mods_full[l, 0:1], lax.dynamic_slice_in_dim(mods_full[l], 1 + dev, 1, axis=0)], axis=0)
            for l in range(L)]

    sc = jnp.stack([mc, chip]).astype(jnp.int32)
    gathers = [_WeightGather(P, l, sc) for l in range(L)]
    X0 = jnp.concatenate([ctx[0], x[0]], axis=0)
    small_w = {k: P[k] for k in SMALL}
    loss_part, dX, shard_g, small, dmods, dfinal = _local_fwd_bwd(
        cfg, X0, loss_target[0], mods, gathers, lambda names, grads: _GradReduce(names, grads, sc), small_w, final_norm_g)
    loss = lax.psum(loss_part, ("x", "y", "c"))
    grad_x = dX[CTX:][None]

    outs = {k: None for k in BIG}
    for l in range(L):
        for k in BIG:
            outs[k] = _adam_layer(P[k], Mo[k], Vo[k], l, shard_g[l][k], outs[k])

    per_layer = [[small[l][k] for k in SMALL] + [dmods[l][1], dmods[l][0]] for l in range(L)]
    payload = _pack([a for lay in per_layer for a in lay] + [dfinal])
    gathered = _gather_small(payload)
    total = _sum8(gathered)
    shapes = [P[k].shape[1:] for k in SMALL] + [(6 * D,), (6 * D,)]
    tot = _unpack(total, shapes * L + [(D,)])
    per = len(shapes)
    g_small = {k: jnp.stack([tot[l * per + q] for l in range(L)]) for q, k in enumerate(SMALL)}
    dmx_sum = jnp.stack([tot[l * per + per - 2] for l in range(L)])
    dmc_sum = jnp.stack([tot[l * per + per - 1] for l in range(L)])
    g_small["b_ada"] = dmx_sum + dmc_sum
    g_small["final_norm_g"] = tot[-1]
    rows = [_tile_rows(s) for s in shapes]
    s16t = s16.T
    ada_out, ds_part = None, jnp.zeros((16, D), F32)
    for l in range(L):
        r0 = l * sum(rows) + sum(rows[:per - 2])
        dmx_all = gathered[:, r0:r0 + rows[per - 2], :].reshape(8, -1)[:, :6 * D]
        dm_full = jnp.concatenate([dmc_sum[l][None], dmx_all, jnp.zeros((7, 6 * D), F32)], axis=0)
        dm = lax.dynamic_slice_in_dim(dm_full, chip * ns_ada, ns_ada, axis=1)
        dw, ds = _ada_bwd(s16t, dm, w_ada, l)
        ds_part = ds_part + ds
        ada_out = _adam_layer(w_ada, Mo["w_ada"], Vo["w_ada"], l, dw, ada_out)
    outs["w_ada"] = ada_out
    ds_all = _gather_small(ds_part[0].reshape(-1, 128)).reshape(8, D)
    d_silu_cc = ds_all[0] + ds_all[2] + ds_all[4] + ds_all[6]
    g_small["c_ctx"] = jax.vjp(jax.nn.silu, c_ctx)[1](d_silu_cc)[0]

    small_names = [k for k in names if k not in BIG and k != "w_ada"]
    sm_shapes = [P[k].shape for k in small_names]
    res = _adam_flat(_pack([P[k] for k in small_names]), _pack([g_small[k] for k in small_names]),
                     _pack([Mo[k] for k in small_names]), _pack([Vo[k] for k in small_names]))
    d_s, m_s, v_s = [_unpack(r, sm_shapes) for r in res]
    for q, k in enumerate(small_names):
        outs[k] = (g_small[k].reshape(P[k].shape), d_s[q], m_s[q], v_s[q])

    return (loss, grad_x, *[outs[k][0] for k in names], *[outs[k][1] for k in names], *[outs[k][2] for k in names],
            *[outs[k][3] for k in names])
```
